```python
import jax, jax.numpy as jnp
from jax import lax
import numpy as np


D_MODEL = 1024
BATCH = 4
SEQ = 8192
DEPTH = 4

GRID_W = 64
CTX_LEN = 256
FOURIER_GROUPS = 4
FOURIER_GROUP_DIM = D_MODEL // 8
FOURIER_DIM = FOURIER_GROUPS * FOURIER_GROUP_DIM
HEAD_DIM = 64
N_Q_HEADS = D_MODEL // 128
N_KV_HEADS = N_Q_HEADS // 4
Q_PER_KV = N_Q_HEADS // N_KV_HEADS
ATTN_DIM = N_Q_HEADS * HEAD_DIM
KV_DIM = N_KV_HEADS * HEAD_DIM
WINDOW = 128
BLOCK = 128
ROPE_THETA = 10000.0
ROPE_FREQS = HEAD_DIM // 4
IN_DIM = FOURIER_DIM + ATTN_DIM + 2 * KV_DIM + 2 * D_MODEL
SPLITS = (FOURIER_DIM, FOURIER_DIM + ATTN_DIM, FOURIER_DIM + ATTN_DIM + KV_DIM,
          FOURIER_DIM + ATTN_DIM + 2 * KV_DIM, FOURIER_DIM + ATTN_DIM + 2 * KV_DIM + D_MODEL)
KV_LO = FOURIER_DIM + ATTN_DIM
KV_HI = FOURIER_DIM + ATTN_DIM + 2 * KV_DIM
N_EXPERTS = 32
TOP_K = 4
D_EXPERT = D_MODEL
SWIGLU_LIMIT = 7.0
SWIGLU_ALPHA = 1.702
MOE_BLOCK = 128
N_MOD = 6
EPS = 1e-5
NEG_INF = -1e30

kernel_name = 'hybrid_fourier_window_moe_dit_block'


def rmsnorm(x, g):
    xf = x.astype(jnp.float32)
    y = xf * lax.rsqrt(jnp.mean(xf * xf, axis=-1, keepdims=True) + EPS)
    return (y * g.astype(jnp.float32)).astype(x.dtype)


def modulate(h, shift, scale):
    return h * (1 + scale) + shift


def axial_rope_tables(n):
    rows = n // GRID_W
    row = jnp.broadcast_to(jnp.arange(rows)[:, None], (rows, GRID_W)).reshape(-1)
    col = jnp.broadcast_to(jnp.arange(GRID_W)[None, :], (rows, GRID_W)).reshape(-1)
    inv_freq = ROPE_THETA ** (-jnp.arange(ROPE_FREQS, dtype=jnp.float32) / ROPE_FREQS)
    ang = jnp.stack([row.astype(jnp.float32)[:, None] * inv_freq,
                     col.astype(jnp.float32)[:, None] * inv_freq], axis=1)
    return jnp.cos(ang), jnp.sin(ang)


def apply_axial_rope(x, cos, sin):
    b, n, h, _ = x.shape
    xs = x.reshape(b, n, h, 2, 2, ROPE_FREQS)
    c = cos.reshape(1, n, 1, 2, ROPE_FREQS).astype(x.dtype)
    s = sin.reshape(1, n, 1, 2, ROPE_FREQS).astype(x.dtype)
    x1, x2 = xs[..., 0, :], xs[..., 1, :]
    out = jnp.stack([x1 * c - x2 * s, x2 * c + x1 * s], axis=-2)
    return out.reshape(x.shape)


def split_projection(p):
    b, n = p.shape[:2]
    f, q, k, v, gf, ga = jnp.split(p, SPLITS, axis=-1)
    return (f, q.reshape(b, n, N_Q_HEADS, HEAD_DIM), k.reshape(b, n, N_KV_HEADS, HEAD_DIM),
            v.reshape(b, n, N_KV_HEADS, HEAD_DIM), gf, ga)


def fourier_mix(f):
    b, n, _ = f.shape
    fg = f.astype(jnp.float32).reshape(b, n, FOURIER_GROUPS, FOURIER_GROUP_DIM)
    out = jnp.fft.fft2(fg, axes=(1, 3), norm='ortho').real
    return out.reshape(b, n, FOURIER_DIM).astype(f.dtype)


def sink_column(sink, shape):
    return jnp.broadcast_to(sink.astype(jnp.float32).reshape((1,) * (len(shape) - 4) + (N_KV_HEADS, Q_PER_KV, 1, 1)), shape)


def latent_attention(q, k, v, kc, vc, sink):
    b, n = q.shape[:2]
    L = kc.shape[1]
    nb = n // BLOCK
    qb = (q * HEAD_DIM ** -0.5).reshape(b, nb, BLOCK, N_KV_HEADS, Q_PER_KV, HEAD_DIM)

    def windows(t):
        tp = jnp.pad(t, ((0, 0), (BLOCK, BLOCK), (0, 0), (0, 0))).reshape(b, nb + 2, BLOCK, N_KV_HEADS, HEAD_DIM)
        return jnp.concatenate([tp[:, :-2], tp[:, 1:-1], tp[:, 2:]], axis=2)

    kw, vw = windows(k), windows(v)
    s_loc = jnp.einsum('bnqhgd,bnkhd->bnhgqk', qb, kw).astype(jnp.float32)
    r = jnp.arange(BLOCK)[:, None]
    s = jnp.arange(3 * BLOCK)[None, :]
    rel = s - BLOCK - r
    key_pos = (jnp.arange(nb)[:, None, None] - 1) * BLOCK + s[None]
    mask = (jnp.abs(rel) <= WINDOW)[None] & (key_pos >= 0) & (key_pos < n)
    s_loc = jnp.where(mask[None, :, None, None], s_loc, NEG_INF)
    s_ctx = jnp.einsum('bnqhgd,bkhd->bnhgqk', qb, kc).astype(jnp.float32)
    sk = sink_column(sink, s_ctx.shape[:-1] + (1,))
    p = jax.nn.softmax(jnp.concatenate([s_loc, s_ctx, sk], axis=-1), axis=-1)
    p_loc = p[..., :3 * BLOCK].astype(v.dtype)
    p_ctx = p[..., 3 * BLOCK:3 * BLOCK + L].astype(v.dtype)
    o = (jnp.einsum('bnhgqk,bnkhd->bnqhgd', p_loc, vw)
         + jnp.einsum('bnhgqk,bkhd->bnqhgd', p_ctx, vc))
    return o.reshape(b, n, ATTN_DIM)


def context_attention(qc, kc, vc, sink):
    b, L = qc.shape[:2]
    qg = (qc * HEAD_DIM ** -0.5).reshape(b, L, N_KV_HEADS, Q_PER_KV, HEAD_DIM)
    s = jnp.einsum('bqhgd,bkhd->bhgqk', qg, kc).astype(jnp.float32)
    sk = sink_column(sink, s.shape[:-1] + (1,))
    p = jax.nn.softmax(jnp.concatenate([s, sk], axis=-1), axis=-1)[..., :L].astype(vc.dtype)
    o = jnp.einsum('bhgqk,bkhd->bqhgd', p, vc)
    return o.reshape(b, L, ATTN_DIM)


def merge_branches(f_mixed, attn, gf, ga, w_fo, w_ao, w_out):
    y = jax.nn.sigmoid(gf) * (f_mixed @ w_fo) + jax.nn.sigmoid(ga) * (attn @ w_ao)
    return y @ w_out


def token_mixer(hx, hc, w_in, w_fo, w_ao, w_out, sink, cos, sin, with_ctx_out):
    fx, qx, kx, vx, gfx, gax = split_projection(hx @ w_in)
    qx, kx = apply_axial_rope(qx, cos, sin), apply_axial_rope(kx, cos, sin)
    if with_ctx_out:
        fc, qc, kc, vc, gfc, gac = split_projection(hc @ w_in)
    else:
        b, L = hc.shape[:2]
        kc, vc = jnp.split(hc @ w_in[:, KV_LO:KV_HI], 2, axis=-1)
        kc, vc = kc.reshape(b, L, N_KV_HEADS, HEAD_DIM), vc.reshape(b, L, N_KV_HEADS, HEAD_DIM)
    ax = latent_attention(qx, kx, vx, kc, vc, sink)
    out_x = merge_branches(fourier_mix(fx), ax, gfx, gax, w_fo, w_ao, w_out)
    if not with_ctx_out:
        return out_x, None
    ac = context_attention(qc, kc, vc, sink)
    out_c = merge_branches(fourier_mix(fc), ac, gfc, gac, w_fo, w_ao, w_out)
    return out_x, out_c


def moe_ffn(h, router_w, router_b, w_gu, b_gu, w_down, b_down):
    t, d = h.shape
    logits = (h @ router_w + router_b).astype(jnp.float32)
    top_v, top_i = lax.top_k(logits, TOP_K)
    gates = jax.nn.softmax(top_v, axis=-1).astype(h.dtype)
    tk = t * TOP_K
    flat_e = top_i.reshape(tk)
    flat_tok = jnp.arange(tk) // TOP_K
    order = jnp.argsort(flat_e)
    e_sorted = flat_e[order]
    tok_sorted = flat_tok[order]
    gate_sorted = gates.reshape(tk)[order]
    counts = jnp.zeros((N_EXPERTS,), jnp.int32).at[flat_e].add(1)
    padded = (counts + MOE_BLOCK - 1) // MOE_BLOCK * MOE_BLOCK
    pad_end = jnp.cumsum(padded)
    pad_start = pad_end - padded
    start = jnp.cumsum(counts) - counts
    dest = pad_start[e_sorted] + jnp.arange(tk) - start[e_sorted]
    n_blocks = -(-(tk + N_EXPERTS * (MOE_BLOCK - 1)) // MOE_BLOCK)
    rows = n_blocks * MOE_BLOCK
    slot_tok = jnp.full((rows,), t, jnp.int32).at[dest].set(tok_sorted)
    h_pad = jnp.concatenate([h, jnp.zeros((1, d), h.dtype)], axis=0)
    xb = h_pad[slot_tok].reshape(n_blocks, MOE_BLOCK, d)
    block_e = jnp.minimum(jnp.searchsorted(pad_end, jnp.arange(n_blocks) * MOE_BLOCK, side='right'), N_EXPERTS - 1)

    def expert_block(args):
        xe, e = args
        gu = xe @ w_gu[e] + b_gu[e]
        a, u = jnp.split(gu, 2, axis=-1)
        a = jnp.minimum(a, SWIGLU_LIMIT)
        u = jnp.clip(u, -SWIGLU_LIMIT, SWIGLU_LIMIT)
        y = a * jax.nn.sigmoid(SWIGLU_ALPHA * a) * (u + 1)
        return y @ w_down[e] + b_down[e]

    yb = lax.map(expert_block, (xb, block_e)).reshape(rows, d)
    contrib = yb[dest] * gate_sorted[:, None]
    return jax.ops.segment_sum(contrib, tok_sorted, num_segments=t)


def setup_inputs(seed: int = 0) -> dict:
    key = jax.random.key(seed)
    ks = jax.random.split(key, 20)

    def nrm(k, shape, scale):
        return jax.random.normal(k, shape, jnp.float32) * scale

    return {
        'x': nrm(ks[0], (BATCH, SEQ, D_MODEL), 1.0),
        'c': nrm(ks[1], (BATCH, D_MODEL), 1.0),
        'ctx': nrm(ks[2], (BATCH, CTX_LEN, D_MODEL), 1.0),
        'c_ctx': nrm(ks[3], (D_MODEL,), 1.0),
        'ada_w': nrm(ks[4], (DEPTH, D_MODEL, N_MOD * D_MODEL), 0.5 * D_MODEL ** -0.5),
        'ada_b': nrm(ks[5], (DEPTH, N_MOD * D_MODEL), 0.02),
        'norm1_g': 1.0 + nrm(ks[6], (DEPTH, D_MODEL), 0.02),
        'norm2_g': 1.0 + nrm(ks[7], (DEPTH, D_MODEL), 0.02),
        'w_in': nrm(ks[8], (DEPTH, D_MODEL, IN_DIM), D_MODEL ** -0.5),
        'attn_sink': nrm(ks[9], (DEPTH, N_Q_HEADS), 0.5),
        'w_fourier_out': nrm(ks[10], (DEPTH, FOURIER_DIM, D_MODEL), FOURIER_DIM ** -0.5),
        'w_attn_out': nrm(ks[11], (DEPTH, ATTN_DIM, D_MODEL), ATTN_DIM ** -0.5),
        'w_out': nrm(ks[12], (DEPTH, D_MODEL, D_MODEL), D_MODEL ** -0.5),
        'router_w': nrm(ks[13], (DEPTH, D_MODEL, N_EXPERTS), D_MODEL ** -0.5),
        'router_b': nrm(ks[14], (DEPTH, N_EXPERTS), 0.01),
        'expert_w_gu': nrm(ks[15], (DEPTH, N_EXPERTS, D_MODEL, 2 * D_EXPERT), D_MODEL ** -0.5),
        'expert_b_gu': nrm(ks[16], (DEPTH, N_EXPERTS, 2 * D_EXPERT), 0.02),
        'expert_w_down': nrm(ks[17], (DEPTH, N_EXPERTS, D_EXPERT, D_MODEL), D_EXPERT ** -0.5),
        'expert_b_down': nrm(ks[18], (DEPTH, N_EXPERTS, D_MODEL), 0.02),
        'final_norm_g': 1.0 + nrm(ks[19], (D_MODEL,), 0.02),
    }


def reference(x, c, ctx, c_ctx, ada_w, ada_b, norm1_g, norm2_g, w_in, attn_sink, w_fourier_out,
              w_attn_out, w_out, router_w, router_b, expert_w_gu, expert_b_gu, expert_w_down,
              expert_b_down, final_norm_g):
    b, n, d = x.shape
    L = ctx.shape[1]
    cos, sin = axial_rope_tables(n)
    silu_c = jax.nn.silu(c)
    silu_cc = jax.nn.silu(c_ctx)
    for l in range(DEPTH):
        last = l == DEPTH - 1
        mod_x = (silu_c @ ada_w[l] + ada_b[l])[:, None, :]
        mod_c = silu_cc @ ada_w[l] + ada_b[l]
        sh1x, sc1x, g1x, sh2x, sc2x, g2x = jnp.split(mod_x, N_MOD, axis=-1)
        sh1c, sc1c, g1c, sh2c, sc2c, g2c = jnp.split(mod_c, N_MOD, axis=-1)
        hx = modulate(rmsnorm(x, norm1_g[l]), sh1x, sc1x)
        hc = modulate(rmsnorm(ctx, norm1_g[l]), sh1c, sc1c)
        out_x, out_c = token_mixer(hx, hc, w_in[l], w_fourier_out[l], w_attn_out[l], w_out[l],
                                   attn_sink[l], cos, sin, not last)
        x = x + g1x * out_x
        hx2 = modulate(rmsnorm(x, norm2_g[l]), sh2x, sc2x).reshape(b * n, d)
        if last:
            m = moe_ffn(hx2, router_w[l], router_b[l], expert_w_gu[l], expert_b_gu[l],
                        expert_w_down[l], expert_b_down[l])
            x = x + g2x * m.reshape(b, n, d)
        else:
            ctx = ctx + g1c * out_c
            hc2 = modulate(rmsnorm(ctx, norm2_g[l]), sh2c, sc2c).reshape(b * L, d)
            m = moe_ffn(jnp.concatenate([hx2, hc2], axis=0), router_w[l], router_b[l], expert_w_gu[l],
                        expert_b_gu[l], expert_w_down[l], expert_b_down[l])
            x = x + g2x * m[:b * n].reshape(b, n, d)
            ctx = ctx + g2c * m[b * n:].reshape(b, L, d)
    return rmsnorm(x, final_norm_g)
```

```python
import functools

import numpy as np
import jax
import jax.numpy as jnp
from jax import lax
from jax.experimental import pallas as pl
from jax.experimental.pallas import tpu as pltpu

GRID_W = 64
HEAD_DIM = 64
N_KV_HEADS = 2
Q_PER_KV = 4
KV_DIM = N_KV_HEADS * HEAD_DIM
WINDOW = 128
BLOCK = 128
ROPE_THETA = 10000.0
ROPE_FREQS = HEAD_DIM // 4
GROUP_DIM = 128
N_EXPERTS = 32
TOP_K = 4
SWIGLU_LIMIT = 7.0
SWIGLU_ALPHA = 1.702
N_MOD = 6
EPS = 1e-5
NEG_INF = -1e30

LANES = 128
MOD_ROWS = 8
TM = 512
TME = 256
TMC = 256
FA_NB = 16
FB_NB = 16
VMEM_LIMIT = 56 * 1024 * 1024

_BF = jnp.bfloat16
_F32 = jnp.float32


def _params(*sem):
    return pltpu.CompilerParams(dimension_semantics=sem, vmem_limit_bytes=VMEM_LIMIT)


def _dot(a, b):
    return jnp.dot(a, b, preferred_element_type=_F32)


def _norm_mod(x, g, shift, scale):
    ms = jnp.mean(x * x, axis=-1, keepdims=True)
    return (x * lax.rsqrt(ms + EPS) * g) * (1 + scale) + shift


def _ada_kernel(c_ref, w_ref, b_ref, o_ref):
    c = c_ref[...]
    s = c * jax.nn.sigmoid(c)
    o_ref[0] = jnp.dot(s, w_ref[0], precision=lax.Precision.HIGHEST,
                       preferred_element_type=_F32) + b_ref[0]


def _ada(cc, ada_w, ada_b):
    depth, d, nd = ada_w.shape
    tn = nd // 4
    return pl.pallas_call(
        _ada_kernel,
        grid=(depth, nd // tn),
        in_specs=[pl.BlockSpec((MOD_ROWS, d), lambda l, j: (0, 0)),
                  pl.BlockSpec((1, d, tn), lambda l, j: (l, 0, j)),
                  pl.BlockSpec((1, 1, tn), lambda l, j: (l, 0, j))],
        out_specs=pl.BlockSpec((1, MOD_ROWS, tn), lambda l, j: (l, 0, j)),
        out_shape=jax.ShapeDtypeStruct((depth, MOD_ROWS, nd), _F32),
        compiler_params=_params("arbitrary", "arbitrary"),
        name="ada_mod",
    )(cc, ada_w, ada_b.reshape(depth, 1, nd))


def _rope(v, cos, sin):
    lane = lax.broadcasted_iota(jnp.int32, v.shape, 1)
    partner = jnp.where((lane & ROPE_FREQS) == 0,
                        pltpu.roll(v, LANES - ROPE_FREQS, 1), pltpu.roll(v, ROPE_FREQS, 1))
    return v * cos + partner * sin


def _proj_kernel(x_ref, mod_ref, g_ref, w_ref, cos_ref, sin_ref, f_ref, q_ref, k_ref, v_ref, *, fd, qd):
    m = mod_ref[0]
    h = _norm_mod(x_ref[...], g_ref[...], m[0:1], m[1:2]).astype(_BF)
    p = _dot(h, w_ref[...])
    cos = cos_ref[...]
    sin = sin_ref[...]
    f_ref[...] = p[:, :fd].astype(_BF)
    for j in range(qd // LANES):
        lo = fd + j * LANES
        q_ref[:, j * LANES:(j + 1) * LANES] = (
            _rope(p[:, lo:lo + LANES], cos, sin) * (HEAD_DIM ** -0.5)).astype(_BF)
    k = _rope(p[:, fd + qd:fd + qd + KV_DIM], cos, sin)
    v = p[:, fd + qd + KV_DIM:fd + qd + 2 * KV_DIM]
    k_ref[:, :KV_DIM] = k.astype(_BF)
    k_ref[:, KV_DIM:] = pltpu.roll(k, HEAD_DIM, 1).astype(_BF)
    v_ref[:, :KV_DIM] = v.astype(_BF)
    v_ref[:, KV_DIM:] = pltpu.roll(v, HEAD_DIM, 1).astype(_BF)


def _proj(xa, mod, g, w_a, cos_t, sin_t, *, n, t_lat, n_batch):
    tt, d = xa.shape
    fd = qd = d // 2
    nt = tt // TM
    nxt = t_lat // TM
    per_seq = n // TM
    mod_idx = lambda i: (jnp.minimum(i * TM // n, n_batch), 0, 0)
    rope_idx = lambda i: (jnp.where(i < nxt, i % per_seq, per_seq), 0)
    row = lambda i: (i, 0)
    return pl.pallas_call(
        functools.partial(_proj_kernel, fd=fd, qd=qd),
        grid=(nt,),
        in_specs=[pl.BlockSpec((TM, d), row),
                  pl.BlockSpec((1, N_MOD, d), mod_idx),
                  pl.BlockSpec((1, d), lambda i: (0, 0)),
                  pl.BlockSpec(w_a.shape, lambda i: (0, 0)),
                  pl.BlockSpec((TM, LANES), rope_idx),
                  pl.BlockSpec((TM, LANES), rope_idx)],
        out_specs=[pl.BlockSpec((TM, fd), row), pl.BlockSpec((TM, qd), row),
                   pl.BlockSpec((TM, 2 * KV_DIM), row), pl.BlockSpec((TM, 2 * KV_DIM), row)],
        out_shape=[jax.ShapeDtypeStruct((tt, fd), _BF), jax.ShapeDtypeStruct((tt, qd), _BF),
                   jax.ShapeDtypeStruct((tt, 2 * KV_DIM), _BF), jax.ShapeDtypeStruct((tt, 2 * KV_DIM), _BF)],
        compiler_params=_params("arbitrary"),
        name="proj_in",
    )(xa, mod, g, w_a, cos_t, sin_t)


def _rope_tables(n):
    pos = jnp.arange(n)
    inv = ROPE_THETA ** (-jnp.arange(ROPE_FREQS, dtype=_F32) / ROPE_FREQS)
    ar = (pos // GRID_W).astype(_F32)[:, None] * inv
    ac = (pos % GRID_W).astype(_F32)[:, None] * inv
    cos = jnp.concatenate([jnp.cos(ar), jnp.cos(ar), jnp.cos(ac), jnp.cos(ac)], axis=1)
    sin = jnp.concatenate([-jnp.sin(ar), jnp.sin(ar), -jnp.sin(ac), jnp.sin(ac)], axis=1)
    cos = jnp.tile(cos, (1, LANES // HEAD_DIM))
    sin = jnp.tile(sin, (1, LANES // HEAD_DIM))
    cos = jnp.concatenate([cos, jnp.ones((TM, LANES), _F32)], axis=0)
    sin = jnp.concatenate([sin, jnp.zeros((TM, LANES), _F32)], axis=0)
    return cos, sin


def _dft_consts(n, ctx_len):
    n1 = n // BLOCK
    b = np.arange(BLOCK, dtype=np.int64)[:, None, None]
    k1 = np.arange(n1, dtype=np.int64)[None, :, None]
    a = np.arange(n1, dtype=np.int64)[None, None, :]
    ang = 2.0 * np.pi * ((a * k1 * BLOCK + b * k1) % n).astype(np.float64) / n
    m_r, m_i = np.cos(ang), -np.sin(ang)
    kk = np.arange(GROUP_DIM, dtype=np.int64)
    ang_c = 2.0 * np.pi * ((kk[:, None] * kk[None, :]) % GROUP_DIM) / GROUP_DIM
    c, s = np.cos(ang_c), np.sin(ang_c)
    cs1 = np.concatenate([c, s], axis=1)
    cs2 = np.concatenate([-s, c], axis=1)
    cc = np.concatenate([c, s], axis=0)
    kl = np.arange(ctx_len, dtype=np.int64)
    ang_l = 2.0 * np.pi * ((kl[:, None] * kl[None, :]) % ctx_len) / ctx_len
    wl = np.concatenate([np.cos(ang_l), -np.sin(ang_l)], axis=0)
    as_bf = lambda v: jnp.asarray(v, dtype=_F32).astype(_BF)
    return dict(m_r=as_bf(m_r), m_i=as_bf(m_i), cs1=as_bf(cs1), cs2=as_bf(cs2), cc=as_bf(cc), wl=as_bf(wl))


def _chan_dft(xr, xi, cc, scale):
    outs = []
    for g in range(xr.shape[1] // GROUP_DIM):
        sl = slice(g * GROUP_DIM, (g + 1) * GROUP_DIM)
        xx = jnp.concatenate([xr[:, sl], xi[:, sl]], axis=1).astype(_BF)
        outs.append(_dot(xx, cc) * scale)
    return jnp.concatenate(outs, axis=1)


def _fa_kernel(f_ref, mr_ref, mi_ref, zr_ref, zi_ref):
    for t in range(FA_NB):
        xb = f_ref[:, t, :]
        zr_ref[:, t, :] = _dot(mr_ref[t], xb).astype(_BF)
        zi_ref[:, t, :] = _dot(mi_ref[t], xb).astype(_BF)


def _fb_kernel(zr_ref, zi_ref, cs1_ref, cs2_ref, cc_ref, o_ref, *, scale):
    cs1, cs2, cc = cs1_ref[...], cs2_ref[...], cc_ref[...]
    for j in range(FB_NB):
        z = jnp.concatenate([zr_ref[j], zi_ref[j]], axis=0)
        o_ref[:, j, :] = _chan_dft(_dot(cs1, z), _dot(cs2, z), cc, scale).astype(_BF)


def _fourier_lat(f, consts, *, n_batch, n):
    tt, fd = f.shape
    n1 = n // BLOCK
    f3 = f.reshape(tt // BLOCK, BLOCK, fd)
    blk_a = pl.BlockSpec((n1, FA_NB, fd), lambda b, j: (b, j, 0))
    mat_a = pl.BlockSpec((FA_NB, n1, n1), lambda b, j: (j, 0, 0))
    z_shape = jax.ShapeDtypeStruct((n_batch * n1, BLOCK, fd), _BF)
    zr, zi = pl.pallas_call(
        _fa_kernel,
        grid=(n_batch, BLOCK // FA_NB),
        in_specs=[blk_a, mat_a, mat_a],
        out_specs=[blk_a, blk_a],
        out_shape=[z_shape, z_shape],
        compiler_params=_params("arbitrary", "arbitrary"),
        name="fourier_seq_a",
    )(f3, consts["m_r"], consts["m_i"])
    blk_z = pl.BlockSpec((FB_NB, BLOCK, fd), lambda b, j: (b * (n1 // FB_NB) + j, 0, 0))
    const = lambda shape: pl.BlockSpec(shape, lambda b, j: (0, 0))
    out = pl.pallas_call(
        functools.partial(_fb_kernel, scale=float((n * GROUP_DIM) ** -0.5)),
        grid=(n_batch, n1 // FB_NB),
        in_specs=[blk_z, blk_z, const((BLOCK, 2 * BLOCK)), const((BLOCK, 2 * BLOCK)),
                  const((2 * GROUP_DIM, GROUP_DIM))],
        out_specs=pl.BlockSpec((BLOCK, FB_NB, fd), lambda b, j: (b, j, 0)),
        out_shape=jax.ShapeDtypeStruct((n_batch * BLOCK, n1, fd), _BF),
        compiler_params=_params("arbitrary", "arbitrary"),
        name="fourier_seq_b",
    )(zr, zi, consts["cs1"], consts["cs2"], consts["cc"])
    return out.reshape(n_batch * n, fd)


def _fc_kernel(f_ref, wl_ref, cc_ref, o_ref, *, scale):
    ctx_len = f_ref.shape[0]
    xx = _dot(wl_ref[...], f_ref[...])
    o_ref[...] = _chan_dft(xx[:ctx_len], xx[ctx_len:], cc_ref[...], scale).astype(_BF)


def _fourier_ctx(f, consts, *, n_batch, ctx_len, t_lat):
    fd = f.shape[1]
    const = lambda shape: pl.BlockSpec(shape, lambda b: (0, 0))
    return pl.pallas_call(
        functools.partial(_fc_kernel, scale=float((ctx_len * GROUP_DIM) ** -0.5)),
        grid=(n_batch,),
        in_specs=[pl.BlockSpec((ctx_len, fd), lambda b: (t_lat // ctx_len + b, 0)),
                  const((2 * ctx_len, ctx_len)), const((2 * GROUP_DIM, GROUP_DIM))],
        out_specs=pl.BlockSpec((ctx_len, fd), lambda b: (b, 0)),
        out_shape=jax.ShapeDtypeStruct((n_batch * ctx_len, fd), _BF),
        compiler_params=_params("arbitrary"),
        name="fourier_ctx",
    )(f, consts["wl"], consts["cc"])


def _attn_core(sink_ref, q_ref, k, v, o_ref, ok):
    tq = q_ref.shape[0]
    lane = lax.broadcasted_iota(jnp.int32, (tq, LANES), 1)
    low = lane < HEAD_DIM
    zero = jnp.zeros((tq, LANES), _BF)
    n_heads = q_ref.shape[1] // HEAD_DIM
    for j in range(n_heads // 2):
        qs = q_ref[:, j * LANES:(j + 1) * LANES]
        halves = []
        for half in range(2):
            hq = 2 * j + half
            kvh = hq // Q_PER_KV
            sel = slice(0, LANES) if half == kvh else slice(LANES, 2 * LANES)
            qm = jnp.where(low if half == 0 else jnp.logical_not(low), qs, zero)
            s = lax.dot_general(qm, k[:, sel], (((1,), (1,)), ((), ())), preferred_element_type=_F32)
            if ok is not None:
                s = jnp.where(ok, s, NEG_INF)
            sk = sink_ref[hq]
            m = jnp.maximum(jnp.max(s, axis=-1, keepdims=True), sk)
            p = jnp.exp(s - m)
            den = jnp.sum(p, axis=-1, keepdims=True) + jnp.exp(sk - m)
            halves.append(_dot(p.astype(_BF), v[:, sel]) / den)
        o_ref[:, j * LANES:(j + 1) * LANES] = jnp.where(low, halves[0], halves[1]).astype(_BF)


def _attn_lat_kernel(sink_ref, q_ref, kp_ref, kc_ref, kn_ref, vp_ref, vc_ref, vn_ref, kx_ref, vx_ref,
                     o_ref, *, nb):
    i = pl.program_id(1)
    k = jnp.concatenate([kp_ref[...], kc_ref[...], kn_ref[...], kx_ref[...]], axis=0)
    v = jnp.concatenate([vp_ref[...], vc_ref[...], vn_ref[...], vx_ref[...]], axis=0)
    shape = (BLOCK, k.shape[0])
    r = lax.broadcasted_iota(jnp.int32, shape, 0)
    s = lax.broadcasted_iota(jnp.int32, shape, 1)
    lo_ok = jnp.where(i > 0, 0, BLOCK)
    hi_ok = jnp.where(i < nb - 1, 3 * BLOCK, 2 * BLOCK)
    ok = (jnp.abs(s - BLOCK - r) <= WINDOW) & (s >= lo_ok) & (s < hi_ok)
    ok = ok | (s >= 3 * BLOCK)
    _attn_core(sink_ref, q_ref, k, v, o_ref, ok)


def _attn_ctx_kernel(sink_ref, q_ref, kx_ref, vx_ref, o_ref):
    _attn_core(sink_ref, q_ref, kx_ref[...], vx_ref[...], o_ref, None)


def _attn_lat(sink, q, k2, v2, *, n_batch, n, ctx_len, t_lat):
    qd = q.shape[1]
    nb = n // BLOCK
    cur = lambda b, i: (b * nb + i, 0)
    prev = lambda b, i: (b * nb + jnp.maximum(i - 1, 0), 0)
    nxt = lambda b, i: (b * nb + jnp.minimum(i + 1, nb - 1), 0)
    cx = lambda b, i: (t_lat // ctx_len + b, 0)
    kb = lambda im: pl.BlockSpec((BLOCK, 2 * KV_DIM), im)
    kcx = pl.BlockSpec((ctx_len, 2 * KV_DIM), cx)
    return pl.pallas_call(
        functools.partial(_attn_lat_kernel, nb=nb),
        grid=(n_batch, nb),
        in_specs=[pl.BlockSpec(memory_space=pltpu.SMEM), pl.BlockSpec((BLOCK, qd), cur),
                  kb(prev), kb(cur), kb(nxt), kb(prev), kb(cur), kb(nxt), kcx, kcx],
        out_specs=pl.BlockSpec((BLOCK, qd), cur),
        out_shape=jax.ShapeDtypeStruct((t_lat, qd), _BF),
        compiler_params=_params("arbitrary", "arbitrary"),
        name="attn_lat",
    )(sink, q, k2, k2, k2, v2, v2, v2, k2, v2)


def _attn_ctx(sink, q, k2, v2, *, n_batch, ctx_len, t_lat):
    qd = q.shape[1]
    cx = lambda b: (t_lat // ctx_len + b, 0)
    kcx = pl.BlockSpec((ctx_len, 2 * KV_DIM), cx)
    return pl.pallas_call(
        _attn_ctx_kernel,
        grid=(n_batch,),
        in_specs=[pl.BlockSpec(memory_space=pltpu.SMEM), pl.BlockSpec((ctx_len, qd), cx), kcx, kcx],
        out_specs=pl.BlockSpec((ctx_len, qd), lambda b: (b, 0)),
        out_shape=jax.ShapeDtypeStruct((n_batch * ctx_len, qd), _BF),
        compiler_params=_params("arbitrary"),
        name="attn_ctx",
    )(sink, q, k2, v2)


def _pack_bf16_pairs(h):
    half = h.shape[1] // 2
    lo = lax.bitcast_convert_type(h[:, :half].astype(_BF).astype(_F32), jnp.uint32)
    hi = lax.bitcast_convert_type(h[:, half:].astype(_BF).astype(_F32), jnp.uint32)
    return (lo >> 16) | hi


def _unpack_bf16_pairs(p):
    lo = lax.bitcast_convert_type(p << 16, _F32)
    hi = lax.bitcast_convert_type(p & jnp.uint32(0xFFFF0000), _F32)
    return jnp.concatenate([lo, hi], axis=1).astype(_BF)


def _merge_kernel(x_ref, mod_ref, g1_ref, g2_ref, wg_ref, fml_ref, fmc_ref, atl_ref, atc_ref,
                  wfo_ref, wao_ref, wout_ref, rw_ref, rb_ref,
                  x1_ref, hp_ref, idx_ref, gate_ref, cnt_ref, *, nxt):
    i = pl.program_id(0)
    d = x_ref.shape[1]
    m = mod_ref[0]
    x = x_ref[...]
    h = _norm_mod(x, g1_ref[...], m[0:1], m[1:2]).astype(_BF)
    gts = _dot(h, wg_ref[...])
    is_lat = i < nxt
    fm = jnp.where(is_lat, fml_ref[...], fmc_ref[...])
    at = jnp.where(is_lat, atl_ref[...], atc_ref[...])
    y = (jax.nn.sigmoid(gts[:, :d]) * _dot(fm, wfo_ref[...])
         + jax.nn.sigmoid(gts[:, d:]) * _dot(at, wao_ref[...]))
    x1 = x + m[2:3] * _dot(y.astype(_BF), wout_ref[...])
    x1_ref[...] = x1
    h2 = _norm_mod(x1, g2_ref[...], m[3:4], m[4:5])
    hp_ref[...] = _pack_bf16_pairs(h2)
    logits = _dot(h2.astype(_BF), rw_ref[...]) + rb_ref[...]
    lane = lax.broadcasted_iota(jnp.int32, logits.shape, 1)
    vals, idxs = [], []
    for _ in range(TOP_K):
        mx = jnp.max(logits, axis=-1, keepdims=True)
        ix = jnp.min(jnp.where(logits == mx, lane, LANES), axis=-1, keepdims=True)
        logits = jnp.where(lane == ix, NEG_INF, logits)
        vals.append(mx)
        idxs.append(ix)
    es = [jnp.exp(v - vals[0]) for v in vals]
    den = es[0] + es[1] + es[2] + es[3]
    idx4 = jnp.zeros(logits.shape, jnp.int32)
    gate4 = jnp.zeros(logits.shape, _F32)
    sel = jnp.zeros(logits.shape, _F32)
    for k in range(TOP_K):
        idx4 = jnp.where(lane == k, idxs[k], idx4)
        gate4 = jnp.where(lane == k, es[k] / den, gate4)
        sel = sel + jnp.where(lane == idxs[k], 1.0, 0.0)
    idx_ref[...] = idx4
    gate_ref[...] = gate4

    @pl.when(i == 0)
    def _():
        cnt_ref[...] = jnp.zeros(cnt_ref.shape, _F32)

    cnt_ref[0:1, :] += jnp.sum(sel, axis=0, keepdims=True)


def _merge(xa, mod, g1, g2, w_g, fm_lat, fm_ctx, at_lat, at_ctx, w_fo, w_ao, w_out, rw, rb,
           *, n, t_lat, n_batch):
    tt, d = xa.shape
    fd = fm_lat.shape[1]
    nt = tt // TM
    nxt = t_lat // TM
    row = lambda i: (i, 0)
    lat = lambda i: (jnp.minimum(i, nxt - 1), 0)
    cxt = lambda i: (jnp.maximum(i - nxt, 0), 0)
    const = lambda a: pl.BlockSpec(a.shape, lambda i: (0, 0))
    mod_idx = lambda i: (jnp.minimum(i * TM // n, n_batch), 0, 0)
    return pl.pallas_call(
        functools.partial(_merge_kernel, nxt=nxt),
        grid=(nt,),
        in_specs=[pl.BlockSpec((TM, d), row), pl.BlockSpec((1, N_MOD, d), mod_idx),
                  const(g1), const(g2), const(w_g),
                  pl.BlockSpec((TM, fd), lat), pl.BlockSpec((TM, fd), cxt),
                  pl.BlockSpec((TM, fd), lat), pl.BlockSpec((TM, fd), cxt),
                  const(w_fo), const(w_ao), const(w_out), const(rw), const(rb)],
        out_specs=[pl.BlockSpec((TM, d), row), pl.BlockSpec((TM, d // 2), row),
                   pl.BlockSpec((TM, LANES), row), pl.BlockSpec((TM, LANES), row),
                   pl.BlockSpec((8, LANES), lambda i: (0, 0))],
        out_shape=[jax.ShapeDtypeStruct((tt, d), _F32), jax.ShapeDtypeStruct((tt, d // 2), jnp.uint32),
                   jax.ShapeDtypeStruct((tt, LANES), jnp.int32), jax.ShapeDtypeStruct((tt, LANES), _F32),
                   jax.ShapeDtypeStruct((8, LANES), _F32)],
        compiler_params=_params("arbitrary"),
        name="merge_router",
    )(xa, mod, g1, g2, w_g, fm_lat, fm_ctx, at_lat, at_ctx, w_fo, w_ao, w_out, rw, rb)


def _route_kernel(idx_ref, ps_ref, d_ref, carry):
    i = pl.program_id(0)

    @pl.when(i == 0)
    def _():
        carry[...] = ps_ref[...]

    idx4 = idx_ref[...]
    lane = lax.broadcasted_iota(jnp.int32, idx4.shape, 1)
    cols = [idx4[:, k:k + 1] for k in range(TOP_K)]
    sel = jnp.zeros(idx4.shape, _F32)
    for k in range(TOP_K):
        sel = sel + jnp.where(lane == cols[k], 1.0, 0.0)
    r = lax.broadcasted_iota(jnp.int32, (TM, TM), 0)
    c = lax.broadcasted_iota(jnp.int32, (TM, TM), 1)
    tri = jnp.where(r > c, 1.0, 0.0).astype(_BF)
    slot = carry[...] + _dot(tri, sel.astype(_BF))
    dest4 = jnp.zeros(idx4.shape, jnp.int32)
    for k in range(TOP_K):
        dk = jnp.sum(jnp.where(lane == cols[k], slot, 0.0), axis=-1, keepdims=True)
        dest4 = jnp.where(lane == k, dk.astype(jnp.int32), dest4)
    d_ref[...] = dest4
    carry[...] += jnp.sum(sel, axis=0, keepdims=True)


def _route(idx4, pad_start):
    tt = idx4.shape[0]
    return pl.pallas_call(
        _route_kernel,
        grid=(tt // TM,),
        in_specs=[pl.BlockSpec((TM, LANES), lambda i: (i, 0)), pl.BlockSpec((1, LANES), lambda i: (0, 0))],
        out_specs=pl.BlockSpec((TM, LANES), lambda i: (i, 0)),
        out_shape=jax.ShapeDtypeStruct((tt, LANES), jnp.int32),
        scratch_shapes=[pltpu.VMEM((1, LANES), _F32)],
        compiler_params=_params("arbitrary"),
        name="route_slots",
    )(idx4, pad_start)


def _dispatch_kernel(dest_ref, src_ref, xs_in_ref, xs_ref, sem):
    del xs_in_ref
    i = pl.program_id(0)

    def body(r, carry):
        t = i * TM + r
        for k in range(TOP_K):
            pltpu.make_async_copy(src_ref.at[pl.ds(t, 1)], xs_ref.at[pl.ds(dest_ref[TOP_K * r + k], 1)],
                                  sem).start()
        return carry

    lax.fori_loop(0, TM, body, 0)
    for _ in range(TOP_K):
        pltpu.make_async_copy(src_ref.at[pl.ds(0, TM)], xs_ref.at[pl.ds(0, TM)], sem).wait()


def _dispatch(dest_flat, hp, n_slots):
    tt, w = hp.shape
    xs0 = jnp.zeros((n_slots, w), hp.dtype)
    return pl.pallas_call(
        _dispatch_kernel,
        grid=(tt // TM,),
        in_specs=[pl.BlockSpec((TOP_K * TM,), lambda i: (i,), memory_space=pltpu.SMEM),
                  pl.BlockSpec(memory_space=pl.ANY), pl.BlockSpec(memory_space=pl.ANY)],
        out_specs=pl.BlockSpec(memory_space=pl.ANY),
        out_shape=jax.ShapeDtypeStruct((n_slots, w), hp.dtype),
        scratch_shapes=[pltpu.SemaphoreType.DMA],
        input_output_aliases={2: 0},
        compiler_params=_params("arbitrary"),
        name="moe_dispatch",
    )(dest_flat, hp, xs0)


def _moe_kernel(be_ref, nv_ref, xs_ref, wgu_ref, bgu_ref, wd_ref, bd_ref, y_ref):
    del be_ref
    i = pl.program_id(0)
    de = wd_ref.shape[1]

    @pl.when(i < nv_ref[0])
    def _():
        xb = _unpack_bf16_pairs(xs_ref[...])
        gu = _dot(xb, wgu_ref[0]) + bgu_ref[0]
        a = jnp.minimum(gu[:, :de], SWIGLU_LIMIT)
        u = jnp.clip(gu[:, de:], -SWIGLU_LIMIT, SWIGLU_LIMIT)
        act = a * jax.nn.sigmoid(SWIGLU_ALPHA * a) * (u + 1)
        y_ref[...] = _dot(act.astype(_BF), wd_ref[0]) + bd_ref[0]

    @pl.when(i >= nv_ref[0])
    def _():
        y_ref[...] = jnp.zeros(y_ref.shape, _F32)


def _moe(block_e, n_valid, xs, w_gu, b_gu, w_down, b_down):
    n_slots, w = xs.shape
    ne, d, de2 = w_gu.shape
    de = de2 // 2
    n_blocks = n_slots // TME
    ex = lambda i, be, nv: (be[i], 0, 0)
    grid_spec = pltpu.PrefetchScalarGridSpec(
        num_scalar_prefetch=2,
        grid=(n_blocks,),
        in_specs=[pl.BlockSpec((TME, w), lambda i, be, nv: (i, 0)),
                  pl.BlockSpec((1, d, de2), ex), pl.BlockSpec((1, 1, de2), ex),
                  pl.BlockSpec((1, de, d), ex), pl.BlockSpec((1, 1, d), ex)],
        out_specs=pl.BlockSpec((TME, d), lambda i, be, nv: (i, 0)),
    )
    return pl.pallas_call(
        _moe_kernel,
        grid_spec=grid_spec,
        out_shape=jax.ShapeDtypeStruct((n_slots, d), _F32),
        compiler_params=_params("arbitrary"),
        name="moe_experts",
    )(block_e, n_valid, xs, w_gu, b_gu.reshape(ne, 1, de2), w_down, b_down.reshape(ne, 1, d))


def _combine_kernel(dest_ref, y_ref, gate_ref, x_ref, mod_ref, fg_ref, o_ref, buf, sem, *, final):
    def body(r, carry):
        for k in range(TOP_K):
            pltpu.make_async_copy(y_ref.at[pl.ds(dest_ref[TOP_K * r + k], 1)], buf.at[k, pl.ds(r, 1)],
                                  sem).start()
        return carry

    lax.fori_loop(0, TMC, body, 0)
    for k in range(TOP_K):
        pltpu.make_async_copy(y_ref.at[pl.ds(0, TMC)], buf.at[k], sem).wait()
    g = gate_ref[...]
    moe = g[:, 0:1] * buf[0]
    for k in range(1, TOP_K):
        moe = moe + g[:, k:k + 1] * buf[k]
    x2 = x_ref[...] + mod_ref[0][5:6] * moe
    if final:
        ms = jnp.mean(x2 * x2, axis=-1, keepdims=True)
        x2 = x2 * lax.rsqrt(ms + EPS) * fg_ref[...]
    o_ref[...] = x2


def _combine(dest_flat, y, gate4, x1, mod, fg, *, n, n_batch, rows, final):
    d = x1.shape[1]
    row = lambda i: (i, 0)
    mod_idx = lambda i: (jnp.minimum(i * TMC // n, n_batch), 0, 0)
    return pl.pallas_call(
        functools.partial(_combine_kernel, final=final),
        grid=(rows // TMC,),
        in_specs=[pl.BlockSpec((TOP_K * TMC,), lambda i: (i,), memory_space=pltpu.SMEM),
                  pl.BlockSpec(memory_space=pl.ANY),
                  pl.BlockSpec((TMC, LANES), row), pl.BlockSpec((TMC, d), row),
                  pl.BlockSpec((1, N_MOD, d), mod_idx), pl.BlockSpec((1, d), lambda i: (0, 0))],
        out_specs=pl.BlockSpec((TMC, d), row),
        out_shape=jax.ShapeDtypeStruct((rows, d), _F32),
        scratch_shapes=[pltpu.VMEM((TOP_K, TMC, d), _F32), pltpu.SemaphoreType.DMA],
        compiler_params=_params("arbitrary"),
        name="moe_combine",
    )(dest_flat, y, gate4, x1, mod, fg)


def _routing_tables(counts, n_blocks):
    cnt = counts[0, :N_EXPERTS].astype(jnp.int32)
    padded = (cnt + TME - 1) // TME * TME
    pad_end = jnp.cumsum(padded)
    pad_start = pad_end - padded
    ps = jnp.zeros((1, LANES), _F32).at[0, :N_EXPERTS].set(pad_start.astype(_F32))
    blk = jnp.arange(n_blocks, dtype=jnp.int32) * TME
    block_e = jnp.sum((pad_end[None, :] <= blk[:, None]).astype(jnp.int32), axis=1)
    e_last = jnp.max(jnp.where(cnt > 0, jnp.arange(N_EXPERTS, dtype=jnp.int32), 0))
    block_e = jnp.minimum(block_e, e_last).astype(jnp.int32)
    n_valid = (pad_end[-1] // TME).astype(jnp.int32).reshape(1)
    return ps, block_e, n_valid


def kernel(x, c, ctx, c_ctx, ada_w, ada_b, norm1_g, norm2_g, w_in, attn_sink, w_fourier_out, w_attn_out,
           w_out, router_w, router_b, expert_w_gu, expert_b_gu, expert_w_down, expert_b_down, final_norm_g):
    n_batch, n, d = x.shape
    ctx_len = ctx.shape[1]
    depth = ada_w.shape[0]
    t_lat = n_batch * n
    t_ctx = n_batch * ctx_len
    tt = t_lat + t_ctx
    fd = qd = d // 2
    assert n % TM == 0 and t_ctx % TM == 0 and n % (BLOCK * FB_NB) == 0 and n_batch < MOD_ROWS
    assert t_lat % ctx_len == 0 and TM % ctx_len == 0

    xa = jnp.concatenate([x.reshape(t_lat, d), ctx.reshape(t_ctx, d)], axis=0)
    cc = jnp.concatenate([c, c_ctx[None, :], jnp.zeros((MOD_ROWS - n_batch - 1, d), _F32)], axis=0)
    mod = _ada(cc, ada_w, ada_b).reshape(depth, MOD_ROWS, N_MOD, d)
    cos_t, sin_t = _rope_tables(n)
    consts = _dft_consts(n, ctx_len)

    n_in = fd + qd + 2 * KV_DIM
    w_a = w_in[:, :, :n_in].astype(_BF)
    w_g = w_in[:, :, n_in:].astype(_BF)
    w_fo, w_ao, w_o = w_fourier_out.astype(_BF), w_attn_out.astype(_BF), w_out.astype(_BF)
    rw = jnp.pad(router_w, ((0, 0), (0, 0), (0, LANES - N_EXPERTS))).astype(_BF)
    rb = jnp.pad(router_b, ((0, 0), (0, LANES - N_EXPERTS)), constant_values=NEG_INF)[:, None, :]
    w_gu, w_dn = expert_w_gu.astype(_BF), expert_w_down.astype(_BF)

    n_blocks = -(-(tt * TOP_K + N_EXPERTS * (TME - 1)) // TME)
    dims = dict(n=n, t_lat=t_lat, n_batch=n_batch)
    for l in range(depth):
        last = l == depth - 1
        f, q, k2, v2 = _proj(xa, mod[l], norm1_g[l][None, :], w_a[l], cos_t, sin_t, **dims)
        fm_lat = _fourier_lat(f, consts, n_batch=n_batch, n=n)
        fm_ctx = _fourier_ctx(f, consts, n_batch=n_batch, ctx_len=ctx_len, t_lat=t_lat)
        at_lat = _attn_lat(attn_sink[l], q, k2, v2, n_batch=n_batch, n=n, ctx_len=ctx_len, t_lat=t_lat)
        at_ctx = _attn_ctx(attn_sink[l], q, k2, v2, n_batch=n_batch, ctx_len=ctx_len, t_lat=t_lat)
        x1, hp, idx4, gate4, counts = _merge(
            xa, mod[l], norm1_g[l][None, :], norm2_g[l][None, :], w_g[l], fm_lat, fm_ctx, at_lat, at_ctx,
            w_fo[l], w_ao[l], w_o[l], rw[l], rb[l], **dims)
        pad_start, block_e, n_valid = _routing_tables(counts, n_blocks)
        dest_flat = _route(idx4, pad_start)[:, :TOP_K].reshape(-1)
        xs = _dispatch(dest_flat, hp, n_blocks * TME)
        y = _moe(block_e, n_valid, xs, w_gu[l], expert_b_gu[l], w_dn[l], expert_b_down[l])
        xa = _combine(dest_flat, y, gate4, x1, mod[l], final_norm_g[None, :], n=n, n_batch=n_batch,
                      rows=t_lat if last else tt, final=last)
    return xa.reshape(n_batch, n, d)
```

```python
import functools

import numpy as np
import jax
import jax.numpy as jnp
from jax import lax
from jax.experimental import pallas as pl
from jax.experimental.pallas import tpu as pltpu

GRID_W = 64
HEAD_DIM = 64
N_KV_HEADS = 2
Q_PER_KV = 4
KV_DIM = N_KV_HEADS * HEAD_DIM
WINDOW = 128
BLOCK = 128
ROPE_THETA = 10000.0
ROPE_FREQS = HEAD_DIM // 4
GROUP_DIM = 128
N_EXPERTS = 32
TOP_K = 4
SWIGLU_LIMIT = 7.0
SWIGLU_ALPHA = 1.702
N_MOD = 6
EPS = 1e-5
NEG_INF = -1e30

LANES = 128
MOD_ROWS = 8
TM = 512
TME = 256
TMC = 256
FA_NB = 16
FB_NB = 16
CAST_ROWS = 64
VMEM_LIMIT = 56 * 1024 * 1024

_BF = jnp.bfloat16
_F32 = jnp.float32


def _params(*sem):
    return pltpu.CompilerParams(dimension_semantics=sem, vmem_limit_bytes=VMEM_LIMIT)


def _dot(a, b):
    return jnp.dot(a, b, preferred_element_type=_F32)


def _norm_mod(x, g, shift, scale):
    ms = jnp.mean(x * x, axis=-1, keepdims=True)
    return (x * lax.rsqrt(ms + EPS) * g) * (1 + scale) + shift


def _ada_kernel(c_ref, w_ref, b_ref, o_ref):
    c = c_ref[...]
    s = c * jax.nn.sigmoid(c)
    o_ref[0] = jnp.dot(s, w_ref[0], precision=lax.Precision.HIGHEST,
                       preferred_element_type=_F32) + b_ref[0]


def _ada(cc, ada_w, ada_b):
    depth, d, nd = ada_w.shape
    tn = nd // 4
    return pl.pallas_call(
        _ada_kernel,
        grid=(depth, nd // tn),
        in_specs=[pl.BlockSpec((MOD_ROWS, d), lambda l, j: (0, 0)),
                  pl.BlockSpec((1, d, tn), lambda l, j: (l, 0, j)),
                  pl.BlockSpec((1, 1, tn), lambda l, j: (l, 0, j))],
        out_specs=pl.BlockSpec((1, MOD_ROWS, tn), lambda l, j: (l, 0, j)),
        out_shape=jax.ShapeDtypeStruct((depth, MOD_ROWS, nd), _F32),
        compiler_params=_params("arbitrary", "arbitrary"),
        name="ada_mod",
    )(cc, ada_w, ada_b.reshape(depth, 1, nd))


def _rope(v, cos, sin):
    lane = lax.broadcasted_iota(jnp.int32, v.shape, 1)
    partner = jnp.where((lane & ROPE_FREQS) == 0,
                        pltpu.roll(v, LANES - ROPE_FREQS, 1), pltpu.roll(v, ROPE_FREQS, 1))
    return v * cos + partner * sin


def _proj_kernel(x_ref, mod_ref, g_ref, w_ref, cos_ref, sin_ref, f_ref, q_ref, k_ref, v_ref, *, fd, qd):
    m = mod_ref[0]
    h = _norm_mod(x_ref[...], g_ref[...], m[0:1], m[1:2]).astype(_BF)
    p = _dot(h, w_ref[...])
    cos = cos_ref[...]
    sin = sin_ref[...]
    f_ref[...] = p[:, :fd].astype(_BF)
    for j in range(qd // LANES):
        lo = fd + j * LANES
        q_ref[:, j * LANES:(j + 1) * LANES] = (
            _rope(p[:, lo:lo + LANES], cos, sin) * (HEAD_DIM ** -0.5)).astype(_BF)
    k = _rope(p[:, fd + qd:fd + qd + KV_DIM], cos, sin)
    v = p[:, fd + qd + KV_DIM:fd + qd + 2 * KV_DIM]
    k_ref[:, :KV_DIM] = k.astype(_BF)
    k_ref[:, KV_DIM:] = pltpu.roll(k, HEAD_DIM, 1).astype(_BF)
    v_ref[:, :KV_DIM] = v.astype(_BF)
    v_ref[:, KV_DIM:] = pltpu.roll(v, HEAD_DIM, 1).astype(_BF)


def _proj(xa, mod, g, w_a, cos_t, sin_t, *, n, t_lat, n_batch):
    tt, d = xa.shape
    fd = qd = d // 2
    nt = tt // TM
    nxt = t_lat // TM
    per_seq = n // TM
    mod_idx = lambda i: (jnp.minimum(i * TM // n, n_batch), 0, 0)
    rope_idx = lambda i: (jnp.where(i < nxt, i % per_seq, per_seq), 0)
    row = lambda i: (i, 0)
    return pl.pallas_call(
        functools.partial(_proj_kernel, fd=fd, qd=qd),
        grid=(nt,),
        in_specs=[pl.BlockSpec((TM, d), row),
                  pl.BlockSpec((1, N_MOD, d), mod_idx),
                  pl.BlockSpec((1, d), lambda i: (0, 0)),
                  pl.BlockSpec(w_a.shape, lambda i: (0, 0)),
                  pl.BlockSpec((TM, LANES), rope_idx),
                  pl.BlockSpec((TM, LANES), rope_idx)],
        out_specs=[pl.BlockSpec((TM, fd), row), pl.BlockSpec((TM, qd), row),
                   pl.BlockSpec((TM, 2 * KV_DIM), row), pl.BlockSpec((TM, 2 * KV_DIM), row)],
        out_shape=[jax.ShapeDtypeStruct((tt, fd), _BF), jax.ShapeDtypeStruct((tt, qd), _BF),
                   jax.ShapeDtypeStruct((tt, 2 * KV_DIM), _BF), jax.ShapeDtypeStruct((tt, 2 * KV_DIM), _BF)],
        compiler_params=_params("arbitrary"),
        name="proj_in",
    )(xa, mod, g, w_a, cos_t, sin_t)


def _rope_tables(n):
    pos = jnp.arange(n)
    inv = ROPE_THETA ** (-jnp.arange(ROPE_FREQS, dtype=_F32) / ROPE_FREQS)
    ar = (pos // GRID_W).astype(_F32)[:, None] * inv
    ac = (pos % GRID_W).astype(_F32)[:, None] * inv
    cos = jnp.concatenate([jnp.cos(ar), jnp.cos(ar), jnp.cos(ac), jnp.cos(ac)], axis=1)
    sin = jnp.concatenate([-jnp.sin(ar), jnp.sin(ar), -jnp.sin(ac), jnp.sin(ac)], axis=1)
    cos = jnp.tile(cos, (1, LANES // HEAD_DIM))
    sin = jnp.tile(sin, (1, LANES // HEAD_DIM))
    cos = jnp.concatenate([cos, jnp.ones((TM, LANES), _F32)], axis=0)
    sin = jnp.concatenate([sin, jnp.zeros((TM, LANES), _F32)], axis=0)
    return cos, sin


def _dft_consts(n, ctx_len):
    n1 = n // BLOCK
    b = np.arange(BLOCK, dtype=np.int64)[:, None, None]
    k1 = np.arange(n1, dtype=np.int64)[None, :, None]
    a = np.arange(n1, dtype=np.int64)[None, None, :]
    ang = 2.0 * np.pi * ((a * k1 * BLOCK + b * k1) % n).astype(np.float64) / n
    m_r, m_i = np.cos(ang), -np.sin(ang)
    kk = np.arange(GROUP_DIM, dtype=np.int64)
    ang_c = 2.0 * np.pi * ((kk[:, None] * kk[None, :]) % GROUP_DIM) / GROUP_DIM
    c, s = np.cos(ang_c), np.sin(ang_c)
    cs1 = np.concatenate([c, s], axis=1)
    cs2 = np.concatenate([-s, c], axis=1)
    cc = np.concatenate([c, s], axis=0)
    kl = np.arange(ctx_len, dtype=np.int64)
    ang_l = 2.0 * np.pi * ((kl[:, None] * kl[None, :]) % ctx_len) / ctx_len
    wl = np.concatenate([np.cos(ang_l), -np.sin(ang_l)], axis=0)
    as_bf = lambda v: jnp.asarray(v, dtype=_F32).astype(_BF)
    return dict(m_r=as_bf(m_r), m_i=as_bf(m_i), cs1=as_bf(cs1), cs2=as_bf(cs2), cc=as_bf(cc), wl=as_bf(wl))


def _chan_dft(xr, xi, cc, scale):
    outs = []
    for g in range(xr.shape[1] // GROUP_DIM):
        sl = slice(g * GROUP_DIM, (g + 1) * GROUP_DIM)
        xx = jnp.concatenate([xr[:, sl], xi[:, sl]], axis=1).astype(_BF)
        outs.append(_dot(xx, cc) * scale)
    return jnp.concatenate(outs, axis=1)


def _fa_kernel(f_ref, mr_ref, mi_ref, zr_ref, zi_ref):
    for t in range(FA_NB):
        xb = f_ref[:, t, :]
        zr_ref[:, t, :] = _dot(mr_ref[t], xb).astype(_BF)
        zi_ref[:, t, :] = _dot(mi_ref[t], xb).astype(_BF)


def _fb_kernel(zr_ref, zi_ref, cs1_ref, cs2_ref, cc_ref, o_ref, *, scale):
    cs1, cs2, cc = cs1_ref[...], cs2_ref[...], cc_ref[...]
    for j in range(FB_NB):
        z = jnp.concatenate([zr_ref[j], zi_ref[j]], axis=0)
        o_ref[:, j, :] = _chan_dft(_dot(cs1, z), _dot(cs2, z), cc, scale).astype(_BF)


def _fourier_lat(f, consts, *, n_batch, n):
    tt, fd = f.shape
    n1 = n // BLOCK
    f3 = f.reshape(tt // BLOCK, BLOCK, fd)
    blk_a = pl.BlockSpec((n1, FA_NB, fd), lambda b, j: (b, j, 0))
    mat_a = pl.BlockSpec((FA_NB, n1, n1), lambda b, j: (j, 0, 0))
    z_shape = jax.ShapeDtypeStruct((n_batch * n1, BLOCK, fd), _BF)
    zr, zi = pl.pallas_call(
        _fa_kernel,
        grid=(n_batch, BLOCK // FA_NB),
        in_specs=[blk_a, mat_a, mat_a],
        out_specs=[blk_a, blk_a],
        out_shape=[z_shape, z_shape],
        compiler_params=_params("arbitrary", "arbitrary"),
        name="fourier_seq_a",
    )(f3, consts["m_r"], consts["m_i"])
    blk_z = pl.BlockSpec((FB_NB, BLOCK, fd), lambda b, j: (b * (n1 // FB_NB) + j, 0, 0))
    const = lambda shape: pl.BlockSpec(shape, lambda b, j: (0, 0))
    out = pl.pallas_call(
        functools.partial(_fb_kernel, scale=float((n * GROUP_DIM) ** -0.5)),
        grid=(n_batch, n1 // FB_NB),
        in_specs=[blk_z, blk_z, const((BLOCK, 2 * BLOCK)), const((BLOCK, 2 * BLOCK)),
                  const((2 * GROUP_DIM, GROUP_DIM))],
        out_specs=pl.BlockSpec((BLOCK, FB_NB, fd), lambda b, j: (b, j, 0)),
        out_shape=jax.ShapeDtypeStruct((n_batch * BLOCK, n1, fd), _BF),
        compiler_params=_params("arbitrary", "arbitrary"),
        name="fourier_seq_b",
    )(zr, zi, consts["cs1"], consts["cs2"], consts["cc"])
    return out.reshape(n_batch * n, fd)


def _fc_kernel(f_ref, wl_ref, cc_ref, o_ref, *, scale):
    ctx_len = f_ref.shape[0]
    xx = _dot(wl_ref[...], f_ref[...])
    o_ref[...] = _chan_dft(xx[:ctx_len], xx[ctx_len:], cc_ref[...], scale).astype(_BF)


def _fourier_ctx(f, consts, *, n_batch, ctx_len, t_lat):
    fd = f.shape[1]
    const = lambda shape: pl.BlockSpec(shape, lambda b: (0, 0))
    return pl.pallas_call(
        functools.partial(_fc_kernel, scale=float((ctx_len * GROUP_DIM) ** -0.5)),
        grid=(n_batch,),
        in_specs=[pl.BlockSpec((ctx_len, fd), lambda b: (t_lat // ctx_len + b, 0)),
                  const((2 * ctx_len, ctx_len)), const((2 * GROUP_DIM, GROUP_DIM))],
        out_specs=pl.BlockSpec((ctx_len, fd), lambda b: (b, 0)),
        out_shape=jax.ShapeDtypeStruct((n_batch * ctx_len, fd), _BF),
        compiler_params=_params("arbitrary"),
        name="fourier_ctx",
    )(f, consts["wl"], consts["cc"])


def _attn_core(sink_ref, q_ref, k, v, o_ref, ok):
    tq = q_ref.shape[0]
    lane = lax.broadcasted_iota(jnp.int32, (tq, LANES), 1)
    low = lane < HEAD_DIM
    zero = jnp.zeros((tq, LANES), _BF)
    n_heads = q_ref.shape[1] // HEAD_DIM
    for j in range(n_heads // 2):
        qs = q_ref[:, j * LANES:(j + 1) * LANES]
        halves = []
        for half in range(2):
            hq = 2 * j + half
            kvh = hq // Q_PER_KV
            sel = slice(0, LANES) if half == kvh else slice(LANES, 2 * LANES)
            qm = jnp.where(low if half == 0 else jnp.logical_not(low), qs, zero)
            s = lax.dot_general(qm, k[:, sel], (((1,), (1,)), ((), ())), preferred_element_type=_F32)
            if ok is not None:
                s = jnp.where(ok, s, NEG_INF)
            sk = sink_ref[hq]
            m = jnp.maximum(jnp.max(s, axis=-1, keepdims=True), sk)
            p = jnp.exp(s - m)
            den = jnp.sum(p, axis=-1, keepdims=True) + jnp.exp(sk - m)
            halves.append(_dot(p.astype(_BF), v[:, sel]) / den)
        o_ref[:, j * LANES:(j + 1) * LANES] = jnp.where(low, halves[0], halves[1]).astype(_BF)


def _attn_lat_kernel(sink_ref, q_ref, kp_ref, kc_ref, kn_ref, vp_ref, vc_ref, vn_ref, kx_ref, vx_ref,
                     o_ref, *, nb):
    i = pl.program_id(1)
    k = jnp.concatenate([kp_ref[...], kc_ref[...], kn_ref[...], kx_ref[...]], axis=0)
    v = jnp.concatenate([vp_ref[...], vc_ref[...], vn_ref[...], vx_ref[...]], axis=0)
    shape = (BLOCK, k.shape[0])
    r = lax.broadcasted_iota(jnp.int32, shape, 0)
    s = lax.broadcasted_iota(jnp.int32, shape, 1)
    lo_ok = jnp.where(i > 0, 0, BLOCK)
    hi_ok = jnp.where(i < nb - 1, 3 * BLOCK, 2 * BLOCK)
    ok = (jnp.abs(s - BLOCK - r) <= WINDOW) & (s >= lo_ok) & (s < hi_ok)
    ok = ok | (s >= 3 * BLOCK)
    _attn_core(sink_ref, q_ref, k, v, o_ref, ok)


def _attn_ctx_kernel(sink_ref, q_ref, kx_ref, vx_ref, o_ref):
    _attn_core(sink_ref, q_ref, kx_ref[...], vx_ref[...], o_ref, None)


def _attn_lat(sink, q, k2, v2, *, n_batch, n, ctx_len, t_lat):
    qd = q.shape[1]
    nb = n // BLOCK
    cur = lambda b, i: (b * nb + i, 0)
    prev = lambda b, i: (b * nb + jnp.maximum(i - 1, 0), 0)
    nxt = lambda b, i: (b * nb + jnp.minimum(i + 1, nb - 1), 0)
    cx = lambda b, i: (t_lat // ctx_len + b, 0)
    kb = lambda im: pl.BlockSpec((BLOCK, 2 * KV_DIM), im)
    kcx = pl.BlockSpec((ctx_len, 2 * KV_DIM), cx)
    return pl.pallas_call(
        functools.partial(_attn_lat_kernel, nb=nb),
        grid=(n_batch, nb),
        in_specs=[pl.BlockSpec(memory_space=pltpu.SMEM), pl.BlockSpec((BLOCK, qd), cur),
                  kb(prev), kb(cur), kb(nxt), kb(prev), kb(cur), kb(nxt), kcx, kcx],
        out_specs=pl.BlockSpec((BLOCK, qd), cur),
        out_shape=jax.ShapeDtypeStruct((t_lat, qd), _BF),
        compiler_params=_params("arbitrary", "arbitrary"),
        name="attn_lat",
    )(sink, q, k2, k2, k2, v2, v2, v2, k2, v2)


def _attn_ctx(sink, q, k2, v2, *, n_batch, ctx_len, t_lat):
    qd = q.shape[1]
    cx = lambda b: (t_lat // ctx_len + b, 0)
    kcx = pl.BlockSpec((ctx_len, 2 * KV_DIM), cx)
    return pl.pallas_call(
        _attn_ctx_kernel,
        grid=(n_batch,),
        in_specs=[pl.BlockSpec(memory_space=pltpu.SMEM), pl.BlockSpec((ctx_len, qd), cx), kcx, kcx],
        out_specs=pl.BlockSpec((ctx_len, qd), lambda b: (b, 0)),
        out_shape=jax.ShapeDtypeStruct((n_batch * ctx_len, qd), _BF),
        compiler_params=_params("arbitrary"),
        name="attn_ctx",
    )(sink, q, k2, v2)


def _pack_bf16_pairs(h):
    half = h.shape[1] // 2
    lo = lax.bitcast_convert_type(h[:, :half].astype(_BF).astype(_F32), jnp.uint32)
    hi = lax.bitcast_convert_type(h[:, half:].astype(_BF).astype(_F32), jnp.uint32)
    return (lo >> 16) | hi


def _unpack_bf16_pairs(p):
    lo = lax.bitcast_convert_type(p << 16, _F32)
    hi = lax.bitcast_convert_type(p & jnp.uint32(0xFFFF0000), _F32)
    return jnp.concatenate([lo, hi], axis=1).astype(_BF)


def _merge_kernel(x_ref, mod_ref, g1_ref, g2_ref, wg_ref, fml_ref, fmc_ref, atl_ref, atc_ref,
                  wfo_ref, wao_ref, wout_ref, rw_ref, rb_ref,
                  x1_ref, hp_ref, idx_ref, gate_ref, cnt_ref, *, nxt):
    i = pl.program_id(0)
    d = x_ref.shape[1]
    m = mod_ref[0]
    x = x_ref[...]
    h = _norm_mod(x, g1_ref[...], m[0:1], m[1:2]).astype(_BF)
    gts = _dot(h, wg_ref[...])
    is_lat = i < nxt
    fm = jnp.where(is_lat, fml_ref[...], fmc_ref[...])
    at = jnp.where(is_lat, atl_ref[...], atc_ref[...])
    y = (jax.nn.sigmoid(gts[:, :d]) * _dot(fm, wfo_ref[...])
         + jax.nn.sigmoid(gts[:, d:]) * _dot(at, wao_ref[...]))
    x1 = x + m[2:3] * _dot(y.astype(_BF), wout_ref[...])
    x1_ref[...] = x1
    h2 = _norm_mod(x1, g2_ref[...], m[3:4], m[4:5])
    hp_ref[...] = _pack_bf16_pairs(h2)
    logits = _dot(h2.astype(_BF), rw_ref[...]) + rb_ref[...]
    lane = lax.broadcasted_iota(jnp.int32, logits.shape, 1)
    vals, idxs = [], []
    for _ in range(TOP_K):
        mx = jnp.max(logits, axis=-1, keepdims=True)
        ix = jnp.min(jnp.where(logits == mx, lane, LANES), axis=-1, keepdims=True)
        logits = jnp.where(lane == ix, NEG_INF, logits)
        vals.append(mx)
        idxs.append(ix)
    es = [jnp.exp(v - vals[0]) for v in vals]
    den = es[0] + es[1] + es[2] + es[3]
    idx4 = jnp.zeros(logits.shape, jnp.int32)
    gate4 = jnp.zeros(logits.shape, _F32)
    sel = jnp.zeros(logits.shape, _F32)
    for k in range(TOP_K):
        idx4 = jnp.where(lane == k, idxs[k], idx4)
        gate4 = jnp.where(lane == k, es[k] / den, gate4)
        sel = sel + jnp.where(lane == idxs[k], 1.0, 0.0)
    idx_ref[...] = idx4
    gate_ref[...] = gate4

    @pl.when(i == 0)
    def _():
        cnt_ref[...] = jnp.zeros(cnt_ref.shape, _F32)

    cnt_ref[0:1, :] += jnp.sum(sel, axis=0, keepdims=True)


def _merge(xa, mod, g1, g2, w_g, fm_lat, fm_ctx, at_lat, at_ctx, w_fo, w_ao, w_out, rw, rb,
           *, n, t_lat, n_batch):
    tt, d = xa.shape
    fd = fm_lat.shape[1]
    nt = tt // TM
    nxt = t_lat // TM
    row = lambda i: (i, 0)
    lat = lambda i: (jnp.minimum(i, nxt - 1), 0)
    cxt = lambda i: (jnp.maximum(i - nxt, 0), 0)
    const = lambda a: pl.BlockSpec(a.shape, lambda i: (0, 0))
    mod_idx = lambda i: (jnp.minimum(i * TM // n, n_batch), 0, 0)
    return pl.pallas_call(
        functools.partial(_merge_kernel, nxt=nxt),
        grid=(nt,),
        in_specs=[pl.BlockSpec((TM, d), row), pl.BlockSpec((1, N_MOD, d), mod_idx),
                  const(g1), const(g2), const(w_g),
                  pl.BlockSpec((TM, fd), lat), pl.BlockSpec((TM, fd), cxt),
                  pl.BlockSpec((TM, fd), lat), pl.BlockSpec((TM, fd), cxt),
                  const(w_fo), const(w_ao), const(w_out), const(rw), const(rb)],
        out_specs=[pl.BlockSpec((TM, d), row), pl.BlockSpec((TM, d // 2), row),
                   pl.BlockSpec((TM, LANES), row), pl.BlockSpec((TM, LANES), row),
                   pl.BlockSpec((8, LANES), lambda i: (0, 0))],
        out_shape=[jax.ShapeDtypeStruct((tt, d), _F32), jax.ShapeDtypeStruct((tt, d // 2), jnp.uint32),
                   jax.ShapeDtypeStruct((tt, LANES), jnp.int32), jax.ShapeDtypeStruct((tt, LANES), _F32),
                   jax.ShapeDtypeStruct((8, LANES), _F32)],
        compiler_params=_params("arbitrary"),
        name="merge_router",
    )(xa, mod, g1, g2, w_g, fm_lat, fm_ctx, at_lat, at_ctx, w_fo, w_ao, w_out, rw, rb)


def _route_kernel(idx_ref, ps_ref, d_ref, carry):
    i = pl.program_id(0)

    @pl.when(i == 0)
    def _():
        carry[...] = ps_ref[...]

    idx4 = idx_ref[...]
    lane = lax.broadcasted_iota(jnp.int32, idx4.shape, 1)
    cols = [idx4[:, k:k + 1] for k in range(TOP_K)]
    sel = jnp.zeros(idx4.shape, _F32)
    for k in range(TOP_K):
        sel = sel + jnp.where(lane == cols[k], 1.0, 0.0)
    r = lax.broadcasted_iota(jnp.int32, (TM, TM), 0)
    c = lax.broadcasted_iota(jnp.int32, (TM, TM), 1)
    tri = jnp.where(r > c, 1.0, 0.0).astype(_BF)
    slot = carry[...] + _dot(tri, sel.astype(_BF))
    dest4 = jnp.zeros(idx4.shape, jnp.int32)
    for k in range(TOP_K):
        dk = jnp.sum(jnp.where(lane == cols[k], slot, 0.0), axis=-1, keepdims=True)
        dest4 = jnp.where(lane == k, dk.astype(jnp.int32), dest4)
    d_ref[...] = dest4
    carry[...] += jnp.sum(sel, axis=0, keepdims=True)


def _route(idx4, pad_start):
    tt = idx4.shape[0]
    return pl.pallas_call(
        _route_kernel,
        grid=(tt // TM,),
        in_specs=[pl.BlockSpec((TM, LANES), lambda i: (i, 0)), pl.BlockSpec((1, LANES), lambda i: (0, 0))],
        out_specs=pl.BlockSpec((TM, LANES), lambda i: (i, 0)),
        out_shape=jax.ShapeDtypeStruct((tt, LANES), jnp.int32),
        scratch_shapes=[pltpu.VMEM((1, LANES), _F32)],
        compiler_params=_params("arbitrary"),
        name="route_slots",
    )(idx4, pad_start)


def _dispatch_kernel(dest_ref, src_ref, xs_in_ref, xs_ref, sem):
    del xs_in_ref

    def body(r, carry):
        for k in range(TOP_K):
            pltpu.make_async_copy(src_ref.at[pl.ds(r, 1)], xs_ref.at[pl.ds(dest_ref[TOP_K * r + k], 1)],
                                  sem).start()
        return carry

    lax.fori_loop(0, TM, body, 0)
    for _ in range(TOP_K):
        pltpu.make_async_copy(src_ref, xs_ref.at[pl.ds(0, TM)], sem).wait()


def _dispatch(dest_flat, hp, n_slots):
    tt, w = hp.shape
    xs0 = jnp.zeros((n_slots, w), hp.dtype)
    return pl.pallas_call(
        _dispatch_kernel,
        grid=(tt // TM,),
        in_specs=[pl.BlockSpec((TOP_K * TM,), lambda i: (i,), memory_space=pltpu.SMEM),
                  pl.BlockSpec((TM, w), lambda i: (i, 0)), pl.BlockSpec(memory_space=pl.ANY)],
        out_specs=pl.BlockSpec(memory_space=pl.ANY),
        out_shape=jax.ShapeDtypeStruct((n_slots, w), hp.dtype),
        scratch_shapes=[pltpu.SemaphoreType.DMA],
        input_output_aliases={2: 0},
        compiler_params=_params("arbitrary"),
        name="moe_dispatch",
    )(dest_flat, hp, xs0)


def _moe_kernel(be_ref, nv_ref, xs_ref, wgu_ref, bgu_ref, wd_ref, bd_ref, y_ref, wgu_bf, wd_bf):
    i = pl.program_id(0)
    de = wd_ref.shape[2]
    valid = i < nv_ref[0]
    new_expert = jnp.logical_or(i == 0, be_ref[i] != be_ref[jnp.maximum(i - 1, 0)])

    @pl.when(jnp.logical_and(valid, new_expert))
    def _():
        def cast_rows(ref, out, rows):
            def body(j, carry):
                sl = pl.ds(pl.multiple_of(j * CAST_ROWS, CAST_ROWS), CAST_ROWS)
                out[sl, :] = ref[0, 0, sl, :].astype(_BF)
                return carry
            lax.fori_loop(0, rows // CAST_ROWS, body, 0)
        cast_rows(wgu_ref, wgu_bf, wgu_ref.shape[2])
        cast_rows(wd_ref, wd_bf, de)

    @pl.when(valid)
    def _():
        xb = _unpack_bf16_pairs(xs_ref[...])
        gu = _dot(xb, wgu_bf[...]) + bgu_ref[0, 0]
        a = jnp.minimum(gu[:, :de], SWIGLU_LIMIT)
        u = jnp.clip(gu[:, de:], -SWIGLU_LIMIT, SWIGLU_LIMIT)
        act = a * jax.nn.sigmoid(SWIGLU_ALPHA * a) * (u + 1)
        y_ref[...] = _dot(act.astype(_BF), wd_bf[...]) + bd_ref[0, 0]

    @pl.when(i >= nv_ref[0])
    def _():
        y_ref[...] = jnp.zeros(y_ref.shape, _F32)


def _moe(block_e, n_valid, xs, w_gu, b_gu, w_down, b_down, *, layer):
    n_slots, w = xs.shape
    depth, ne, d, de2 = w_gu.shape
    de = de2 // 2
    n_blocks = n_slots // TME
    ex = lambda i, be, nv: (layer, be[i], 0, 0)
    grid_spec = pltpu.PrefetchScalarGridSpec(
        num_scalar_prefetch=2,
        grid=(n_blocks,),
        in_specs=[pl.BlockSpec((TME, w), lambda i, be, nv: (i, 0)),
                  pl.BlockSpec((1, 1, d, de2), ex), pl.BlockSpec((1, 1, 1, de2), ex),
                  pl.BlockSpec((1, 1, de, d), ex), pl.BlockSpec((1, 1, 1, d), ex)],
        out_specs=pl.BlockSpec((TME, d), lambda i, be, nv: (i, 0)),
        scratch_shapes=[pltpu.VMEM((d, de2), _BF), pltpu.VMEM((de, d), _BF)],
    )
    return pl.pallas_call(
        _moe_kernel,
        grid_spec=grid_spec,
        out_shape=jax.ShapeDtypeStruct((n_slots, d), _F32),
        compiler_params=_params("arbitrary"),
        name="moe_experts",
    )(block_e, n_valid, xs, w_gu, b_gu.reshape(depth, ne, 1, de2), w_down, b_down.reshape(depth, ne, 1, d))


def _combine_kernel(dest_ref, y_ref, gate_ref, x_ref, mod_ref, fg_ref, o_ref, buf, sem, *, final):
    def body(r, carry):
        for k in range(TOP_K):
            pltpu.make_async_copy(y_ref.at[pl.ds(dest_ref[TOP_K * r + k], 1)], buf.at[k, pl.ds(r, 1)],
                                  sem).start()
        return carry

    lax.fori_loop(0, TMC, body, 0)
    for k in range(TOP_K):
        pltpu.make_async_copy(y_ref.at[pl.ds(0, TMC)], buf.at[k], sem).wait()
    g = gate_ref[...]
    moe = g[:, 0:1] * buf[0]
    for k in range(1, TOP_K):
        moe = moe + g[:, k:k + 1] * buf[k]
    x2 = x_ref[...] + mod_ref[0][5:6] * moe
    if final:
        ms = jnp.mean(x2 * x2, axis=-1, keepdims=True)
        x2 = x2 * lax.rsqrt(ms + EPS) * fg_ref[...]
    o_ref[...] = x2


def _combine(dest_flat, y, gate4, x1, mod, fg, *, n, n_batch, rows, final):
    d = x1.shape[1]
    row = lambda i: (i, 0)
    mod_idx = lambda i: (jnp.minimum(i * TMC // n, n_batch), 0, 0)
    return pl.pallas_call(
        functools.partial(_combine_kernel, final=final),
        grid=(rows // TMC,),
        in_specs=[pl.BlockSpec((TOP_K * TMC,), lambda i: (i,), memory_space=pltpu.SMEM),
                  pl.BlockSpec(memory_space=pl.ANY),
                  pl.BlockSpec((TMC, LANES), row), pl.BlockSpec((TMC, d), row),
                  pl.BlockSpec((1, N_MOD, d), mod_idx), pl.BlockSpec((1, d), lambda i: (0, 0))],
        out_specs=pl.BlockSpec((TMC, d), row),
        out_shape=jax.ShapeDtypeStruct((rows, d), _F32),
        scratch_shapes=[pltpu.VMEM((TOP_K, TMC, d), _F32), pltpu.SemaphoreType.DMA],
        compiler_params=_params("arbitrary"),
        name="moe_combine",
    )(dest_flat, y, gate4, x1, mod, fg)


def _routing_tables(counts, n_blocks):
    cnt = counts[0, :N_EXPERTS].astype(jnp.int32)
    padded = (cnt + TME - 1) // TME * TME
    pad_end = jnp.cumsum(padded)
    pad_start = pad_end - padded
    ps = jnp.zeros((1, LANES), _F32).at[0, :N_EXPERTS].set(pad_start.astype(_F32))
    blk = jnp.arange(n_blocks, dtype=jnp.int32) * TME
    block_e = jnp.sum((pad_end[None, :] <= blk[:, None]).astype(jnp.int32), axis=1)
    e_last = jnp.max(jnp.where(cnt > 0, jnp.arange(N_EXPERTS, dtype=jnp.int32), 0))
    block_e = jnp.minimum(block_e, e_last).astype(jnp.int32)
    n_valid = (pad_end[-1] // TME).astype(jnp.int32).reshape(1)
    return ps, block_e, n_valid


def kernel(x, c, ctx, c_ctx, ada_w, ada_b, norm1_g, norm2_g, w_in, attn_sink, w_fourier_out, w_attn_out,
           w_out, router_w, router_b, expert_w_gu, expert_b_gu, expert_w_down, expert_b_down, final_norm_g):
    n_batch, n, d = x.shape
    ctx_len = ctx.shape[1]
    depth = ada_w.shape[0]
    t_lat = n_batch * n
    t_ctx = n_batch * ctx_len
    tt = t_lat + t_ctx
    fd = qd = d // 2
    assert n % TM == 0 and t_ctx % TM == 0 and n % (BLOCK * FB_NB) == 0 and n_batch < MOD_ROWS
    assert t_lat % ctx_len == 0 and TM % ctx_len == 0

    xa = jnp.concatenate([x.reshape(t_lat, d), ctx.reshape(t_ctx, d)], axis=0)
    cc = jnp.concatenate([c, c_ctx[None, :], jnp.zeros((MOD_ROWS - n_batch - 1, d), _F32)], axis=0)
    mod = _ada(cc, ada_w, ada_b).reshape(depth, MOD_ROWS, N_MOD, d)
    cos_t, sin_t = _rope_tables(n)
    consts = _dft_consts(n, ctx_len)

    n_in = fd + qd + 2 * KV_DIM
    w_a = w_in[:, :, :n_in].astype(_BF)
    w_g = w_in[:, :, n_in:].astype(_BF)
    w_fo, w_ao, w_o = w_fourier_out.astype(_BF), w_attn_out.astype(_BF), w_out.astype(_BF)
    rw = jnp.pad(router_w, ((0, 0), (0, 0), (0, LANES - N_EXPERTS))).astype(_BF)
    rb = jnp.pad(router_b, ((0, 0), (0, LANES - N_EXPERTS)), constant_values=NEG_INF)[:, None, :]

    n_blocks = -(-(tt * TOP_K + N_EXPERTS * (TME - 1)) // TME)
    dims = dict(n=n, t_lat=t_lat, n_batch=n_batch)
    for l in range(depth):
        last = l == depth - 1
        f, q, k2, v2 = _proj(xa, mod[l], norm1_g[l][None, :], w_a[l], cos_t, sin_t, **dims)
        fm_lat = _fourier_lat(f, consts, n_batch=n_batch, n=n)
        fm_ctx = _fourier_ctx(f, consts, n_batch=n_batch, ctx_len=ctx_len, t_lat=t_lat)
        at_lat = _attn_lat(attn_sink[l], q, k2, v2, n_batch=n_batch, n=n, ctx_len=ctx_len, t_lat=t_lat)
        at_ctx = _attn_ctx(attn_sink[l], q, k2, v2, n_batch=n_batch, ctx_len=ctx_len, t_lat=t_lat)
        x1, hp, idx4, gate4, counts = _merge(
            xa, mod[l], norm1_g[l][None, :], norm2_g[l][None, :], w_g[l], fm_lat, fm_ctx, at_lat, at_ctx,
            w_fo[l], w_ao[l], w_o[l], rw[l], rb[l], **dims)
        pad_start, block_e, n_valid = _routing_tables(counts, n_blocks)
        dest_flat = _route(idx4, pad_start)[:, :TOP_K].reshape(-1)
        xs = _dispatch(dest_flat, hp, n_blocks * TME)
        y = _moe(block_e, n_valid, xs, expert_w_gu, expert_b_gu, expert_w_down, expert_b_down, layer=l)
        xa = _combine(dest_flat, y, gate4, x1, mod[l], final_norm_g[None, :], n=n, n_batch=n_batch,
                      rows=t_lat if last else tt, final=last)
    return xa.reshape(n_batch, n, d)
```

```python
import functools

import numpy as np
import jax
import jax.numpy as jnp
from jax import lax
from jax.experimental import pallas as pl
from jax.experimental.pallas import tpu as pltpu

GRID_W = 64
HEAD_DIM = 64
N_KV_HEADS = 2
Q_PER_KV = 4
KV_DIM = N_KV_HEADS * HEAD_DIM
WINDOW = 128
BLOCK = 128
ROPE_THETA = 10000.0
ROPE_FREQS = HEAD_DIM // 4
GROUP_DIM = 128
N_EXPERTS = 32
TOP_K = 4
SWIGLU_LIMIT = 7.0
SWIGLU_ALPHA = 1.702
N_MOD = 6
EPS = 1e-5
NEG_INF = -1e30

LANES = 128
MOD_ROWS = 8
TM = 512
TME = 512
TMC = 256
FA_NB = 16
FB_NB = 16
CAST_ROWS = 64
VMEM_LIMIT = 56 * 1024 * 1024

_BF = jnp.bfloat16
_F32 = jnp.float32


def _params(*sem):
    return pltpu.CompilerParams(dimension_semantics=sem, vmem_limit_bytes=VMEM_LIMIT)


def _dot(a, b):
    return jnp.dot(a, b, preferred_element_type=_F32)


def _norm_mod(x, g, shift, scale):
    ms = jnp.mean(x * x, axis=-1, keepdims=True)
    return (x * lax.rsqrt(ms + EPS) * g) * (1 + scale) + shift


def _ada_kernel(c_ref, w_ref, b_ref, o_ref):
    c = c_ref[...]
    s = c * jax.nn.sigmoid(c)
    o_ref[0] = jnp.dot(s, w_ref[0], precision=lax.Precision.HIGHEST,
                       preferred_element_type=_F32) + b_ref[0]


def _ada(cc, ada_w, ada_b):
    depth, d, nd = ada_w.shape
    tn = nd // 4
    return pl.pallas_call(
        _ada_kernel,
        grid=(depth, nd // tn),
        in_specs=[pl.BlockSpec((MOD_ROWS, d), lambda l, j: (0, 0)),
                  pl.BlockSpec((1, d, tn), lambda l, j: (l, 0, j)),
                  pl.BlockSpec((1, 1, tn), lambda l, j: (l, 0, j))],
        out_specs=pl.BlockSpec((1, MOD_ROWS, tn), lambda l, j: (l, 0, j)),
        out_shape=jax.ShapeDtypeStruct((depth, MOD_ROWS, nd), _F32),
        compiler_params=_params("arbitrary", "arbitrary"),
        name="ada_mod",
    )(cc, ada_w, ada_b.reshape(depth, 1, nd))


def _rope(v, cos, sin):
    lane = lax.broadcasted_iota(jnp.int32, v.shape, 1)
    partner = jnp.where((lane & ROPE_FREQS) == 0,
                        pltpu.roll(v, LANES - ROPE_FREQS, 1), pltpu.roll(v, ROPE_FREQS, 1))
    return v * cos + partner * sin


def _proj_kernel(x_ref, mod_ref, g_ref, w_ref, cos_ref, sin_ref, f_ref, q_ref, k_ref, v_ref, *, fd, qd):
    m = mod_ref[0]
    h = _norm_mod(x_ref[...], g_ref[...], m[0:1], m[1:2]).astype(_BF)
    p = _dot(h, w_ref[...])
    cos = cos_ref[...]
    sin = sin_ref[...]
    f_ref[...] = p[:, :fd].astype(_BF)
    for j in range(qd // LANES):
        lo = fd + j * LANES
        q_ref[:, j * LANES:(j + 1) * LANES] = (
            _rope(p[:, lo:lo + LANES], cos, sin) * (HEAD_DIM ** -0.5)).astype(_BF)
    k = _rope(p[:, fd + qd:fd + qd + KV_DIM], cos, sin)
    v = p[:, fd + qd + KV_DIM:fd + qd + 2 * KV_DIM]
    k_ref[:, :KV_DIM] = k.astype(_BF)
    k_ref[:, KV_DIM:] = pltpu.roll(k, HEAD_DIM, 1).astype(_BF)
    v_ref[:, :KV_DIM] = v.astype(_BF)
    v_ref[:, KV_DIM:] = pltpu.roll(v, HEAD_DIM, 1).astype(_BF)


def _proj(xa, mod, g, w_a, cos_t, sin_t, *, n, t_lat, n_batch):
    tt, d = xa.shape
    fd = qd = d // 2
    nt = tt // TM
    nxt = t_lat // TM
    per_seq = n // TM
    mod_idx = lambda i: (jnp.minimum(i * TM // n, n_batch), 0, 0)
    rope_idx = lambda i: (jnp.where(i < nxt, i % per_seq, per_seq), 0)
    row = lambda i: (i, 0)
    return pl.pallas_call(
        functools.partial(_proj_kernel, fd=fd, qd=qd),
        grid=(nt,),
        in_specs=[pl.BlockSpec((TM, d), row),
                  pl.BlockSpec((1, N_MOD, d), mod_idx),
                  pl.BlockSpec((1, d), lambda i: (0, 0)),
                  pl.BlockSpec(w_a.shape, lambda i: (0, 0)),
                  pl.BlockSpec((TM, LANES), rope_idx),
                  pl.BlockSpec((TM, LANES), rope_idx)],
        out_specs=[pl.BlockSpec((TM, fd), row), pl.BlockSpec((TM, qd), row),
                   pl.BlockSpec((TM, 2 * KV_DIM), row), pl.BlockSpec((TM, 2 * KV_DIM), row)],
        out_shape=[jax.ShapeDtypeStruct((tt, fd), _BF), jax.ShapeDtypeStruct((tt, qd), _BF),
                   jax.ShapeDtypeStruct((tt, 2 * KV_DIM), _BF), jax.ShapeDtypeStruct((tt, 2 * KV_DIM), _BF)],
        compiler_params=_params("arbitrary"),
        name="proj_in",
    )(xa, mod, g, w_a, cos_t, sin_t)


def _rope_tables(n):
    pos = jnp.arange(n)
    inv = ROPE_THETA ** (-jnp.arange(ROPE_FREQS, dtype=_F32) / ROPE_FREQS)
    ar = (pos // GRID_W).astype(_F32)[:, None] * inv
    ac = (pos % GRID_W).astype(_F32)[:, None] * inv
    cos = jnp.concatenate([jnp.cos(ar), jnp.cos(ar), jnp.cos(ac), jnp.cos(ac)], axis=1)
    sin = jnp.concatenate([-jnp.sin(ar), jnp.sin(ar), -jnp.sin(ac), jnp.sin(ac)], axis=1)
    cos = jnp.tile(cos, (1, LANES // HEAD_DIM))
    sin = jnp.tile(sin, (1, LANES // HEAD_DIM))
    cos = jnp.concatenate([cos, jnp.ones((TM, LANES), _F32)], axis=0)
    sin = jnp.concatenate([sin, jnp.zeros((TM, LANES), _F32)], axis=0)
    return cos, sin


def _dft_consts(n, ctx_len):
    n1 = n // BLOCK
    b = np.arange(BLOCK, dtype=np.int64)[:, None, None]
    k1 = np.arange(n1, dtype=np.int64)[None, :, None]
    a = np.arange(n1, dtype=np.int64)[None, None, :]
    ang = 2.0 * np.pi * ((a * k1 * BLOCK + b * k1) % n).astype(np.float64) / n
    m_ri = np.concatenate([np.cos(ang), -np.sin(ang)], axis=1)
    kk = np.arange(GROUP_DIM, dtype=np.int64)
    ang_c = 2.0 * np.pi * ((kk[:, None] * kk[None, :]) % GROUP_DIM) / GROUP_DIM
    c, s = np.cos(ang_c), np.sin(ang_c)
    cs = np.concatenate([np.concatenate([c, s], axis=1), np.concatenate([-s, c], axis=1)], axis=0)
    cc = np.concatenate([c, s], axis=0)
    kl = np.arange(ctx_len, dtype=np.int64)
    ang_l = 2.0 * np.pi * ((kl[:, None] * kl[None, :]) % ctx_len) / ctx_len
    wl = np.concatenate([np.cos(ang_l), -np.sin(ang_l)], axis=0)
    as_bf = lambda v: jnp.asarray(v, dtype=_F32).astype(_BF)
    return dict(m_ri=as_bf(m_ri), cs=as_bf(cs), cc=as_bf(cc), wl=as_bf(wl))


def _chan_dft(xr, xi, cc, scale):
    outs = []
    for g in range(xr.shape[1] // GROUP_DIM):
        sl = slice(g * GROUP_DIM, (g + 1) * GROUP_DIM)
        xx = jnp.concatenate([xr[:, sl], xi[:, sl]], axis=1).astype(_BF)
        outs.append(_dot(xx, cc) * scale)
    return jnp.concatenate(outs, axis=1)


def _fa_kernel(f_ref, m_ref, zr_ref, zi_ref):
    n1 = f_ref.shape[0]
    for t in range(FA_NB):
        zz = _dot(m_ref[t], f_ref[:, t, :])
        zr_ref[:, t, :] = zz[:n1].astype(_BF)
        zi_ref[:, t, :] = zz[n1:].astype(_BF)


def _fb_kernel(zr_ref, zi_ref, cs_ref, cc_ref, o_ref, *, scale):
    cs, cc = cs_ref[...], cc_ref[...]
    for j in range(FB_NB):
        xx = _dot(cs, jnp.concatenate([zr_ref[j], zi_ref[j]], axis=0))
        o_ref[:, j, :] = _chan_dft(xx[:BLOCK], xx[BLOCK:], cc, scale).astype(_BF)


def _fourier_lat(f, consts, *, n_batch, n):
    tt, fd = f.shape
    n1 = n // BLOCK
    f3 = f.reshape(tt // BLOCK, BLOCK, fd)
    blk_a = pl.BlockSpec((n1, FA_NB, fd), lambda b, j: (b, j, 0))
    mat_a = pl.BlockSpec((FA_NB, 2 * n1, n1), lambda b, j: (j, 0, 0))
    z_shape = jax.ShapeDtypeStruct((n_batch * n1, BLOCK, fd), _BF)
    zr, zi = pl.pallas_call(
        _fa_kernel,
        grid=(n_batch, BLOCK // FA_NB),
        in_specs=[blk_a, mat_a],
        out_specs=[blk_a, blk_a],
        out_shape=[z_shape, z_shape],
        compiler_params=_params("arbitrary", "arbitrary"),
        name="fourier_seq_a",
    )(f3, consts["m_ri"])
    blk_z = pl.BlockSpec((FB_NB, BLOCK, fd), lambda b, j: (b * (n1 // FB_NB) + j, 0, 0))
    const = lambda shape: pl.BlockSpec(shape, lambda b, j: (0, 0))
    out = pl.pallas_call(
        functools.partial(_fb_kernel, scale=float((n * GROUP_DIM) ** -0.5)),
        grid=(n_batch, n1 // FB_NB),
        in_specs=[blk_z, blk_z, const((2 * BLOCK, 2 * BLOCK)), const((2 * GROUP_DIM, GROUP_DIM))],
        out_specs=pl.BlockSpec((BLOCK, FB_NB, fd), lambda b, j: (b, j, 0)),
        out_shape=jax.ShapeDtypeStruct((n_batch * BLOCK, n1, fd), _BF),
        compiler_params=_params("arbitrary", "arbitrary"),
        name="fourier_seq_b",
    )(zr, zi, consts["cs"], consts["cc"])
    return out.reshape(n_batch * n, fd)


def _fc_kernel(f_ref, wl_ref, cc_ref, o_ref, *, scale):
    ctx_len = f_ref.shape[0]
    xx = _dot(wl_ref[...], f_ref[...])
    o_ref[...] = _chan_dft(xx[:ctx_len], xx[ctx_len:], cc_ref[...], scale).astype(_BF)


def _fourier_ctx(f, consts, *, n_batch, ctx_len, t_lat):
    fd = f.shape[1]
    const = lambda shape: pl.BlockSpec(shape, lambda b: (0, 0))
    return pl.pallas_call(
        functools.partial(_fc_kernel, scale=float((ctx_len * GROUP_DIM) ** -0.5)),
        grid=(n_batch,),
        in_specs=[pl.BlockSpec((ctx_len, fd), lambda b: (t_lat // ctx_len + b, 0)),
                  const((2 * ctx_len, ctx_len)), const((2 * GROUP_DIM, GROUP_DIM))],
        out_specs=pl.BlockSpec((ctx_len, fd), lambda b: (b, 0)),
        out_shape=jax.ShapeDtypeStruct((n_batch * ctx_len, fd), _BF),
        compiler_params=_params("arbitrary"),
        name="fourier_ctx",
    )(f, consts["wl"], consts["cc"])


def _attn_core(sink_ref, q_ref, k, v, o_ref, ok):
    tq = q_ref.shape[0]
    lane = lax.broadcasted_iota(jnp.int32, (tq, LANES), 1)
    low = lane < HEAD_DIM
    zero = jnp.zeros((tq, LANES), _BF)
    n_heads = q_ref.shape[1] // HEAD_DIM
    for j in range(n_heads // 2):
        qs = q_ref[:, j * LANES:(j + 1) * LANES]
        halves = []
        for half in range(2):
            hq = 2 * j + half
            kvh = hq // Q_PER_KV
            sel = slice(0, LANES) if half == kvh else slice(LANES, 2 * LANES)
            qm = jnp.where(low if half == 0 else jnp.logical_not(low), qs, zero)
            s = lax.dot_general(qm, k[:, sel], (((1,), (1,)), ((), ())), preferred_element_type=_F32)
            if ok is not None:
                s = jnp.where(ok, s, NEG_INF)
            sk = sink_ref[hq]
            m = jnp.maximum(jnp.max(s, axis=-1, keepdims=True), sk)
            p = jnp.exp(s - m)
            den = jnp.sum(p, axis=-1, keepdims=True) + jnp.exp(sk - m)
            halves.append(_dot(p.astype(_BF), v[:, sel]) / den)
        o_ref[:, j * LANES:(j + 1) * LANES] = jnp.where(low, halves[0], halves[1]).astype(_BF)


def _attn_lat_kernel(sink_ref, q_ref, kp_ref, kc_ref, kn_ref, vp_ref, vc_ref, vn_ref, kx_ref, vx_ref,
                     o_ref, *, nb):
    i = pl.program_id(1)
    k = jnp.concatenate([kp_ref[...], kc_ref[...], kn_ref[...], kx_ref[...]], axis=0)
    v = jnp.concatenate([vp_ref[...], vc_ref[...], vn_ref[...], vx_ref[...]], axis=0)
    shape = (BLOCK, k.shape[0])
    r = lax.broadcasted_iota(jnp.int32, shape, 0)
    s = lax.broadcasted_iota(jnp.int32, shape, 1)
    lo_ok = jnp.where(i > 0, 0, BLOCK)
    hi_ok = jnp.where(i < nb - 1, 3 * BLOCK, 2 * BLOCK)
    ok = (jnp.abs(s - BLOCK - r) <= WINDOW) & (s >= lo_ok) & (s < hi_ok)
    ok = ok | (s >= 3 * BLOCK)
    _attn_core(sink_ref, q_ref, k, v, o_ref, ok)


def _attn_ctx_kernel(sink_ref, q_ref, kx_ref, vx_ref, o_ref):
    _attn_core(sink_ref, q_ref, kx_ref[...], vx_ref[...], o_ref, None)


def _attn_lat(sink, q, k2, v2, *, n_batch, n, ctx_len, t_lat):
    qd = q.shape[1]
    nb = n // BLOCK
    cur = lambda b, i: (b * nb + i, 0)
    prev = lambda b, i: (b * nb + jnp.maximum(i - 1, 0), 0)
    nxt = lambda b, i: (b * nb + jnp.minimum(i + 1, nb - 1), 0)
    cx = lambda b, i: (t_lat // ctx_len + b, 0)
    kb = lambda im: pl.BlockSpec((BLOCK, 2 * KV_DIM), im)
    kcx = pl.BlockSpec((ctx_len, 2 * KV_DIM), cx)
    return pl.pallas_call(
        functools.partial(_attn_lat_kernel, nb=nb),
        grid=(n_batch, nb),
        in_specs=[pl.BlockSpec(memory_space=pltpu.SMEM), pl.BlockSpec((BLOCK, qd), cur),
                  kb(prev), kb(cur), kb(nxt), kb(prev), kb(cur), kb(nxt), kcx, kcx],
        out_specs=pl.BlockSpec((BLOCK, qd), cur),
        out_shape=jax.ShapeDtypeStruct((t_lat, qd), _BF),
        compiler_params=_params("arbitrary", "arbitrary"),
        name="attn_lat",
    )(sink, q, k2, k2, k2, v2, v2, v2, k2, v2)


def _attn_ctx(sink, q, k2, v2, *, n_batch, ctx_len, t_lat):
    qd = q.shape[1]
    cx = lambda b: (t_lat // ctx_len + b, 0)
    kcx = pl.BlockSpec((ctx_len, 2 * KV_DIM), cx)
    return pl.pallas_call(
        _attn_ctx_kernel,
        grid=(n_batch,),
        in_specs=[pl.BlockSpec(memory_space=pltpu.SMEM), pl.BlockSpec((ctx_len, qd), cx), kcx, kcx],
        out_specs=pl.BlockSpec((ctx_len, qd), lambda b: (b, 0)),
        out_shape=jax.ShapeDtypeStruct((n_batch * ctx_len, qd), _BF),
        compiler_params=_params("arbitrary"),
        name="attn_ctx",
    )(sink, q, k2, v2)


def _pack_bf16_pairs(h):
    half = h.shape[1] // 2
    lo = lax.bitcast_convert_type(h[:, :half].astype(_BF).astype(_F32), jnp.uint32)
    hi = lax.bitcast_convert_type(h[:, half:].astype(_BF).astype(_F32), jnp.uint32)
    return (lo >> 16) | hi


def _unpack_bf16_pairs(p):
    lo = lax.bitcast_convert_type(p << 16, _F32)
    hi = lax.bitcast_convert_type(p & jnp.uint32(0xFFFF0000), _F32)
    return jnp.concatenate([lo, hi], axis=1).astype(_BF)


def _merge_kernel(x_ref, mod_ref, g1_ref, g2_ref, wg_ref, fml_ref, fmc_ref, atl_ref, atc_ref,
                  wfo_ref, wao_ref, wout_ref, rw_ref, rb_ref,
                  x1_ref, hp_ref, idx_ref, gate_ref, cnt_ref, *, nxt):
    i = pl.program_id(0)
    d = x_ref.shape[1]
    m = mod_ref[0]
    x = x_ref[...]
    h = _norm_mod(x, g1_ref[...], m[0:1], m[1:2]).astype(_BF)
    gts = _dot(h, wg_ref[...])
    is_lat = i < nxt
    fm = jnp.where(is_lat, fml_ref[...], fmc_ref[...])
    at = jnp.where(is_lat, atl_ref[...], atc_ref[...])
    y = (jax.nn.sigmoid(gts[:, :d]) * _dot(fm, wfo_ref[...])
         + jax.nn.sigmoid(gts[:, d:]) * _dot(at, wao_ref[...]))
    x1 = x + m[2:3] * _dot(y.astype(_BF), wout_ref[...])
    x1_ref[...] = x1
    h2 = _norm_mod(x1, g2_ref[...], m[3:4], m[4:5])
    hp_ref[...] = _pack_bf16_pairs(h2)
    logits = _dot(h2.astype(_BF), rw_ref[...]) + rb_ref[...]
    lane = lax.broadcasted_iota(jnp.int32, logits.shape, 1)
    vals, idxs = [], []
    for _ in range(TOP_K):
        mx = jnp.max(logits, axis=-1, keepdims=True)
        ix = jnp.min(jnp.where(logits == mx, lane, LANES), axis=-1, keepdims=True)
        logits = jnp.where(lane == ix, NEG_INF, logits)
        vals.append(mx)
        idxs.append(ix)
    es = [jnp.exp(v - vals[0]) for v in vals]
    den = es[0] + es[1] + es[2] + es[3]
    idx4 = jnp.zeros(logits.shape, jnp.int32)
    gate4 = jnp.zeros(logits.shape, _F32)
    sel = jnp.zeros(logits.shape, _F32)
    for k in range(TOP_K):
        idx4 = jnp.where(lane == k, idxs[k], idx4)
        gate4 = jnp.where(lane == k, es[k] / den, gate4)
        sel = sel + jnp.where(lane == idxs[k], 1.0, 0.0)
    idx_ref[...] = idx4
    gate_ref[...] = gate4

    @pl.when(i == 0)
    def _():
        cnt_ref[...] = jnp.zeros(cnt_ref.shape, _F32)

    cnt_ref[0:1, :] += jnp.sum(sel, axis=0, keepdims=True)


def _merge(xa, mod, g1, g2, w_g, fm_lat, fm_ctx, at_lat, at_ctx, w_fo, w_ao, w_out, rw, rb,
           *, n, t_lat, n_batch):
    tt, d = xa.shape
    fd = fm_lat.shape[1]
    nt = tt // TM
    nxt = t_lat // TM
    row = lambda i: (i, 0)
    lat = lambda i: (jnp.minimum(i, nxt - 1), 0)
    cxt = lambda i: (jnp.maximum(i - nxt, 0), 0)
    const = lambda a: pl.BlockSpec(a.shape, lambda i: (0, 0))
    mod_idx = lambda i: (jnp.minimum(i * TM // n, n_batch), 0, 0)
    return pl.pallas_call(
        functools.partial(_merge_kernel, nxt=nxt),
        grid=(nt,),
        in_specs=[pl.BlockSpec((TM, d), row), pl.BlockSpec((1, N_MOD, d), mod_idx),
                  const(g1), const(g2), const(w_g),
                  pl.BlockSpec((TM, fd), lat), pl.BlockSpec((TM, fd), cxt),
                  pl.BlockSpec((TM, fd), lat), pl.BlockSpec((TM, fd), cxt),
                  const(w_fo), const(w_ao), const(w_out), const(rw), const(rb)],
        out_specs=[pl.BlockSpec((TM, d), row), pl.BlockSpec((TM, d // 2), row),
                   pl.BlockSpec((TM, LANES), row), pl.BlockSpec((TM, LANES), row),
                   pl.BlockSpec((8, LANES), lambda i: (0, 0))],
        out_shape=[jax.ShapeDtypeStruct((tt, d), _F32), jax.ShapeDtypeStruct((tt, d // 2), jnp.uint32),
                   jax.ShapeDtypeStruct((tt, LANES), jnp.int32), jax.ShapeDtypeStruct((tt, LANES), _F32),
                   jax.ShapeDtypeStruct((8, LANES), _F32)],
        compiler_params=_params("arbitrary"),
        name="merge_router",
    )(xa, mod, g1, g2, w_g, fm_lat, fm_ctx, at_lat, at_ctx, w_fo, w_ao, w_out, rw, rb)


def _route_kernel(idx_ref, ps_ref, d_ref, carry):
    i = pl.program_id(0)

    @pl.when(i == 0)
    def _():
        carry[...] = ps_ref[...]

    idx4 = idx_ref[...]
    lane = lax.broadcasted_iota(jnp.int32, idx4.shape, 1)
    cols = [idx4[:, k:k + 1] for k in range(TOP_K)]
    sel = jnp.zeros(idx4.shape, _F32)
    for k in range(TOP_K):
        sel = sel + jnp.where(lane == cols[k], 1.0, 0.0)
    r = lax.broadcasted_iota(jnp.int32, (TM, TM), 0)
    c = lax.broadcasted_iota(jnp.int32, (TM, TM), 1)
    tri = jnp.where(r > c, 1.0, 0.0).astype(_BF)
    slot = carry[...] + _dot(tri, sel.astype(_BF))
    dest4 = jnp.zeros(idx4.shape, jnp.int32)
    for k in range(TOP_K):
        dk = jnp.sum(jnp.where(lane == cols[k], slot, 0.0), axis=-1, keepdims=True)
        dest4 = jnp.where(lane == k, dk.astype(jnp.int32), dest4)
    d_ref[...] = dest4
    carry[...] += jnp.sum(sel, axis=0, keepdims=True)


def _route(idx4, pad_start):
    tt = idx4.shape[0]
    return pl.pallas_call(
        _route_kernel,
        grid=(tt // TM,),
        in_specs=[pl.BlockSpec((TM, LANES), lambda i: (i, 0)), pl.BlockSpec((1, LANES), lambda i: (0, 0))],
        out_specs=pl.BlockSpec((TM, LANES), lambda i: (i, 0)),
        out_shape=jax.ShapeDtypeStruct((tt, LANES), jnp.int32),
        scratch_shapes=[pltpu.VMEM((1, LANES), _F32)],
        compiler_params=_params("arbitrary"),
        name="route_slots",
    )(idx4, pad_start)


def _dispatch_kernel(pend_ref, cnt_ref, nv_ref, dest_ref, src_ref, xs_ref, zbuf, sem, zsem):
    i = pl.program_id(0)
    n_blocks = xs_ref.shape[0] // TME

    @pl.when(i == 0)
    def _():
        zbuf[...] = jnp.zeros(zbuf.shape, zbuf.dtype)

        def zero_block(start):
            return pltpu.make_async_copy(zbuf, xs_ref.at[pl.ds(pl.multiple_of(start, TME), TME)], zsem)

        for e in range(N_EXPERTS):
            @pl.when(cnt_ref[e] > 0)
            def _():
                zero_block(pend_ref[e] - TME).start()

        def start_unused(j, carry):
            zero_block(j * TME).start()
            return carry

        def wait_unused(j, carry):
            zero_block(j * TME).wait()
            return carry

        lax.fori_loop(nv_ref[0], n_blocks, start_unused, 0)
        for e in range(N_EXPERTS):
            @pl.when(cnt_ref[e] > 0)
            def _():
                zero_block(pend_ref[e] - TME).wait()
        lax.fori_loop(nv_ref[0], n_blocks, wait_unused, 0)

    def body(r, carry):
        for k in range(TOP_K):
            pltpu.make_async_copy(src_ref.at[pl.ds(r, 1)], xs_ref.at[pl.ds(dest_ref[TOP_K * r + k], 1)],
                                  sem).start()
        return carry

    lax.fori_loop(0, TM, body, 0)
    for _ in range(TOP_K):
        pltpu.make_async_copy(src_ref, xs_ref.at[pl.ds(0, TM)], sem).wait()


def _dispatch(pad_end, cnt, n_valid, dest_flat, hp, n_slots):
    tt, w = hp.shape
    grid_spec = pltpu.PrefetchScalarGridSpec(
        num_scalar_prefetch=3,
        grid=(tt // TM,),
        in_specs=[pl.BlockSpec((TOP_K * TM,), lambda i, pe, ct, nv: (i,), memory_space=pltpu.SMEM),
                  pl.BlockSpec((TM, w), lambda i, pe, ct, nv: (i, 0))],
        out_specs=pl.BlockSpec(memory_space=pl.ANY),
        scratch_shapes=[pltpu.VMEM((TME, w), hp.dtype), pltpu.SemaphoreType.DMA, pltpu.SemaphoreType.DMA],
    )
    return pl.pallas_call(
        _dispatch_kernel,
        grid_spec=grid_spec,
        out_shape=jax.ShapeDtypeStruct((n_slots, w), hp.dtype),
        compiler_params=_params("arbitrary"),
        name="moe_dispatch",
    )(pad_end, cnt, n_valid, dest_flat, hp)


def _moe_kernel(be_ref, nv_ref, xs_ref, wgu_ref, bgu_ref, wd_ref, bd_ref, y_ref, wgu_bf, wd_bf):
    i = pl.program_id(0)
    de = wd_ref.shape[2]
    valid = i < nv_ref[0]
    new_expert = jnp.logical_or(i == 0, be_ref[i] != be_ref[jnp.maximum(i - 1, 0)])

    @pl.when(jnp.logical_and(valid, new_expert))
    def _():
        def cast_rows(ref, out, rows):
            def body(j, carry):
                sl = pl.ds(pl.multiple_of(j * CAST_ROWS, CAST_ROWS), CAST_ROWS)
                out[sl, :] = ref[0, 0, sl, :].astype(_BF)
                return carry
            lax.fori_loop(0, rows // CAST_ROWS, body, 0)
        cast_rows(wgu_ref, wgu_bf, wgu_ref.shape[2])
        cast_rows(wd_ref, wd_bf, de)

    @pl.when(valid)
    def _():
        xb = _unpack_bf16_pairs(xs_ref[...])
        gu = _dot(xb, wgu_bf[...]) + bgu_ref[0, 0]
        a = jnp.minimum(gu[:, :de], SWIGLU_LIMIT)
        u = jnp.clip(gu[:, de:], -SWIGLU_LIMIT, SWIGLU_LIMIT)
        act = a * jax.nn.sigmoid(SWIGLU_ALPHA * a) * (u + 1)
        y_ref[...] = _dot(act.astype(_BF), wd_bf[...]) + bd_ref[0, 0]

    @pl.when(i >= nv_ref[0])
    def _():
        y_ref[...] = jnp.zeros(y_ref.shape, _F32)


def _moe(block_e, n_valid, xs, w_gu, b_gu, w_down, b_down, *, layer):
    n_slots, w = xs.shape
    depth, ne, d, de2 = w_gu.shape
    de = de2 // 2
    n_blocks = n_slots // TME
    ex = lambda i, be, nv: (layer, be[i], 0, 0)
    grid_spec = pltpu.PrefetchScalarGridSpec(
        num_scalar_prefetch=2,
        grid=(n_blocks,),
        in_specs=[pl.BlockSpec((TME, w), lambda i, be, nv: (jnp.minimum(i, nv[0] - 1), 0)),
                  pl.BlockSpec((1, 1, d, de2), ex), pl.BlockSpec((1, 1, 1, de2), ex),
                  pl.BlockSpec((1, 1, de, d), ex), pl.BlockSpec((1, 1, 1, d), ex)],
        out_specs=pl.BlockSpec((TME, d), lambda i, be, nv: (i, 0)),
        scratch_shapes=[pltpu.VMEM((d, de2), _BF), pltpu.VMEM((de, d), _BF)],
    )
    return pl.pallas_call(
        _moe_kernel,
        grid_spec=grid_spec,
        out_shape=jax.ShapeDtypeStruct((n_slots, d), _F32),
        compiler_params=_params("arbitrary"),
        name="moe_experts",
    )(block_e, n_valid, xs, w_gu, b_gu.reshape(depth, ne, 1, de2), w_down, b_down.reshape(depth, ne, 1, d))


def _combine_kernel(dest_ref, dest_next_ref, y_ref, gate_ref, x_ref, mod_ref, fg_ref, o_ref, buf, sem,
                    *, final, n_steps):
    i = pl.program_id(0)
    slot = i % 2

    def gather(d_ref, s):
        def body(r, carry):
            for k in range(TOP_K):
                pltpu.make_async_copy(y_ref.at[pl.ds(d_ref[TOP_K * r + k], 1)], buf.at[s, k, pl.ds(r, 1)],
                                      sem.at[s]).start()
            return carry
        lax.fori_loop(0, TMC, body, 0)

    @pl.when(i == 0)
    def _():
        gather(dest_ref, 0)

    @pl.when(i + 1 < n_steps)
    def _():
        gather(dest_next_ref, 1 - slot)

    for k in range(TOP_K):
        pltpu.make_async_copy(y_ref.at[pl.ds(0, TMC)], buf.at[slot, k], sem.at[slot]).wait()
    g = gate_ref[...]
    moe = g[:, 0:1] * buf[slot, 0]
    for k in range(1, TOP_K):
        moe = moe + g[:, k:k + 1] * buf[slot, k]
    x2 = x_ref[...] + mod_ref[0][5:6] * moe
    if final:
        ms = jnp.mean(x2 * x2, axis=-1, keepdims=True)
        x2 = x2 * lax.rsqrt(ms + EPS) * fg_ref[...]
    o_ref[...] = x2


def _combine(dest_flat, y, gate4, x1, mod, fg, *, n, n_batch, rows, final):
    d = x1.shape[1]
    row = lambda i: (i, 0)
    mod_idx = lambda i: (jnp.minimum(i * TMC // n, n_batch), 0, 0)
    n_steps = rows // TMC
    return pl.pallas_call(
        functools.partial(_combine_kernel, final=final, n_steps=n_steps),
        grid=(n_steps,),
        in_specs=[pl.BlockSpec((TOP_K * TMC,), lambda i: (i,), memory_space=pltpu.SMEM),
                  pl.BlockSpec((TOP_K * TMC,), lambda i: (jnp.minimum(i + 1, n_steps - 1),),
                               memory_space=pltpu.SMEM),
                  pl.BlockSpec(memory_space=pl.ANY),
                  pl.BlockSpec((TMC, LANES), row), pl.BlockSpec((TMC, d), row),
                  pl.BlockSpec((1, N_MOD, d), mod_idx), pl.BlockSpec((1, d), lambda i: (0, 0))],
        out_specs=pl.BlockSpec((TMC, d), row),
        out_shape=jax.ShapeDtypeStruct((rows, d), _F32),
        scratch_shapes=[pltpu.VMEM((2, TOP_K, TMC, d), _F32), pltpu.SemaphoreType.DMA((2,))],
        compiler_params=_params("arbitrary"),
        name="moe_combine",
    )(dest_flat, dest_flat, y, gate4, x1, mod, fg)


def _routing_tables(counts, n_blocks):
    cnt = counts[0, :N_EXPERTS].astype(jnp.int32)
    padded = (cnt + TME - 1) // TME * TME
    pad_end = jnp.cumsum(padded)
    pad_start = pad_end - padded
    ps = jnp.zeros((1, LANES), _F32).at[0, :N_EXPERTS].set(pad_start.astype(_F32))
    blk = jnp.arange(n_blocks, dtype=jnp.int32) * TME
    block_e = jnp.sum((pad_end[None, :] <= blk[:, None]).astype(jnp.int32), axis=1)
    e_last = jnp.max(jnp.where(cnt > 0, jnp.arange(N_EXPERTS, dtype=jnp.int32), 0))
    block_e = jnp.minimum(block_e, e_last).astype(jnp.int32)
    n_valid = (pad_end[-1] // TME).astype(jnp.int32).reshape(1)
    return ps, block_e, n_valid, pad_end.astype(jnp.int32), cnt


def kernel(x, c, ctx, c_ctx, ada_w, ada_b, norm1_g, norm2_g, w_in, attn_sink, w_fourier_out, w_attn_out,
           w_out, router_w, router_b, expert_w_gu, expert_b_gu, expert_w_down, expert_b_down, final_norm_g):
    n_batch, n, d = x.shape
    ctx_len = ctx.shape[1]
    depth = ada_w.shape[0]
    t_lat = n_batch * n
    t_ctx = n_batch * ctx_len
    tt = t_lat + t_ctx
    fd = qd = d // 2
    assert n % TM == 0 and t_ctx % TM == 0 and n % (BLOCK * FB_NB) == 0 and n_batch < MOD_ROWS
    assert t_lat % ctx_len == 0 and TM % ctx_len == 0

    xa = jnp.concatenate([x.reshape(t_lat, d), ctx.reshape(t_ctx, d)], axis=0)
    cc = jnp.concatenate([c, c_ctx[None, :], jnp.zeros((MOD_ROWS - n_batch - 1, d), _F32)], axis=0)
    mod = _ada(cc, ada_w, ada_b).reshape(depth, MOD_ROWS, N_MOD, d)
    cos_t, sin_t = _rope_tables(n)
    consts = _dft_consts(n, ctx_len)

    n_in = fd + qd + 2 * KV_DIM
    w_a = w_in[:, :, :n_in].astype(_BF)
    w_g = w_in[:, :, n_in:].astype(_BF)
    w_fo, w_ao, w_o = w_fourier_out.astype(_BF), w_attn_out.astype(_BF), w_out.astype(_BF)
    rw = jnp.pad(router_w, ((0, 0), (0, 0), (0, LANES - N_EXPERTS))).astype(_BF)
    rb = jnp.pad(router_b, ((0, 0), (0, LANES - N_EXPERTS)), constant_values=NEG_INF)[:, None, :]

    n_blocks = -(-(tt * TOP_K + N_EXPERTS * (TME - 1)) // TME)
    dims = dict(n=n, t_lat=t_lat, n_batch=n_batch)
    for l in range(depth):
        last = l == depth - 1
        f, q, k2, v2 = _proj(xa, mod[l], norm1_g[l][None, :], w_a[l], cos_t, sin_t, **dims)
        fm_lat = _fourier_lat(f, consts, n_batch=n_batch, n=n)
        fm_ctx = _fourier_ctx(f, consts, n_batch=n_batch, ctx_len=ctx_len, t_lat=t_lat)
        at_lat = _attn_lat(attn_sink[l], q, k2, v2, n_batch=n_batch, n=n, ctx_len=ctx_len, t_lat=t_lat)
        at_ctx = _attn_ctx(attn_sink[l], q, k2, v2, n_batch=n_batch, ctx_len=ctx_len, t_lat=t_lat)
        x1, hp, idx4, gate4, counts = _merge(
            xa, mod[l], norm1_g[l][None, :], norm2_g[l][None, :], w_g[l], fm_lat, fm_ctx, at_lat, at_ctx,
            w_fo[l], w_ao[l], w_o[l], rw[l], rb[l], **dims)
        pad_start, block_e, n_valid, pad_end, cnt = _routing_tables(counts, n_blocks)
        dest_flat = _route(idx4, pad_start)[:, :TOP_K].reshape(-1)
        xs = _dispatch(pad_end, cnt, n_valid, dest_flat, hp, n_blocks * TME)
        y = _moe(block_e, n_valid, xs, expert_w_gu, expert_b_gu, expert_w_down, expert_b_down, layer=l)
        xa = _combine(dest_flat, y, gate4, x1, mod[l], final_norm_g[None, :], n=n, n_batch=n_batch,
                      rows=t_lat if last else tt, final=last)
    return xa.reshape(n_batch, n, d)
```

```python
import functools

import numpy as np
import jax
import jax.numpy as jnp
from jax import lax
from jax.experimental import pallas as pl
from jax.experimental.pallas import tpu as pltpu

GRID_W = 64
HEAD_DIM = 64
N_KV_HEADS = 2
Q_PER_KV = 4
KV_DIM = N_KV_HEADS * HEAD_DIM
WINDOW = 128
BLOCK = 128
ROPE_THETA = 10000.0
ROPE_FREQS = HEAD_DIM // 4
GROUP_DIM = 128
N_EXPERTS = 32
TOP_K = 4
SWIGLU_LIMIT = 7.0
SWIGLU_ALPHA = 1.702
N_MOD = 6
EPS = 1e-5
NEG_INF = -1e30
LOG2E = 1.4426950408889634
Q_SCALE = HEAD_DIM ** -0.5 * LOG2E

LANES = 128
MOD_ROWS = 8
TM = 512
TME = 512
TMC = 256
FA_NB = 16
FB_NB = 16
CAST_ROWS = 64
MERGE_COLS = 256
VMEM_LIMIT = 56 * 1024 * 1024

_BF = jnp.bfloat16
_F32 = jnp.float32


def _params(*sem):
    return pltpu.CompilerParams(dimension_semantics=sem, vmem_limit_bytes=VMEM_LIMIT)


def _dot(a, b):
    return jnp.dot(a, b, preferred_element_type=_F32)


def _sigmoid(x):
    return 0.5 * jnp.tanh(0.5 * x) + 0.5


def _norm_mod(x, g, shift, scale):
    ms = jnp.mean(x * x, axis=-1, keepdims=True)
    return (x * lax.rsqrt(ms + EPS) * g) * (1 + scale) + shift


def _ada_kernel(c_ref, w_ref, b_ref, o_ref):
    c = c_ref[...]
    s = c * jax.nn.sigmoid(c)
    o_ref[0] = jnp.dot(s, w_ref[0], precision=lax.Precision.HIGHEST,
                       preferred_element_type=_F32) + b_ref[0]


def _ada(cc, ada_w, ada_b):
    depth, d, nd = ada_w.shape
    tn = nd // 4
    return pl.pallas_call(
        _ada_kernel,
        grid=(depth, nd // tn),
        in_specs=[pl.BlockSpec((MOD_ROWS, d), lambda l, j: (0, 0)),
                  pl.BlockSpec((1, d, tn), lambda l, j: (l, 0, j)),
                  pl.BlockSpec((1, 1, tn), lambda l, j: (l, 0, j))],
        out_specs=pl.BlockSpec((1, MOD_ROWS, tn), lambda l, j: (l, 0, j)),
        out_shape=jax.ShapeDtypeStruct((depth, MOD_ROWS, nd), _F32),
        compiler_params=_params("arbitrary", "arbitrary"),
        name="ada_mod",
    )(cc, ada_w, ada_b.reshape(depth, 1, nd))


def _rope(v, cos, sin):
    lane = lax.broadcasted_iota(jnp.int32, v.shape, 1)
    partner = jnp.where((lane & ROPE_FREQS) == 0,
                        pltpu.roll(v, LANES - ROPE_FREQS, 1), pltpu.roll(v, ROPE_FREQS, 1))
    return v * cos + partner * sin


def _proj_kernel(x_ref, mod_ref, g_ref, w_ref, cos_ref, sin_ref, f_ref, q_ref, k_ref, vt_ref, *, fd, qd):
    m = mod_ref[0]
    h = _norm_mod(x_ref[...], g_ref[...], m[0:1], m[1:2]).astype(_BF)
    p = _dot(h, w_ref[...])
    cos = cos_ref[...]
    sin = sin_ref[...]
    f_ref[...] = p[:, :fd].astype(_BF)
    for j in range(qd // LANES):
        lo = fd + j * LANES
        q_ref[:, j * LANES:(j + 1) * LANES] = (
            _rope(p[:, lo:lo + LANES], cos, sin) * Q_SCALE).astype(_BF)
    k = _rope(p[:, fd + qd:fd + qd + KV_DIM], cos, sin)
    v = p[:, fd + qd + KV_DIM:fd + qd + 2 * KV_DIM]
    k_ref[:, :KV_DIM] = k.astype(_BF)
    k_ref[:, KV_DIM:] = pltpu.roll(k, HEAD_DIM, 1).astype(_BF)
    vt_ref[:KV_DIM, :] = v.T.astype(_BF)
    vt_ref[KV_DIM:, :] = pltpu.roll(v, HEAD_DIM, 1).T.astype(_BF)


def _proj(xa, mod, g, w_a, cos_t, sin_t, *, n, t_lat, n_batch):
    tt, d = xa.shape
    fd = qd = d // 2
    nt = tt // TM
    nxt = t_lat // TM
    per_seq = n // TM
    mod_idx = lambda i: (jnp.minimum(i * TM // n, n_batch), 0, 0)
    rope_idx = lambda i: (jnp.where(i < nxt, i % per_seq, per_seq), 0)
    row = lambda i: (i, 0)
    return pl.pallas_call(
        functools.partial(_proj_kernel, fd=fd, qd=qd),
        grid=(nt,),
        in_specs=[pl.BlockSpec((TM, d), row),
                  pl.BlockSpec((1, N_MOD, d), mod_idx),
                  pl.BlockSpec((1, d), lambda i: (0, 0)),
                  pl.BlockSpec(w_a.shape, lambda i: (0, 0)),
                  pl.BlockSpec((TM, LANES), rope_idx),
                  pl.BlockSpec((TM, LANES), rope_idx)],
        out_specs=[pl.BlockSpec((TM, fd), row), pl.BlockSpec((TM, qd), row),
                   pl.BlockSpec((TM, 2 * KV_DIM), row), pl.BlockSpec((2 * KV_DIM, TM), lambda i: (0, i))],
        out_shape=[jax.ShapeDtypeStruct((tt, fd), _BF), jax.ShapeDtypeStruct((tt, qd), _BF),
                   jax.ShapeDtypeStruct((tt, 2 * KV_DIM), _BF), jax.ShapeDtypeStruct((2 * KV_DIM, tt), _BF)],
        compiler_params=_params("arbitrary"),
        name="proj_in",
    )(xa, mod, g, w_a, cos_t, sin_t)


def _rope_tables(n):
    pos = jnp.arange(n)
    inv = ROPE_THETA ** (-jnp.arange(ROPE_FREQS, dtype=_F32) / ROPE_FREQS)
    ar = (pos // GRID_W).astype(_F32)[:, None] * inv
    ac = (pos % GRID_W).astype(_F32)[:, None] * inv
    cos = jnp.concatenate([jnp.cos(ar), jnp.cos(ar), jnp.cos(ac), jnp.cos(ac)], axis=1)
    sin = jnp.concatenate([-jnp.sin(ar), jnp.sin(ar), -jnp.sin(ac), jnp.sin(ac)], axis=1)
    cos = jnp.tile(cos, (1, LANES // HEAD_DIM))
    sin = jnp.tile(sin, (1, LANES // HEAD_DIM))
    cos = jnp.concatenate([cos, jnp.ones((TM, LANES), _F32)], axis=0)
    sin = jnp.concatenate([sin, jnp.zeros((TM, LANES), _F32)], axis=0)
    return cos, sin


def _dft_consts(n, ctx_len):
    n1 = n // BLOCK
    b = np.arange(BLOCK, dtype=np.int64)[:, None, None]
    k1 = np.arange(n1, dtype=np.int64)[None, :, None]
    a = np.arange(n1, dtype=np.int64)[None, None, :]
    ang = 2.0 * np.pi * ((a * k1 * BLOCK + b * k1) % n).astype(np.float64) / n
    m_ri = np.concatenate([np.cos(ang), -np.sin(ang)], axis=1)
    kk = np.arange(GROUP_DIM, dtype=np.int64)
    ang_c = 2.0 * np.pi * ((kk[:, None] * kk[None, :]) % GROUP_DIM) / GROUP_DIM
    c, s = np.cos(ang_c), np.sin(ang_c)
    cs = np.concatenate([np.concatenate([c, s], axis=1), np.concatenate([-s, c], axis=1)], axis=0)
    cc = np.concatenate([c, s], axis=0)
    kl = np.arange(ctx_len, dtype=np.int64)
    ang_l = 2.0 * np.pi * ((kl[:, None] * kl[None, :]) % ctx_len) / ctx_len
    wl = np.concatenate([np.cos(ang_l), -np.sin(ang_l)], axis=0)
    as_bf = lambda v: jnp.asarray(v, dtype=_F32).astype(_BF)
    return dict(m_ri=as_bf(m_ri), cs=as_bf(cs), cc=as_bf(cc), wl=as_bf(wl))


def _chan_dft(xr, xi, cc, scale):
    outs = []
    for g in range(xr.shape[1] // GROUP_DIM):
        sl = slice(g * GROUP_DIM, (g + 1) * GROUP_DIM)
        xx = jnp.concatenate([xr[:, sl], xi[:, sl]], axis=1).astype(_BF)
        outs.append(_dot(xx, cc) * scale)
    return jnp.concatenate(outs, axis=1)


def _fa_kernel(f_ref, m_ref, zr_ref, zi_ref):
    n1 = f_ref.shape[0]
    for t in range(FA_NB):
        zz = _dot(m_ref[t], f_ref[:, t, :])
        zr_ref[:, t, :] = zz[:n1].astype(_BF)
        zi_ref[:, t, :] = zz[n1:].astype(_BF)


def _fb_kernel(zr_ref, zi_ref, cs_ref, cc_ref, o_ref, *, scale):
    cs, cc = cs_ref[...], cc_ref[...]
    for j in range(FB_NB):
        xx = _dot(cs, jnp.concatenate([zr_ref[j], zi_ref[j]], axis=0))
        o_ref[:, j, :] = _chan_dft(xx[:BLOCK], xx[BLOCK:], cc, scale).astype(_BF)


def _fourier_lat(f, consts, *, n_batch, n):
    tt, fd = f.shape
    n1 = n // BLOCK
    f3 = f.reshape(tt // BLOCK, BLOCK, fd)
    blk_a = pl.BlockSpec((n1, FA_NB, fd), lambda b, j: (b, j, 0))
    mat_a = pl.BlockSpec((FA_NB, 2 * n1, n1), lambda b, j: (j, 0, 0))
    z_shape = jax.ShapeDtypeStruct((n_batch * n1, BLOCK, fd), _BF)
    zr, zi = pl.pallas_call(
        _fa_kernel,
        grid=(n_batch, BLOCK // FA_NB),
        in_specs=[blk_a, mat_a],
        out_specs=[blk_a, blk_a],
        out_shape=[z_shape, z_shape],
        compiler_params=_params("arbitrary", "arbitrary"),
        name="fourier_seq_a",
    )(f3, consts["m_ri"])
    blk_z = pl.BlockSpec((FB_NB, BLOCK, fd), lambda b, j: (b * (n1 // FB_NB) + j, 0, 0))
    const = lambda shape: pl.BlockSpec(shape, lambda b, j: (0, 0))
    out = pl.pallas_call(
        functools.partial(_fb_kernel, scale=float((n * GROUP_DIM) ** -0.5)),
        grid=(n_batch, n1 // FB_NB),
        in_specs=[blk_z, blk_z, const((2 * BLOCK, 2 * BLOCK)), const((2 * GROUP_DIM, GROUP_DIM))],
        out_specs=pl.BlockSpec((BLOCK, FB_NB, fd), lambda b, j: (b, j, 0)),
        out_shape=jax.ShapeDtypeStruct((n_batch * BLOCK, n1, fd), _BF),
        compiler_params=_params("arbitrary", "arbitrary"),
        name="fourier_seq_b",
    )(zr, zi, consts["cs"], consts["cc"])
    return out.reshape(n_batch * n, fd)


def _fc_kernel(f_ref, wl_ref, cc_ref, o_ref, *, scale):
    ctx_len = f_ref.shape[0]
    xx = _dot(wl_ref[...], f_ref[...])
    o_ref[...] = _chan_dft(xx[:ctx_len], xx[ctx_len:], cc_ref[...], scale).astype(_BF)


def _fourier_ctx(f, consts, *, n_batch, ctx_len, t_lat):
    fd = f.shape[1]
    const = lambda shape: pl.BlockSpec(shape, lambda b: (0, 0))
    return pl.pallas_call(
        functools.partial(_fc_kernel, scale=float((ctx_len * GROUP_DIM) ** -0.5)),
        grid=(n_batch,),
        in_specs=[pl.BlockSpec((ctx_len, fd), lambda b: (t_lat // ctx_len + b, 0)),
                  const((2 * ctx_len, ctx_len)), const((2 * GROUP_DIM, GROUP_DIM))],
        out_specs=pl.BlockSpec((ctx_len, fd), lambda b: (b, 0)),
        out_shape=jax.ShapeDtypeStruct((n_batch * ctx_len, fd), _BF),
        compiler_params=_params("arbitrary"),
        name="fourier_ctx",
    )(f, consts["wl"], consts["cc"])


def _attn_core(sink_ref, q_ref, k, vt, o_ref, bias, s_scr, p_scr):
    tq = q_ref.shape[0]
    lane = lax.broadcasted_iota(jnp.int32, (tq, LANES), 1)
    low = lane < HEAD_DIM
    zero = jnp.zeros((tq, LANES), _BF)
    low_row = lax.broadcasted_iota(jnp.int32, (LANES, 2 * tq), 0) < HEAD_DIM
    first = lax.broadcasted_iota(jnp.int32, (1, 2 * tq), 1) < tq
    pairs = [(kvh, half) for kvh in range(N_KV_HEADS) for half in range(2)]
    sels = [slice(0, LANES) if half == kvh else slice(LANES, 2 * LANES) for kvh, half in pairs]
    for pi, (kvh, half) in enumerate(pairs):
        slabs = [q_ref[:, (2 * kvh + c) * LANES:(2 * kvh + c + 1) * LANES] for c in range(2)]
        keep = low if half == 0 else jnp.logical_not(low)
        qm = jnp.concatenate([jnp.where(keep, sl, zero) for sl in slabs], axis=0)
        s = lax.dot_general(k[:, sels[pi]], qm, (((1,), (1,)), ((), ())), preferred_element_type=_F32)
        s_scr[pi] = s if bias is None else s + bias
    dens = []
    for pi, (kvh, half) in enumerate(pairs):
        hq = Q_PER_KV * kvh + half
        sk = jnp.where(first, sink_ref[hq], sink_ref[hq + 2]) * LOG2E
        s = s_scr[pi]
        m = jnp.maximum(jnp.max(s, axis=0, keepdims=True), sk)
        p = jnp.exp2(s - m)
        dens.append(jnp.sum(p, axis=0, keepdims=True) + jnp.exp2(sk - m))
        p_scr[pi] = p.astype(_BF)
    outs = [_dot(vt[sels[pi], :], p_scr[pi]) * (1.0 / dens[pi]) for pi in range(len(pairs))]
    for kvh in range(N_KV_HEADS):
        ot = jnp.where(low_row, outs[2 * kvh], outs[2 * kvh + 1])
        for c in range(2):
            o_ref[:, (2 * kvh + c) * LANES:(2 * kvh + c + 1) * LANES] = ot[:, c * tq:(c + 1) * tq].T.astype(_BF)


def _attn_scratch(keys, tq):
    return [pltpu.VMEM((2 * N_KV_HEADS, keys, 2 * tq), _F32), pltpu.VMEM((2 * N_KV_HEADS, keys, 2 * tq), _BF)]


def _attn_lat_kernel(sink_ref, q_ref, kp_ref, kc_ref, kn_ref, vp_ref, vc_ref, vn_ref, kx_ref, vx_ref, bias_ref,
                     o_ref, s_scr, p_scr):
    k = jnp.concatenate([kp_ref[...], kc_ref[...], kn_ref[...], kx_ref[...]], axis=0)
    vt = jnp.concatenate([vp_ref[...], vc_ref[...], vn_ref[...], vx_ref[...]], axis=1)
    _attn_core(sink_ref, q_ref, k, vt, o_ref, bias_ref[0], s_scr, p_scr)


def _window_bias(ctx_len):
    s = np.arange(3 * BLOCK + ctx_len)[:, None]
    r = np.arange(2 * BLOCK)[None, :] % BLOCK
    out = []
    for v in range(4):
        lo = 0 if v & 1 else BLOCK
        hi = 3 * BLOCK if v & 2 else 2 * BLOCK
        ok = ((np.abs(s - BLOCK - r) <= WINDOW) & (s >= lo) & (s < hi)) | (s >= 3 * BLOCK)
        out.append(np.where(ok, 0.0, NEG_INF))
    return jnp.asarray(np.stack(out), dtype=_F32)


def _attn_ctx_kernel(sink_ref, q_ref, kx_ref, vx_ref, o_ref, s_scr, p_scr):
    _attn_core(sink_ref, q_ref, kx_ref[...], vx_ref[...], o_ref, None, s_scr, p_scr)


def _attn_lat(sink, q, k2, v2t, *, n_batch, n, ctx_len, t_lat):
    qd = q.shape[1]
    nb = n // BLOCK
    cur = lambda b, i: (b * nb + i, 0)
    prev = lambda b, i: (b * nb + jnp.maximum(i - 1, 0), 0)
    nxt = lambda b, i: (b * nb + jnp.minimum(i + 1, nb - 1), 0)
    cx = lambda b, i: (t_lat // ctx_len + b, 0)
    swap = lambda im: (lambda b, i: im(b, i)[::-1])
    kb = lambda im: pl.BlockSpec((BLOCK, 2 * KV_DIM), im)
    vb = lambda im: pl.BlockSpec((2 * KV_DIM, BLOCK), swap(im))
    bias = _window_bias(ctx_len)
    bias_idx = lambda b, i: ((i > 0).astype(jnp.int32) + 2 * (i < nb - 1).astype(jnp.int32), 0, 0)
    return pl.pallas_call(
        _attn_lat_kernel,
        grid=(n_batch, nb),
        in_specs=[pl.BlockSpec(memory_space=pltpu.SMEM), pl.BlockSpec((BLOCK, qd), cur),
                  kb(prev), kb(cur), kb(nxt), vb(prev), vb(cur), vb(nxt),
                  pl.BlockSpec((ctx_len, 2 * KV_DIM), cx), pl.BlockSpec((2 * KV_DIM, ctx_len), swap(cx)),
                  pl.BlockSpec((1,) + bias.shape[1:], bias_idx)],
        out_specs=pl.BlockSpec((BLOCK, qd), cur),
        out_shape=jax.ShapeDtypeStruct((t_lat, qd), _BF),
        scratch_shapes=_attn_scratch(3 * BLOCK + ctx_len, BLOCK),
        compiler_params=_params("arbitrary", "arbitrary"),
        name="attn_lat",
    )(sink, q, k2, k2, k2, v2t, v2t, v2t, k2, v2t, bias)


def _attn_ctx(sink, q, k2, v2t, *, n_batch, ctx_len, t_lat):
    qd = q.shape[1]
    cx = lambda b: (t_lat // ctx_len + b, 0)
    return pl.pallas_call(
        _attn_ctx_kernel,
        grid=(n_batch,),
        in_specs=[pl.BlockSpec(memory_space=pltpu.SMEM), pl.BlockSpec((ctx_len, qd), cx),
                  pl.BlockSpec((ctx_len, 2 * KV_DIM), cx),
                  pl.BlockSpec((2 * KV_DIM, ctx_len), lambda b: (0, t_lat // ctx_len + b))],
        out_specs=pl.BlockSpec((ctx_len, qd), lambda b: (b, 0)),
        out_shape=jax.ShapeDtypeStruct((n_batch * ctx_len, qd), _BF),
        scratch_shapes=_attn_scratch(ctx_len, ctx_len),
        compiler_params=_params("arbitrary"),
        name="attn_ctx",
    )(sink, q, k2, v2t)


def _pack_bf16_pairs(h):
    half = h.shape[1] // 2
    lo = lax.bitcast_convert_type(h[:, :half].astype(_BF).astype(_F32), jnp.uint32)
    hi = lax.bitcast_convert_type(h[:, half:].astype(_BF).astype(_F32), jnp.uint32)
    return (lo >> 16) | hi


def _unpack_bf16_pairs(p):
    lo = lax.bitcast_convert_type(p << 16, _F32)
    hi = lax.bitcast_convert_type(p & jnp.uint32(0xFFFF0000), _F32)
    return jnp.concatenate([lo, hi], axis=1).astype(_BF)


def _merge_kernel(x_ref, mod_ref, g1_ref, g2_ref, wg_ref, fml_ref, fmc_ref, atl_ref, atc_ref,
                  wfo_ref, wao_ref, wout_ref, rw_ref, rb_ref,
                  x1_ref, hp_ref, idx_ref, gate_ref, cnt_ref, h_scr, y_scr, *, nxt):
    i = pl.program_id(0)
    d = x_ref.shape[1]
    m = mod_ref[0]
    h_scr[...] = _norm_mod(x_ref[...], g1_ref[...], m[0:1], m[1:2]).astype(_BF)
    is_lat = i < nxt
    fm = jnp.where(is_lat, fml_ref[...], fmc_ref[...])
    at = jnp.where(is_lat, atl_ref[...], atc_ref[...])
    for c in range(d // MERGE_COLS):
        sl = slice(c * MERGE_COLS, (c + 1) * MERGE_COLS)
        sg = slice(d + c * MERGE_COLS, d + (c + 1) * MERGE_COLS)
        h = h_scr[...]
        y = (_sigmoid(_dot(h, wg_ref[:, sl])) * _dot(fm, wfo_ref[:, sl])
             + _sigmoid(_dot(h, wg_ref[:, sg])) * _dot(at, wao_ref[:, sl]))
        y_scr[:, sl] = y.astype(_BF)
    x1_ref[...] = x_ref[...] + m[2:3] * _dot(y_scr[...], wout_ref[...])
    h2 = _norm_mod(x1_ref[...], g2_ref[...], m[3:4], m[4:5])
    hp_ref[...] = _pack_bf16_pairs(h2)
    logits = _dot(h2.astype(_BF), rw_ref[...]) + rb_ref[...]
    lane = lax.broadcasted_iota(jnp.int32, logits.shape, 1)
    vals, idxs = [], []
    for _ in range(TOP_K):
        mx = jnp.max(logits, axis=-1, keepdims=True)
        ix = jnp.min(jnp.where(logits == mx, lane, LANES), axis=-1, keepdims=True)
        logits = jnp.where(lane == ix, NEG_INF, logits)
        vals.append(mx)
        idxs.append(ix)
    es = [jnp.exp(v - vals[0]) for v in vals]
    den = es[0] + es[1] + es[2] + es[3]
    idx4 = jnp.zeros(logits.shape, jnp.int32)
    gate4 = jnp.zeros(logits.shape, _F32)
    sel = jnp.zeros(logits.shape, _F32)
    for k in range(TOP_K):
        idx4 = jnp.where(lane == k, idxs[k], idx4)
        gate4 = jnp.where(lane == k, es[k] / den, gate4)
        sel = sel + jnp.where(lane == idxs[k], 1.0, 0.0)
    idx_ref[...] = idx4
    gate_ref[...] = gate4

    @pl.when(i == 0)
    def _():
        cnt_ref[...] = jnp.zeros(cnt_ref.shape, _F32)

    cnt_ref[0:1, :] += jnp.sum(sel, axis=0, keepdims=True)


def _merge(xa, mod, g1, g2, w_g, fm_lat, fm_ctx, at_lat, at_ctx, w_fo, w_ao, w_out, rw, rb,
           *, n, t_lat, n_batch):
    tt, d = xa.shape
    fd = fm_lat.shape[1]
    nt = tt // TM
    nxt = t_lat // TM
    row = lambda i: (i, 0)
    lat = lambda i: (jnp.minimum(i, nxt - 1), 0)
    cxt = lambda i: (jnp.maximum(i - nxt, 0), 0)
    const = lambda a: pl.BlockSpec(a.shape, lambda i: (0, 0))
    mod_idx = lambda i: (jnp.minimum(i * TM // n, n_batch), 0, 0)
    return pl.pallas_call(
        functools.partial(_merge_kernel, nxt=nxt),
        grid=(nt,),
        in_specs=[pl.BlockSpec((TM, d), row), pl.BlockSpec((1, N_MOD, d), mod_idx),
                  const(g1), const(g2), const(w_g),
                  pl.BlockSpec((TM, fd), lat), pl.BlockSpec((TM, fd), cxt),
                  pl.BlockSpec((TM, fd), lat), pl.BlockSpec((TM, fd), cxt),
                  const(w_fo), const(w_ao), const(w_out), const(rw), const(rb)],
        out_specs=[pl.BlockSpec((TM, d), row), pl.BlockSpec((TM, d // 2), row),
                   pl.BlockSpec((TM, LANES), row), pl.BlockSpec((TM, LANES), row),
                   pl.BlockSpec((8, LANES), lambda i: (0, 0))],
        out_shape=[jax.ShapeDtypeStruct((tt, d), _F32), jax.ShapeDtypeStruct((tt, d // 2), jnp.uint32),
                   jax.ShapeDtypeStruct((tt, LANES), jnp.int32), jax.ShapeDtypeStruct((tt, LANES), _F32),
                   jax.ShapeDtypeStruct((8, LANES), _F32)],
        scratch_shapes=[pltpu.VMEM((TM, d), _BF), pltpu.VMEM((TM, d), _BF)],
        compiler_params=_params("arbitrary"),
        name="merge_router",
    )(xa, mod, g1, g2, w_g, fm_lat, fm_ctx, at_lat, at_ctx, w_fo, w_ao, w_out, rw, rb)


def _route_kernel(idx_ref, ps_ref, d_ref, carry):
    i = pl.program_id(0)

    @pl.when(i == 0)
    def _():
        carry[...] = ps_ref[...]

    idx4 = idx_ref[...]
    lane = lax.broadcasted_iota(jnp.int32, idx4.shape, 1)
    cols = [idx4[:, k:k + 1] for k in range(TOP_K)]
    sel = jnp.zeros(idx4.shape, _F32)
    for k in range(TOP_K):
        sel = sel + jnp.where(lane == cols[k], 1.0, 0.0)
    r = lax.broadcasted_iota(jnp.int32, (TM, TM), 0)
    c = lax.broadcasted_iota(jnp.int32, (TM, TM), 1)
    tri = jnp.where(r > c, 1.0, 0.0).astype(_BF)
    slot = carry[...] + _dot(tri, sel.astype(_BF))
    dest4 = jnp.zeros(idx4.shape, jnp.int32)
    for k in range(TOP_K):
        dk = jnp.sum(jnp.where(lane == cols[k], slot, 0.0), axis=-1, keepdims=True)
        dest4 = jnp.where(lane == k, dk.astype(jnp.int32), dest4)
    d_ref[...] = dest4
    carry[...] += jnp.sum(sel, axis=0, keepdims=True)


def _route(idx4, pad_start):
    tt = idx4.shape[0]
    return pl.pallas_call(
        _route_kernel,
        grid=(tt // TM,),
        in_specs=[pl.BlockSpec((TM, LANES), lambda i: (i, 0)), pl.BlockSpec((1, LANES), lambda i: (0, 0))],
        out_specs=pl.BlockSpec((TM, LANES), lambda i: (i, 0)),
        out_shape=jax.ShapeDtypeStruct((tt, LANES), jnp.int32),
        scratch_shapes=[pltpu.VMEM((1, LANES), _F32)],
        compiler_params=_params("arbitrary"),
        name="route_slots",
    )(idx4, pad_start)


def _dispatch_kernel(pend_ref, cnt_ref, nv_ref, dest_ref, src_ref, xs_ref, zbuf, sem, zsem):
    i = pl.program_id(0)
    n_blocks = xs_ref.shape[0] // TME

    @pl.when(i == 0)
    def _():
        zbuf[...] = jnp.zeros(zbuf.shape, zbuf.dtype)

        def zero_block(start):
            return pltpu.make_async_copy(zbuf, xs_ref.at[pl.ds(pl.multiple_of(start, TME), TME)], zsem)

        for e in range(N_EXPERTS):
            @pl.when(cnt_ref[e] > 0)
            def _():
                zero_block(pend_ref[e] - TME).start()

        def start_unused(j, carry):
            zero_block(j * TME).start()
            return carry

        def wait_unused(j, carry):
            zero_block(j * TME).wait()
            return carry

        lax.fori_loop(nv_ref[0], n_blocks, start_unused, 0)
        for e in range(N_EXPERTS):
            @pl.when(cnt_ref[e] > 0)
            def _():
                zero_block(pend_ref[e] - TME).wait()
        lax.fori_loop(nv_ref[0], n_blocks, wait_unused, 0)

    def body(r, carry):
        for k in range(TOP_K):
            pltpu.make_async_copy(src_ref.at[pl.ds(r, 1)], xs_ref.at[pl.ds(dest_ref[TOP_K * r + k], 1)],
                                  sem).start()
        return carry

    lax.fori_loop(0, TM, body, 0)
    for _ in range(TOP_K):
        pltpu.make_async_copy(src_ref, xs_ref.at[pl.ds(0, TM)], sem).wait()


def _dispatch(pad_end, cnt, n_valid, dest_flat, hp, n_slots):
    tt, w = hp.shape
    grid_spec = pltpu.PrefetchScalarGridSpec(
        num_scalar_prefetch=3,
        grid=(tt // TM,),
        in_specs=[pl.BlockSpec((TOP_K * TM,), lambda i, pe, ct, nv: (i,), memory_space=pltpu.SMEM),
                  pl.BlockSpec((TM, w), lambda i, pe, ct, nv: (i, 0))],
        out_specs=pl.BlockSpec(memory_space=pl.ANY),
        scratch_shapes=[pltpu.VMEM((TME, w), hp.dtype), pltpu.SemaphoreType.DMA, pltpu.SemaphoreType.DMA],
    )
    return pl.pallas_call(
        _dispatch_kernel,
        grid_spec=grid_spec,
        out_shape=jax.ShapeDtypeStruct((n_slots, w), hp.dtype),
        compiler_params=_params("arbitrary"),
        name="moe_dispatch",
    )(pad_end, cnt, n_valid, dest_flat, hp)


def _moe_kernel(be_ref, nv_ref, xs_ref, wgu_ref, bgu_ref, wd_ref, bd_ref, y_ref, wgu_bf, wd_bf):
    i = pl.program_id(0)
    de = wd_ref.shape[2]
    valid = i < nv_ref[0]
    new_expert = jnp.logical_or(i == 0, be_ref[i] != be_ref[jnp.maximum(i - 1, 0)])

    @pl.when(jnp.logical_and(valid, new_expert))
    def _():
        def cast_rows(ref, out, rows):
            def body(j, carry):
                sl = pl.ds(pl.multiple_of(j * CAST_ROWS, CAST_ROWS), CAST_ROWS)
                out[sl, :] = ref[0, 0, sl, :].astype(_BF)
                return carry
            lax.fori_loop(0, rows // CAST_ROWS, body, 0)
        cast_rows(wgu_ref, wgu_bf, wgu_ref.shape[2])
        cast_rows(wd_ref, wd_bf, de)

    @pl.when(valid)
    def _():
        xb = _unpack_bf16_pairs(xs_ref[...])
        gu = _dot(xb, wgu_bf[...]) + bgu_ref[0, 0]
        a = jnp.minimum(gu[:, :de], SWIGLU_LIMIT)
        u = jnp.clip(gu[:, de:], -SWIGLU_LIMIT, SWIGLU_LIMIT)
        act = a * _sigmoid(SWIGLU_ALPHA * a) * (u + 1)
        y_ref[...] = _dot(act.astype(_BF), wd_bf[...]) + bd_ref[0, 0]

    @pl.when(i >= nv_ref[0])
    def _():
        y_ref[...] = jnp.zeros(y_ref.shape, _F32)


def _moe(block_e, n_valid, xs, w_gu, b_gu, w_down, b_down, *, layer):
    n_slots, w = xs.shape
    depth, ne, d, de2 = w_gu.shape
    de = de2 // 2
    n_blocks = n_slots // TME
    ex = lambda i, be, nv: (layer, be[i], 0, 0)
    grid_spec = pltpu.PrefetchScalarGridSpec(
        num_scalar_prefetch=2,
        grid=(n_blocks,),
        in_specs=[pl.BlockSpec((TME, w), lambda i, be, nv: (jnp.minimum(i, nv[0] - 1), 0)),
                  pl.BlockSpec((1, 1, d, de2), ex), pl.BlockSpec((1, 1, 1, de2), ex),
                  pl.BlockSpec((1, 1, de, d), ex), pl.BlockSpec((1, 1, 1, d), ex)],
        out_specs=pl.BlockSpec((TME, d), lambda i, be, nv: (i, 0)),
        scratch_shapes=[pltpu.VMEM((d, de2), _BF), pltpu.VMEM((de, d), _BF)],
    )
    return pl.pallas_call(
        _moe_kernel,
        grid_spec=grid_spec,
        out_shape=jax.ShapeDtypeStruct((n_slots, d), _F32),
        compiler_params=_params("arbitrary"),
        name="moe_experts",
    )(block_e, n_valid, xs, w_gu, b_gu.reshape(depth, ne, 1, de2), w_down, b_down.reshape(depth, ne, 1, d))


def _combine_kernel(dest_ref, dest_next_ref, y_ref, gate_ref, x_ref, mod_ref, fg_ref, o_ref, buf, sem,
                    *, final, n_steps):
    i = pl.program_id(0)
    slot = i % 2

    def gather(d_ref, s):
        def body(r, carry):
            for k in range(TOP_K):
                pltpu.make_async_copy(y_ref.at[pl.ds(d_ref[TOP_K * r + k], 1)], buf.at[s, k, pl.ds(r, 1)],
                                      sem.at[s]).start()
            return carry
        lax.fori_loop(0, TMC, body, 0)

    @pl.when(i == 0)
    def _():
        gather(dest_ref, 0)

    @pl.when(i + 1 < n_steps)
    def _():
        gather(dest_next_ref, 1 - slot)

    for k in range(TOP_K):
        pltpu.make_async_copy(y_ref.at[pl.ds(0, TMC)], buf.at[slot, k], sem.at[slot]).wait()
    g = gate_ref[...]
    moe = g[:, 0:1] * buf[slot, 0]
    for k in range(1, TOP_K):
        moe = moe + g[:, k:k + 1] * buf[slot, k]
    x2 = x_ref[...] + mod_ref[0][5:6] * moe
    if final:
        ms = jnp.mean(x2 * x2, axis=-1, keepdims=True)
        x2 = x2 * lax.rsqrt(ms + EPS) * fg_ref[...]
    o_ref[...] = x2


def _combine(dest_flat, y, gate4, x1, mod, fg, *, n, n_batch, rows, final):
    d = x1.shape[1]
    row = lambda i: (i, 0)
    mod_idx = lambda i: (jnp.minimum(i * TMC // n, n_batch), 0, 0)
    n_steps = rows // TMC
    return pl.pallas_call(
        functools.partial(_combine_kernel, final=final, n_steps=n_steps),
        grid=(n_steps,),
        in_specs=[pl.BlockSpec((TOP_K * TMC,), lambda i: (i,), memory_space=pltpu.SMEM),
                  pl.BlockSpec((TOP_K * TMC,), lambda i: (jnp.minimum(i + 1, n_steps - 1),),
                               memory_space=pltpu.SMEM),
                  pl.BlockSpec(memory_space=pl.ANY),
                  pl.BlockSpec((TMC, LANES), row), pl.BlockSpec((TMC, d), row),
                  pl.BlockSpec((1, N_MOD, d), mod_idx), pl.BlockSpec((1, d), lambda i: (0, 0))],
        out_specs=pl.BlockSpec((TMC, d), row),
        out_shape=jax.ShapeDtypeStruct((rows, d), _F32),
        scratch_shapes=[pltpu.VMEM((2, TOP_K, TMC, d), _F32), pltpu.SemaphoreType.DMA((2,))],
        compiler_params=_params("arbitrary"),
        name="moe_combine",
    )(dest_flat, dest_flat, y, gate4, x1, mod, fg)


def _routing_tables(counts, n_blocks):
    cnt = counts[0, :N_EXPERTS].astype(jnp.int32)
    padded = (cnt + TME - 1) // TME * TME
    pad_end = jnp.cumsum(padded)
    pad_start = pad_end - padded
    ps = jnp.zeros((1, LANES), _F32).at[0, :N_EXPERTS].set(pad_start.astype(_F32))
    blk = jnp.arange(n_blocks, dtype=jnp.int32) * TME
    block_e = jnp.sum((pad_end[None, :] <= blk[:, None]).astype(jnp.int32), axis=1)
    e_last = jnp.max(jnp.where(cnt > 0, jnp.arange(N_EXPERTS, dtype=jnp.int32), 0))
    block_e = jnp.minimum(block_e, e_last).astype(jnp.int32)
    n_valid = (pad_end[-1] // TME).astype(jnp.int32).reshape(1)
    return ps, block_e, n_valid, pad_end.astype(jnp.int32), cnt


def kernel(x, c, ctx, c_ctx, ada_w, ada_b, norm1_g, norm2_g, w_in, attn_sink, w_fourier_out, w_attn_out,
           w_out, router_w, router_b, expert_w_gu, expert_b_gu, expert_w_down, expert_b_down, final_norm_g):
    n_batch, n, d = x.shape
    ctx_len = ctx.shape[1]
    depth = ada_w.shape[0]
    t_lat = n_batch * n
    t_ctx = n_batch * ctx_len
    tt = t_lat + t_ctx
    fd = qd = d // 2
    assert n % TM == 0 and t_ctx % TM == 0 and n % (BLOCK * FB_NB) == 0 and n_batch < MOD_ROWS
    assert t_lat % ctx_len == 0 and TM % ctx_len == 0

    xa = jnp.concatenate([x.reshape(t_lat, d), ctx.reshape(t_ctx, d)], axis=0)
    cc = jnp.concatenate([c, c_ctx[None, :], jnp.zeros((MOD_ROWS - n_batch - 1, d), _F32)], axis=0)
    mod = _ada(cc, ada_w, ada_b).reshape(depth, MOD_ROWS, N_MOD, d)
    cos_t, sin_t = _rope_tables(n)
    consts = _dft_consts(n, ctx_len)

    n_in = fd + qd + 2 * KV_DIM
    w_a = w_in[:, :, :n_in].astype(_BF)
    w_g = w_in[:, :, n_in:].astype(_BF)
    w_fo, w_ao, w_o = w_fourier_out.astype(_BF), w_attn_out.astype(_BF), w_out.astype(_BF)
    rw = jnp.pad(router_w, ((0, 0), (0, 0), (0, LANES - N_EXPERTS))).astype(_BF)
    rb = jnp.pad(router_b, ((0, 0), (0, LANES - N_EXPERTS)), constant_values=NEG_INF)[:, None, :]

    n_blocks = -(-(tt * TOP_K + N_EXPERTS * (TME - 1)) // TME)
    dims = dict(n=n, t_lat=t_lat, n_batch=n_batch)
    for l in range(depth):
        last = l == depth - 1
        f, q, k2, v2 = _proj(xa, mod[l], norm1_g[l][None, :], w_a[l], cos_t, sin_t, **dims)
        fm_lat = _fourier_lat(f, consts, n_batch=n_batch, n=n)
        fm_ctx = _fourier_ctx(f, consts, n_batch=n_batch, ctx_len=ctx_len, t_lat=t_lat)
        at_lat = _attn_lat(attn_sink[l], q, k2, v2, n_batch=n_batch, n=n, ctx_len=ctx_len, t_lat=t_lat)
        at_ctx = _attn_ctx(attn_sink[l], q, k2, v2, n_batch=n_batch, ctx_len=ctx_len, t_lat=t_lat)
        x1, hp, idx4, gate4, counts = _merge(
            xa, mod[l], norm1_g[l][None, :], norm2_g[l][None, :], w_g[l], fm_lat, fm_ctx, at_lat, at_ctx,
            w_fo[l], w_ao[l], w_o[l], rw[l], rb[l], **dims)
        pad_start, block_e, n_valid, pad_end, cnt = _routing_tables(counts, n_blocks)
        dest_flat = _route(idx4, pad_start)[:, :TOP_K].reshape(-1)
        xs = _dispatch(pad_end, cnt, n_valid, dest_flat, hp, n_blocks * TME)
        y = _moe(block_e, n_valid, xs, expert_w_gu, expert_b_gu, expert_w_down, expert_b_down, layer=l)
        xa = _combine(dest_flat, y, gate4, x1, mod[l], final_norm_g[None, :], n=n, n_batch=n_batch,
                      rows=t_lat if last else tt, final=last)
    return xa.reshape(n_batch, n, d)
```

```python
import functools

import numpy as np
import jax
import jax.numpy as jnp
from jax import lax
from jax.experimental import pallas as pl
from jax.experimental.pallas import tpu as pltpu
from jax.experimental.pallas import tpu_sc as plsc

GRID_W = 64
HEAD_DIM = 64
N_KV_HEADS = 2
Q_PER_KV = 4
KV_DIM = N_KV_HEADS * HEAD_DIM
WINDOW = 128
BLOCK = 128
ROPE_THETA = 10000.0
ROPE_FREQS = HEAD_DIM // 4
GROUP_DIM = 128
N_EXPERTS = 32
TOP_K = 4
SWIGLU_LIMIT = 7.0
SWIGLU_ALPHA = 1.702
N_MOD = 6
EPS = 1e-5
NEG_INF = -1e30
LOG2E = 1.4426950408889634
Q_SCALE = HEAD_DIM ** -0.5 * LOG2E

LANES = 128
MOD_ROWS = 8
TM = 512
TME = 512
TMC = 512
SC_CORES = 2
SC_SUBCORES = 16
SC_WINDOW = 128
FA_NB = 16
FB_NB = 16
CAST_ROWS = 64
MERGE_COLS = 256
VMEM_LIMIT = 56 * 1024 * 1024

_BF = jnp.bfloat16
_F32 = jnp.float32


def _params(*sem):
    return pltpu.CompilerParams(dimension_semantics=sem, vmem_limit_bytes=VMEM_LIMIT)


def _dot(a, b):
    return jnp.dot(a, b, preferred_element_type=_F32)


def _sigmoid(x):
    return 0.5 * jnp.tanh(0.5 * x) + 0.5


def _norm_mod(x, g, shift, scale):
    ms = jnp.mean(x * x, axis=-1, keepdims=True)
    return (x * lax.rsqrt(ms + EPS) * g) * (1 + scale) + shift


def _ada_kernel(c_ref, w_ref, b_ref, o_ref):
    c = c_ref[...]
    s = c * jax.nn.sigmoid(c)
    o_ref[0] = jnp.dot(s, w_ref[0], precision=lax.Precision.HIGHEST,
                       preferred_element_type=_F32) + b_ref[0]


def _ada(cc, ada_w, ada_b):
    depth, d, nd = ada_w.shape
    tn = nd // 4
    return pl.pallas_call(
        _ada_kernel,
        grid=(depth, nd // tn),
        in_specs=[pl.BlockSpec((MOD_ROWS, d), lambda l, j: (0, 0)),
                  pl.BlockSpec((1, d, tn), lambda l, j: (l, 0, j)),
                  pl.BlockSpec((1, 1, tn), lambda l, j: (l, 0, j))],
        out_specs=pl.BlockSpec((1, MOD_ROWS, tn), lambda l, j: (l, 0, j)),
        out_shape=jax.ShapeDtypeStruct((depth, MOD_ROWS, nd), _F32),
        compiler_params=_params("arbitrary", "arbitrary"),
        name="ada_mod",
    )(cc, ada_w, ada_b.reshape(depth, 1, nd))


def _rope(v, cos, sin):
    lane = lax.broadcasted_iota(jnp.int32, v.shape, 1)
    partner = jnp.where((lane & ROPE_FREQS) == 0,
                        pltpu.roll(v, LANES - ROPE_FREQS, 1), pltpu.roll(v, ROPE_FREQS, 1))
    return v * cos + partner * sin


def _proj_kernel(x_ref, mod_ref, g_ref, w_ref, cos_ref, sin_ref, f_ref, q_ref, k_ref, vt_ref, *, fd, qd):
    m = mod_ref[0]
    h = _norm_mod(x_ref[...], g_ref[...], m[0:1], m[1:2]).astype(_BF)
    p = _dot(h, w_ref[...])
    cos = cos_ref[...]
    sin = sin_ref[...]
    f_ref[...] = p[:, :fd].astype(_BF)
    for j in range(qd // LANES):
        lo = fd + j * LANES
        q_ref[:, j * LANES:(j + 1) * LANES] = (
            _rope(p[:, lo:lo + LANES], cos, sin) * Q_SCALE).astype(_BF)
    k = _rope(p[:, fd + qd:fd + qd + KV_DIM], cos, sin)
    v = p[:, fd + qd + KV_DIM:fd + qd + 2 * KV_DIM]
    k_ref[:, :KV_DIM] = k.astype(_BF)
    k_ref[:, KV_DIM:] = pltpu.roll(k, HEAD_DIM, 1).astype(_BF)
    vt_ref[:KV_DIM, :] = v.T.astype(_BF)
    vt_ref[KV_DIM:, :] = pltpu.roll(v, HEAD_DIM, 1).T.astype(_BF)


def _proj(xa, mod, g, w_a, cos_t, sin_t, *, n, t_lat, n_batch):
    tt, d = xa.shape
    fd = qd = d // 2
    nt = tt // TM
    nxt = t_lat // TM
    per_seq = n // TM
    mod_idx = lambda i: (jnp.minimum(i * TM // n, n_batch), 0, 0)
    rope_idx = lambda i: (jnp.where(i < nxt, i % per_seq, per_seq), 0)
    row = lambda i: (i, 0)
    return pl.pallas_call(
        functools.partial(_proj_kernel, fd=fd, qd=qd),
        grid=(nt,),
        in_specs=[pl.BlockSpec((TM, d), row),
                  pl.BlockSpec((1, N_MOD, d), mod_idx),
                  pl.BlockSpec((1, d), lambda i: (0, 0)),
                  pl.BlockSpec(w_a.shape, lambda i: (0, 0)),
                  pl.BlockSpec((TM, LANES), rope_idx),
                  pl.BlockSpec((TM, LANES), rope_idx)],
        out_specs=[pl.BlockSpec((TM, fd), row), pl.BlockSpec((TM, qd), row),
                   pl.BlockSpec((TM, 2 * KV_DIM), row), pl.BlockSpec((2 * KV_DIM, TM), lambda i: (0, i))],
        out_shape=[jax.ShapeDtypeStruct((tt, fd), _BF), jax.ShapeDtypeStruct((tt, qd), _BF),
                   jax.ShapeDtypeStruct((tt, 2 * KV_DIM), _BF), jax.ShapeDtypeStruct((2 * KV_DIM, tt), _BF)],
        compiler_params=_params("arbitrary"),
        name="proj_in",
    )(xa, mod, g, w_a, cos_t, sin_t)


def _rope_tables(n):
    pos = jnp.arange(n)
    inv = ROPE_THETA ** (-jnp.arange(ROPE_FREQS, dtype=_F32) / ROPE_FREQS)
    ar = (pos // GRID_W).astype(_F32)[:, None] * inv
    ac = (pos % GRID_W).astype(_F32)[:, None] * inv
    cos = jnp.concatenate([jnp.cos(ar), jnp.cos(ar), jnp.cos(ac), jnp.cos(ac)], axis=1)
    sin = jnp.concatenate([-jnp.sin(ar), jnp.sin(ar), -jnp.sin(ac), jnp.sin(ac)], axis=1)
    cos = jnp.tile(cos, (1, LANES // HEAD_DIM))
    sin = jnp.tile(sin, (1, LANES // HEAD_DIM))
    cos = jnp.concatenate([cos, jnp.ones((TM, LANES), _F32)], axis=0)
    sin = jnp.concatenate([sin, jnp.zeros((TM, LANES), _F32)], axis=0)
    return cos, sin


def _dft_consts(n, ctx_len):
    n1 = n // BLOCK
    b = np.arange(BLOCK, dtype=np.int64)[:, None, None]
    k1 = np.arange(n1, dtype=np.int64)[None, :, None]
    a = np.arange(n1, dtype=np.int64)[None, None, :]
    ang = 2.0 * np.pi * ((a * k1 * BLOCK + b * k1) % n).astype(np.float64) / n
    m_ri = np.concatenate([np.cos(ang), -np.sin(ang)], axis=1)
    kk = np.arange(GROUP_DIM, dtype=np.int64)
    ang_c = 2.0 * np.pi * ((kk[:, None] * kk[None, :]) % GROUP_DIM) / GROUP_DIM
    c, s = np.cos(ang_c), np.sin(ang_c)
    cs = np.concatenate([np.concatenate([c, s], axis=1), np.concatenate([-s, c], axis=1)], axis=0)
    cc = np.concatenate([c, s], axis=0)
    kl = np.arange(ctx_len, dtype=np.int64)
    ang_l = 2.0 * np.pi * ((kl[:, None] * kl[None, :]) % ctx_len) / ctx_len
    wl = np.concatenate([np.cos(ang_l), -np.sin(ang_l)], axis=0)
    as_bf = lambda v: jnp.asarray(v, dtype=_F32).astype(_BF)
    return dict(m_ri=as_bf(m_ri), cs=as_bf(cs), cc=as_bf(cc), wl=as_bf(wl))


def _chan_dft(xr, xi, cc, scale):
    outs = []
    for g in range(xr.shape[1] // GROUP_DIM):
        sl = slice(g * GROUP_DIM, (g + 1) * GROUP_DIM)
        xx = jnp.concatenate([xr[:, sl], xi[:, sl]], axis=1).astype(_BF)
        outs.append(_dot(xx, cc) * scale)
    return jnp.concatenate(outs, axis=1)


def _fa_kernel(f_ref, m_ref, zr_ref, zi_ref):
    n1 = f_ref.shape[0]
    for t in range(FA_NB):
        zz = _dot(m_ref[t], f_ref[:, t, :])
        zr_ref[:, t, :] = zz[:n1].astype(_BF)
        zi_ref[:, t, :] = zz[n1:].astype(_BF)


def _fb_kernel(zr_ref, zi_ref, cs_ref, cc_ref, o_ref, *, scale):
    cs, cc = cs_ref[...], cc_ref[...]
    for j in range(FB_NB):
        xx = _dot(cs, jnp.concatenate([zr_ref[j], zi_ref[j]], axis=0))
        o_ref[:, j, :] = _chan_dft(xx[:BLOCK], xx[BLOCK:], cc, scale).astype(_BF)


def _fourier_lat(f, consts, *, n_batch, n):
    tt, fd = f.shape
    n1 = n // BLOCK
    f3 = f.reshape(tt // BLOCK, BLOCK, fd)
    blk_a = pl.BlockSpec((n1, FA_NB, fd), lambda b, j: (b, j, 0))
    mat_a = pl.BlockSpec((FA_NB, 2 * n1, n1), lambda b, j: (j, 0, 0))
    z_shape = jax.ShapeDtypeStruct((n_batch * n1, BLOCK, fd), _BF)
    zr, zi = pl.pallas_call(
        _fa_kernel,
        grid=(n_batch, BLOCK // FA_NB),
        in_specs=[blk_a, mat_a],
        out_specs=[blk_a, blk_a],
        out_shape=[z_shape, z_shape],
        compiler_params=_params("arbitrary", "arbitrary"),
        name="fourier_seq_a",
    )(f3, consts["m_ri"])
    blk_z = pl.BlockSpec((FB_NB, BLOCK, fd), lambda b, j: (b * (n1 // FB_NB) + j, 0, 0))
    const = lambda shape: pl.BlockSpec(shape, lambda b, j: (0, 0))
    out = pl.pallas_call(
        functools.partial(_fb_kernel, scale=float((n * GROUP_DIM) ** -0.5)),
        grid=(n_batch, n1 // FB_NB),
        in_specs=[blk_z, blk_z, const((2 * BLOCK, 2 * BLOCK)), const((2 * GROUP_DIM, GROUP_DIM))],
        out_specs=pl.BlockSpec((BLOCK, FB_NB, fd), lambda b, j: (b, j, 0)),
        out_shape=jax.ShapeDtypeStruct((n_batch * BLOCK, n1, fd), _BF),
        compiler_params=_params("arbitrary", "arbitrary"),
        name="fourier_seq_b",
    )(zr, zi, consts["cs"], consts["cc"])
    return out.reshape(n_batch * n, fd)


def _fc_kernel(f_ref, wl_ref, cc_ref, o_ref, *, scale):
    ctx_len = f_ref.shape[0]
    xx = _dot(wl_ref[...], f_ref[...])
    o_ref[...] = _chan_dft(xx[:ctx_len], xx[ctx_len:], cc_ref[...], scale).astype(_BF)


def _fourier_ctx(f, consts, *, n_batch, ctx_len, t_lat):
    fd = f.shape[1]
    const = lambda shape: pl.BlockSpec(shape, lambda b: (0, 0))
    return pl.pallas_call(
        functools.partial(_fc_kernel, scale=float((ctx_len * GROUP_DIM) ** -0.5)),
        grid=(n_batch,),
        in_specs=[pl.BlockSpec((ctx_len, fd), lambda b: (t_lat // ctx_len + b, 0)),
                  const((2 * ctx_len, ctx_len)), const((2 * GROUP_DIM, GROUP_DIM))],
        out_specs=pl.BlockSpec((ctx_len, fd), lambda b: (b, 0)),
        out_shape=jax.ShapeDtypeStruct((n_batch * ctx_len, fd), _BF),
        compiler_params=_params("arbitrary"),
        name="fourier_ctx",
    )(f, consts["wl"], consts["cc"])


def _attn_core(sink_ref, q_ref, k, vt, o_ref, bias, s_scr, p_scr):
    tq = q_ref.shape[0]
    lane = lax.broadcasted_iota(jnp.int32, (tq, LANES), 1)
    low = lane < HEAD_DIM
    zero = jnp.zeros((tq, LANES), _BF)
    low_row = lax.broadcasted_iota(jnp.int32, (LANES, 2 * tq), 0) < HEAD_DIM
    first = lax.broadcasted_iota(jnp.int32, (1, 2 * tq), 1) < tq
    pairs = [(kvh, half) for kvh in range(N_KV_HEADS) for half in range(2)]
    sels = [slice(0, LANES) if half == kvh else slice(LANES, 2 * LANES) for kvh, half in pairs]
    for pi, (kvh, half) in enumerate(pairs):
        slabs = [q_ref[:, (2 * kvh + c) * LANES:(2 * kvh + c + 1) * LANES] for c in range(2)]
        keep = low if half == 0 else jnp.logical_not(low)
        qm = jnp.concatenate([jnp.where(keep, sl, zero) for sl in slabs], axis=0)
        s = lax.dot_general(k[:, sels[pi]], qm, (((1,), (1,)), ((), ())), preferred_element_type=_F32)
        s_scr[pi] = s if bias is None else s + bias
    dens = []
    for pi, (kvh, half) in enumerate(pairs):
        hq = Q_PER_KV * kvh + half
        sk = jnp.where(first, sink_ref[hq], sink_ref[hq + 2]) * LOG2E
        s = s_scr[pi]
        m = jnp.maximum(jnp.max(s, axis=0, keepdims=True), sk)
        p = jnp.exp2(s - m)
        dens.append(jnp.sum(p, axis=0, keepdims=True) + jnp.exp2(sk - m))
        p_scr[pi] = p.astype(_BF)
    outs = [_dot(vt[sels[pi], :], p_scr[pi]) * (1.0 / dens[pi]) for pi in range(len(pairs))]
    for kvh in range(N_KV_HEADS):
        ot = jnp.where(low_row, outs[2 * kvh], outs[2 * kvh + 1])
        for c in range(2):
            o_ref[:, (2 * kvh + c) * LANES:(2 * kvh + c + 1) * LANES] = ot[:, c * tq:(c + 1) * tq].T.astype(_BF)


def _attn_scratch(keys, tq):
    return [pltpu.VMEM((2 * N_KV_HEADS, keys, 2 * tq), _F32), pltpu.VMEM((2 * N_KV_HEADS, keys, 2 * tq), _BF)]


def _attn_lat_kernel(sink_ref, q_ref, kp_ref, kc_ref, kn_ref, vp_ref, vc_ref, vn_ref, kx_ref, vx_ref, bias_ref,
                     o_ref, s_scr, p_scr):
    k = jnp.concatenate([kp_ref[...], kc_ref[...], kn_ref[...], kx_ref[...]], axis=0)
    vt = jnp.concatenate([vp_ref[...], vc_ref[...], vn_ref[...], vx_ref[...]], axis=1)
    _attn_core(sink_ref, q_ref, k, vt, o_ref, bias_ref[0], s_scr, p_scr)


def _window_bias(ctx_len):
    s = np.arange(3 * BLOCK + ctx_len)[:, None]
    r = np.arange(2 * BLOCK)[None, :] % BLOCK
    out = []
    for v in range(4):
        lo = 0 if v & 1 else BLOCK
        hi = 3 * BLOCK if v & 2 else 2 * BLOCK
        ok = ((np.abs(s - BLOCK - r) <= WINDOW) & (s >= lo) & (s < hi)) | (s >= 3 * BLOCK)
        out.append(np.where(ok, 0.0, NEG_INF))
    return jnp.asarray(np.stack(out), dtype=_F32)


def _attn_ctx_kernel(sink_ref, q_ref, kx_ref, vx_ref, o_ref, s_scr, p_scr):
    _attn_core(sink_ref, q_ref, kx_ref[...], vx_ref[...], o_ref, None, s_scr, p_scr)


def _attn_lat(sink, q, k2, v2t, *, n_batch, n, ctx_len, t_lat):
    qd = q.shape[1]
    nb = n // BLOCK
    cur = lambda b, i: (b * nb + i, 0)
    prev = lambda b, i: (b * nb + jnp.maximum(i - 1, 0), 0)
    nxt = lambda b, i: (b * nb + jnp.minimum(i + 1, nb - 1), 0)
    cx = lambda b, i: (t_lat // ctx_len + b, 0)
    swap = lambda im: (lambda b, i: im(b, i)[::-1])
    kb = lambda im: pl.BlockSpec((BLOCK, 2 * KV_DIM), im)
    vb = lambda im: pl.BlockSpec((2 * KV_DIM, BLOCK), swap(im))
    bias = _window_bias(ctx_len)
    bias_idx = lambda b, i: ((i > 0).astype(jnp.int32) + 2 * (i < nb - 1).astype(jnp.int32), 0, 0)
    return pl.pallas_call(
        _attn_lat_kernel,
        grid=(n_batch, nb),
        in_specs=[pl.BlockSpec(memory_space=pltpu.SMEM), pl.BlockSpec((BLOCK, qd), cur),
                  kb(prev), kb(cur), kb(nxt), vb(prev), vb(cur), vb(nxt),
                  pl.BlockSpec((ctx_len, 2 * KV_DIM), cx), pl.BlockSpec((2 * KV_DIM, ctx_len), swap(cx)),
                  pl.BlockSpec((1,) + bias.shape[1:], bias_idx)],
        out_specs=pl.BlockSpec((BLOCK, qd), cur),
        out_shape=jax.ShapeDtypeStruct((t_lat, qd), _BF),
        scratch_shapes=_attn_scratch(3 * BLOCK + ctx_len, BLOCK),
        compiler_params=_params("arbitrary", "arbitrary"),
        name="attn_lat",
    )(sink, q, k2, k2, k2, v2t, v2t, v2t, k2, v2t, bias)


def _attn_ctx(sink, q, k2, v2t, *, n_batch, ctx_len, t_lat):
    qd = q.shape[1]
    cx = lambda b: (t_lat // ctx_len + b, 0)
    return pl.pallas_call(
        _attn_ctx_kernel,
        grid=(n_batch,),
        in_specs=[pl.BlockSpec(memory_space=pltpu.SMEM), pl.BlockSpec((ctx_len, qd), cx),
                  pl.BlockSpec((ctx_len, 2 * KV_DIM), cx),
                  pl.BlockSpec((2 * KV_DIM, ctx_len), lambda b: (0, t_lat // ctx_len + b))],
        out_specs=pl.BlockSpec((ctx_len, qd), lambda b: (b, 0)),
        out_shape=jax.ShapeDtypeStruct((n_batch * ctx_len, qd), _BF),
        scratch_shapes=_attn_scratch(ctx_len, ctx_len),
        compiler_params=_params("arbitrary"),
        name="attn_ctx",
    )(sink, q, k2, v2t)


def _pack_bf16_pairs(h):
    half = h.shape[1] // 2
    lo = lax.bitcast_convert_type(h[:, :half].astype(_BF).astype(_F32), jnp.uint32)
    hi = lax.bitcast_convert_type(h[:, half:].astype(_BF).astype(_F32), jnp.uint32)
    return (lo >> 16) | hi


def _unpack_bf16_pairs(p):
    lo = lax.bitcast_convert_type(p << 16, _F32)
    hi = lax.bitcast_convert_type(p & jnp.uint32(0xFFFF0000), _F32)
    return jnp.concatenate([lo, hi], axis=1).astype(_BF)


def _merge_kernel(x_ref, mod_ref, g1_ref, g2_ref, wg_ref, fml_ref, fmc_ref, atl_ref, atc_ref,
                  wfo_ref, wao_ref, wout_ref, rw_ref, rb_ref,
                  x1_ref, hp_ref, idx_ref, gate_ref, cnt_ref, h_scr, y_scr, *, nxt):
    i = pl.program_id(0)
    d = x_ref.shape[1]
    m = mod_ref[0]
    h_scr[...] = _norm_mod(x_ref[...], g1_ref[...], m[0:1], m[1:2]).astype(_BF)
    is_lat = i < nxt
    fm = jnp.where(is_lat, fml_ref[...], fmc_ref[...])
    at = jnp.where(is_lat, atl_ref[...], atc_ref[...])
    for c in range(d // MERGE_COLS):
        sl = slice(c * MERGE_COLS, (c + 1) * MERGE_COLS)
        sg = slice(d + c * MERGE_COLS, d + (c + 1) * MERGE_COLS)
        h = h_scr[...]
        y = (_sigmoid(_dot(h, wg_ref[:, sl])) * _dot(fm, wfo_ref[:, sl])
             + _sigmoid(_dot(h, wg_ref[:, sg])) * _dot(at, wao_ref[:, sl]))
        y_scr[:, sl] = y.astype(_BF)
    x1_ref[...] = x_ref[...] + m[2:3] * _dot(y_scr[...], wout_ref[...])
    h2 = _norm_mod(x1_ref[...], g2_ref[...], m[3:4], m[4:5])
    hp_ref[...] = _pack_bf16_pairs(h2)
    logits = _dot(h2.astype(_BF), rw_ref[...]) + rb_ref[...]
    lane = lax.broadcasted_iota(jnp.int32, logits.shape, 1)
    vals, idxs = [], []
    for _ in range(TOP_K):
        mx = jnp.max(logits, axis=-1, keepdims=True)
        ix = jnp.min(jnp.where(logits == mx, lane, LANES), axis=-1, keepdims=True)
        logits = jnp.where(lane == ix, NEG_INF, logits)
        vals.append(mx)
        idxs.append(ix)
    es = [jnp.exp(v - vals[0]) for v in vals]
    den = es[0] + es[1] + es[2] + es[3]
    idx4 = jnp.zeros(logits.shape, jnp.int32)
    gate4 = jnp.zeros(logits.shape, _F32)
    sel = jnp.zeros(logits.shape, _F32)
    for k in range(TOP_K):
        idx4 = jnp.where(lane == k, idxs[k], idx4)
        gate4 = jnp.where(lane == k, es[k] / den, gate4)
        sel = sel + jnp.where(lane == idxs[k], 1.0, 0.0)
    idx_ref[...] = idx4
    gate_ref[...] = gate4

    @pl.when(i == 0)
    def _():
        cnt_ref[...] = jnp.zeros(cnt_ref.shape, _F32)

    cnt_ref[0:1, :] += jnp.sum(sel, axis=0, keepdims=True)


def _merge(xa, mod, g1, g2, w_g, fm_lat, fm_ctx, at_lat, at_ctx, w_fo, w_ao, w_out, rw, rb,
           *, n, t_lat, n_batch):
    tt, d = xa.shape
    fd = fm_lat.shape[1]
    nt = tt // TM
    nxt = t_lat // TM
    row = lambda i: (i, 0)
    lat = lambda i: (jnp.minimum(i, nxt - 1), 0)
    cxt = lambda i: (jnp.maximum(i - nxt, 0), 0)
    const = lambda a: pl.BlockSpec(a.shape, lambda i: (0, 0))
    mod_idx = lambda i: (jnp.minimum(i * TM // n, n_batch), 0, 0)
    return pl.pallas_call(
        functools.partial(_merge_kernel, nxt=nxt),
        grid=(nt,),
        in_specs=[pl.BlockSpec((TM, d), row), pl.BlockSpec((1, N_MOD, d), mod_idx),
                  const(g1), const(g2), const(w_g),
                  pl.BlockSpec((TM, fd), lat), pl.BlockSpec((TM, fd), cxt),
                  pl.BlockSpec((TM, fd), lat), pl.BlockSpec((TM, fd), cxt),
                  const(w_fo), const(w_ao), const(w_out), const(rw), const(rb)],
        out_specs=[pl.BlockSpec((TM, d), row), pl.BlockSpec((TM, d // 2), row),
                   pl.BlockSpec((TM, LANES), row), pl.BlockSpec((TM, LANES), row),
                   pl.BlockSpec((8, LANES), lambda i: (0, 0))],
        out_shape=[jax.ShapeDtypeStruct((tt, d), _F32), jax.ShapeDtypeStruct((tt, d // 2), jnp.uint32),
                   jax.ShapeDtypeStruct((tt, LANES), jnp.int32), jax.ShapeDtypeStruct((tt, LANES), _F32),
                   jax.ShapeDtypeStruct((8, LANES), _F32)],
        scratch_shapes=[pltpu.VMEM((TM, d), _BF), pltpu.VMEM((TM, d), _BF)],
        compiler_params=_params("arbitrary"),
        name="merge_router",
    )(xa, mod, g1, g2, w_g, fm_lat, fm_ctx, at_lat, at_ctx, w_fo, w_ao, w_out, rw, rb)


def _route_kernel(idx_ref, ps_ref, d_ref, carry):
    i = pl.program_id(0)

    @pl.when(i == 0)
    def _():
        carry[...] = ps_ref[...]

    idx4 = idx_ref[...]
    lane = lax.broadcasted_iota(jnp.int32, idx4.shape, 1)
    cols = [idx4[:, k:k + 1] for k in range(TOP_K)]
    sel = jnp.zeros(idx4.shape, _F32)
    for k in range(TOP_K):
        sel = sel + jnp.where(lane == cols[k], 1.0, 0.0)
    r = lax.broadcasted_iota(jnp.int32, (TM, TM), 0)
    c = lax.broadcasted_iota(jnp.int32, (TM, TM), 1)
    tri = jnp.where(r > c, 1.0, 0.0).astype(_BF)
    slot = carry[...] + _dot(tri, sel.astype(_BF))
    dest4 = jnp.zeros(idx4.shape, jnp.int32)
    for k in range(TOP_K):
        dk = jnp.sum(jnp.where(lane == cols[k], slot, 0.0), axis=-1, keepdims=True)
        dest4 = jnp.where(lane == k, dk.astype(jnp.int32), dest4)
    d_ref[...] = dest4
    carry[...] += jnp.sum(sel, axis=0, keepdims=True)


def _route(idx4, pad_start):
    tt = idx4.shape[0]
    return pl.pallas_call(
        _route_kernel,
        grid=(tt // TM,),
        in_specs=[pl.BlockSpec((TM, LANES), lambda i: (i, 0)), pl.BlockSpec((1, LANES), lambda i: (0, 0))],
        out_specs=pl.BlockSpec((TM, LANES), lambda i: (i, 0)),
        out_shape=jax.ShapeDtypeStruct((tt, LANES), jnp.int32),
        scratch_shapes=[pltpu.VMEM((1, LANES), _F32)],
        compiler_params=_params("arbitrary"),
        name="route_slots",
    )(idx4, pad_start)


def _dispatch_kernel(pend_ref, cnt_ref, nv_ref, dest_ref, src_ref, xs_ref, zbuf, sem, zsem):
    i = pl.program_id(0)
    n_blocks = xs_ref.shape[0] // TME

    @pl.when(i == 0)
    def _():
        zbuf[...] = jnp.zeros(zbuf.shape, zbuf.dtype)

        def zero_block(start):
            return pltpu.make_async_copy(zbuf, xs_ref.at[pl.ds(pl.multiple_of(start, TME), TME)], zsem)

        for e in range(N_EXPERTS):
            @pl.when(cnt_ref[e] > 0)
            def _():
                zero_block(pend_ref[e] - TME).start()

        def start_unused(j, carry):
            zero_block(j * TME).start()
            return carry

        def wait_unused(j, carry):
            zero_block(j * TME).wait()
            return carry

        lax.fori_loop(nv_ref[0], n_blocks, start_unused, 0)
        for e in range(N_EXPERTS):
            @pl.when(cnt_ref[e] > 0)
            def _():
                zero_block(pend_ref[e] - TME).wait()
        lax.fori_loop(nv_ref[0], n_blocks, wait_unused, 0)

    def body(r, carry):
        for k in range(TOP_K):
            pltpu.make_async_copy(src_ref.at[pl.ds(r, 1)], xs_ref.at[pl.ds(dest_ref[TOP_K * r + k], 1)],
                                  sem).start()
        return carry

    lax.fori_loop(0, TM, body, 0)
    for _ in range(TOP_K):
        pltpu.make_async_copy(src_ref, xs_ref.at[pl.ds(0, TM)], sem).wait()


def _dispatch(pad_end, cnt, n_valid, dest_flat, hp, n_slots):
    tt, w = hp.shape
    grid_spec = pltpu.PrefetchScalarGridSpec(
        num_scalar_prefetch=3,
        grid=(tt // TM,),
        in_specs=[pl.BlockSpec((TOP_K * TM,), lambda i, pe, ct, nv: (i,), memory_space=pltpu.SMEM),
                  pl.BlockSpec((TM, w), lambda i, pe, ct, nv: (i, 0))],
        out_specs=pl.BlockSpec(memory_space=pl.ANY),
        scratch_shapes=[pltpu.VMEM((TME, w), hp.dtype), pltpu.SemaphoreType.DMA, pltpu.SemaphoreType.DMA],
    )
    return pl.pallas_call(
        _dispatch_kernel,
        grid_spec=grid_spec,
        out_shape=jax.ShapeDtypeStruct((n_slots, w), hp.dtype),
        compiler_params=_params("arbitrary"),
        name="moe_dispatch",
    )(pad_end, cnt, n_valid, dest_flat, hp)


def _moe_kernel(be_ref, nv_ref, xs_ref, wgu_ref, bgu_ref, wd_ref, bd_ref, y_ref, wgu_bf, wd_bf):
    i = pl.program_id(0)
    de = wd_ref.shape[2]
    valid = i < nv_ref[0]
    new_expert = jnp.logical_or(i == 0, be_ref[i] != be_ref[jnp.maximum(i - 1, 0)])

    @pl.when(jnp.logical_and(valid, new_expert))
    def _():
        def cast_rows(ref, out, rows):
            def body(j, carry):
                sl = pl.ds(pl.multiple_of(j * CAST_ROWS, CAST_ROWS), CAST_ROWS)
                out[sl, :] = ref[0, 0, sl, :].astype(_BF)
                return carry
            lax.fori_loop(0, rows // CAST_ROWS, body, 0)
        cast_rows(wgu_ref, wgu_bf, wgu_ref.shape[2])
        cast_rows(wd_ref, wd_bf, de)

    @pl.when(valid)
    def _():
        xb = _unpack_bf16_pairs(xs_ref[...])
        gu = _dot(xb, wgu_bf[...]) + bgu_ref[0, 0]
        a = jnp.minimum(gu[:, :de], SWIGLU_LIMIT)
        u = jnp.clip(gu[:, de:], -SWIGLU_LIMIT, SWIGLU_LIMIT)
        act = a * _sigmoid(SWIGLU_ALPHA * a) * (u + 1)
        y_ref[...] = _pack_bf16_pairs(_dot(act.astype(_BF), wd_bf[...]) + bd_ref[0, 0])

    @pl.when(i >= nv_ref[0])
    def _():
        y_ref[...] = jnp.zeros(y_ref.shape, y_ref.dtype)


def _moe(block_e, n_valid, xs, w_gu, b_gu, w_down, b_down, *, layer):
    n_slots, w = xs.shape
    depth, ne, d, de2 = w_gu.shape
    de = de2 // 2
    n_blocks = n_slots // TME
    ex = lambda i, be, nv: (layer, be[i], 0, 0)
    grid_spec = pltpu.PrefetchScalarGridSpec(
        num_scalar_prefetch=2,
        grid=(n_blocks,),
        in_specs=[pl.BlockSpec((TME, w), lambda i, be, nv: (jnp.minimum(i, nv[0] - 1), 0)),
                  pl.BlockSpec((1, 1, d, de2), ex), pl.BlockSpec((1, 1, 1, de2), ex),
                  pl.BlockSpec((1, 1, de, d), ex), pl.BlockSpec((1, 1, 1, d), ex)],
        out_specs=pl.BlockSpec((TME, d // 2), lambda i, be, nv: (i, 0)),
        scratch_shapes=[pltpu.VMEM((d, de2), _BF), pltpu.VMEM((de, d), _BF)],
    )
    return pl.pallas_call(
        _moe_kernel,
        grid_spec=grid_spec,
        out_shape=jax.ShapeDtypeStruct((n_slots, d // 2), jnp.uint32),
        compiler_params=_params("arbitrary"),
        name="moe_experts",
    )(block_e, n_valid, xs, w_gu, b_gu.reshape(depth, ne, 1, de2), w_down, b_down.reshape(depth, ne, 1, d))


def _sc_gather(table, idx):
    n_idx, (_, w) = idx.shape[0], table.shape
    workers = SC_CORES * SC_SUBCORES
    per_worker = n_idx // workers
    assert n_idx % workers == 0 and per_worker % SC_WINDOW == 0
    mesh = plsc.VectorSubcoreMesh(core_axis_name="core", subcore_axis_name="subcore")

    @pl.kernel(out_type=jax.ShapeDtypeStruct((n_idx, w), table.dtype), mesh=mesh,
               scratch_types=[pltpu.VMEM((SC_WINDOW,), jnp.int32), pltpu.VMEM((SC_WINDOW, w), table.dtype),
                              pltpu.SemaphoreType.DMA])
    def gather(t_hbm, i_hbm, o_hbm, idx_v, rows_v, sem):
        wid = lax.axis_index("subcore") * SC_CORES + lax.axis_index("core")

        @pl.loop(0, per_worker // SC_WINDOW)
        def _(c):
            base = wid * per_worker + c * SC_WINDOW
            pltpu.sync_copy(i_hbm.at[pl.ds(base, SC_WINDOW)], idx_v)
            pltpu.async_copy(t_hbm.at[idx_v], rows_v, sem).wait()
            pltpu.sync_copy(rows_v, o_hbm.at[pl.ds(base, SC_WINDOW)])

    return gather(table, idx)


def _combine_kernel(yg_ref, gate_ref, x_ref, mod_ref, fg_ref, o_ref, *, final):
    g = gate_ref[...]
    half = x_ref.shape[1] // 2
    lo = jnp.zeros((x_ref.shape[0], half), _F32)
    hi = jnp.zeros((x_ref.shape[0], half), _F32)
    for k in range(TOP_K):
        p = yg_ref[k]
        lo = lo + g[:, k:k + 1] * lax.bitcast_convert_type(p << 16, _F32)
        hi = hi + g[:, k:k + 1] * lax.bitcast_convert_type(p & jnp.uint32(0xFFFF0000), _F32)
    x2 = x_ref[...] + mod_ref[0][5:6] * jnp.concatenate([lo, hi], axis=1)
    if final:
        ms = jnp.mean(x2 * x2, axis=-1, keepdims=True)
        x2 = x2 * lax.rsqrt(ms + EPS) * fg_ref[...]
    o_ref[...] = x2


def _combine(yg, gate4, x1, mod, fg, *, n, n_batch, rows, final):
    d = x1.shape[1]
    row = lambda i: (i, 0)
    mod_idx = lambda i: (jnp.minimum(i * TMC // n, n_batch), 0, 0)
    return pl.pallas_call(
        functools.partial(_combine_kernel, final=final),
        grid=(rows // TMC,),
        in_specs=[pl.BlockSpec((TOP_K, TMC, d // 2), lambda i: (0, i, 0)),
                  pl.BlockSpec((TMC, LANES), row), pl.BlockSpec((TMC, d), row),
                  pl.BlockSpec((1, N_MOD, d), mod_idx), pl.BlockSpec((1, d), lambda i: (0, 0))],
        out_specs=pl.BlockSpec((TMC, d), row),
        out_shape=jax.ShapeDtypeStruct((rows, d), _F32),
        compiler_params=_params("arbitrary"),
        name="moe_combine",
    )(yg, gate4, x1, mod, fg)


def _routing_tables(counts, n_blocks):
    cnt = counts[0, :N_EXPERTS].astype(jnp.int32)
    padded = (cnt + TME - 1) // TME * TME
    pad_end = jnp.cumsum(padded)
    pad_start = pad_end - padded
    ps = jnp.zeros((1, LANES), _F32).at[0, :N_EXPERTS].set(pad_start.astype(_F32))
    blk = jnp.arange(n_blocks, dtype=jnp.int32) * TME
    block_e = jnp.sum((pad_end[None, :] <= blk[:, None]).astype(jnp.int32), axis=1)
    e_last = jnp.max(jnp.where(cnt > 0, jnp.arange(N_EXPERTS, dtype=jnp.int32), 0))
    block_e = jnp.minimum(block_e, e_last).astype(jnp.int32)
    n_valid = (pad_end[-1] // TME).astype(jnp.int32).reshape(1)
    return ps, block_e, n_valid, pad_end.astype(jnp.int32), cnt


def kernel(x, c, ctx, c_ctx, ada_w, ada_b, norm1_g, norm2_g, w_in, attn_sink, w_fourier_out, w_attn_out,
           w_out, router_w, router_b, expert_w_gu, expert_b_gu, expert_w_down, expert_b_down, final_norm_g):
    n_batch, n, d = x.shape
    ctx_len = ctx.shape[1]
    depth = ada_w.shape[0]
    t_lat = n_batch * n
    t_ctx = n_batch * ctx_len
    tt = t_lat + t_ctx
    fd = qd = d // 2
    assert n % TM == 0 and t_ctx % TM == 0 and n % (BLOCK * FB_NB) == 0 and n_batch < MOD_ROWS
    assert t_lat % ctx_len == 0 and TM % ctx_len == 0

    xa = jnp.concatenate([x.reshape(t_lat, d), ctx.reshape(t_ctx, d)], axis=0)
    cc = jnp.concatenate([c, c_ctx[None, :], jnp.zeros((MOD_ROWS - n_batch - 1, d), _F32)], axis=0)
    mod = _ada(cc, ada_w, ada_b).reshape(depth, MOD_ROWS, N_MOD, d)
    cos_t, sin_t = _rope_tables(n)
    consts = _dft_consts(n, ctx_len)

    n_in = fd + qd + 2 * KV_DIM
    w_a = w_in[:, :, :n_in].astype(_BF)
    w_g = w_in[:, :, n_in:].astype(_BF)
    w_fo, w_ao, w_o = w_fourier_out.astype(_BF), w_attn_out.astype(_BF), w_out.astype(_BF)
    rw = jnp.pad(router_w, ((0, 0), (0, 0), (0, LANES - N_EXPERTS))).astype(_BF)
    rb = jnp.pad(router_b, ((0, 0), (0, LANES - N_EXPERTS)), constant_values=NEG_INF)[:, None, :]

    n_blocks = -(-(tt * TOP_K + N_EXPERTS * (TME - 1)) // TME)
    dims = dict(n=n, t_lat=t_lat, n_batch=n_batch)
    for l in range(depth):
        last = l == depth - 1
        f, q, k2, v2 = _proj(xa, mod[l], norm1_g[l][None, :], w_a[l], cos_t, sin_t, **dims)
        fm_lat = _fourier_lat(f, consts, n_batch=n_batch, n=n)
        fm_ctx = _fourier_ctx(f, consts, n_batch=n_batch, ctx_len=ctx_len, t_lat=t_lat)
        at_lat = _attn_lat(attn_sink[l], q, k2, v2, n_batch=n_batch, n=n, ctx_len=ctx_len, t_lat=t_lat)
        at_ctx = _attn_ctx(attn_sink[l], q, k2, v2, n_batch=n_batch, ctx_len=ctx_len, t_lat=t_lat)
        x1, hp, idx4, gate4, counts = _merge(
            xa, mod[l], norm1_g[l][None, :], norm2_g[l][None, :], w_g[l], fm_lat, fm_ctx, at_lat, at_ctx,
            w_fo[l], w_ao[l], w_o[l], rw[l], rb[l], **dims)
        pad_start, block_e, n_valid, pad_end, cnt = _routing_tables(counts, n_blocks)
        dest = _route(idx4, pad_start)[:, :TOP_K]
        xs = _dispatch(pad_end, cnt, n_valid, dest.reshape(-1), hp, n_blocks * TME)
        y = _moe(block_e, n_valid, xs, expert_w_gu, expert_b_gu, expert_w_down, expert_b_down, layer=l)
        yg = _sc_gather(y, dest.T.reshape(-1)).reshape(TOP_K, tt, d // 2)
        xa = _combine(yg, gate4, x1, mod[l], final_norm_g[None, :], n=n, n_batch=n_batch,
                      rows=t_lat if last else tt, final=last)
    return xa.reshape(n_batch, n, d)
```

```python
import functools

import numpy as np
import jax
import jax.numpy as jnp
from jax import lax
from jax.experimental import pallas as pl
from jax.experimental.pallas import tpu as pltpu
from jax.experimental.pallas import tpu_sc as plsc

GRID_W = 64
HEAD_DIM = 64
N_KV_HEADS = 2
Q_PER_KV = 4
KV_DIM = N_KV_HEADS * HEAD_DIM
WINDOW = 128
BLOCK = 128
ROPE_THETA = 10000.0
ROPE_FREQS = HEAD_DIM // 4
GROUP_DIM = 128
N_EXPERTS = 32
TOP_K = 4
SWIGLU_LIMIT = 7.0
SWIGLU_ALPHA = 1.702
N_MOD = 6
EPS = 1e-5
NEG_INF = -1e30
LOG2E = 1.4426950408889634
Q_SCALE = HEAD_DIM ** -0.5 * LOG2E

LANES = 128
MOD_ROWS = 8
TM = 512
TME = 512
TMC = 512
SC_CORES = 2
SC_SUBCORES = 16
SC_WINDOW = 128
SC_SCATTER_WINDOW = 96
FA_NB = 16
FB_NB = 16
CAST_ROWS = 64
MERGE_COLS = 256
VMEM_LIMIT = 56 * 1024 * 1024

_BF = jnp.bfloat16
_F32 = jnp.float32


def _params(*sem):
    return pltpu.CompilerParams(dimension_semantics=sem, vmem_limit_bytes=VMEM_LIMIT)


def _dot(a, b):
    return jnp.dot(a, b, preferred_element_type=_F32)


def _sigmoid(x):
    return 0.5 * jnp.tanh(0.5 * x) + 0.5


def _norm_mod(x, g, shift, scale):
    ms = jnp.mean(x * x, axis=-1, keepdims=True)
    return (x * lax.rsqrt(ms + EPS) * g) * (1 + scale) + shift


def _ada_kernel(c_ref, w_ref, b_ref, o_ref):
    c = c_ref[...]
    s = c * jax.nn.sigmoid(c)
    o_ref[0] = jnp.dot(s, w_ref[0], precision=lax.Precision.HIGHEST,
                       preferred_element_type=_F32) + b_ref[0]


def _ada(cc, ada_w, ada_b):
    depth, d, nd = ada_w.shape
    tn = nd // 4
    return pl.pallas_call(
        _ada_kernel,
        grid=(depth, nd // tn),
        in_specs=[pl.BlockSpec((MOD_ROWS, d), lambda l, j: (0, 0)),
                  pl.BlockSpec((1, d, tn), lambda l, j: (l, 0, j)),
                  pl.BlockSpec((1, 1, tn), lambda l, j: (l, 0, j))],
        out_specs=pl.BlockSpec((1, MOD_ROWS, tn), lambda l, j: (l, 0, j)),
        out_shape=jax.ShapeDtypeStruct((depth, MOD_ROWS, nd), _F32),
        compiler_params=_params("arbitrary", "arbitrary"),
        name="ada_mod",
    )(cc, ada_w, ada_b.reshape(depth, 1, nd))


def _rope(v, cos, sin):
    lane = lax.broadcasted_iota(jnp.int32, v.shape, 1)
    partner = jnp.where((lane & ROPE_FREQS) == 0,
                        pltpu.roll(v, LANES - ROPE_FREQS, 1), pltpu.roll(v, ROPE_FREQS, 1))
    return v * cos + partner * sin


def _proj_kernel(x_ref, mod_ref, g_ref, w_ref, cos_ref, sin_ref, f_ref, q_ref, k_ref, vt_ref, *, fd, qd):
    m = mod_ref[0]
    h = _norm_mod(x_ref[...], g_ref[...], m[0:1], m[1:2]).astype(_BF)
    p = _dot(h, w_ref[...])
    cos = cos_ref[...]
    sin = sin_ref[...]
    f_ref[...] = p[:, :fd].astype(_BF)
    for j in range(qd // LANES):
        lo = fd + j * LANES
        q_ref[:, j * LANES:(j + 1) * LANES] = (
            _rope(p[:, lo:lo + LANES], cos, sin) * Q_SCALE).astype(_BF)
    k = _rope(p[:, fd + qd:fd + qd + KV_DIM], cos, sin)
    v = p[:, fd + qd + KV_DIM:fd + qd + 2 * KV_DIM]
    k_ref[:, :KV_DIM] = k.astype(_BF)
    k_ref[:, KV_DIM:] = pltpu.roll(k, HEAD_DIM, 1).astype(_BF)
    vt_ref[:KV_DIM, :] = v.T.astype(_BF)
    vt_ref[KV_DIM:, :] = pltpu.roll(v, HEAD_DIM, 1).T.astype(_BF)


def _proj(xa, mod, g, w_a, cos_t, sin_t, *, n, t_lat, n_batch):
    tt, d = xa.shape
    fd = qd = d // 2
    nt = tt // TM
    nxt = t_lat // TM
    per_seq = n // TM
    mod_idx = lambda i: (jnp.minimum(i * TM // n, n_batch), 0, 0)
    rope_idx = lambda i: (jnp.where(i < nxt, i % per_seq, per_seq), 0)
    row = lambda i: (i, 0)
    return pl.pallas_call(
        functools.partial(_proj_kernel, fd=fd, qd=qd),
        grid=(nt,),
        in_specs=[pl.BlockSpec((TM, d), row),
                  pl.BlockSpec((1, N_MOD, d), mod_idx),
                  pl.BlockSpec((1, d), lambda i: (0, 0)),
                  pl.BlockSpec(w_a.shape, lambda i: (0, 0)),
                  pl.BlockSpec((TM, LANES), rope_idx),
                  pl.BlockSpec((TM, LANES), rope_idx)],
        out_specs=[pl.BlockSpec((TM, fd), row), pl.BlockSpec((TM, qd), row),
                   pl.BlockSpec((TM, 2 * KV_DIM), row), pl.BlockSpec((2 * KV_DIM, TM), lambda i: (0, i))],
        out_shape=[jax.ShapeDtypeStruct((tt, fd), _BF), jax.ShapeDtypeStruct((tt, qd), _BF),
                   jax.ShapeDtypeStruct((tt, 2 * KV_DIM), _BF), jax.ShapeDtypeStruct((2 * KV_DIM, tt), _BF)],
        compiler_params=_params("arbitrary"),
        name="proj_in",
    )(xa, mod, g, w_a, cos_t, sin_t)


def _rope_tables(n):
    pos = jnp.arange(n)
    inv = ROPE_THETA ** (-jnp.arange(ROPE_FREQS, dtype=_F32) / ROPE_FREQS)
    ar = (pos // GRID_W).astype(_F32)[:, None] * inv
    ac = (pos % GRID_W).astype(_F32)[:, None] * inv
    cos = jnp.concatenate([jnp.cos(ar), jnp.cos(ar), jnp.cos(ac), jnp.cos(ac)], axis=1)
    sin = jnp.concatenate([-jnp.sin(ar), jnp.sin(ar), -jnp.sin(ac), jnp.sin(ac)], axis=1)
    cos = jnp.tile(cos, (1, LANES // HEAD_DIM))
    sin = jnp.tile(sin, (1, LANES // HEAD_DIM))
    cos = jnp.concatenate([cos, jnp.ones((TM, LANES), _F32)], axis=0)
    sin = jnp.concatenate([sin, jnp.zeros((TM, LANES), _F32)], axis=0)
    return cos, sin


def _dft_consts(n, ctx_len):
    n1 = n // BLOCK
    b = np.arange(BLOCK, dtype=np.int64)[:, None, None]
    k1 = np.arange(n1, dtype=np.int64)[None, :, None]
    a = np.arange(n1, dtype=np.int64)[None, None, :]
    ang = 2.0 * np.pi * ((a * k1 * BLOCK + b * k1) % n).astype(np.float64) / n
    m_ri = np.concatenate([np.cos(ang), -np.sin(ang)], axis=1)
    kk = np.arange(GROUP_DIM, dtype=np.int64)
    ang_c = 2.0 * np.pi * ((kk[:, None] * kk[None, :]) % GROUP_DIM) / GROUP_DIM
    c, s = np.cos(ang_c), np.sin(ang_c)
    cs = np.concatenate([np.concatenate([c, s], axis=1), np.concatenate([-s, c], axis=1)], axis=0)
    cc = np.concatenate([c, s], axis=0)
    kl = np.arange(ctx_len, dtype=np.int64)
    ang_l = 2.0 * np.pi * ((kl[:, None] * kl[None, :]) % ctx_len) / ctx_len
    wl = np.concatenate([np.cos(ang_l), -np.sin(ang_l)], axis=0)
    as_bf = lambda v: jnp.asarray(v, dtype=_F32).astype(_BF)
    return dict(m_ri=as_bf(m_ri), cs=as_bf(cs), cc=as_bf(cc), wl=as_bf(wl))


def _chan_dft(xr, xi, cc, scale):
    outs = []
    for g in range(xr.shape[1] // GROUP_DIM):
        sl = slice(g * GROUP_DIM, (g + 1) * GROUP_DIM)
        xx = jnp.concatenate([xr[:, sl], xi[:, sl]], axis=1).astype(_BF)
        outs.append(_dot(xx, cc) * scale)
    return jnp.concatenate(outs, axis=1)


def _fa_kernel(f_ref, m_ref, zr_ref, zi_ref):
    n1 = f_ref.shape[0]
    for t in range(FA_NB):
        zz = _dot(m_ref[t], f_ref[:, t, :])
        zr_ref[:, t, :] = zz[:n1].astype(_BF)
        zi_ref[:, t, :] = zz[n1:].astype(_BF)


def _fb_kernel(zr_ref, zi_ref, cs_ref, cc_ref, o_ref, *, scale):
    cs, cc = cs_ref[...], cc_ref[...]
    for j in range(FB_NB):
        xx = _dot(cs, jnp.concatenate([zr_ref[j], zi_ref[j]], axis=0))
        o_ref[:, j, :] = _chan_dft(xx[:BLOCK], xx[BLOCK:], cc, scale).astype(_BF)


def _fourier_lat(f, consts, *, n_batch, n):
    tt, fd = f.shape
    n1 = n // BLOCK
    f3 = f.reshape(tt // BLOCK, BLOCK, fd)
    blk_a = pl.BlockSpec((n1, FA_NB, fd), lambda b, j: (b, j, 0))
    mat_a = pl.BlockSpec((FA_NB, 2 * n1, n1), lambda b, j: (j, 0, 0))
    z_shape = jax.ShapeDtypeStruct((n_batch * n1, BLOCK, fd), _BF)
    zr, zi = pl.pallas_call(
        _fa_kernel,
        grid=(n_batch, BLOCK // FA_NB),
        in_specs=[blk_a, mat_a],
        out_specs=[blk_a, blk_a],
        out_shape=[z_shape, z_shape],
        compiler_params=_params("arbitrary", "arbitrary"),
        name="fourier_seq_a",
    )(f3, consts["m_ri"])
    blk_z = pl.BlockSpec((FB_NB, BLOCK, fd), lambda b, j: (b * (n1 // FB_NB) + j, 0, 0))
    const = lambda shape: pl.BlockSpec(shape, lambda b, j: (0, 0))
    out = pl.pallas_call(
        functools.partial(_fb_kernel, scale=float((n * GROUP_DIM) ** -0.5)),
        grid=(n_batch, n1 // FB_NB),
        in_specs=[blk_z, blk_z, const((2 * BLOCK, 2 * BLOCK)), const((2 * GROUP_DIM, GROUP_DIM))],
        out_specs=pl.BlockSpec((BLOCK, FB_NB, fd), lambda b, j: (b, j, 0)),
        out_shape=jax.ShapeDtypeStruct((n_batch * BLOCK, n1, fd), _BF),
        compiler_params=_params("arbitrary", "arbitrary"),
        name="fourier_seq_b",
    )(zr, zi, consts["cs"], consts["cc"])
    return out.reshape(n_batch * n, fd)


def _fc_kernel(f_ref, wl_ref, cc_ref, o_ref, *, scale):
    ctx_len = f_ref.shape[0]
    xx = _dot(wl_ref[...], f_ref[...])
    o_ref[...] = _chan_dft(xx[:ctx_len], xx[ctx_len:], cc_ref[...], scale).astype(_BF)


def _fourier_ctx(f, consts, *, n_batch, ctx_len, t_lat):
    fd = f.shape[1]
    const = lambda shape: pl.BlockSpec(shape, lambda b: (0, 0))
    return pl.pallas_call(
        functools.partial(_fc_kernel, scale=float((ctx_len * GROUP_DIM) ** -0.5)),
        grid=(n_batch,),
        in_specs=[pl.BlockSpec((ctx_len, fd), lambda b: (t_lat // ctx_len + b, 0)),
                  const((2 * ctx_len, ctx_len)), const((2 * GROUP_DIM, GROUP_DIM))],
        out_specs=pl.BlockSpec((ctx_len, fd), lambda b: (b, 0)),
        out_shape=jax.ShapeDtypeStruct((n_batch * ctx_len, fd), _BF),
        compiler_params=_params("arbitrary"),
        name="fourier_ctx",
    )(f, consts["wl"], consts["cc"])


def _attn_core(sink_ref, q_ref, k, vt, o_ref, bias, s_scr, p_scr):
    tq = q_ref.shape[0]
    lane = lax.broadcasted_iota(jnp.int32, (tq, LANES), 1)
    low = lane < HEAD_DIM
    zero = jnp.zeros((tq, LANES), _BF)
    low_row = lax.broadcasted_iota(jnp.int32, (LANES, 2 * tq), 0) < HEAD_DIM
    first = lax.broadcasted_iota(jnp.int32, (1, 2 * tq), 1) < tq
    pairs = [(kvh, half) for kvh in range(N_KV_HEADS) for half in range(2)]
    sels = [slice(0, LANES) if half == kvh else slice(LANES, 2 * LANES) for kvh, half in pairs]
    for pi, (kvh, half) in enumerate(pairs):
        slabs = [q_ref[:, (2 * kvh + c) * LANES:(2 * kvh + c + 1) * LANES] for c in range(2)]
        keep = low if half == 0 else jnp.logical_not(low)
        qm = jnp.concatenate([jnp.where(keep, sl, zero) for sl in slabs], axis=0)
        s = lax.dot_general(k[:, sels[pi]], qm, (((1,), (1,)), ((), ())), preferred_element_type=_F32)
        s_scr[pi] = s if bias is None else s + bias
    dens = []
    for pi, (kvh, half) in enumerate(pairs):
        hq = Q_PER_KV * kvh + half
        sk = jnp.where(first, sink_ref[hq], sink_ref[hq + 2]) * LOG2E
        s = s_scr[pi]
        m = jnp.maximum(jnp.max(s, axis=0, keepdims=True), sk)
        p = jnp.exp2(s - m)
        dens.append(jnp.sum(p, axis=0, keepdims=True) + jnp.exp2(sk - m))
        p_scr[pi] = p.astype(_BF)
    outs = [_dot(vt[sels[pi], :], p_scr[pi]) * (1.0 / dens[pi]) for pi in range(len(pairs))]
    for kvh in range(N_KV_HEADS):
        ot = jnp.where(low_row, outs[2 * kvh], outs[2 * kvh + 1])
        for c in range(2):
            o_ref[:, (2 * kvh + c) * LANES:(2 * kvh + c + 1) * LANES] = ot[:, c * tq:(c + 1) * tq].T.astype(_BF)


def _attn_scratch(keys, tq):
    return [pltpu.VMEM((2 * N_KV_HEADS, keys, 2 * tq), _F32), pltpu.VMEM((2 * N_KV_HEADS, keys, 2 * tq), _BF)]


def _attn_lat_kernel(sink_ref, q_ref, kp_ref, kc_ref, kn_ref, vp_ref, vc_ref, vn_ref, kx_ref, vx_ref, bias_ref,
                     o_ref, s_scr, p_scr):
    k = jnp.concatenate([kp_ref[...], kc_ref[...], kn_ref[...], kx_ref[...]], axis=0)
    vt = jnp.concatenate([vp_ref[...], vc_ref[...], vn_ref[...], vx_ref[...]], axis=1)
    _attn_core(sink_ref, q_ref, k, vt, o_ref, bias_ref[0], s_scr, p_scr)


def _window_bias(ctx_len):
    s = np.arange(3 * BLOCK + ctx_len)[:, None]
    r = np.arange(2 * BLOCK)[None, :] % BLOCK
    out = []
    for v in range(4):
        lo = 0 if v & 1 else BLOCK
        hi = 3 * BLOCK if v & 2 else 2 * BLOCK
        ok = ((np.abs(s - BLOCK - r) <= WINDOW) & (s >= lo) & (s < hi)) | (s >= 3 * BLOCK)
        out.append(np.where(ok, 0.0, NEG_INF))
    return jnp.asarray(np.stack(out), dtype=_F32)


def _attn_ctx_kernel(sink_ref, q_ref, kx_ref, vx_ref, o_ref, s_scr, p_scr):
    _attn_core(sink_ref, q_ref, kx_ref[...], vx_ref[...], o_ref, None, s_scr, p_scr)


def _attn_lat(sink, q, k2, v2t, *, n_batch, n, ctx_len, t_lat):
    qd = q.shape[1]
    nb = n // BLOCK
    cur = lambda b, i: (b * nb + i, 0)
    prev = lambda b, i: (b * nb + jnp.maximum(i - 1, 0), 0)
    nxt = lambda b, i: (b * nb + jnp.minimum(i + 1, nb - 1), 0)
    cx = lambda b, i: (t_lat // ctx_len + b, 0)
    swap = lambda im: (lambda b, i: im(b, i)[::-1])
    kb = lambda im: pl.BlockSpec((BLOCK, 2 * KV_DIM), im)
    vb = lambda im: pl.BlockSpec((2 * KV_DIM, BLOCK), swap(im))
    bias = _window_bias(ctx_len)
    bias_idx = lambda b, i: ((i > 0).astype(jnp.int32) + 2 * (i < nb - 1).astype(jnp.int32), 0, 0)
    return pl.pallas_call(
        _attn_lat_kernel,
        grid=(n_batch, nb),
        in_specs=[pl.BlockSpec(memory_space=pltpu.SMEM), pl.BlockSpec((BLOCK, qd), cur),
                  kb(prev), kb(cur), kb(nxt), vb(prev), vb(cur), vb(nxt),
                  pl.BlockSpec((ctx_len, 2 * KV_DIM), cx), pl.BlockSpec((2 * KV_DIM, ctx_len), swap(cx)),
                  pl.BlockSpec((1,) + bias.shape[1:], bias_idx)],
        out_specs=pl.BlockSpec((BLOCK, qd), cur),
        out_shape=jax.ShapeDtypeStruct((t_lat, qd), _BF),
        scratch_shapes=_attn_scratch(3 * BLOCK + ctx_len, BLOCK),
        compiler_params=_params("arbitrary", "arbitrary"),
        name="attn_lat",
    )(sink, q, k2, k2, k2, v2t, v2t, v2t, k2, v2t, bias)


def _attn_ctx(sink, q, k2, v2t, *, n_batch, ctx_len, t_lat):
    qd = q.shape[1]
    cx = lambda b: (t_lat // ctx_len + b, 0)
    return pl.pallas_call(
        _attn_ctx_kernel,
        grid=(n_batch,),
        in_specs=[pl.BlockSpec(memory_space=pltpu.SMEM), pl.BlockSpec((ctx_len, qd), cx),
                  pl.BlockSpec((ctx_len, 2 * KV_DIM), cx),
                  pl.BlockSpec((2 * KV_DIM, ctx_len), lambda b: (0, t_lat // ctx_len + b))],
        out_specs=pl.BlockSpec((ctx_len, qd), lambda b: (b, 0)),
        out_shape=jax.ShapeDtypeStruct((n_batch * ctx_len, qd), _BF),
        scratch_shapes=_attn_scratch(ctx_len, ctx_len),
        compiler_params=_params("arbitrary"),
        name="attn_ctx",
    )(sink, q, k2, v2t)


def _pack_bf16_pairs(h):
    half = h.shape[1] // 2
    lo = lax.bitcast_convert_type(h[:, :half].astype(_BF).astype(_F32), jnp.uint32)
    hi = lax.bitcast_convert_type(h[:, half:].astype(_BF).astype(_F32), jnp.uint32)
    return (lo >> 16) | hi


def _unpack_bf16_pairs(p):
    lo = lax.bitcast_convert_type(p << 16, _F32)
    hi = lax.bitcast_convert_type(p & jnp.uint32(0xFFFF0000), _F32)
    return jnp.concatenate([lo, hi], axis=1).astype(_BF)


def _merge_kernel(x_ref, mod_ref, g1_ref, g2_ref, wg_ref, fml_ref, fmc_ref, atl_ref, atc_ref,
                  wfo_ref, wao_ref, wout_ref, rw_ref, rb_ref,
                  x1_ref, hp_ref, idx_ref, gate_ref, cnt_ref, h_scr, y_scr, *, nxt):
    i = pl.program_id(0)
    d = x_ref.shape[1]
    m = mod_ref[0]
    h_scr[...] = _norm_mod(x_ref[...], g1_ref[...], m[0:1], m[1:2]).astype(_BF)
    is_lat = i < nxt
    fm = jnp.where(is_lat, fml_ref[...], fmc_ref[...])
    at = jnp.where(is_lat, atl_ref[...], atc_ref[...])
    for c in range(d // MERGE_COLS):
        sl = slice(c * MERGE_COLS, (c + 1) * MERGE_COLS)
        sg = slice(d + c * MERGE_COLS, d + (c + 1) * MERGE_COLS)
        h = h_scr[...]
        y = (_sigmoid(_dot(h, wg_ref[:, sl])) * _dot(fm, wfo_ref[:, sl])
             + _sigmoid(_dot(h, wg_ref[:, sg])) * _dot(at, wao_ref[:, sl]))
        y_scr[:, sl] = y.astype(_BF)
    x1_ref[...] = x_ref[...] + m[2:3] * _dot(y_scr[...], wout_ref[...])
    h2 = _norm_mod(x1_ref[...], g2_ref[...], m[3:4], m[4:5])
    hp_ref[...] = _pack_bf16_pairs(h2)
    logits = _dot(h2.astype(_BF), rw_ref[...]) + rb_ref[...]
    lane = lax.broadcasted_iota(jnp.int32, logits.shape, 1)
    vals, idxs = [], []
    for _ in range(TOP_K):
        mx = jnp.max(logits, axis=-1, keepdims=True)
        ix = jnp.min(jnp.where(logits == mx, lane, LANES), axis=-1, keepdims=True)
        logits = jnp.where(lane == ix, NEG_INF, logits)
        vals.append(mx)
        idxs.append(ix)
    es = [jnp.exp(v - vals[0]) for v in vals]
    den = es[0] + es[1] + es[2] + es[3]
    idx4 = jnp.zeros(logits.shape, jnp.int32)
    gate4 = jnp.zeros(logits.shape, _F32)
    sel = jnp.zeros(logits.shape, _F32)
    for k in range(TOP_K):
        idx4 = jnp.where(lane == k, idxs[k], idx4)
        gate4 = jnp.where(lane == k, es[k] / den, gate4)
        sel = sel + jnp.where(lane == idxs[k], 1.0, 0.0)
    idx_ref[...] = idx4
    gate_ref[...] = gate4

    @pl.when(i == 0)
    def _():
        cnt_ref[...] = jnp.zeros(cnt_ref.shape, _F32)

    cnt_ref[0:1, :] += jnp.sum(sel, axis=0, keepdims=True)


def _merge(xa, mod, g1, g2, w_g, fm_lat, fm_ctx, at_lat, at_ctx, w_fo, w_ao, w_out, rw, rb,
           *, n, t_lat, n_batch):
    tt, d = xa.shape
    fd = fm_lat.shape[1]
    nt = tt // TM
    nxt = t_lat // TM
    row = lambda i: (i, 0)
    lat = lambda i: (jnp.minimum(i, nxt - 1), 0)
    cxt = lambda i: (jnp.maximum(i - nxt, 0), 0)
    const = lambda a: pl.BlockSpec(a.shape, lambda i: (0, 0))
    mod_idx = lambda i: (jnp.minimum(i * TM // n, n_batch), 0, 0)
    return pl.pallas_call(
        functools.partial(_merge_kernel, nxt=nxt),
        grid=(nt,),
        in_specs=[pl.BlockSpec((TM, d), row), pl.BlockSpec((1, N_MOD, d), mod_idx),
                  const(g1), const(g2), const(w_g),
                  pl.BlockSpec((TM, fd), lat), pl.BlockSpec((TM, fd), cxt),
                  pl.BlockSpec((TM, fd), lat), pl.BlockSpec((TM, fd), cxt),
                  const(w_fo), const(w_ao), const(w_out), const(rw), const(rb)],
        out_specs=[pl.BlockSpec((TM, d), row), pl.BlockSpec((TM, d // 2), row),
                   pl.BlockSpec((TM, LANES), row), pl.BlockSpec((TM, LANES), row),
                   pl.BlockSpec((8, LANES), lambda i: (0, 0))],
        out_shape=[jax.ShapeDtypeStruct((tt, d), _F32), jax.ShapeDtypeStruct((tt, d // 2), jnp.uint32),
                   jax.ShapeDtypeStruct((tt, LANES), jnp.int32), jax.ShapeDtypeStruct((tt, LANES), _F32),
                   jax.ShapeDtypeStruct((8, LANES), _F32)],
        scratch_shapes=[pltpu.VMEM((TM, d), _BF), pltpu.VMEM((TM, d), _BF)],
        compiler_params=_params("arbitrary"),
        name="merge_router",
    )(xa, mod, g1, g2, w_g, fm_lat, fm_ctx, at_lat, at_ctx, w_fo, w_ao, w_out, rw, rb)


def _route_kernel(idx_ref, ps_ref, d_ref, carry):
    i = pl.program_id(0)

    @pl.when(i == 0)
    def _():
        carry[...] = ps_ref[...]

    idx4 = idx_ref[...]
    lane = lax.broadcasted_iota(jnp.int32, idx4.shape, 1)
    cols = [idx4[:, k:k + 1] for k in range(TOP_K)]
    sel = jnp.zeros(idx4.shape, _F32)
    for k in range(TOP_K):
        sel = sel + jnp.where(lane == cols[k], 1.0, 0.0)
    r = lax.broadcasted_iota(jnp.int32, (TM, TM), 0)
    c = lax.broadcasted_iota(jnp.int32, (TM, TM), 1)
    tri = jnp.where(r > c, 1.0, 0.0).astype(_BF)
    slot = carry[...] + _dot(tri, sel.astype(_BF))
    dest4 = jnp.zeros(idx4.shape, jnp.int32)
    for k in range(TOP_K):
        dk = jnp.sum(jnp.where(lane == cols[k], slot, 0.0), axis=-1, keepdims=True)
        dest4 = jnp.where(lane == k, dk.astype(jnp.int32), dest4)
    d_ref[...] = dest4
    carry[...] += jnp.sum(sel, axis=0, keepdims=True)


def _route(idx4, pad_start):
    tt = idx4.shape[0]
    return pl.pallas_call(
        _route_kernel,
        grid=(tt // TM,),
        in_specs=[pl.BlockSpec((TM, LANES), lambda i: (i, 0)), pl.BlockSpec((1, LANES), lambda i: (0, 0))],
        out_specs=pl.BlockSpec((TM, LANES), lambda i: (i, 0)),
        out_shape=jax.ShapeDtypeStruct((tt, LANES), jnp.int32),
        scratch_shapes=[pltpu.VMEM((1, LANES), _F32)],
        compiler_params=_params("arbitrary"),
        name="route_slots",
    )(idx4, pad_start)


def _sc_scatter(rows, idx, n_slots):
    tt, w = rows.shape
    workers = SC_CORES * SC_SUBCORES
    per_worker = tt // workers
    assert tt % workers == 0 and per_worker % SC_SCATTER_WINDOW == 0
    mesh = plsc.VectorSubcoreMesh(core_axis_name="core", subcore_axis_name="subcore")

    @pl.kernel(out_type=jax.ShapeDtypeStruct((n_slots, w), rows.dtype), mesh=mesh,
               scratch_types=[pltpu.VMEM((SC_SCATTER_WINDOW,), jnp.int32),
                              pltpu.VMEM((SC_SCATTER_WINDOW, w), rows.dtype), pltpu.SemaphoreType.DMA])
    def scatter(r_hbm, i_hbm, o_hbm, idx_v, rows_v, sem):
        wid = lax.axis_index("subcore") * SC_CORES + lax.axis_index("core")

        @pl.loop(0, per_worker // SC_SCATTER_WINDOW)
        def _(c):
            base = wid * per_worker + c * SC_SCATTER_WINDOW
            pltpu.sync_copy(r_hbm.at[pl.ds(base, SC_SCATTER_WINDOW)], rows_v)
            for k in range(TOP_K):
                pltpu.sync_copy(i_hbm.at[pl.ds(k * tt + base, SC_SCATTER_WINDOW)], idx_v)
                pltpu.async_copy(rows_v, o_hbm.at[idx_v], sem).wait()

    return scatter(rows, idx)


def _pad_fixup_kernel(pend_ref, cnt_ref, nv_ref, xs_in_ref, xs_ref, buf, sem):
    del xs_in_ref
    n_blocks = xs_ref.shape[0] // TME
    row = lax.broadcasted_iota(jnp.int32, buf.shape, 0)

    def block(start):
        return xs_ref.at[pl.ds(pl.multiple_of(start, TME), TME)]

    for e in range(N_EXPERTS):
        @pl.when(cnt_ref[e] > 0)
        def _():
            start = pend_ref[e] - TME
            load = pltpu.make_async_copy(block(start), buf, sem)
            load.start()
            load.wait()
            used = lax.rem(cnt_ref[e] - 1, TME) + 1
            buf[...] = jnp.where(row < used, buf[...], jnp.zeros(buf.shape, buf.dtype))
            store = pltpu.make_async_copy(buf, block(start), sem)
            store.start()
            store.wait()

    buf[...] = jnp.zeros(buf.shape, buf.dtype)

    def start_unused(j, carry):
        pltpu.make_async_copy(buf, block(j * TME), sem).start()
        return carry

    def wait_unused(j, carry):
        pltpu.make_async_copy(buf, block(j * TME), sem).wait()
        return carry

    lax.fori_loop(nv_ref[0], n_blocks, start_unused, 0)
    lax.fori_loop(nv_ref[0], n_blocks, wait_unused, 0)


def _pad_fixup(pad_end, cnt, n_valid, xs):
    grid_spec = pltpu.PrefetchScalarGridSpec(
        num_scalar_prefetch=3,
        grid=(1,),
        in_specs=[pl.BlockSpec(memory_space=pl.ANY)],
        out_specs=pl.BlockSpec(memory_space=pl.ANY),
        scratch_shapes=[pltpu.VMEM((TME, xs.shape[1]), xs.dtype), pltpu.SemaphoreType.DMA],
    )
    return pl.pallas_call(
        _pad_fixup_kernel,
        grid_spec=grid_spec,
        out_shape=jax.ShapeDtypeStruct(xs.shape, xs.dtype),
        input_output_aliases={3: 0},
        compiler_params=_params("arbitrary"),
        name="moe_pad_fixup",
    )(pad_end, cnt, n_valid, xs)


def _moe_kernel(be_ref, nv_ref, xs_ref, wgu_ref, bgu_ref, wd_ref, bd_ref, y_ref, wgu_bf, wd_bf):
    i = pl.program_id(0)
    de = wd_ref.shape[2]
    valid = i < nv_ref[0]
    new_expert = jnp.logical_or(i == 0, be_ref[i] != be_ref[jnp.maximum(i - 1, 0)])

    @pl.when(jnp.logical_and(valid, new_expert))
    def _():
        def cast_rows(ref, out, rows):
            def body(j, carry):
                sl = pl.ds(pl.multiple_of(j * CAST_ROWS, CAST_ROWS), CAST_ROWS)
                out[sl, :] = ref[0, 0, sl, :].astype(_BF)
                return carry
            lax.fori_loop(0, rows // CAST_ROWS, body, 0)
        cast_rows(wgu_ref, wgu_bf, wgu_ref.shape[2])
        cast_rows(wd_ref, wd_bf, de)

    @pl.when(valid)
    def _():
        xb = _unpack_bf16_pairs(xs_ref[...])
        gu = _dot(xb, wgu_bf[...]) + bgu_ref[0, 0]
        a = jnp.minimum(gu[:, :de], SWIGLU_LIMIT)
        u = jnp.clip(gu[:, de:], -SWIGLU_LIMIT, SWIGLU_LIMIT)
        act = a * _sigmoid(SWIGLU_ALPHA * a) * (u + 1)
        y_ref[...] = _pack_bf16_pairs(_dot(act.astype(_BF), wd_bf[...]) + bd_ref[0, 0])

    @pl.when(i >= nv_ref[0])
    def _():
        y_ref[...] = jnp.zeros(y_ref.shape, y_ref.dtype)


def _moe(block_e, n_valid, xs, w_gu, b_gu, w_down, b_down, *, layer):
    n_slots, w = xs.shape
    depth, ne, d, de2 = w_gu.shape
    de = de2 // 2
    n_blocks = n_slots // TME
    ex = lambda i, be, nv: (layer, be[i], 0, 0)
    grid_spec = pltpu.PrefetchScalarGridSpec(
        num_scalar_prefetch=2,
        grid=(n_blocks,),
        in_specs=[pl.BlockSpec((TME, w), lambda i, be, nv: (jnp.minimum(i, nv[0] - 1), 0)),
                  pl.BlockSpec((1, 1, d, de2), ex), pl.BlockSpec((1, 1, 1, de2), ex),
                  pl.BlockSpec((1, 1, de, d), ex), pl.BlockSpec((1, 1, 1, d), ex)],
        out_specs=pl.BlockSpec((TME, d // 2), lambda i, be, nv: (i, 0)),
        scratch_shapes=[pltpu.VMEM((d, de2), _BF), pltpu.VMEM((de, d), _BF)],
    )
    return pl.pallas_call(
        _moe_kernel,
        grid_spec=grid_spec,
        out_shape=jax.ShapeDtypeStruct((n_slots, d // 2), jnp.uint32),
        compiler_params=_params("arbitrary"),
        name="moe_experts",
    )(block_e, n_valid, xs, w_gu, b_gu.reshape(depth, ne, 1, de2), w_down, b_down.reshape(depth, ne, 1, d))


def _sc_gather(table, idx):
    n_idx, (_, w) = idx.shape[0], table.shape
    workers = SC_CORES * SC_SUBCORES
    per_worker = n_idx // workers
    assert n_idx % workers == 0 and per_worker % SC_WINDOW == 0
    mesh = plsc.VectorSubcoreMesh(core_axis_name="core", subcore_axis_name="subcore")

    @pl.kernel(out_type=jax.ShapeDtypeStruct((n_idx, w), table.dtype), mesh=mesh,
               scratch_types=[pltpu.VMEM((SC_WINDOW,), jnp.int32), pltpu.VMEM((SC_WINDOW, w), table.dtype),
                              pltpu.SemaphoreType.DMA])
    def gather(t_hbm, i_hbm, o_hbm, idx_v, rows_v, sem):
        wid = lax.axis_index("subcore") * SC_CORES + lax.axis_index("core")

        @pl.loop(0, per_worker // SC_WINDOW)
        def _(c):
            base = wid * per_worker + c * SC_WINDOW
            pltpu.sync_copy(i_hbm.at[pl.ds(base, SC_WINDOW)], idx_v)
            pltpu.async_copy(t_hbm.at[idx_v], rows_v, sem).wait()
            pltpu.sync_copy(rows_v, o_hbm.at[pl.ds(base, SC_WINDOW)])

    return gather(table, idx)


def _combine_kernel(yg_ref, gate_ref, x_ref, mod_ref, fg_ref, o_ref, *, final):
    g = gate_ref[...]
    half = x_ref.shape[1] // 2
    lo = jnp.zeros((x_ref.shape[0], half), _F32)
    hi = jnp.zeros((x_ref.shape[0], half), _F32)
    for k in range(TOP_K):
        p = yg_ref[k]
        lo = lo + g[:, k:k + 1] * lax.bitcast_convert_type(p << 16, _F32)
        hi = hi + g[:, k:k + 1] * lax.bitcast_convert_type(p & jnp.uint32(0xFFFF0000), _F32)
    x2 = x_ref[...] + mod_ref[0][5:6] * jnp.concatenate([lo, hi], axis=1)
    if final:
        ms = jnp.mean(x2 * x2, axis=-1, keepdims=True)
        x2 = x2 * lax.rsqrt(ms + EPS) * fg_ref[...]
    o_ref[...] = x2


def _combine(yg, gate4, x1, mod, fg, *, n, n_batch, rows, final):
    d = x1.shape[1]
    row = lambda i: (i, 0)
    mod_idx = lambda i: (jnp.minimum(i * TMC // n, n_batch), 0, 0)
    return pl.pallas_call(
        functools.partial(_combine_kernel, final=final),
        grid=(rows // TMC,),
        in_specs=[pl.BlockSpec((TOP_K, TMC, d // 2), lambda i: (0, i, 0)),
                  pl.BlockSpec((TMC, LANES), row), pl.BlockSpec((TMC, d), row),
                  pl.BlockSpec((1, N_MOD, d), mod_idx), pl.BlockSpec((1, d), lambda i: (0, 0))],
        out_specs=pl.BlockSpec((TMC, d), row),
        out_shape=jax.ShapeDtypeStruct((rows, d), _F32),
        compiler_params=_params("arbitrary"),
        name="moe_combine",
    )(yg, gate4, x1, mod, fg)


def _routing_tables(counts, n_blocks):
    cnt = counts[0, :N_EXPERTS].astype(jnp.int32)
    padded = (cnt + TME - 1) // TME * TME
    pad_end = jnp.cumsum(padded)
    pad_start = pad_end - padded
    ps = jnp.zeros((1, LANES), _F32).at[0, :N_EXPERTS].set(pad_start.astype(_F32))
    blk = jnp.arange(n_blocks, dtype=jnp.int32) * TME
    block_e = jnp.sum((pad_end[None, :] <= blk[:, None]).astype(jnp.int32), axis=1)
    e_last = jnp.max(jnp.where(cnt > 0, jnp.arange(N_EXPERTS, dtype=jnp.int32), 0))
    block_e = jnp.minimum(block_e, e_last).astype(jnp.int32)
    n_valid = (pad_end[-1] // TME).astype(jnp.int32).reshape(1)
    return ps, block_e, n_valid, pad_end.astype(jnp.int32), cnt


def kernel(x, c, ctx, c_ctx, ada_w, ada_b, norm1_g, norm2_g, w_in, attn_sink, w_fourier_out, w_attn_out,
           w_out, router_w, router_b, expert_w_gu, expert_b_gu, expert_w_down, expert_b_down, final_norm_g):
    n_batch, n, d = x.shape
    ctx_len = ctx.shape[1]
    depth = ada_w.shape[0]
    t_lat = n_batch * n
    t_ctx = n_batch * ctx_len
    tt = t_lat + t_ctx
    fd = qd = d // 2
    assert n % TM == 0 and t_ctx % TM == 0 and n % (BLOCK * FB_NB) == 0 and n_batch < MOD_ROWS
    assert t_lat % ctx_len == 0 and TM % ctx_len == 0

    xa = jnp.concatenate([x.reshape(t_lat, d), ctx.reshape(t_ctx, d)], axis=0)
    cc = jnp.concatenate([c, c_ctx[None, :], jnp.zeros((MOD_ROWS - n_batch - 1, d), _F32)], axis=0)
    mod = _ada(cc, ada_w, ada_b).reshape(depth, MOD_ROWS, N_MOD, d)
    cos_t, sin_t = _rope_tables(n)
    consts = _dft_consts(n, ctx_len)

    n_in = fd + qd + 2 * KV_DIM
    w_a = w_in[:, :, :n_in].astype(_BF)
    w_g = w_in[:, :, n_in:].astype(_BF)
    w_fo, w_ao, w_o = w_fourier_out.astype(_BF), w_attn_out.astype(_BF), w_out.astype(_BF)
    rw = jnp.pad(router_w, ((0, 0), (0, 0), (0, LANES - N_EXPERTS))).astype(_BF)
    rb = jnp.pad(router_b, ((0, 0), (0, LANES - N_EXPERTS)), constant_values=NEG_INF)[:, None, :]

    n_blocks = -(-(tt * TOP_K + N_EXPERTS * (TME - 1)) // TME)
    dims = dict(n=n, t_lat=t_lat, n_batch=n_batch)
    for l in range(depth):
        last = l == depth - 1
        f, q, k2, v2 = _proj(xa, mod[l], norm1_g[l][None, :], w_a[l], cos_t, sin_t, **dims)
        fm_lat = _fourier_lat(f, consts, n_batch=n_batch, n=n)
        fm_ctx = _fourier_ctx(f, consts, n_batch=n_batch, ctx_len=ctx_len, t_lat=t_lat)
        at_lat = _attn_lat(attn_sink[l], q, k2, v2, n_batch=n_batch, n=n, ctx_len=ctx_len, t_lat=t_lat)
        at_ctx = _attn_ctx(attn_sink[l], q, k2, v2, n_batch=n_batch, ctx_len=ctx_len, t_lat=t_lat)
        x1, hp, idx4, gate4, counts = _merge(
            xa, mod[l], norm1_g[l][None, :], norm2_g[l][None, :], w_g[l], fm_lat, fm_ctx, at_lat, at_ctx,
            w_fo[l], w_ao[l], w_o[l], rw[l], rb[l], **dims)
        pad_start, block_e, n_valid, pad_end, cnt = _routing_tables(counts, n_blocks)
        dest = _route(idx4, pad_start)[:, :TOP_K].T.reshape(-1)
        xs = _pad_fixup(pad_end, cnt, n_valid, _sc_scatter(hp, dest, n_blocks * TME))
        y = _moe(block_e, n_valid, xs, expert_w_gu, expert_b_gu, expert_w_down, expert_b_down, layer=l)
        yg = _sc_gather(y, dest).reshape(TOP_K, tt, d // 2)
        xa = _combine(yg, gate4, x1, mod[l], final_norm_g[None, :], n=n, n_batch=n_batch,
                      rows=t_lat if last else tt, final=last)
    return xa.reshape(n_batch, n, d)
```

```python
import functools

import numpy as np
import jax
import jax.numpy as jnp
from jax import lax
from jax.experimental import pallas as pl
from jax.experimental.pallas import tpu as pltpu
from jax.experimental.pallas import tpu_sc as plsc

GRID_W = 64
HEAD_DIM = 64
N_KV_HEADS = 2
Q_PER_KV = 4
KV_DIM = N_KV_HEADS * HEAD_DIM
WINDOW = 128
BLOCK = 128
ROPE_THETA = 10000.0
ROPE_FREQS = HEAD_DIM // 4
GROUP_DIM = 128
N_EXPERTS = 32
TOP_K = 4
SWIGLU_LIMIT = 7.0
SWIGLU_ALPHA = 1.702
N_MOD = 6
EPS = 1e-5
NEG_INF = -1e30
LOG2E = 1.4426950408889634
Q_SCALE = HEAD_DIM ** -0.5 * LOG2E

LANES = 128
MOD_ROWS = 8
TM = 512
TME = 512
TMC = 512
SC_CORES = 2
SC_SUBCORES = 16
SC_WINDOW = 128
SC_SCATTER_WINDOW = 96
FA_NB = 16
FB_NB = 16
CAST_ROWS = 64
MERGE_COLS = 256
MERGE_ROW_SPLIT = 1
VMEM_LIMIT = 56 * 1024 * 1024

_BF = jnp.bfloat16
_F32 = jnp.float32


def _params(*sem):
    return pltpu.CompilerParams(dimension_semantics=sem, vmem_limit_bytes=VMEM_LIMIT)


def _dot(a, b):
    return jnp.dot(a, b, preferred_element_type=_F32)


def _sigmoid(x):
    return 0.5 * jnp.tanh(0.5 * x) + 0.5


def _norm_mod(x, g, shift, scale):
    ms = jnp.mean(x * x, axis=-1, keepdims=True)
    return (x * lax.rsqrt(ms + EPS) * g) * (1 + scale) + shift


def _ada_kernel(c_ref, w_ref, b_ref, o_ref):
    c = c_ref[...]
    s = c * jax.nn.sigmoid(c)
    o_ref[0] = jnp.dot(s, w_ref[0], precision=lax.Precision.HIGHEST,
                       preferred_element_type=_F32) + b_ref[0]


def _ada(cc, ada_w, ada_b):
    depth, d, nd = ada_w.shape
    tn = nd // 4
    return pl.pallas_call(
        _ada_kernel,
        grid=(depth, nd // tn),
        in_specs=[pl.BlockSpec((MOD_ROWS, d), lambda l, j: (0, 0)),
                  pl.BlockSpec((1, d, tn), lambda l, j: (l, 0, j)),
                  pl.BlockSpec((1, 1, tn), lambda l, j: (l, 0, j))],
        out_specs=pl.BlockSpec((1, MOD_ROWS, tn), lambda l, j: (l, 0, j)),
        out_shape=jax.ShapeDtypeStruct((depth, MOD_ROWS, nd), _F32),
        compiler_params=_params("arbitrary", "arbitrary"),
        name="ada_mod",
    )(cc, ada_w, ada_b.reshape(depth, 1, nd))


def _rope(v, cos, sin):
    lane = lax.broadcasted_iota(jnp.int32, v.shape, 1)
    partner = jnp.where((lane & ROPE_FREQS) == 0,
                        pltpu.roll(v, LANES - ROPE_FREQS, 1), pltpu.roll(v, ROPE_FREQS, 1))
    return v * cos + partner * sin


def _proj_kernel(x_ref, mod_ref, g_ref, w_ref, cos_ref, sin_ref, f_ref, q_ref, k_ref, vt_ref, *, fd, qd):
    m = mod_ref[0]
    h = _norm_mod(x_ref[...], g_ref[...], m[0:1], m[1:2]).astype(_BF)
    p = _dot(h, w_ref[...])
    cos = cos_ref[...]
    sin = sin_ref[...]
    f_ref[...] = p[:, :fd].astype(_BF)
    for j in range(qd // LANES):
        lo = fd + j * LANES
        q_ref[:, j * LANES:(j + 1) * LANES] = (
            _rope(p[:, lo:lo + LANES], cos, sin) * Q_SCALE).astype(_BF)
    k = _rope(p[:, fd + qd:fd + qd + KV_DIM], cos, sin)
    v = p[:, fd + qd + KV_DIM:fd + qd + 2 * KV_DIM]
    k_ref[:, :KV_DIM] = k.astype(_BF)
    k_ref[:, KV_DIM:] = pltpu.roll(k, HEAD_DIM, 1).astype(_BF)
    vt_ref[:KV_DIM, :] = v.T.astype(_BF)
    vt_ref[KV_DIM:, :] = pltpu.roll(v, HEAD_DIM, 1).T.astype(_BF)


def _proj(xa, mod, g, w_a, cos_t, sin_t, *, n, t_lat, n_batch):
    tt, d = xa.shape
    fd = qd = d // 2
    nt = tt // TM
    nxt = t_lat // TM
    per_seq = n // TM
    mod_idx = lambda i: (jnp.minimum(i * TM // n, n_batch), 0, 0)
    rope_idx = lambda i: (jnp.where(i < nxt, i % per_seq, per_seq), 0)
    row = lambda i: (i, 0)
    return pl.pallas_call(
        functools.partial(_proj_kernel, fd=fd, qd=qd),
        grid=(nt,),
        in_specs=[pl.BlockSpec((TM, d), row),
                  pl.BlockSpec((1, N_MOD, d), mod_idx),
                  pl.BlockSpec((1, d), lambda i: (0, 0)),
                  pl.BlockSpec(w_a.shape, lambda i: (0, 0)),
                  pl.BlockSpec((TM, LANES), rope_idx),
                  pl.BlockSpec((TM, LANES), rope_idx)],
        out_specs=[pl.BlockSpec((TM, fd), row), pl.BlockSpec((TM, qd), row),
                   pl.BlockSpec((TM, 2 * KV_DIM), row), pl.BlockSpec((2 * KV_DIM, TM), lambda i: (0, i))],
        out_shape=[jax.ShapeDtypeStruct((tt, fd), _BF), jax.ShapeDtypeStruct((tt, qd), _BF),
                   jax.ShapeDtypeStruct((tt, 2 * KV_DIM), _BF), jax.ShapeDtypeStruct((2 * KV_DIM, tt), _BF)],
        compiler_params=_params("arbitrary"),
        name="proj_in",
    )(xa, mod, g, w_a, cos_t, sin_t)


def _rope_tables(n):
    pos = jnp.arange(n)
    inv = ROPE_THETA ** (-jnp.arange(ROPE_FREQS, dtype=_F32) / ROPE_FREQS)
    ar = (pos // GRID_W).astype(_F32)[:, None] * inv
    ac = (pos % GRID_W).astype(_F32)[:, None] * inv
    cos = jnp.concatenate([jnp.cos(ar), jnp.cos(ar), jnp.cos(ac), jnp.cos(ac)], axis=1)
    sin = jnp.concatenate([-jnp.sin(ar), jnp.sin(ar), -jnp.sin(ac), jnp.sin(ac)], axis=1)
    cos = jnp.tile(cos, (1, LANES // HEAD_DIM))
    sin = jnp.tile(sin, (1, LANES // HEAD_DIM))
    cos = jnp.concatenate([cos, jnp.ones((TM, LANES), _F32)], axis=0)
    sin = jnp.concatenate([sin, jnp.zeros((TM, LANES), _F32)], axis=0)
    return cos, sin


def _dft_consts(n, ctx_len):
    n1 = n // BLOCK
    b = np.arange(BLOCK, dtype=np.int64)[:, None, None]
    k1 = np.arange(n1, dtype=np.int64)[None, :, None]
    a = np.arange(n1, dtype=np.int64)[None, None, :]
    ang = 2.0 * np.pi * ((a * k1 * BLOCK + b * k1) % n).astype(np.float64) / n
    m_ri = np.concatenate([np.cos(ang), -np.sin(ang)], axis=1)
    kk = np.arange(GROUP_DIM, dtype=np.int64)
    ang_c = 2.0 * np.pi * ((kk[:, None] * kk[None, :]) % GROUP_DIM) / GROUP_DIM
    c, s = np.cos(ang_c), np.sin(ang_c)
    cs = np.concatenate([np.concatenate([c, s], axis=1), np.concatenate([-s, c], axis=1)], axis=0)
    cc = np.concatenate([c, s], axis=0)
    kl = np.arange(ctx_len, dtype=np.int64)
    ang_l = 2.0 * np.pi * ((kl[:, None] * kl[None, :]) % ctx_len) / ctx_len
    wl = np.concatenate([np.cos(ang_l), -np.sin(ang_l)], axis=0)
    as_bf = lambda v: jnp.asarray(v, dtype=_F32).astype(_BF)
    return dict(m_ri=as_bf(m_ri), cs=as_bf(cs), cc=as_bf(cc), wl=as_bf(wl))


def _chan_dft(xr, xi, cc, scale):
    outs = []
    for g in range(xr.shape[1] // GROUP_DIM):
        sl = slice(g * GROUP_DIM, (g + 1) * GROUP_DIM)
        xx = jnp.concatenate([xr[:, sl], xi[:, sl]], axis=1).astype(_BF)
        outs.append(_dot(xx, cc) * scale)
    return jnp.concatenate(outs, axis=1)


def _fa_kernel(f_ref, m_ref, zr_ref, zi_ref):
    n1 = f_ref.shape[0]
    for t in range(FA_NB):
        zz = _dot(m_ref[t], f_ref[:, t, :])
        zr_ref[:, t, :] = zz[:n1].astype(_BF)
        zi_ref[:, t, :] = zz[n1:].astype(_BF)


def _fb_kernel(zr_ref, zi_ref, cs_ref, cc_ref, o_ref, *, scale):
    cs, cc = cs_ref[...], cc_ref[...]
    for j in range(FB_NB):
        xx = _dot(cs, jnp.concatenate([zr_ref[j], zi_ref[j]], axis=0))
        o_ref[:, j, :] = _chan_dft(xx[:BLOCK], xx[BLOCK:], cc, scale).astype(_BF)


def _fourier_lat(f, consts, *, n_batch, n):
    tt, fd = f.shape
    n1 = n // BLOCK
    f3 = f.reshape(tt // BLOCK, BLOCK, fd)
    blk_a = pl.BlockSpec((n1, FA_NB, fd), lambda b, j: (b, j, 0))
    mat_a = pl.BlockSpec((FA_NB, 2 * n1, n1), lambda b, j: (j, 0, 0))
    z_shape = jax.ShapeDtypeStruct((n_batch * n1, BLOCK, fd), _BF)
    zr, zi = pl.pallas_call(
        _fa_kernel,
        grid=(n_batch, BLOCK // FA_NB),
        in_specs=[blk_a, mat_a],
        out_specs=[blk_a, blk_a],
        out_shape=[z_shape, z_shape],
        compiler_params=_params("arbitrary", "arbitrary"),
        name="fourier_seq_a",
    )(f3, consts["m_ri"])
    blk_z = pl.BlockSpec((FB_NB, BLOCK, fd), lambda b, j: (b * (n1 // FB_NB) + j, 0, 0))
    const = lambda shape: pl.BlockSpec(shape, lambda b, j: (0, 0))
    out = pl.pallas_call(
        functools.partial(_fb_kernel, scale=float((n * GROUP_DIM) ** -0.5)),
        grid=(n_batch, n1 // FB_NB),
        in_specs=[blk_z, blk_z, const((2 * BLOCK, 2 * BLOCK)), const((2 * GROUP_DIM, GROUP_DIM))],
        out_specs=pl.BlockSpec((BLOCK, FB_NB, fd), lambda b, j: (b, j, 0)),
        out_shape=jax.ShapeDtypeStruct((n_batch * BLOCK, n1, fd), _BF),
        compiler_params=_params("arbitrary", "arbitrary"),
        name="fourier_seq_b",
    )(zr, zi, consts["cs"], consts["cc"])
    return out.reshape(n_batch * n, fd)


def _fc_kernel(f_ref, wl_ref, cc_ref, o_ref, *, scale):
    ctx_len = f_ref.shape[0]
    xx = _dot(wl_ref[...], f_ref[...])
    o_ref[...] = _chan_dft(xx[:ctx_len], xx[ctx_len:], cc_ref[...], scale).astype(_BF)


def _fourier_ctx(f, consts, *, n_batch, ctx_len, t_lat):
    fd = f.shape[1]
    const = lambda shape: pl.BlockSpec(shape, lambda b: (0, 0))
    return pl.pallas_call(
        functools.partial(_fc_kernel, scale=float((ctx_len * GROUP_DIM) ** -0.5)),
        grid=(n_batch,),
        in_specs=[pl.BlockSpec((ctx_len, fd), lambda b: (t_lat // ctx_len + b, 0)),
                  const((2 * ctx_len, ctx_len)), const((2 * GROUP_DIM, GROUP_DIM))],
        out_specs=pl.BlockSpec((ctx_len, fd), lambda b: (b, 0)),
        out_shape=jax.ShapeDtypeStruct((n_batch * ctx_len, fd), _BF),
        compiler_params=_params("arbitrary"),
        name="fourier_ctx",
    )(f, consts["wl"], consts["cc"])


def _attn_core(sink_ref, q_ref, k, vt, o_ref, bias, s_scr, p_scr):
    tq = q_ref.shape[0]
    lane = lax.broadcasted_iota(jnp.int32, (tq, LANES), 1)
    low = lane < HEAD_DIM
    zero = jnp.zeros((tq, LANES), _BF)
    low_row = lax.broadcasted_iota(jnp.int32, (LANES, 2 * tq), 0) < HEAD_DIM
    first = lax.broadcasted_iota(jnp.int32, (1, 2 * tq), 1) < tq
    pairs = [(kvh, half) for kvh in range(N_KV_HEADS) for half in range(2)]
    sels = [slice(0, LANES) if half == kvh else slice(LANES, 2 * LANES) for kvh, half in pairs]
    for pi, (kvh, half) in enumerate(pairs):
        slabs = [q_ref[:, (2 * kvh + c) * LANES:(2 * kvh + c + 1) * LANES] for c in range(2)]
        keep = low if half == 0 else jnp.logical_not(low)
        qm = jnp.concatenate([jnp.where(keep, sl, zero) for sl in slabs], axis=0)
        s = lax.dot_general(k[:, sels[pi]], qm, (((1,), (1,)), ((), ())), preferred_element_type=_F32)
        s_scr[pi] = s if bias is None else s + bias
    dens = []
    for pi, (kvh, half) in enumerate(pairs):
        hq = Q_PER_KV * kvh + half
        sk = jnp.where(first, sink_ref[hq], sink_ref[hq + 2]) * LOG2E
        s = s_scr[pi]
        m = jnp.maximum(jnp.max(s, axis=0, keepdims=True), sk)
        p = jnp.exp2(s - m)
        dens.append(jnp.sum(p, axis=0, keepdims=True) + jnp.exp2(sk - m))
        p_scr[pi] = p.astype(_BF)
    outs = [_dot(vt[sels[pi], :], p_scr[pi]) * (1.0 / dens[pi]) for pi in range(len(pairs))]
    for kvh in range(N_KV_HEADS):
        ot = jnp.where(low_row, outs[2 * kvh], outs[2 * kvh + 1])
        for c in range(2):
            o_ref[:, (2 * kvh + c) * LANES:(2 * kvh + c + 1) * LANES] = ot[:, c * tq:(c + 1) * tq].T.astype(_BF)


def _attn_scratch(keys, tq):
    return [pltpu.VMEM((2 * N_KV_HEADS, keys, 2 * tq), _F32), pltpu.VMEM((2 * N_KV_HEADS, keys, 2 * tq), _BF)]


def _attn_lat_kernel(sink_ref, q_ref, kp_ref, kc_ref, kn_ref, vp_ref, vc_ref, vn_ref, kx_ref, vx_ref, bias_ref,
                     o_ref, s_scr, p_scr):
    k = jnp.concatenate([kp_ref[...], kc_ref[...], kn_ref[...], kx_ref[...]], axis=0)
    vt = jnp.concatenate([vp_ref[...], vc_ref[...], vn_ref[...], vx_ref[...]], axis=1)
    _attn_core(sink_ref, q_ref, k, vt, o_ref, bias_ref[0], s_scr, p_scr)


def _window_bias(ctx_len):
    s = np.arange(3 * BLOCK + ctx_len)[:, None]
    r = np.arange(2 * BLOCK)[None, :] % BLOCK
    out = []
    for v in range(4):
        lo = 0 if v & 1 else BLOCK
        hi = 3 * BLOCK if v & 2 else 2 * BLOCK
        ok = ((np.abs(s - BLOCK - r) <= WINDOW) & (s >= lo) & (s < hi)) | (s >= 3 * BLOCK)
        out.append(np.where(ok, 0.0, NEG_INF))
    return jnp.asarray(np.stack(out), dtype=_F32)


def _attn_ctx_kernel(sink_ref, q_ref, kx_ref, vx_ref, o_ref, s_scr, p_scr):
    _attn_core(sink_ref, q_ref, kx_ref[...], vx_ref[...], o_ref, None, s_scr, p_scr)


def _attn_lat(sink, q, k2, v2t, *, n_batch, n, ctx_len, t_lat):
    qd = q.shape[1]
    nb = n // BLOCK
    cur = lambda b, i: (b * nb + i, 0)
    prev = lambda b, i: (b * nb + jnp.maximum(i - 1, 0), 0)
    nxt = lambda b, i: (b * nb + jnp.minimum(i + 1, nb - 1), 0)
    cx = lambda b, i: (t_lat // ctx_len + b, 0)
    swap = lambda im: (lambda b, i: im(b, i)[::-1])
    kb = lambda im: pl.BlockSpec((BLOCK, 2 * KV_DIM), im)
    vb = lambda im: pl.BlockSpec((2 * KV_DIM, BLOCK), swap(im))
    bias = _window_bias(ctx_len)
    bias_idx = lambda b, i: ((i > 0).astype(jnp.int32) + 2 * (i < nb - 1).astype(jnp.int32), 0, 0)
    return pl.pallas_call(
        _attn_lat_kernel,
        grid=(n_batch, nb),
        in_specs=[pl.BlockSpec(memory_space=pltpu.SMEM), pl.BlockSpec((BLOCK, qd), cur),
                  kb(prev), kb(cur), kb(nxt), vb(prev), vb(cur), vb(nxt),
                  pl.BlockSpec((ctx_len, 2 * KV_DIM), cx), pl.BlockSpec((2 * KV_DIM, ctx_len), swap(cx)),
                  pl.BlockSpec((1,) + bias.shape[1:], bias_idx)],
        out_specs=pl.BlockSpec((BLOCK, qd), cur),
        out_shape=jax.ShapeDtypeStruct((t_lat, qd), _BF),
        scratch_shapes=_attn_scratch(3 * BLOCK + ctx_len, BLOCK),
        compiler_params=_params("arbitrary", "arbitrary"),
        name="attn_lat",
    )(sink, q, k2, k2, k2, v2t, v2t, v2t, k2, v2t, bias)


def _attn_ctx(sink, q, k2, v2t, *, n_batch, ctx_len, t_lat):
    qd = q.shape[1]
    cx = lambda b: (t_lat // ctx_len + b, 0)
    return pl.pallas_call(
        _attn_ctx_kernel,
        grid=(n_batch,),
        in_specs=[pl.BlockSpec(memory_space=pltpu.SMEM), pl.BlockSpec((ctx_len, qd), cx),
                  pl.BlockSpec((ctx_len, 2 * KV_DIM), cx),
                  pl.BlockSpec((2 * KV_DIM, ctx_len), lambda b: (0, t_lat // ctx_len + b))],
        out_specs=pl.BlockSpec((ctx_len, qd), lambda b: (b, 0)),
        out_shape=jax.ShapeDtypeStruct((n_batch * ctx_len, qd), _BF),
        scratch_shapes=_attn_scratch(ctx_len, ctx_len),
        compiler_params=_params("arbitrary"),
        name="attn_ctx",
    )(sink, q, k2, v2t)


def _pack_bf16_pairs(h):
    half = h.shape[1] // 2
    lo = lax.bitcast_convert_type(h[:, :half].astype(_BF).astype(_F32), jnp.uint32)
    hi = lax.bitcast_convert_type(h[:, half:].astype(_BF).astype(_F32), jnp.uint32)
    return (lo >> 16) | hi


def _unpack_bf16_pairs(p):
    lo = lax.bitcast_convert_type(p << 16, _F32)
    hi = lax.bitcast_convert_type(p & jnp.uint32(0xFFFF0000), _F32)
    return jnp.concatenate([lo, hi], axis=1).astype(_BF)


def _merge_kernel(x_ref, mod_ref, g1_ref, g2_ref, wg_ref, fml_ref, fmc_ref, atl_ref, atc_ref,
                  wfo_ref, wao_ref, wout_ref, rw_ref, rb_ref,
                  x1_ref, hp_ref, idx_ref, gate_ref, cnt_ref, h_scr, y_scr, *, nxt):
    i = pl.program_id(0)
    d = x_ref.shape[1]
    m = mod_ref[0]
    is_lat = i < nxt
    rows_per = x_ref.shape[0] // MERGE_ROW_SPLIT
    sel_sum = jnp.zeros((1, LANES), _F32)
    for grp in range(MERGE_ROW_SPLIT):
        rows = slice(grp * rows_per, (grp + 1) * rows_per)
        h_scr[rows, :] = _norm_mod(x_ref[rows, :], g1_ref[...], m[0:1], m[1:2]).astype(_BF)
        fm = jnp.where(is_lat, fml_ref[rows, :], fmc_ref[rows, :])
        at = jnp.where(is_lat, atl_ref[rows, :], atc_ref[rows, :])
        for c in range(d // MERGE_COLS):
            sl = slice(c * MERGE_COLS, (c + 1) * MERGE_COLS)
            sg = slice(d + c * MERGE_COLS, d + (c + 1) * MERGE_COLS)
            h = h_scr[rows, :]
            y = (_sigmoid(_dot(h, wg_ref[:, sl])) * _dot(fm, wfo_ref[:, sl])
                 + _sigmoid(_dot(h, wg_ref[:, sg])) * _dot(at, wao_ref[:, sl]))
            y_scr[rows, sl] = y.astype(_BF)
        x1_ref[rows, :] = x_ref[rows, :] + m[2:3] * _dot(y_scr[rows, :], wout_ref[...])
        h2 = _norm_mod(x1_ref[rows, :], g2_ref[...], m[3:4], m[4:5])
        hp_ref[rows, :] = _pack_bf16_pairs(h2)
        logits = _dot(h2.astype(_BF), rw_ref[...]) + rb_ref[...]
        lane = lax.broadcasted_iota(jnp.int32, logits.shape, 1)
        vals, idxs = [], []
        for _ in range(TOP_K):
            mx = jnp.max(logits, axis=-1, keepdims=True)
            ix = jnp.min(jnp.where(logits == mx, lane, LANES), axis=-1, keepdims=True)
            logits = jnp.where(lane == ix, NEG_INF, logits)
            vals.append(mx)
            idxs.append(ix)
        es = [jnp.exp(v - vals[0]) for v in vals]
        den = es[0] + es[1] + es[2] + es[3]
        idx4 = jnp.zeros(logits.shape, jnp.int32)
        gate4 = jnp.zeros(logits.shape, _F32)
        sel = jnp.zeros(logits.shape, _F32)
        for k in range(TOP_K):
            idx4 = jnp.where(lane == k, idxs[k], idx4)
            gate4 = jnp.where(lane == k, es[k] / den, gate4)
            sel = sel + jnp.where(lane == idxs[k], 1.0, 0.0)
        idx_ref[rows, :] = idx4
        gate_ref[rows, :] = gate4
        sel_sum = sel_sum + jnp.sum(sel, axis=0, keepdims=True)

    @pl.when(i == 0)
    def _():
        cnt_ref[...] = jnp.zeros(cnt_ref.shape, _F32)

    cnt_ref[0:1, :] += sel_sum


def _merge(xa, mod, g1, g2, w_g, fm_lat, fm_ctx, at_lat, at_ctx, w_fo, w_ao, w_out, rw, rb,
           *, n, t_lat, n_batch):
    tt, d = xa.shape
    fd = fm_lat.shape[1]
    nt = tt // TM
    nxt = t_lat // TM
    row = lambda i: (i, 0)
    lat = lambda i: (jnp.minimum(i, nxt - 1), 0)
    cxt = lambda i: (jnp.maximum(i - nxt, 0), 0)
    const = lambda a: pl.BlockSpec(a.shape, lambda i: (0, 0))
    mod_idx = lambda i: (jnp.minimum(i * TM // n, n_batch), 0, 0)
    return pl.pallas_call(
        functools.partial(_merge_kernel, nxt=nxt),
        grid=(nt,),
        in_specs=[pl.BlockSpec((TM, d), row), pl.BlockSpec((1, N_MOD, d), mod_idx),
                  const(g1), const(g2), const(w_g),
                  pl.BlockSpec((TM, fd), lat), pl.BlockSpec((TM, fd), cxt),
                  pl.BlockSpec((TM, fd), lat), pl.BlockSpec((TM, fd), cxt),
                  const(w_fo), const(w_ao), const(w_out), const(rw), const(rb)],
        out_specs=[pl.BlockSpec((TM, d), row), pl.BlockSpec((TM, d // 2), row),
                   pl.BlockSpec((TM, LANES), row), pl.BlockSpec((TM, LANES), row),
                   pl.BlockSpec((8, LANES), lambda i: (0, 0))],
        out_shape=[jax.ShapeDtypeStruct((tt, d), _F32), jax.ShapeDtypeStruct((tt, d // 2), jnp.uint32),
                   jax.ShapeDtypeStruct((tt, LANES), jnp.int32), jax.ShapeDtypeStruct((tt, LANES), _F32),
                   jax.ShapeDtypeStruct((8, LANES), _F32)],
        scratch_shapes=[pltpu.VMEM((TM, d), _BF), pltpu.VMEM((TM, d), _BF)],
        compiler_params=_params("arbitrary"),
        name="merge_router",
    )(xa, mod, g1, g2, w_g, fm_lat, fm_ctx, at_lat, at_ctx, w_fo, w_ao, w_out, rw, rb)


def _route_kernel(idx_ref, ps_ref, d_ref, carry):
    i = pl.program_id(0)

    @pl.when(i == 0)
    def _():
        carry[...] = ps_ref[...]

    idx4 = idx_ref[...]
    lane = lax.broadcasted_iota(jnp.int32, idx4.shape, 1)
    cols = [idx4[:, k:k + 1] for k in range(TOP_K)]
    sel = jnp.zeros(idx4.shape, _F32)
    for k in range(TOP_K):
        sel = sel + jnp.where(lane == cols[k], 1.0, 0.0)
    r = lax.broadcasted_iota(jnp.int32, (TM, TM), 0)
    c = lax.broadcasted_iota(jnp.int32, (TM, TM), 1)
    tri = jnp.where(r > c, 1.0, 0.0).astype(_BF)
    slot = carry[...] + _dot(tri, sel.astype(_BF))
    dest4 = jnp.zeros(idx4.shape, jnp.int32)
    for k in range(TOP_K):
        dk = jnp.sum(jnp.where(lane == cols[k], slot, 0.0), axis=-1, keepdims=True)
        dest4 = jnp.where(lane == k, dk.astype(jnp.int32), dest4)
    d_ref[...] = dest4
    carry[...] += jnp.sum(sel, axis=0, keepdims=True)


def _route(idx4, pad_start):
    tt = idx4.shape[0]
    return pl.pallas_call(
        _route_kernel,
        grid=(tt // TM,),
        in_specs=[pl.BlockSpec((TM, LANES), lambda i: (i, 0)), pl.BlockSpec((1, LANES), lambda i: (0, 0))],
        out_specs=pl.BlockSpec((TM, LANES), lambda i: (i, 0)),
        out_shape=jax.ShapeDtypeStruct((tt, LANES), jnp.int32),
        scratch_shapes=[pltpu.VMEM((1, LANES), _F32)],
        compiler_params=_params("arbitrary"),
        name="route_slots",
    )(idx4, pad_start)


def _zero_tails_kernel(pend_ref, cnt_ref, nv_ref, xs_ref, zbuf, sem):
    n_blocks = xs_ref.shape[0] // TME
    zbuf[...] = jnp.zeros(zbuf.shape, zbuf.dtype)

    def zero_block(start):
        return pltpu.make_async_copy(zbuf, xs_ref.at[pl.ds(pl.multiple_of(start, TME), TME)], sem)

    for e in range(N_EXPERTS):
        @pl.when(cnt_ref[e] > 0)
        def _():
            zero_block(pend_ref[e] - TME).start()

    def start_unused(j, carry):
        zero_block(j * TME).start()
        return carry

    def wait_unused(j, carry):
        zero_block(j * TME).wait()
        return carry

    lax.fori_loop(nv_ref[0], n_blocks, start_unused, 0)
    for e in range(N_EXPERTS):
        @pl.when(cnt_ref[e] > 0)
        def _():
            zero_block(pend_ref[e] - TME).wait()
    lax.fori_loop(nv_ref[0], n_blocks, wait_unused, 0)


def _zero_tails(pad_end, cnt, n_valid, n_slots, w, dtype):
    grid_spec = pltpu.PrefetchScalarGridSpec(
        num_scalar_prefetch=3,
        grid=(1,),
        in_specs=[],
        out_specs=pl.BlockSpec(memory_space=pl.ANY),
        scratch_shapes=[pltpu.VMEM((TME, w), dtype), pltpu.SemaphoreType.DMA],
    )
    return pl.pallas_call(
        _zero_tails_kernel,
        grid_spec=grid_spec,
        out_shape=jax.ShapeDtypeStruct((n_slots, w), dtype),
        compiler_params=_params("arbitrary"),
        name="moe_zero_tails",
    )(pad_end, cnt, n_valid)


def _sc_scatter(rows, idx, base):
    tt, w = rows.shape
    workers = SC_CORES * SC_SUBCORES
    per_worker = tt // workers
    assert tt % workers == 0 and per_worker % SC_SCATTER_WINDOW == 0
    mesh = plsc.VectorSubcoreMesh(core_axis_name="core", subcore_axis_name="subcore")
    out = jax.new_ref(base)

    @pl.kernel(out_type=(), mesh=mesh,
               scratch_types=[pltpu.VMEM((SC_SCATTER_WINDOW,), jnp.int32),
                              pltpu.VMEM((SC_SCATTER_WINDOW, w), rows.dtype), pltpu.SemaphoreType.DMA])
    def scatter(r_hbm, i_hbm, o_hbm, idx_v, rows_v, sem):
        wid = lax.axis_index("subcore") * SC_CORES + lax.axis_index("core")

        @pl.loop(0, per_worker // SC_SCATTER_WINDOW)
        def _(c):
            first = wid * per_worker + c * SC_SCATTER_WINDOW
            pltpu.sync_copy(r_hbm.at[pl.ds(first, SC_SCATTER_WINDOW)], rows_v)
            for k in range(TOP_K):
                pltpu.sync_copy(i_hbm.at[pl.ds(k * tt + first, SC_SCATTER_WINDOW)], idx_v)
                pltpu.async_copy(rows_v, o_hbm.at[idx_v], sem).wait()

    scatter(rows, idx, out)
    return jax.freeze(out)


def _moe_kernel(be_ref, nv_ref, xs_ref, wgu_ref, bgu_ref, wd_ref, bd_ref, y_ref, wgu_bf, wd_bf):
    i = pl.program_id(0)
    de = wd_ref.shape[2]
    valid = i < nv_ref[0]
    new_expert = jnp.logical_or(i == 0, be_ref[i] != be_ref[jnp.maximum(i - 1, 0)])

    @pl.when(jnp.logical_and(valid, new_expert))
    def _():
        def cast_rows(ref, out, rows):
            def body(j, carry):
                sl = pl.ds(pl.multiple_of(j * CAST_ROWS, CAST_ROWS), CAST_ROWS)
                out[sl, :] = ref[0, 0, sl, :].astype(_BF)
                return carry
            lax.fori_loop(0, rows // CAST_ROWS, body, 0)
        cast_rows(wgu_ref, wgu_bf, wgu_ref.shape[2])
        cast_rows(wd_ref, wd_bf, de)

    @pl.when(valid)
    def _():
        xb = _unpack_bf16_pairs(xs_ref[...])
        gu = _dot(xb, wgu_bf[...]) + bgu_ref[0, 0]
        a = jnp.minimum(gu[:, :de], SWIGLU_LIMIT)
        u = jnp.clip(gu[:, de:], -SWIGLU_LIMIT, SWIGLU_LIMIT)
        act = a * _sigmoid(SWIGLU_ALPHA * a) * (u + 1)
        y_ref[...] = _pack_bf16_pairs(_dot(act.astype(_BF), wd_bf[...]) + bd_ref[0, 0])

    @pl.when(i >= nv_ref[0])
    def _():
        y_ref[...] = jnp.zeros(y_ref.shape, y_ref.dtype)


def _moe(block_e, n_valid, xs, w_gu, b_gu, w_down, b_down, *, layer):
    n_slots, w = xs.shape
    depth, ne, d, de2 = w_gu.shape
    de = de2 // 2
    n_blocks = n_slots // TME
    ex = lambda i, be, nv: (layer, be[i], 0, 0)
    grid_spec = pltpu.PrefetchScalarGridSpec(
        num_scalar_prefetch=2,
        grid=(n_blocks,),
        in_specs=[pl.BlockSpec((TME, w), lambda i, be, nv: (jnp.minimum(i, nv[0] - 1), 0)),
                  pl.BlockSpec((1, 1, d, de2), ex), pl.BlockSpec((1, 1, 1, de2), ex),
                  pl.BlockSpec((1, 1, de, d), ex), pl.BlockSpec((1, 1, 1, d), ex)],
        out_specs=pl.BlockSpec((TME, d // 2), lambda i, be, nv: (i, 0)),
        scratch_shapes=[pltpu.VMEM((d, de2), _BF), pltpu.VMEM((de, d), _BF)],
    )
    return pl.pallas_call(
        _moe_kernel,
        grid_spec=grid_spec,
        out_shape=jax.ShapeDtypeStruct((n_slots, d // 2), jnp.uint32),
        compiler_params=_params("arbitrary"),
        name="moe_experts",
    )(block_e, n_valid, xs, w_gu, b_gu.reshape(depth, ne, 1, de2), w_down, b_down.reshape(depth, ne, 1, d))


def _sc_gather(table, idx):
    n_idx, (_, w) = idx.shape[0], table.shape
    workers = SC_CORES * SC_SUBCORES
    per_worker = n_idx // workers
    assert n_idx % workers == 0 and per_worker % SC_WINDOW == 0
    mesh = plsc.VectorSubcoreMesh(core_axis_name="core", subcore_axis_name="subcore")

    @pl.kernel(out_type=jax.ShapeDtypeStruct((n_idx, w), table.dtype), mesh=mesh,
               scratch_types=[pltpu.VMEM((SC_WINDOW,), jnp.int32), pltpu.VMEM((SC_WINDOW, w), table.dtype),
                              pltpu.SemaphoreType.DMA])
    def gather(t_hbm, i_hbm, o_hbm, idx_v, rows_v, sem):
        wid = lax.axis_index("subcore") * SC_CORES + lax.axis_index("core")

        @pl.loop(0, per_worker // SC_WINDOW)
        def _(c):
            base = wid * per_worker + c * SC_WINDOW
            pltpu.sync_copy(i_hbm.at[pl.ds(base, SC_WINDOW)], idx_v)
            pltpu.async_copy(t_hbm.at[idx_v], rows_v, sem).wait()
            pltpu.sync_copy(rows_v, o_hbm.at[pl.ds(base, SC_WINDOW)])

    return gather(table, idx)


def _combine_kernel(yg_ref, gate_ref, x_ref, mod_ref, fg_ref, o_ref, *, final):
    g = gate_ref[...]
    half = x_ref.shape[1] // 2
    lo = jnp.zeros((x_ref.shape[0], half), _F32)
    hi = jnp.zeros((x_ref.shape[0], half), _F32)
    for k in range(TOP_K):
        p = yg_ref[k]
        lo = lo + g[:, k:k + 1] * lax.bitcast_convert_type(p << 16, _F32)
        hi = hi + g[:, k:k + 1] * lax.bitcast_convert_type(p & jnp.uint32(0xFFFF0000), _F32)
    x2 = x_ref[...] + mod_ref[0][5:6] * jnp.concatenate([lo, hi], axis=1)
    if final:
        ms = jnp.mean(x2 * x2, axis=-1, keepdims=True)
        x2 = x2 * lax.rsqrt(ms + EPS) * fg_ref[...]
    o_ref[...] = x2


def _combine(yg, gate4, x1, mod, fg, *, n, n_batch, rows, final):
    d = x1.shape[1]
    row = lambda i: (i, 0)
    mod_idx = lambda i: (jnp.minimum(i * TMC // n, n_batch), 0, 0)
    return pl.pallas_call(
        functools.partial(_combine_kernel, final=final),
        grid=(rows // TMC,),
        in_specs=[pl.BlockSpec((TOP_K, TMC, d // 2), lambda i: (0, i, 0)),
                  pl.BlockSpec((TMC, LANES), row), pl.BlockSpec((TMC, d), row),
                  pl.BlockSpec((1, N_MOD, d), mod_idx), pl.BlockSpec((1, d), lambda i: (0, 0))],
        out_specs=pl.BlockSpec((TMC, d), row),
        out_shape=jax.ShapeDtypeStruct((rows, d), _F32),
        compiler_params=_params("arbitrary"),
        name="moe_combine",
    )(yg, gate4, x1, mod, fg)


def _routing_tables(counts, n_blocks):
    cnt = counts[0, :N_EXPERTS].astype(jnp.int32)
    padded = (cnt + TME - 1) // TME * TME
    pad_end = jnp.cumsum(padded)
    pad_start = pad_end - padded
    ps = jnp.zeros((1, LANES), _F32).at[0, :N_EXPERTS].set(pad_start.astype(_F32))
    blk = jnp.arange(n_blocks, dtype=jnp.int32) * TME
    block_e = jnp.sum((pad_end[None, :] <= blk[:, None]).astype(jnp.int32), axis=1)
    e_last = jnp.max(jnp.where(cnt > 0, jnp.arange(N_EXPERTS, dtype=jnp.int32), 0))
    block_e = jnp.minimum(block_e, e_last).astype(jnp.int32)
    n_valid = (pad_end[-1] // TME).astype(jnp.int32).reshape(1)
    return ps, block_e, n_valid, pad_end.astype(jnp.int32), cnt


def kernel(x, c, ctx, c_ctx, ada_w, ada_b, norm1_g, norm2_g, w_in, attn_sink, w_fourier_out, w_attn_out,
           w_out, router_w, router_b, expert_w_gu, expert_b_gu, expert_w_down, expert_b_down, final_norm_g):
    n_batch, n, d = x.shape
    ctx_len = ctx.shape[1]
    depth = ada_w.shape[0]
    t_lat = n_batch * n
    t_ctx = n_batch * ctx_len
    tt = t_lat + t_ctx
    fd = qd = d // 2
    assert n % TM == 0 and t_ctx % TM == 0 and n % (BLOCK * FB_NB) == 0 and n_batch < MOD_ROWS
    assert t_lat % ctx_len == 0 and TM % ctx_len == 0

    xa = jnp.concatenate([x.reshape(t_lat, d), ctx.reshape(t_ctx, d)], axis=0)
    cc = jnp.concatenate([c, c_ctx[None, :], jnp.zeros((MOD_ROWS - n_batch - 1, d), _F32)], axis=0)
    mod = _ada(cc, ada_w, ada_b).reshape(depth, MOD_ROWS, N_MOD, d)
    cos_t, sin_t = _rope_tables(n)
    consts = _dft_consts(n, ctx_len)

    n_in = fd + qd + 2 * KV_DIM
    w_a = w_in[:, :, :n_in].astype(_BF)
    w_g = w_in[:, :, n_in:].astype(_BF)
    w_fo, w_ao, w_o = w_fourier_out.astype(_BF), w_attn_out.astype(_BF), w_out.astype(_BF)
    rw = jnp.pad(router_w, ((0, 0), (0, 0), (0, LANES - N_EXPERTS))).astype(_BF)
    rb = jnp.pad(router_b, ((0, 0), (0, LANES - N_EXPERTS)), constant_values=NEG_INF)[:, None, :]

    n_blocks = -(-(tt * TOP_K + N_EXPERTS * (TME - 1)) // TME)
    dims = dict(n=n, t_lat=t_lat, n_batch=n_batch)
    for l in range(depth):
        last = l == depth - 1
        f, q, k2, v2 = _proj(xa, mod[l], norm1_g[l][None, :], w_a[l], cos_t, sin_t, **dims)
        fm_lat = _fourier_lat(f, consts, n_batch=n_batch, n=n)
        fm_ctx = _fourier_ctx(f, consts, n_batch=n_batch, ctx_len=ctx_len, t_lat=t_lat)
        at_lat = _attn_lat(attn_sink[l], q, k2, v2, n_batch=n_batch, n=n, ctx_len=ctx_len, t_lat=t_lat)
        at_ctx = _attn_ctx(attn_sink[l], q, k2, v2, n_batch=n_batch, ctx_len=ctx_len, t_lat=t_lat)
        x1, hp, idx4, gate4, counts = _merge(
            xa, mod[l], norm1_g[l][None, :], norm2_g[l][None, :], w_g[l], fm_lat, fm_ctx, at_lat, at_ctx,
            w_fo[l], w_ao[l], w_o[l], rw[l], rb[l], **dims)
        pad_start, block_e, n_valid, pad_end, cnt = _routing_tables(counts, n_blocks)
        dest = _route(idx4, pad_start)[:, :TOP_K].T.reshape(-1)
        xs = _sc_scatter(hp, dest, _zero_tails(pad_end, cnt, n_valid, n_blocks * TME, hp.shape[1], hp.dtype))
        y = _moe(block_e, n_valid, xs, expert_w_gu, expert_b_gu, expert_w_down, expert_b_down, layer=l)
        yg = _sc_gather(y, dest).reshape(TOP_K, tt, d // 2)
        xa = _combine(yg, gate4, x1, mod[l], final_norm_g[None, :], n=n, n_batch=n_batch,
                      rows=t_lat if last else tt, final=last)
    return xa.reshape(n_batch, n, d)
```

```python
import functools

import numpy as np
import jax
import jax.numpy as jnp
from jax import lax
from jax.experimental import pallas as pl
from jax.experimental.pallas import tpu as pltpu
from jax.experimental.pallas import tpu_sc as plsc

GRID_W = 64
HEAD_DIM = 64
N_KV_HEADS = 2
Q_PER_KV = 4
KV_DIM = N_KV_HEADS * HEAD_DIM
WINDOW = 128
BLOCK = 128
ROPE_THETA = 10000.0
ROPE_FREQS = HEAD_DIM // 4
GROUP_DIM = 128
N_EXPERTS = 32
TOP_K = 4
SWIGLU_LIMIT = 7.0
SWIGLU_ALPHA = 1.702
N_MOD = 6
EPS = 1e-5
NEG_INF = -1e30
LOG2E = 1.4426950408889634
Q_SCALE = HEAD_DIM ** -0.5 * LOG2E

LANES = 128
MOD_ROWS = 8
TM = 512
TME = 512
TMC = 512
SC_CORES = 2
SC_SUBCORES = 16
SC_WINDOW = 96
SC_SCATTER_WINDOW = 96
FA_NB = 16
FB_NB = 16
CAST_ROWS = 64
MERGE_COLS = 256
MERGE_ROW_SPLIT = 1
VMEM_LIMIT = 56 * 1024 * 1024

_BF = jnp.bfloat16
_F32 = jnp.float32


def _params(*sem):
    return pltpu.CompilerParams(dimension_semantics=sem, vmem_limit_bytes=VMEM_LIMIT)


def _dot(a, b):
    return jnp.dot(a, b, preferred_element_type=_F32)


def _sigmoid(x):
    return 0.5 * jnp.tanh(0.5 * x) + 0.5


def _norm_mod(x, g, shift, scale):
    ms = jnp.mean(x * x, axis=-1, keepdims=True)
    return (x * lax.rsqrt(ms + EPS) * g) * (1 + scale) + shift


def _ada_kernel(c_ref, w_ref, b_ref, o_ref):
    c = c_ref[...]
    s = c * jax.nn.sigmoid(c)
    o_ref[0] = jnp.dot(s, w_ref[0], precision=lax.Precision.HIGHEST,
                       preferred_element_type=_F32) + b_ref[0]


def _ada(cc, ada_w, ada_b):
    depth, d, nd = ada_w.shape
    tn = nd // 4
    return pl.pallas_call(
        _ada_kernel,
        grid=(depth, nd // tn),
        in_specs=[pl.BlockSpec((MOD_ROWS, d), lambda l, j: (0, 0)),
                  pl.BlockSpec((1, d, tn), lambda l, j: (l, 0, j)),
                  pl.BlockSpec((1, 1, tn), lambda l, j: (l, 0, j))],
        out_specs=pl.BlockSpec((1, MOD_ROWS, tn), lambda l, j: (l, 0, j)),
        out_shape=jax.ShapeDtypeStruct((depth, MOD_ROWS, nd), _F32),
        compiler_params=_params("arbitrary", "arbitrary"),
        name="ada_mod",
    )(cc, ada_w, ada_b.reshape(depth, 1, nd))


def _rope(v, cos, sin):
    lane = lax.broadcasted_iota(jnp.int32, v.shape, 1)
    partner = jnp.where((lane & ROPE_FREQS) == 0,
                        pltpu.roll(v, LANES - ROPE_FREQS, 1), pltpu.roll(v, ROPE_FREQS, 1))
    return v * cos + partner * sin


def _proj_kernel(x_ref, mod_ref, g_ref, w_ref, cos_ref, sin_ref, f_ref, q_ref, k_ref, vt_ref, *, fd, qd):
    m = mod_ref[0]
    h = _norm_mod(x_ref[...], g_ref[...], m[0:1], m[1:2]).astype(_BF)
    p = _dot(h, w_ref[...])
    cos = cos_ref[...]
    sin = sin_ref[...]
    half = BLOCK // 2
    for a in range(p.shape[0] // BLOCK):
        f_ref[a * half:(a + 1) * half, :] = _pack_rows(p[a * BLOCK:a * BLOCK + half, :fd],
                                                       p[a * BLOCK + half:(a + 1) * BLOCK, :fd])
    for j in range(qd // LANES):
        lo = fd + j * LANES
        q_ref[:, j * LANES:(j + 1) * LANES] = (
            _rope(p[:, lo:lo + LANES], cos, sin) * Q_SCALE).astype(_BF)
    k = _rope(p[:, fd + qd:fd + qd + KV_DIM], cos, sin)
    v = p[:, fd + qd + KV_DIM:fd + qd + 2 * KV_DIM]
    k_ref[:, :KV_DIM] = k.astype(_BF)
    k_ref[:, KV_DIM:] = pltpu.roll(k, HEAD_DIM, 1).astype(_BF)
    vt_ref[:KV_DIM, :] = v.T.astype(_BF)
    vt_ref[KV_DIM:, :] = pltpu.roll(v, HEAD_DIM, 1).T.astype(_BF)


def _proj(xa, mod, g, w_a, cos_t, sin_t, *, n, t_lat, n_batch):
    tt, d = xa.shape
    fd = qd = d // 2
    nt = tt // TM
    nxt = t_lat // TM
    per_seq = n // TM
    mod_idx = lambda i: (jnp.minimum(i * TM // n, n_batch), 0, 0)
    rope_idx = lambda i: (jnp.where(i < nxt, i % per_seq, per_seq), 0)
    row = lambda i: (i, 0)
    return pl.pallas_call(
        functools.partial(_proj_kernel, fd=fd, qd=qd),
        grid=(nt,),
        in_specs=[pl.BlockSpec((TM, d), row),
                  pl.BlockSpec((1, N_MOD, d), mod_idx),
                  pl.BlockSpec((1, d), lambda i: (0, 0)),
                  pl.BlockSpec(w_a.shape, lambda i: (0, 0)),
                  pl.BlockSpec((TM, LANES), rope_idx),
                  pl.BlockSpec((TM, LANES), rope_idx)],
        out_specs=[pl.BlockSpec((TM // 2, fd), row), pl.BlockSpec((TM, qd), row),
                   pl.BlockSpec((TM, 2 * KV_DIM), row), pl.BlockSpec((2 * KV_DIM, TM), lambda i: (0, i))],
        out_shape=[jax.ShapeDtypeStruct((tt // 2, fd), jnp.uint32), jax.ShapeDtypeStruct((tt, qd), _BF),
                   jax.ShapeDtypeStruct((tt, 2 * KV_DIM), _BF), jax.ShapeDtypeStruct((2 * KV_DIM, tt), _BF)],
        compiler_params=_params("arbitrary"),
        name="proj_in",
    )(xa, mod, g, w_a, cos_t, sin_t)


def _rope_tables(n):
    pos = jnp.arange(n)
    inv = ROPE_THETA ** (-jnp.arange(ROPE_FREQS, dtype=_F32) / ROPE_FREQS)
    ar = (pos // GRID_W).astype(_F32)[:, None] * inv
    ac = (pos % GRID_W).astype(_F32)[:, None] * inv
    cos = jnp.concatenate([jnp.cos(ar), jnp.cos(ar), jnp.cos(ac), jnp.cos(ac)], axis=1)
    sin = jnp.concatenate([-jnp.sin(ar), jnp.sin(ar), -jnp.sin(ac), jnp.sin(ac)], axis=1)
    cos = jnp.tile(cos, (1, LANES // HEAD_DIM))
    sin = jnp.tile(sin, (1, LANES // HEAD_DIM))
    cos = jnp.concatenate([cos, jnp.ones((TM, LANES), _F32)], axis=0)
    sin = jnp.concatenate([sin, jnp.zeros((TM, LANES), _F32)], axis=0)
    return cos, sin


def _dft_consts(n, ctx_len):
    n1 = n // BLOCK
    b = np.arange(BLOCK, dtype=np.int64)[:, None, None]
    k1 = np.arange(n1, dtype=np.int64)[None, :, None]
    a = np.arange(n1, dtype=np.int64)[None, None, :]
    ang = 2.0 * np.pi * ((a * k1 * BLOCK + b * k1) % n).astype(np.float64) / n
    m_ri = np.concatenate([np.cos(ang), -np.sin(ang)], axis=1)
    kk = np.arange(GROUP_DIM, dtype=np.int64)
    ang_c = 2.0 * np.pi * ((kk[:, None] * kk[None, :]) % GROUP_DIM) / GROUP_DIM
    c, s = np.cos(ang_c), np.sin(ang_c)
    cs = np.concatenate([np.concatenate([c, s], axis=1), np.concatenate([-s, c], axis=1)], axis=0)
    cc = np.concatenate([c, s], axis=0)
    kl = np.arange(ctx_len, dtype=np.int64)
    ang_l = 2.0 * np.pi * ((kl[:, None] * kl[None, :]) % ctx_len) / ctx_len
    wl = np.concatenate([np.cos(ang_l), -np.sin(ang_l)], axis=0)
    as_bf = lambda v: jnp.asarray(v, dtype=_F32).astype(_BF)
    return dict(m_ri=as_bf(m_ri), cs=as_bf(cs), cc=as_bf(cc), wl=as_bf(wl))


def _chan_dft(xr, xi, cc, scale):
    outs = []
    for g in range(xr.shape[1] // GROUP_DIM):
        sl = slice(g * GROUP_DIM, (g + 1) * GROUP_DIM)
        xx = jnp.concatenate([xr[:, sl], xi[:, sl]], axis=1).astype(_BF)
        outs.append(_dot(xx, cc) * scale)
    return jnp.concatenate(outs, axis=1)


def _pack_rows(lo, hi):
    a = lax.bitcast_convert_type(lo.astype(_BF).astype(_F32), jnp.uint32)
    b = lax.bitcast_convert_type(hi.astype(_BF).astype(_F32), jnp.uint32)
    return (a >> 16) | b


def _unpack_rows(p):
    lo = lax.bitcast_convert_type(p << 16, _F32).astype(_BF)
    hi = lax.bitcast_convert_type(p & jnp.uint32(0xFFFF0000), _F32).astype(_BF)
    return lo, hi


def _fa_kernel(f_ref, mlo_ref, mhi_ref, zr_ref, zi_ref):
    n1 = f_ref.shape[0]
    for t in range(FA_NB):
        lo, hi = _unpack_rows(f_ref[:, t, :])
        za = _dot(mlo_ref[t], lo)
        zb = _dot(mhi_ref[t], hi)
        zr_ref[:, t, :] = _pack_rows(za[:n1], zb[:n1])
        zi_ref[:, t, :] = _pack_rows(za[n1:], zb[n1:])


def _fb_kernel(zr_ref, zi_ref, cs_ref, cc_ref, o_ref, *, scale):
    cs, cc = cs_ref[...], cc_ref[...]
    for j in range(FB_NB):
        xx = _dot(cs, jnp.concatenate(_unpack_rows(zr_ref[j]) + _unpack_rows(zi_ref[j]), axis=0))
        o_ref[:, j, :] = _chan_dft(xx[:BLOCK], xx[BLOCK:], cc, scale).astype(_BF)


def _fourier_lat(f, consts, *, n_batch, n):
    fd = f.shape[1]
    n1 = n // BLOCK
    half = BLOCK // 2
    f3 = f.reshape(f.shape[0] // half, half, fd)
    steps = half // FA_NB
    blk_a = pl.BlockSpec((n1, FA_NB, fd), lambda b, j: (b, j, 0))
    mat_lo = pl.BlockSpec((FA_NB, 2 * n1, n1), lambda b, j: (j, 0, 0))
    mat_hi = pl.BlockSpec((FA_NB, 2 * n1, n1), lambda b, j: (j + steps, 0, 0))
    z_shape = jax.ShapeDtypeStruct((n_batch * n1, half, fd), jnp.uint32)
    zr, zi = pl.pallas_call(
        _fa_kernel,
        grid=(n_batch, steps),
        in_specs=[blk_a, mat_lo, mat_hi],
        out_specs=[blk_a, blk_a],
        out_shape=[z_shape, z_shape],
        compiler_params=_params("arbitrary", "arbitrary"),
        name="fourier_seq_a",
    )(f3, consts["m_ri"], consts["m_ri"])
    blk_z = pl.BlockSpec((FB_NB, half, fd), lambda b, j: (b * (n1 // FB_NB) + j, 0, 0))
    const = lambda shape: pl.BlockSpec(shape, lambda b, j: (0, 0))
    out = pl.pallas_call(
        functools.partial(_fb_kernel, scale=float((n * GROUP_DIM) ** -0.5)),
        grid=(n_batch, n1 // FB_NB),
        in_specs=[blk_z, blk_z, const((2 * BLOCK, 2 * BLOCK)), const((2 * GROUP_DIM, GROUP_DIM))],
        out_specs=pl.BlockSpec((BLOCK, FB_NB, fd), lambda b, j: (b, j, 0)),
        out_shape=jax.ShapeDtypeStruct((n_batch * BLOCK, n1, fd), _BF),
        compiler_params=_params("arbitrary", "arbitrary"),
        name="fourier_seq_b",
    )(zr, zi, consts["cs"], consts["cc"])
    return out.reshape(n_batch * n, fd)


def _fc_kernel(f_ref, wl_ref, cc_ref, o_ref, *, scale):
    half = BLOCK // 2
    ctx_len = 2 * f_ref.shape[0]
    lo, hi = _unpack_rows(f_ref[...])
    rows = []
    for a in range(ctx_len // BLOCK):
        rows += [lo[a * half:(a + 1) * half], hi[a * half:(a + 1) * half]]
    xx = _dot(wl_ref[...], jnp.concatenate(rows, axis=0))
    o_ref[...] = _chan_dft(xx[:ctx_len], xx[ctx_len:], cc_ref[...], scale).astype(_BF)


def _fourier_ctx(f, consts, *, n_batch, ctx_len, t_lat):
    fd = f.shape[1]
    const = lambda shape: pl.BlockSpec(shape, lambda b: (0, 0))
    return pl.pallas_call(
        functools.partial(_fc_kernel, scale=float((ctx_len * GROUP_DIM) ** -0.5)),
        grid=(n_batch,),
        in_specs=[pl.BlockSpec((ctx_len // 2, fd), lambda b: (t_lat // ctx_len + b, 0)),
                  const((2 * ctx_len, ctx_len)), const((2 * GROUP_DIM, GROUP_DIM))],
        out_specs=pl.BlockSpec((ctx_len, fd), lambda b: (b, 0)),
        out_shape=jax.ShapeDtypeStruct((n_batch * ctx_len, fd), _BF),
        compiler_params=_params("arbitrary"),
        name="fourier_ctx",
    )(f, consts["wl"], consts["cc"])


def _attn_core(sink_ref, q_ref, k, vt, o_ref, bias, s_scr, p_scr):
    tq = q_ref.shape[0]
    lane = lax.broadcasted_iota(jnp.int32, (tq, LANES), 1)
    low = lane < HEAD_DIM
    zero = jnp.zeros((tq, LANES), _BF)
    low_row = lax.broadcasted_iota(jnp.int32, (LANES, 2 * tq), 0) < HEAD_DIM
    first = lax.broadcasted_iota(jnp.int32, (1, 2 * tq), 1) < tq
    pairs = [(kvh, half) for kvh in range(N_KV_HEADS) for half in range(2)]
    sels = [slice(0, LANES) if half == kvh else slice(LANES, 2 * LANES) for kvh, half in pairs]
    for pi, (kvh, half) in enumerate(pairs):
        slabs = [q_ref[:, (2 * kvh + c) * LANES:(2 * kvh + c + 1) * LANES] for c in range(2)]
        keep = low if half == 0 else jnp.logical_not(low)
        qm = jnp.concatenate([jnp.where(keep, sl, zero) for sl in slabs], axis=0)
        s = lax.dot_general(k[:, sels[pi]], qm, (((1,), (1,)), ((), ())), preferred_element_type=_F32)
        s_scr[pi] = s if bias is None else s + bias
    dens = []
    for pi, (kvh, half) in enumerate(pairs):
        hq = Q_PER_KV * kvh + half
        sk = jnp.where(first, sink_ref[hq], sink_ref[hq + 2]) * LOG2E
        s = s_scr[pi]
        m = jnp.maximum(jnp.max(s, axis=0, keepdims=True), sk)
        p = jnp.exp2(s - m)
        dens.append(jnp.sum(p, axis=0, keepdims=True) + jnp.exp2(sk - m))
        p_scr[pi] = p.astype(_BF)
    outs = [_dot(vt[sels[pi], :], p_scr[pi]) * (1.0 / dens[pi]) for pi in range(len(pairs))]
    for kvh in range(N_KV_HEADS):
        ot = jnp.where(low_row, outs[2 * kvh], outs[2 * kvh + 1])
        for c in range(2):
            o_ref[:, (2 * kvh + c) * LANES:(2 * kvh + c + 1) * LANES] = ot[:, c * tq:(c + 1) * tq].T.astype(_BF)


def _attn_scratch(keys, tq):
    return [pltpu.VMEM((2 * N_KV_HEADS, keys, 2 * tq), _F32), pltpu.VMEM((2 * N_KV_HEADS, keys, 2 * tq), _BF)]


def _attn_lat_kernel(sink_ref, q_ref, kp_ref, kc_ref, kn_ref, vp_ref, vc_ref, vn_ref, kx_ref, vx_ref, bias_ref,
                     o_ref, s_scr, p_scr):
    k = jnp.concatenate([kp_ref[...], kc_ref[...], kn_ref[...], kx_ref[...]], axis=0)
    vt = jnp.concatenate([vp_ref[...], vc_ref[...], vn_ref[...], vx_ref[...]], axis=1)
    _attn_core(sink_ref, q_ref, k, vt, o_ref, bias_ref[0], s_scr, p_scr)


def _window_bias(ctx_len):
    s = np.arange(3 * BLOCK + ctx_len)[:, None]
    r = np.arange(2 * BLOCK)[None, :] % BLOCK
    out = []
    for v in range(4):
        lo = 0 if v & 1 else BLOCK
        hi = 3 * BLOCK if v & 2 else 2 * BLOCK
        ok = ((np.abs(s - BLOCK - r) <= WINDOW) & (s >= lo) & (s < hi)) | (s >= 3 * BLOCK)
        out.append(np.where(ok, 0.0, NEG_INF))
    return jnp.asarray(np.stack(out), dtype=_F32)


def _attn_ctx_kernel(sink_ref, q_ref, kx_ref, vx_ref, o_ref, s_scr, p_scr):
    _attn_core(sink_ref, q_ref, kx_ref[...], vx_ref[...], o_ref, None, s_scr, p_scr)


def _attn_lat(sink, q, k2, v2t, *, n_batch, n, ctx_len, t_lat):
    qd = q.shape[1]
    nb = n // BLOCK
    cur = lambda b, i: (b * nb + i, 0)
    prev = lambda b, i: (b * nb + jnp.maximum(i - 1, 0), 0)
    nxt = lambda b, i: (b * nb + jnp.minimum(i + 1, nb - 1), 0)
    cx = lambda b, i: (t_lat // ctx_len + b, 0)
    swap = lambda im: (lambda b, i: im(b, i)[::-1])
    kb = lambda im: pl.BlockSpec((BLOCK, 2 * KV_DIM), im)
    vb = lambda im: pl.BlockSpec((2 * KV_DIM, BLOCK), swap(im))
    bias = _window_bias(ctx_len)
    bias_idx = lambda b, i: ((i > 0).astype(jnp.int32) + 2 * (i < nb - 1).astype(jnp.int32), 0, 0)
    return pl.pallas_call(
        _attn_lat_kernel,
        grid=(n_batch, nb),
        in_specs=[pl.BlockSpec(memory_space=pltpu.SMEM), pl.BlockSpec((BLOCK, qd), cur),
                  kb(prev), kb(cur), kb(nxt), vb(prev), vb(cur), vb(nxt),
                  pl.BlockSpec((ctx_len, 2 * KV_DIM), cx), pl.BlockSpec((2 * KV_DIM, ctx_len), swap(cx)),
                  pl.BlockSpec((1,) + bias.shape[1:], bias_idx)],
        out_specs=pl.BlockSpec((BLOCK, qd), cur),
        out_shape=jax.ShapeDtypeStruct((t_lat, qd), _BF),
        scratch_shapes=_attn_scratch(3 * BLOCK + ctx_len, BLOCK),
        compiler_params=_params("arbitrary", "arbitrary"),
        name="attn_lat",
    )(sink, q, k2, k2, k2, v2t, v2t, v2t, k2, v2t, bias)


def _attn_ctx(sink, q, k2, v2t, *, n_batch, ctx_len, t_lat):
    qd = q.shape[1]
    cx = lambda b: (t_lat // ctx_len + b, 0)
    return pl.pallas_call(
        _attn_ctx_kernel,
        grid=(n_batch,),
        in_specs=[pl.BlockSpec(memory_space=pltpu.SMEM), pl.BlockSpec((ctx_len, qd), cx),
                  pl.BlockSpec((ctx_len, 2 * KV_DIM), cx),
                  pl.BlockSpec((2 * KV_DIM, ctx_len), lambda b: (0, t_lat // ctx_len + b))],
        out_specs=pl.BlockSpec((ctx_len, qd), lambda b: (b, 0)),
        out_shape=jax.ShapeDtypeStruct((n_batch * ctx_len, qd), _BF),
        scratch_shapes=_attn_scratch(ctx_len, ctx_len),
        compiler_params=_params("arbitrary"),
        name="attn_ctx",
    )(sink, q, k2, v2t)


def _pack_bf16_pairs(h):
    half = h.shape[1] // 2
    lo = lax.bitcast_convert_type(h[:, :half].astype(_BF).astype(_F32), jnp.uint32)
    hi = lax.bitcast_convert_type(h[:, half:].astype(_BF).astype(_F32), jnp.uint32)
    return (lo >> 16) | hi


def _unpack_bf16_pairs(p):
    lo = lax.bitcast_convert_type(p << 16, _F32)
    hi = lax.bitcast_convert_type(p & jnp.uint32(0xFFFF0000), _F32)
    return jnp.concatenate([lo, hi], axis=1).astype(_BF)


def _merge_kernel(x_ref, mod_ref, g1_ref, g2_ref, wg_ref, fml_ref, fmc_ref, atl_ref, atc_ref,
                  wfo_ref, wao_ref, wout_ref, rw_ref, rb_ref,
                  x1_ref, hp_ref, idx_ref, gate_ref, cnt_ref, h_scr, y_scr, *, nxt):
    i = pl.program_id(0)
    d = x_ref.shape[1]
    m = mod_ref[0]
    is_lat = i < nxt
    rows_per = x_ref.shape[0] // MERGE_ROW_SPLIT
    sel_sum = jnp.zeros((1, LANES), _F32)
    for grp in range(MERGE_ROW_SPLIT):
        rows = slice(grp * rows_per, (grp + 1) * rows_per)
        h_scr[rows, :] = _norm_mod(x_ref[rows, :], g1_ref[...], m[0:1], m[1:2]).astype(_BF)
        fm = jnp.where(is_lat, fml_ref[rows, :], fmc_ref[rows, :])
        at = jnp.where(is_lat, atl_ref[rows, :], atc_ref[rows, :])
        for c in range(d // MERGE_COLS):
            sl = slice(c * MERGE_COLS, (c + 1) * MERGE_COLS)
            sg = slice(d + c * MERGE_COLS, d + (c + 1) * MERGE_COLS)
            h = h_scr[rows, :]
            y = (_sigmoid(_dot(h, wg_ref[:, sl])) * _dot(fm, wfo_ref[:, sl])
                 + _sigmoid(_dot(h, wg_ref[:, sg])) * _dot(at, wao_ref[:, sl]))
            y_scr[rows, sl] = y.astype(_BF)
        x1_ref[rows, :] = x_ref[rows, :] + m[2:3] * _dot(y_scr[rows, :], wout_ref[...])
        h2 = _norm_mod(x1_ref[rows, :], g2_ref[...], m[3:4], m[4:5])
        hp_ref[rows, :] = _pack_bf16_pairs(h2)
        logits = _dot(h2.astype(_BF), rw_ref[...]) + rb_ref[...]
        lane = lax.broadcasted_iota(jnp.int32, logits.shape, 1)
        vals, idxs = [], []
        for _ in range(TOP_K):
            mx = jnp.max(logits, axis=-1, keepdims=True)
            ix = jnp.min(jnp.where(logits == mx, lane, LANES), axis=-1, keepdims=True)
            logits = jnp.where(lane == ix, NEG_INF, logits)
            vals.append(mx)
            idxs.append(ix)
        es = [jnp.exp(v - vals[0]) for v in vals]
        den = es[0] + es[1] + es[2] + es[3]
        idx4 = jnp.zeros(logits.shape, jnp.int32)
        gate4 = jnp.zeros(logits.shape, _F32)
        sel = jnp.zeros(logits.shape, _F32)
        for k in range(TOP_K):
            idx4 = jnp.where(lane == k, idxs[k], idx4)
            gate4 = jnp.where(lane == k, es[k] / den, gate4)
            sel = sel + jnp.where(lane == idxs[k], 1.0, 0.0)
        idx_ref[rows, :] = idx4
        gate_ref[rows, :] = gate4
        sel_sum = sel_sum + jnp.sum(sel, axis=0, keepdims=True)

    @pl.when(i == 0)
    def _():
        cnt_ref[...] = jnp.zeros(cnt_ref.shape, _F32)

    cnt_ref[0:1, :] += sel_sum


def _merge(xa, mod, g1, g2, w_g, fm_lat, fm_ctx, at_lat, at_ctx, w_fo, w_ao, w_out, rw, rb,
           *, n, t_lat, n_batch):
    tt, d = xa.shape
    fd = fm_lat.shape[1]
    nt = tt // TM
    nxt = t_lat // TM
    row = lambda i: (i, 0)
    lat = lambda i: (jnp.minimum(i, nxt - 1), 0)
    cxt = lambda i: (jnp.maximum(i - nxt, 0), 0)
    const = lambda a: pl.BlockSpec(a.shape, lambda i: (0, 0))
    mod_idx = lambda i: (jnp.minimum(i * TM // n, n_batch), 0, 0)
    return pl.pallas_call(
        functools.partial(_merge_kernel, nxt=nxt),
        grid=(nt,),
        in_specs=[pl.BlockSpec((TM, d), row), pl.BlockSpec((1, N_MOD, d), mod_idx),
                  const(g1), const(g2), const(w_g),
                  pl.BlockSpec((TM, fd), lat), pl.BlockSpec((TM, fd), cxt),
                  pl.BlockSpec((TM, fd), lat), pl.BlockSpec((TM, fd), cxt),
                  const(w_fo), const(w_ao), const(w_out), const(rw), const(rb)],
        out_specs=[pl.BlockSpec((TM, d), row), pl.BlockSpec((TM, d // 2), row),
                   pl.BlockSpec((TM, LANES), row), pl.BlockSpec((TM, LANES), row),
                   pl.BlockSpec((8, LANES), lambda i: (0, 0))],
        out_shape=[jax.ShapeDtypeStruct((tt, d), _F32), jax.ShapeDtypeStruct((tt, d // 2), jnp.uint32),
                   jax.ShapeDtypeStruct((tt, LANES), jnp.int32), jax.ShapeDtypeStruct((tt, LANES), _F32),
                   jax.ShapeDtypeStruct((8, LANES), _F32)],
        scratch_shapes=[pltpu.VMEM((TM, d), _BF), pltpu.VMEM((TM, d), _BF)],
        compiler_params=_params("arbitrary"),
        name="merge_router",
    )(xa, mod, g1, g2, w_g, fm_lat, fm_ctx, at_lat, at_ctx, w_fo, w_ao, w_out, rw, rb)


def _route_kernel(idx_ref, ps_ref, d_ref, carry):
    i = pl.program_id(0)

    @pl.when(i == 0)
    def _():
        carry[...] = ps_ref[...]

    idx4 = idx_ref[...]
    lane = lax.broadcasted_iota(jnp.int32, idx4.shape, 1)
    cols = [idx4[:, k:k + 1] for k in range(TOP_K)]
    sel = jnp.zeros(idx4.shape, _F32)
    for k in range(TOP_K):
        sel = sel + jnp.where(lane == cols[k], 1.0, 0.0)
    r = lax.broadcasted_iota(jnp.int32, (TM, TM), 0)
    c = lax.broadcasted_iota(jnp.int32, (TM, TM), 1)
    tri = jnp.where(r > c, 1.0, 0.0).astype(_BF)
    slot = carry[...] + _dot(tri, sel.astype(_BF))
    dest4 = jnp.zeros(idx4.shape, jnp.int32)
    for k in range(TOP_K):
        dk = jnp.sum(jnp.where(lane == cols[k], slot, 0.0), axis=-1, keepdims=True)
        dest4 = jnp.where(lane == k, dk.astype(jnp.int32), dest4)
    d_ref[...] = dest4
    carry[...] += jnp.sum(sel, axis=0, keepdims=True)


def _route(idx4, pad_start):
    tt = idx4.shape[0]
    return pl.pallas_call(
        _route_kernel,
        grid=(tt // TM,),
        in_specs=[pl.BlockSpec((TM, LANES), lambda i: (i, 0)), pl.BlockSpec((1, LANES), lambda i: (0, 0))],
        out_specs=pl.BlockSpec((TM, LANES), lambda i: (i, 0)),
        out_shape=jax.ShapeDtypeStruct((tt, LANES), jnp.int32),
        scratch_shapes=[pltpu.VMEM((1, LANES), _F32)],
        compiler_params=_params("arbitrary"),
        name="route_slots",
    )(idx4, pad_start)


def _zero_tails_kernel(pend_ref, cnt_ref, nv_ref, xs_ref, zbuf, sem):
    n_blocks = xs_ref.shape[0] // TME
    zbuf[...] = jnp.zeros(zbuf.shape, zbuf.dtype)

    def zero_block(start):
        return pltpu.make_async_copy(zbuf, xs_ref.at[pl.ds(pl.multiple_of(start, TME), TME)], sem)

    for e in range(N_EXPERTS):
        @pl.when(cnt_ref[e] > 0)
        def _():
            zero_block(pend_ref[e] - TME).start()

    def start_unused(j, carry):
        zero_block(j * TME).start()
        return carry

    def wait_unused(j, carry):
        zero_block(j * TME).wait()
        return carry

    lax.fori_loop(nv_ref[0], n_blocks, start_unused, 0)
    for e in range(N_EXPERTS):
        @pl.when(cnt_ref[e] > 0)
        def _():
            zero_block(pend_ref[e] - TME).wait()
    lax.fori_loop(nv_ref[0], n_blocks, wait_unused, 0)


def _zero_tails(pad_end, cnt, n_valid, n_slots, w, dtype):
    grid_spec = pltpu.PrefetchScalarGridSpec(
        num_scalar_prefetch=3,
        grid=(1,),
        in_specs=[],
        out_specs=pl.BlockSpec(memory_space=pl.ANY),
        scratch_shapes=[pltpu.VMEM((TME, w), dtype), pltpu.SemaphoreType.DMA],
    )
    return pl.pallas_call(
        _zero_tails_kernel,
        grid_spec=grid_spec,
        out_shape=jax.ShapeDtypeStruct((n_slots, w), dtype),
        compiler_params=_params("arbitrary"),
        name="moe_zero_tails",
    )(pad_end, cnt, n_valid)


def _sc_scatter(rows, idx, base):
    tt, w = rows.shape
    workers = SC_CORES * SC_SUBCORES
    per_worker = tt // workers
    assert tt % workers == 0 and per_worker % SC_SCATTER_WINDOW == 0
    mesh = plsc.VectorSubcoreMesh(core_axis_name="core", subcore_axis_name="subcore")
    out = jax.new_ref(base)

    @pl.kernel(out_type=(), mesh=mesh,
               scratch_types=[pltpu.VMEM((SC_SCATTER_WINDOW,), jnp.int32),
                              pltpu.VMEM((SC_SCATTER_WINDOW, w), rows.dtype), pltpu.SemaphoreType.DMA])
    def scatter(r_hbm, i_hbm, o_hbm, idx_v, rows_v, sem):
        wid = lax.axis_index("subcore") * SC_CORES + lax.axis_index("core")

        @pl.loop(0, per_worker // SC_SCATTER_WINDOW)
        def _(c):
            first = wid * per_worker + c * SC_SCATTER_WINDOW
            pltpu.sync_copy(r_hbm.at[pl.ds(first, SC_SCATTER_WINDOW)], rows_v)
            for k in range(TOP_K):
                pltpu.sync_copy(i_hbm.at[pl.ds(k * tt + first, SC_SCATTER_WINDOW)], idx_v)
                pltpu.async_copy(rows_v, o_hbm.at[idx_v], sem).wait()

    scatter(rows, idx, out)
    return jax.freeze(out)


def _moe_kernel(be_ref, nv_ref, xs_ref, wgu_ref, bgu_ref, wd_ref, bd_ref, y_ref, wgu_bf, wd_bf):
    i = pl.program_id(0)
    de = wd_ref.shape[2]
    valid = i < nv_ref[0]
    new_expert = jnp.logical_or(i == 0, be_ref[i] != be_ref[jnp.maximum(i - 1, 0)])

    @pl.when(jnp.logical_and(valid, new_expert))
    def _():
        def cast_rows(ref, out, rows):
            def body(j, carry):
                sl = pl.ds(pl.multiple_of(j * CAST_ROWS, CAST_ROWS), CAST_ROWS)
                out[sl, :] = ref[0, 0, sl, :].astype(_BF)
                return carry
            lax.fori_loop(0, rows // CAST_ROWS, body, 0)
        cast_rows(wgu_ref, wgu_bf, wgu_ref.shape[2])
        cast_rows(wd_ref, wd_bf, de)

    @pl.when(valid)
    def _():
        xb = _unpack_bf16_pairs(xs_ref[...])
        gu = _dot(xb, wgu_bf[...]) + bgu_ref[0, 0]
        a = jnp.minimum(gu[:, :de], SWIGLU_LIMIT)
        u = jnp.clip(gu[:, de:], -SWIGLU_LIMIT, SWIGLU_LIMIT)
        act = a * _sigmoid(SWIGLU_ALPHA * a) * (u + 1)
        y_ref[...] = _pack_bf16_pairs(_dot(act.astype(_BF), wd_bf[...]) + bd_ref[0, 0])

    @pl.when(i >= nv_ref[0])
    def _():
        y_ref[...] = jnp.zeros(y_ref.shape, y_ref.dtype)


def _moe(block_e, n_valid, xs, w_gu, b_gu, w_down, b_down, *, layer):
    n_slots, w = xs.shape
    depth, ne, d, de2 = w_gu.shape
    de = de2 // 2
    n_blocks = n_slots // TME
    ex = lambda i, be, nv: (layer, be[i], 0, 0)
    grid_spec = pltpu.PrefetchScalarGridSpec(
        num_scalar_prefetch=2,
        grid=(n_blocks,),
        in_specs=[pl.BlockSpec((TME, w), lambda i, be, nv: (jnp.minimum(i, nv[0] - 1), 0)),
                  pl.BlockSpec((1, 1, d, de2), ex), pl.BlockSpec((1, 1, 1, de2), ex),
                  pl.BlockSpec((1, 1, de, d), ex), pl.BlockSpec((1, 1, 1, d), ex)],
        out_specs=pl.BlockSpec((TME, d // 2), lambda i, be, nv: (i, 0)),
        scratch_shapes=[pltpu.VMEM((d, de2), _BF), pltpu.VMEM((de, d), _BF)],
    )
    return pl.pallas_call(
        _moe_kernel,
        grid_spec=grid_spec,
        out_shape=jax.ShapeDtypeStruct((n_slots, d // 2), jnp.uint32),
        compiler_params=_params("arbitrary"),
        name="moe_experts",
    )(block_e, n_valid, xs, w_gu, b_gu.reshape(depth, ne, 1, de2), w_down, b_down.reshape(depth, ne, 1, d))


def _sc_gather(table, idx):
    n_idx, (_, w) = idx.shape[0], table.shape
    workers = SC_CORES * SC_SUBCORES
    per_worker = n_idx // workers
    n_win = per_worker // SC_WINDOW
    assert n_idx % workers == 0 and per_worker % SC_WINDOW == 0 and n_win % 2 == 0
    mesh = plsc.VectorSubcoreMesh(core_axis_name="core", subcore_axis_name="subcore")
    idx_buf = pltpu.VMEM((SC_WINDOW,), jnp.int32)
    row_buf = pltpu.VMEM((SC_WINDOW, w), table.dtype)

    @pl.kernel(out_type=jax.ShapeDtypeStruct((n_idx, w), table.dtype), mesh=mesh,
               scratch_types=[idx_buf, idx_buf, row_buf, row_buf] + [pltpu.SemaphoreType.DMA] * 4)
    def gather(t_hbm, i_hbm, o_hbm, idx_a, idx_b, rows_a, rows_b, gsem_a, gsem_b, ssem_a, ssem_b):
        wid = lax.axis_index("subcore") * SC_CORES + lax.axis_index("core")
        first = wid * per_worker

        def gather_start(c, idx_v, rows_v, sem):
            pltpu.sync_copy(i_hbm.at[pl.ds(first + c * SC_WINDOW, SC_WINDOW)], idx_v)
            pltpu.make_async_copy(t_hbm.at[idx_v], rows_v, sem).start()

        def gather_wait(idx_v, rows_v, sem):
            pltpu.make_async_copy(t_hbm.at[idx_v], rows_v, sem).wait()

        def store_start(c, rows_v, sem):
            pltpu.make_async_copy(rows_v, o_hbm.at[pl.ds(first + c * SC_WINDOW, SC_WINDOW)], sem).start()

        def store_wait(rows_v, sem):
            pltpu.make_async_copy(rows_v, o_hbm.at[pl.ds(first, SC_WINDOW)], sem).wait()

        gather_start(0, idx_a, rows_a, gsem_a)

        @pl.loop(0, n_win // 2)
        def _(i):
            c = 2 * i

            @pl.when(i > 0)
            def _():
                store_wait(rows_b, ssem_b)

            gather_start(c + 1, idx_b, rows_b, gsem_b)
            gather_wait(idx_a, rows_a, gsem_a)
            store_start(c, rows_a, ssem_a)
            store_wait(rows_a, ssem_a)

            @pl.when(c + 2 < n_win)
            def _():
                gather_start(c + 2, idx_a, rows_a, gsem_a)

            gather_wait(idx_b, rows_b, gsem_b)
            store_start(c + 1, rows_b, ssem_b)

        store_wait(rows_b, ssem_b)

    return gather(table, idx)


def _combine_kernel(yg_ref, gate_ref, x_ref, mod_ref, fg_ref, o_ref, *, final):
    g = gate_ref[...]
    half = x_ref.shape[1] // 2
    lo = jnp.zeros((x_ref.shape[0], half), _F32)
    hi = jnp.zeros((x_ref.shape[0], half), _F32)
    for k in range(TOP_K):
        p = yg_ref[k]
        lo = lo + g[:, k:k + 1] * lax.bitcast_convert_type(p << 16, _F32)
        hi = hi + g[:, k:k + 1] * lax.bitcast_convert_type(p & jnp.uint32(0xFFFF0000), _F32)
    x2 = x_ref[...] + mod_ref[0][5:6] * jnp.concatenate([lo, hi], axis=1)
    if final:
        ms = jnp.mean(x2 * x2, axis=-1, keepdims=True)
        x2 = x2 * lax.rsqrt(ms + EPS) * fg_ref[...]
    o_ref[...] = x2


def _combine(yg, gate4, x1, mod, fg, *, n, n_batch, rows, final):
    d = x1.shape[1]
    row = lambda i: (i, 0)
    mod_idx = lambda i: (jnp.minimum(i * TMC // n, n_batch), 0, 0)
    return pl.pallas_call(
        functools.partial(_combine_kernel, final=final),
        grid=(rows // TMC,),
        in_specs=[pl.BlockSpec((TOP_K, TMC, d // 2), lambda i: (0, i, 0)),
                  pl.BlockSpec((TMC, LANES), row), pl.BlockSpec((TMC, d), row),
                  pl.BlockSpec((1, N_MOD, d), mod_idx), pl.BlockSpec((1, d), lambda i: (0, 0))],
        out_specs=pl.BlockSpec((TMC, d), row),
        out_shape=jax.ShapeDtypeStruct((rows, d), _F32),
        compiler_params=_params("arbitrary"),
        name="moe_combine",
    )(yg, gate4, x1, mod, fg)


def _routing_tables(counts, n_blocks):
    cnt = counts[0, :N_EXPERTS].astype(jnp.int32)
    padded = (cnt + TME - 1) // TME * TME
    pad_end = jnp.cumsum(padded)
    pad_start = pad_end - padded
    ps = jnp.zeros((1, LANES), _F32).at[0, :N_EXPERTS].set(pad_start.astype(_F32))
    blk = jnp.arange(n_blocks, dtype=jnp.int32) * TME
    block_e = jnp.sum((pad_end[None, :] <= blk[:, None]).astype(jnp.int32), axis=1)
    e_last = jnp.max(jnp.where(cnt > 0, jnp.arange(N_EXPERTS, dtype=jnp.int32), 0))
    block_e = jnp.minimum(block_e, e_last).astype(jnp.int32)
    n_valid = (pad_end[-1] // TME).astype(jnp.int32).reshape(1)
    return ps, block_e, n_valid, pad_end.astype(jnp.int32), cnt


def kernel(x, c, ctx, c_ctx, ada_w, ada_b, norm1_g, norm2_g, w_in, attn_sink, w_fourier_out, w_attn_out,
           w_out, router_w, router_b, expert_w_gu, expert_b_gu, expert_w_down, expert_b_down, final_norm_g):
    n_batch, n, d = x.shape
    ctx_len = ctx.shape[1]
    depth = ada_w.shape[0]
    t_lat = n_batch * n
    t_ctx = n_batch * ctx_len
    tt = t_lat + t_ctx
    fd = qd = d // 2
    assert n % TM == 0 and t_ctx % TM == 0 and n % (BLOCK * FB_NB) == 0 and n_batch < MOD_ROWS
    assert t_lat % ctx_len == 0 and TM % ctx_len == 0

    xa = jnp.concatenate([x.reshape(t_lat, d), ctx.reshape(t_ctx, d)], axis=0)
    cc = jnp.concatenate([c, c_ctx[None, :], jnp.zeros((MOD_ROWS - n_batch - 1, d), _F32)], axis=0)
    mod = _ada(cc, ada_w, ada_b).reshape(depth, MOD_ROWS, N_MOD, d)
    cos_t, sin_t = _rope_tables(n)
    consts = _dft_consts(n, ctx_len)

    n_in = fd + qd + 2 * KV_DIM
    w_a = w_in[:, :, :n_in].astype(_BF)
    w_g = w_in[:, :, n_in:].astype(_BF)
    w_fo, w_ao, w_o = w_fourier_out.astype(_BF), w_attn_out.astype(_BF), w_out.astype(_BF)
    rw = jnp.pad(router_w, ((0, 0), (0, 0), (0, LANES - N_EXPERTS))).astype(_BF)
    rb = jnp.pad(router_b, ((0, 0), (0, LANES - N_EXPERTS)), constant_values=NEG_INF)[:, None, :]

    n_blocks = -(-(tt * TOP_K + N_EXPERTS * (TME - 1)) // TME)
    dims = dict(n=n, t_lat=t_lat, n_batch=n_batch)
    for l in range(depth):
        last = l == depth - 1
        f, q, k2, v2 = _proj(xa, mod[l], norm1_g[l][None, :], w_a[l], cos_t, sin_t, **dims)
        fm_lat = _fourier_lat(f, consts, n_batch=n_batch, n=n)
        fm_ctx = _fourier_ctx(f, consts, n_batch=n_batch, ctx_len=ctx_len, t_lat=t_lat)
        at_lat = _attn_lat(attn_sink[l], q, k2, v2, n_batch=n_batch, n=n, ctx_len=ctx_len, t_lat=t_lat)
        at_ctx = _attn_ctx(attn_sink[l], q, k2, v2, n_batch=n_batch, ctx_len=ctx_len, t_lat=t_lat)
        x1, hp, idx4, gate4, counts = _merge(
            xa, mod[l], norm1_g[l][None, :], norm2_g[l][None, :], w_g[l], fm_lat, fm_ctx, at_lat, at_ctx,
            w_fo[l], w_ao[l], w_o[l], rw[l], rb[l], **dims)
        pad_start, block_e, n_valid, pad_end, cnt = _routing_tables(counts, n_blocks)
        dest = _route(idx4, pad_start)[:, :TOP_K].T.reshape(-1)
        xs = _sc_scatter(hp, dest, _zero_tails(pad_end, cnt, n_valid, n_blocks * TME, hp.shape[1], hp.dtype))
        y = _moe(block_e, n_valid, xs, expert_w_gu, expert_b_gu, expert_w_down, expert_b_down, layer=l)
        yg = _sc_gather(y, dest).reshape(TOP_K, tt, d // 2)
        xa = _combine(yg, gate4, x1, mod[l], final_norm_g[None, :], n=n, n_batch=n_batch,
                      rows=t_lat if last else tt, final=last)
    return xa.reshape(n_batch, n, d)
```

```python
import functools

import numpy as np
import jax
import jax.numpy as jnp
from jax import lax
from jax.experimental import pallas as pl
from jax.experimental.pallas import tpu as pltpu
from jax.experimental.pallas import tpu_sc as plsc

GRID_W = 64
HEAD_DIM = 64
N_KV_HEADS = 2
Q_PER_KV = 4
KV_DIM = N_KV_HEADS * HEAD_DIM
WINDOW = 128
BLOCK = 128
ROPE_THETA = 10000.0
ROPE_FREQS = HEAD_DIM // 4
GROUP_DIM = 128
N_EXPERTS = 32
TOP_K = 4
SWIGLU_LIMIT = 7.0
SWIGLU_ALPHA = 1.702
N_MOD = 6
EPS = 1e-5
NEG_INF = -1e30
LOG2E = 1.4426950408889634
Q_SCALE = HEAD_DIM ** -0.5 * LOG2E

LANES = 128
MOD_ROWS = 8
TM = 1024
TME = 512
TMC = 512
SC_CORES = 2
SC_SUBCORES = 16
SC_WINDOW = 96
SC_SCATTER_WINDOW = 96
FA_NB = 16
FB_NB = 16
CAST_ROWS = 64
MERGE_COLS = 256
MERGE_ROW_SPLIT = 1
VMEM_LIMIT = 56 * 1024 * 1024

_BF = jnp.bfloat16
_F32 = jnp.float32


def _params(*sem):
    return pltpu.CompilerParams(dimension_semantics=sem, vmem_limit_bytes=VMEM_LIMIT)


def _dot(a, b):
    return jnp.dot(a, b, preferred_element_type=_F32)


def _sigmoid(x):
    return 0.5 * jnp.tanh(0.5 * x) + 0.5


def _norm_mod(x, g, shift, scale):
    ms = jnp.mean(x * x, axis=-1, keepdims=True)
    return (x * lax.rsqrt(ms + EPS) * g) * (1 + scale) + shift


def _ada_kernel(c_ref, w_ref, b_ref, o_ref):
    c = c_ref[...]
    s = c * jax.nn.sigmoid(c)
    o_ref[0] = jnp.dot(s, w_ref[0], precision=lax.Precision.HIGHEST,
                       preferred_element_type=_F32) + b_ref[0]


def _ada(cc, ada_w, ada_b):
    depth, d, nd = ada_w.shape
    tn = nd // 4
    return pl.pallas_call(
        _ada_kernel,
        grid=(depth, nd // tn),
        in_specs=[pl.BlockSpec((MOD_ROWS, d), lambda l, j: (0, 0)),
                  pl.BlockSpec((1, d, tn), lambda l, j: (l, 0, j)),
                  pl.BlockSpec((1, 1, tn), lambda l, j: (l, 0, j))],
        out_specs=pl.BlockSpec((1, MOD_ROWS, tn), lambda l, j: (l, 0, j)),
        out_shape=jax.ShapeDtypeStruct((depth, MOD_ROWS, nd), _F32),
        compiler_params=_params("arbitrary", "arbitrary"),
        name="ada_mod",
    )(cc, ada_w, ada_b.reshape(depth, 1, nd))


def _rope(v, cos, sin):
    lane = lax.broadcasted_iota(jnp.int32, v.shape, 1)
    partner = jnp.where((lane & ROPE_FREQS) == 0,
                        pltpu.roll(v, LANES - ROPE_FREQS, 1), pltpu.roll(v, ROPE_FREQS, 1))
    return v * cos + partner * sin


def _proj_kernel(x_ref, mod_ref, g_ref, w_ref, cos_ref, sin_ref, f_ref, q_ref, k_ref, vt_ref, *, fd, qd):
    m = mod_ref[0]
    h = _norm_mod(x_ref[...], g_ref[...], m[0:1], m[1:2]).astype(_BF)
    p = _dot(h, w_ref[...])
    cos = cos_ref[...]
    sin = sin_ref[...]
    half = BLOCK // 2
    for a in range(p.shape[0] // BLOCK):
        f_ref[a * half:(a + 1) * half, :] = _pack_rows(p[a * BLOCK:a * BLOCK + half, :fd],
                                                       p[a * BLOCK + half:(a + 1) * BLOCK, :fd])
    for j in range(qd // LANES):
        lo = fd + j * LANES
        q_ref[:, j * LANES:(j + 1) * LANES] = (
            _rope(p[:, lo:lo + LANES], cos, sin) * Q_SCALE).astype(_BF)
    k = _rope(p[:, fd + qd:fd + qd + KV_DIM], cos, sin)
    v = p[:, fd + qd + KV_DIM:fd + qd + 2 * KV_DIM]
    k_ref[:, :KV_DIM] = k.astype(_BF)
    k_ref[:, KV_DIM:] = pltpu.roll(k, HEAD_DIM, 1).astype(_BF)
    vt_ref[:KV_DIM, :] = v.T.astype(_BF)
    vt_ref[KV_DIM:, :] = pltpu.roll(v, HEAD_DIM, 1).T.astype(_BF)


def _proj(xa, mod, g, w_a, cos_t, sin_t, *, n, t_lat, n_batch):
    tt, d = xa.shape
    fd = qd = d // 2
    nt = tt // TM
    nxt = t_lat // TM
    per_seq = n // TM
    mod_idx = lambda i: (jnp.minimum(i * TM // n, n_batch), 0, 0)
    rope_idx = lambda i: (jnp.where(i < nxt, i % per_seq, per_seq), 0)
    row = lambda i: (i, 0)
    return pl.pallas_call(
        functools.partial(_proj_kernel, fd=fd, qd=qd),
        grid=(nt,),
        in_specs=[pl.BlockSpec((TM, d), row),
                  pl.BlockSpec((1, N_MOD, d), mod_idx),
                  pl.BlockSpec((1, d), lambda i: (0, 0)),
                  pl.BlockSpec(w_a.shape, lambda i: (0, 0)),
                  pl.BlockSpec((TM, LANES), rope_idx),
                  pl.BlockSpec((TM, LANES), rope_idx)],
        out_specs=[pl.BlockSpec((TM // 2, fd), row), pl.BlockSpec((TM, qd), row),
                   pl.BlockSpec((TM, 2 * KV_DIM), row), pl.BlockSpec((2 * KV_DIM, TM), lambda i: (0, i))],
        out_shape=[jax.ShapeDtypeStruct((tt // 2, fd), jnp.uint32), jax.ShapeDtypeStruct((tt, qd), _BF),
                   jax.ShapeDtypeStruct((tt, 2 * KV_DIM), _BF), jax.ShapeDtypeStruct((2 * KV_DIM, tt), _BF)],
        compiler_params=_params("arbitrary"),
        name="proj_in",
    )(xa, mod, g, w_a, cos_t, sin_t)


def _rope_tables(n):
    pos = jnp.arange(n)
    inv = ROPE_THETA ** (-jnp.arange(ROPE_FREQS, dtype=_F32) / ROPE_FREQS)
    ar = (pos // GRID_W).astype(_F32)[:, None] * inv
    ac = (pos % GRID_W).astype(_F32)[:, None] * inv
    cos = jnp.concatenate([jnp.cos(ar), jnp.cos(ar), jnp.cos(ac), jnp.cos(ac)], axis=1)
    sin = jnp.concatenate([-jnp.sin(ar), jnp.sin(ar), -jnp.sin(ac), jnp.sin(ac)], axis=1)
    cos = jnp.tile(cos, (1, LANES // HEAD_DIM))
    sin = jnp.tile(sin, (1, LANES // HEAD_DIM))
    cos = jnp.concatenate([cos, jnp.ones((TM, LANES), _F32)], axis=0)
    sin = jnp.concatenate([sin, jnp.zeros((TM, LANES), _F32)], axis=0)
    return cos, sin


def _dft_consts(n, ctx_len):
    n1 = n // BLOCK
    b = np.arange(BLOCK, dtype=np.int64)[:, None, None]
    k1 = np.arange(n1, dtype=np.int64)[None, :, None]
    a = np.arange(n1, dtype=np.int64)[None, None, :]
    ang = 2.0 * np.pi * ((a * k1 * BLOCK + b * k1) % n).astype(np.float64) / n
    m_ri = np.concatenate([np.cos(ang), -np.sin(ang)], axis=1)
    kk = np.arange(GROUP_DIM, dtype=np.int64)
    ang_c = 2.0 * np.pi * ((kk[:, None] * kk[None, :]) % GROUP_DIM) / GROUP_DIM
    c, s = np.cos(ang_c), np.sin(ang_c)
    cs = np.concatenate([np.concatenate([c, s], axis=1), np.concatenate([-s, c], axis=1)], axis=0)
    cc = np.concatenate([c, s], axis=0)
    kl = np.arange(ctx_len, dtype=np.int64)
    ang_l = 2.0 * np.pi * ((kl[:, None] * kl[None, :]) % ctx_len) / ctx_len
    wl = np.concatenate([np.cos(ang_l), -np.sin(ang_l)], axis=0)
    as_bf = lambda v: jnp.asarray(v, dtype=_F32).astype(_BF)
    return dict(m_ri=as_bf(m_ri), cs=as_bf(cs), cc=as_bf(cc), wl=as_bf(wl))


def _chan_dft(xr, xi, cc, scale):
    outs = []
    for g in range(xr.shape[1] // GROUP_DIM):
        sl = slice(g * GROUP_DIM, (g + 1) * GROUP_DIM)
        xx = jnp.concatenate([xr[:, sl], xi[:, sl]], axis=1).astype(_BF)
        outs.append(_dot(xx, cc) * scale)
    return jnp.concatenate(outs, axis=1)


def _pack_rows(lo, hi):
    a = lax.bitcast_convert_type(lo.astype(_BF).astype(_F32), jnp.uint32)
    b = lax.bitcast_convert_type(hi.astype(_BF).astype(_F32), jnp.uint32)
    return (a >> 16) | b


def _unpack_rows(p):
    lo = lax.bitcast_convert_type(p << 16, _F32).astype(_BF)
    hi = lax.bitcast_convert_type(p & jnp.uint32(0xFFFF0000), _F32).astype(_BF)
    return lo, hi


def _fa_kernel(f_ref, mlo_ref, mhi_ref, zr_ref, zi_ref):
    n1 = f_ref.shape[0]
    for t in range(FA_NB):
        lo, hi = _unpack_rows(f_ref[:, t, :])
        za = _dot(mlo_ref[t], lo)
        zb = _dot(mhi_ref[t], hi)
        zr_ref[:, t, :] = _pack_rows(za[:n1], zb[:n1])
        zi_ref[:, t, :] = _pack_rows(za[n1:], zb[n1:])


def _fb_kernel(zr_ref, zi_ref, cs_ref, cc_ref, o_ref, *, scale):
    cs, cc = cs_ref[...], cc_ref[...]
    for j in range(FB_NB):
        xx = _dot(cs, jnp.concatenate(_unpack_rows(zr_ref[j]) + _unpack_rows(zi_ref[j]), axis=0))
        o_ref[:, j, :] = _chan_dft(xx[:BLOCK], xx[BLOCK:], cc, scale).astype(_BF)


def _fourier_lat(f, consts, *, n_batch, n):
    fd = f.shape[1]
    n1 = n // BLOCK
    half = BLOCK // 2
    f3 = f.reshape(f.shape[0] // half, half, fd)
    steps = half // FA_NB
    blk_a = pl.BlockSpec((n1, FA_NB, fd), lambda b, j: (b, j, 0))
    mat_lo = pl.BlockSpec((FA_NB, 2 * n1, n1), lambda b, j: (j, 0, 0))
    mat_hi = pl.BlockSpec((FA_NB, 2 * n1, n1), lambda b, j: (j + steps, 0, 0))
    z_shape = jax.ShapeDtypeStruct((n_batch * n1, half, fd), jnp.uint32)
    zr, zi = pl.pallas_call(
        _fa_kernel,
        grid=(n_batch, steps),
        in_specs=[blk_a, mat_lo, mat_hi],
        out_specs=[blk_a, blk_a],
        out_shape=[z_shape, z_shape],
        compiler_params=_params("arbitrary", "arbitrary"),
        name="fourier_seq_a",
    )(f3, consts["m_ri"], consts["m_ri"])
    blk_z = pl.BlockSpec((FB_NB, half, fd), lambda b, j: (b * (n1 // FB_NB) + j, 0, 0))
    const = lambda shape: pl.BlockSpec(shape, lambda b, j: (0, 0))
    out = pl.pallas_call(
        functools.partial(_fb_kernel, scale=float((n * GROUP_DIM) ** -0.5)),
        grid=(n_batch, n1 // FB_NB),
        in_specs=[blk_z, blk_z, const((2 * BLOCK, 2 * BLOCK)), const((2 * GROUP_DIM, GROUP_DIM))],
        out_specs=pl.BlockSpec((BLOCK, FB_NB, fd), lambda b, j: (b, j, 0)),
        out_shape=jax.ShapeDtypeStruct((n_batch * BLOCK, n1, fd), _BF),
        compiler_params=_params("arbitrary", "arbitrary"),
        name="fourier_seq_b",
    )(zr, zi, consts["cs"], consts["cc"])
    return out.reshape(n_batch * n, fd)


def _fc_kernel(f_ref, wl_ref, cc_ref, o_ref, *, scale):
    half = BLOCK // 2
    ctx_len = 2 * f_ref.shape[0]
    lo, hi = _unpack_rows(f_ref[...])
    rows = []
    for a in range(ctx_len // BLOCK):
        rows += [lo[a * half:(a + 1) * half], hi[a * half:(a + 1) * half]]
    xx = _dot(wl_ref[...], jnp.concatenate(rows, axis=0))
    o_ref[...] = _chan_dft(xx[:ctx_len], xx[ctx_len:], cc_ref[...], scale).astype(_BF)


def _fourier_ctx(f, consts, *, n_batch, ctx_len, t_lat):
    fd = f.shape[1]
    const = lambda shape: pl.BlockSpec(shape, lambda b: (0, 0))
    return pl.pallas_call(
        functools.partial(_fc_kernel, scale=float((ctx_len * GROUP_DIM) ** -0.5)),
        grid=(n_batch,),
        in_specs=[pl.BlockSpec((ctx_len // 2, fd), lambda b: (t_lat // ctx_len + b, 0)),
                  const((2 * ctx_len, ctx_len)), const((2 * GROUP_DIM, GROUP_DIM))],
        out_specs=pl.BlockSpec((ctx_len, fd), lambda b: (b, 0)),
        out_shape=jax.ShapeDtypeStruct((n_batch * ctx_len, fd), _BF),
        compiler_params=_params("arbitrary"),
        name="fourier_ctx",
    )(f, consts["wl"], consts["cc"])


def _attn_core(sink_ref, q_ref, k, vt, o_ref, bias, s_scr, p_scr):
    tq = q_ref.shape[0]
    lane = lax.broadcasted_iota(jnp.int32, (tq, LANES), 1)
    low = lane < HEAD_DIM
    zero = jnp.zeros((tq, LANES), _BF)
    low_row = lax.broadcasted_iota(jnp.int32, (LANES, 2 * tq), 0) < HEAD_DIM
    first = lax.broadcasted_iota(jnp.int32, (1, 2 * tq), 1) < tq
    pairs = [(kvh, half) for kvh in range(N_KV_HEADS) for half in range(2)]
    sels = [slice(0, LANES) if half == kvh else slice(LANES, 2 * LANES) for kvh, half in pairs]
    for pi, (kvh, half) in enumerate(pairs):
        slabs = [q_ref[:, (2 * kvh + c) * LANES:(2 * kvh + c + 1) * LANES] for c in range(2)]
        keep = low if half == 0 else jnp.logical_not(low)
        qm = jnp.concatenate([jnp.where(keep, sl, zero) for sl in slabs], axis=0)
        s = lax.dot_general(k[:, sels[pi]], qm, (((1,), (1,)), ((), ())), preferred_element_type=_F32)
        s_scr[pi] = s if bias is None else s + bias
    dens = []
    for pi, (kvh, half) in enumerate(pairs):
        hq = Q_PER_KV * kvh + half
        sk = jnp.where(first, sink_ref[hq], sink_ref[hq + 2]) * LOG2E
        s = s_scr[pi]
        m = jnp.maximum(jnp.max(s, axis=0, keepdims=True), sk)
        p = jnp.exp2(s - m)
        dens.append(jnp.sum(p, axis=0, keepdims=True) + jnp.exp2(sk - m))
        p_scr[pi] = p.astype(_BF)
    outs = [_dot(vt[sels[pi], :], p_scr[pi]) * (1.0 / dens[pi]) for pi in range(len(pairs))]
    for kvh in range(N_KV_HEADS):
        ot = jnp.where(low_row, outs[2 * kvh], outs[2 * kvh + 1])
        for c in range(2):
            o_ref[:, (2 * kvh + c) * LANES:(2 * kvh + c + 1) * LANES] = ot[:, c * tq:(c + 1) * tq].T.astype(_BF)


def _attn_scratch(keys, tq):
    return [pltpu.VMEM((2 * N_KV_HEADS, keys, 2 * tq), _F32), pltpu.VMEM((2 * N_KV_HEADS, keys, 2 * tq), _BF)]


def _attn_lat_kernel(sink_ref, q_ref, kp_ref, kc_ref, kn_ref, vp_ref, vc_ref, vn_ref, kx_ref, vx_ref, bias_ref,
                     o_ref, s_scr, p_scr):
    k = jnp.concatenate([kp_ref[...], kc_ref[...], kn_ref[...], kx_ref[...]], axis=0)
    vt = jnp.concatenate([vp_ref[...], vc_ref[...], vn_ref[...], vx_ref[...]], axis=1)
    _attn_core(sink_ref, q_ref, k, vt, o_ref, bias_ref[0], s_scr, p_scr)


def _window_bias(ctx_len):
    s = np.arange(3 * BLOCK + ctx_len)[:, None]
    r = np.arange(2 * BLOCK)[None, :] % BLOCK
    out = []
    for v in range(4):
        lo = 0 if v & 1 else BLOCK
        hi = 3 * BLOCK if v & 2 else 2 * BLOCK
        ok = ((np.abs(s - BLOCK - r) <= WINDOW) & (s >= lo) & (s < hi)) | (s >= 3 * BLOCK)
        out.append(np.where(ok, 0.0, NEG_INF))
    return jnp.asarray(np.stack(out), dtype=_F32)


def _attn_ctx_kernel(sink_ref, q_ref, kx_ref, vx_ref, o_ref, s_scr, p_scr):
    _attn_core(sink_ref, q_ref, kx_ref[...], vx_ref[...], o_ref, None, s_scr, p_scr)


def _attn_lat(sink, q, k2, v2t, *, n_batch, n, ctx_len, t_lat):
    qd = q.shape[1]
    nb = n // BLOCK
    cur = lambda b, i: (b * nb + i, 0)
    prev = lambda b, i: (b * nb + jnp.maximum(i - 1, 0), 0)
    nxt = lambda b, i: (b * nb + jnp.minimum(i + 1, nb - 1), 0)
    cx = lambda b, i: (t_lat // ctx_len + b, 0)
    swap = lambda im: (lambda b, i: im(b, i)[::-1])
    kb = lambda im: pl.BlockSpec((BLOCK, 2 * KV_DIM), im)
    vb = lambda im: pl.BlockSpec((2 * KV_DIM, BLOCK), swap(im))
    bias = _window_bias(ctx_len)
    bias_idx = lambda b, i: ((i > 0).astype(jnp.int32) + 2 * (i < nb - 1).astype(jnp.int32), 0, 0)
    return pl.pallas_call(
        _attn_lat_kernel,
        grid=(n_batch, nb),
        in_specs=[pl.BlockSpec(memory_space=pltpu.SMEM), pl.BlockSpec((BLOCK, qd), cur),
                  kb(prev), kb(cur), kb(nxt), vb(prev), vb(cur), vb(nxt),
                  pl.BlockSpec((ctx_len, 2 * KV_DIM), cx), pl.BlockSpec((2 * KV_DIM, ctx_len), swap(cx)),
                  pl.BlockSpec((1,) + bias.shape[1:], bias_idx)],
        out_specs=pl.BlockSpec((BLOCK, qd), cur),
        out_shape=jax.ShapeDtypeStruct((t_lat, qd), _BF),
        scratch_shapes=_attn_scratch(3 * BLOCK + ctx_len, BLOCK),
        compiler_params=_params("arbitrary", "arbitrary"),
        name="attn_lat",
    )(sink, q, k2, k2, k2, v2t, v2t, v2t, k2, v2t, bias)


def _attn_ctx(sink, q, k2, v2t, *, n_batch, ctx_len, t_lat):
    qd = q.shape[1]
    cx = lambda b: (t_lat // ctx_len + b, 0)
    return pl.pallas_call(
        _attn_ctx_kernel,
        grid=(n_batch,),
        in_specs=[pl.BlockSpec(memory_space=pltpu.SMEM), pl.BlockSpec((ctx_len, qd), cx),
                  pl.BlockSpec((ctx_len, 2 * KV_DIM), cx),
                  pl.BlockSpec((2 * KV_DIM, ctx_len), lambda b: (0, t_lat // ctx_len + b))],
        out_specs=pl.BlockSpec((ctx_len, qd), lambda b: (b, 0)),
        out_shape=jax.ShapeDtypeStruct((n_batch * ctx_len, qd), _BF),
        scratch_shapes=_attn_scratch(ctx_len, ctx_len),
        compiler_params=_params("arbitrary"),
        name="attn_ctx",
    )(sink, q, k2, v2t)


def _pack_bf16_pairs(h):
    half = h.shape[1] // 2
    lo = lax.bitcast_convert_type(h[:, :half].astype(_BF).astype(_F32), jnp.uint32)
    hi = lax.bitcast_convert_type(h[:, half:].astype(_BF).astype(_F32), jnp.uint32)
    return (lo >> 16) | hi


def _unpack_bf16_pairs(p):
    lo = lax.bitcast_convert_type(p << 16, _F32)
    hi = lax.bitcast_convert_type(p & jnp.uint32(0xFFFF0000), _F32)
    return jnp.concatenate([lo, hi], axis=1).astype(_BF)


def _merge_kernel(x_ref, mod_ref, g1_ref, g2_ref, wg_ref, fml_ref, fmc_ref, atl_ref, atc_ref,
                  wfo_ref, wao_ref, wout_ref, rw_ref, rb_ref,
                  x1_ref, hp_ref, idx_ref, gate_ref, cnt_ref, h_scr, y_scr, *, nxt):
    i = pl.program_id(0)
    d = x_ref.shape[1]
    m = mod_ref[0]
    is_lat = i < nxt
    rows_per = x_ref.shape[0] // MERGE_ROW_SPLIT
    sel_sum = jnp.zeros((1, LANES), _F32)
    for grp in range(MERGE_ROW_SPLIT):
        rows = slice(grp * rows_per, (grp + 1) * rows_per)
        h_scr[rows, :] = _norm_mod(x_ref[rows, :], g1_ref[...], m[0:1], m[1:2]).astype(_BF)
        fm = jnp.where(is_lat, fml_ref[rows, :], fmc_ref[rows, :])
        at = jnp.where(is_lat, atl_ref[rows, :], atc_ref[rows, :])
        for c in range(d // MERGE_COLS):
            sl = slice(c * MERGE_COLS, (c + 1) * MERGE_COLS)
            sg = slice(d + c * MERGE_COLS, d + (c + 1) * MERGE_COLS)
            h = h_scr[rows, :]
            y = (_sigmoid(_dot(h, wg_ref[:, sl])) * _dot(fm, wfo_ref[:, sl])
                 + _sigmoid(_dot(h, wg_ref[:, sg])) * _dot(at, wao_ref[:, sl]))
            y_scr[rows, sl] = y.astype(_BF)
        x1_ref[rows, :] = x_ref[rows, :] + m[2:3] * _dot(y_scr[rows, :], wout_ref[...])
        h2 = _norm_mod(x1_ref[rows, :], g2_ref[...], m[3:4], m[4:5])
        hp_ref[rows, :] = _pack_bf16_pairs(h2)
        logits = _dot(h2.astype(_BF), rw_ref[...]) + rb_ref[...]
        lane = lax.broadcasted_iota(jnp.int32, logits.shape, 1).astype(_F32)
        vals, idxs = [], []
        for _ in range(TOP_K):
            mx = jnp.max(logits, axis=-1, keepdims=True)
            ix = jnp.min(jnp.where(logits == mx, lane, float(LANES)), axis=-1, keepdims=True)
            logits = jnp.where(lane == ix, NEG_INF, logits)
            vals.append(mx)
            idxs.append(ix)
        es = [jnp.exp(v - vals[0]) for v in vals]
        den = es[0] + es[1] + es[2] + es[3]
        idx4 = jnp.zeros(logits.shape, _F32)
        gate4 = jnp.zeros(logits.shape, _F32)
        sel = jnp.zeros(logits.shape, _F32)
        for k in range(TOP_K):
            idx4 = jnp.where(lane == float(k), idxs[k], idx4)
            gate4 = jnp.where(lane == float(k), es[k] / den, gate4)
            sel = sel + jnp.where(lane == idxs[k], 1.0, 0.0)
        idx_ref[rows, :] = idx4.astype(jnp.int32)
        gate_ref[rows, :] = gate4
        sel_sum = sel_sum + jnp.sum(sel, axis=0, keepdims=True)

    @pl.when(i == 0)
    def _():
        cnt_ref[...] = jnp.zeros(cnt_ref.shape, _F32)

    cnt_ref[0:1, :] += sel_sum


def _merge(xa, mod, g1, g2, w_g, fm_lat, fm_ctx, at_lat, at_ctx, w_fo, w_ao, w_out, rw, rb,
           *, n, t_lat, n_batch):
    tt, d = xa.shape
    fd = fm_lat.shape[1]
    nt = tt // TM
    nxt = t_lat // TM
    row = lambda i: (i, 0)
    lat = lambda i: (jnp.minimum(i, nxt - 1), 0)
    cxt = lambda i: (jnp.maximum(i - nxt, 0), 0)
    const = lambda a: pl.BlockSpec(a.shape, lambda i: (0, 0))
    mod_idx = lambda i: (jnp.minimum(i * TM // n, n_batch), 0, 0)
    return pl.pallas_call(
        functools.partial(_merge_kernel, nxt=nxt),
        grid=(nt,),
        in_specs=[pl.BlockSpec((TM, d), row), pl.BlockSpec((1, N_MOD, d), mod_idx),
                  const(g1), const(g2), const(w_g),
                  pl.BlockSpec((TM, fd), lat), pl.BlockSpec((TM, fd), cxt),
                  pl.BlockSpec((TM, fd), lat), pl.BlockSpec((TM, fd), cxt),
                  const(w_fo), const(w_ao), const(w_out), const(rw), const(rb)],
        out_specs=[pl.BlockSpec((TM, d), row), pl.BlockSpec((TM, d // 2), row),
                   pl.BlockSpec((TM, LANES), row), pl.BlockSpec((TM, LANES), row),
                   pl.BlockSpec((8, LANES), lambda i: (0, 0))],
        out_shape=[jax.ShapeDtypeStruct((tt, d), _F32), jax.ShapeDtypeStruct((tt, d // 2), jnp.uint32),
                   jax.ShapeDtypeStruct((tt, LANES), jnp.int32), jax.ShapeDtypeStruct((tt, LANES), _F32),
                   jax.ShapeDtypeStruct((8, LANES), _F32)],
        scratch_shapes=[pltpu.VMEM((TM, d), _BF), pltpu.VMEM((TM, d), _BF)],
        compiler_params=_params("arbitrary"),
        name="merge_router",
    )(xa, mod, g1, g2, w_g, fm_lat, fm_ctx, at_lat, at_ctx, w_fo, w_ao, w_out, rw, rb)


def _route_kernel(idx_ref, ps_ref, d_ref, carry):
    i = pl.program_id(0)

    @pl.when(i == 0)
    def _():
        carry[...] = ps_ref[...]

    idx4 = idx_ref[...]
    lane = lax.broadcasted_iota(jnp.int32, idx4.shape, 1)
    cols = [idx4[:, k:k + 1] for k in range(TOP_K)]
    sel = jnp.zeros(idx4.shape, _F32)
    for k in range(TOP_K):
        sel = sel + jnp.where(lane == cols[k], 1.0, 0.0)
    r = lax.broadcasted_iota(jnp.int32, (TM, TM), 0)
    c = lax.broadcasted_iota(jnp.int32, (TM, TM), 1)
    tri = jnp.where(r > c, 1.0, 0.0).astype(_BF)
    slot = carry[...] + _dot(tri, sel.astype(_BF))
    dest4 = jnp.zeros(idx4.shape, jnp.int32)
    for k in range(TOP_K):
        dk = jnp.sum(jnp.where(lane == cols[k], slot, 0.0), axis=-1, keepdims=True)
        dest4 = jnp.where(lane == k, dk.astype(jnp.int32), dest4)
    d_ref[...] = dest4
    carry[...] += jnp.sum(sel, axis=0, keepdims=True)


def _route(idx4, pad_start):
    tt = idx4.shape[0]
    return pl.pallas_call(
        _route_kernel,
        grid=(tt // TM,),
        in_specs=[pl.BlockSpec((TM, LANES), lambda i: (i, 0)), pl.BlockSpec((1, LANES), lambda i: (0, 0))],
        out_specs=pl.BlockSpec((TM, LANES), lambda i: (i, 0)),
        out_shape=jax.ShapeDtypeStruct((tt, LANES), jnp.int32),
        scratch_shapes=[pltpu.VMEM((1, LANES), _F32)],
        compiler_params=_params("arbitrary"),
        name="route_slots",
    )(idx4, pad_start)


def _zero_tails_kernel(pend_ref, cnt_ref, nv_ref, xs_ref, zbuf, sem):
    n_blocks = xs_ref.shape[0] // TME
    zbuf[...] = jnp.zeros(zbuf.shape, zbuf.dtype)

    def zero_block(start):
        return pltpu.make_async_copy(zbuf, xs_ref.at[pl.ds(pl.multiple_of(start, TME), TME)], sem)

    for e in range(N_EXPERTS):
        @pl.when(cnt_ref[e] > 0)
        def _():
            zero_block(pend_ref[e] - TME).start()

    def start_unused(j, carry):
        zero_block(j * TME).start()
        return carry

    def wait_unused(j, carry):
        zero_block(j * TME).wait()
        return carry

    lax.fori_loop(nv_ref[0], n_blocks, start_unused, 0)
    for e in range(N_EXPERTS):
        @pl.when(cnt_ref[e] > 0)
        def _():
            zero_block(pend_ref[e] - TME).wait()
    lax.fori_loop(nv_ref[0], n_blocks, wait_unused, 0)


def _zero_tails(pad_end, cnt, n_valid, n_slots, w, dtype):
    grid_spec = pltpu.PrefetchScalarGridSpec(
        num_scalar_prefetch=3,
        grid=(1,),
        in_specs=[],
        out_specs=pl.BlockSpec(memory_space=pl.ANY),
        scratch_shapes=[pltpu.VMEM((TME, w), dtype), pltpu.SemaphoreType.DMA],
    )
    return pl.pallas_call(
        _zero_tails_kernel,
        grid_spec=grid_spec,
        out_shape=jax.ShapeDtypeStruct((n_slots, w), dtype),
        compiler_params=_params("arbitrary"),
        name="moe_zero_tails",
    )(pad_end, cnt, n_valid)


def _sc_scatter(rows, idx, base):
    tt, w = rows.shape
    workers = SC_CORES * SC_SUBCORES
    per_worker = tt // workers
    assert tt % workers == 0 and per_worker % SC_SCATTER_WINDOW == 0
    mesh = plsc.VectorSubcoreMesh(core_axis_name="core", subcore_axis_name="subcore")
    out = jax.new_ref(base)

    @pl.kernel(out_type=(), mesh=mesh,
               scratch_types=[pltpu.VMEM((SC_SCATTER_WINDOW,), jnp.int32),
                              pltpu.VMEM((SC_SCATTER_WINDOW, w), rows.dtype), pltpu.SemaphoreType.DMA])
    def scatter(r_hbm, i_hbm, o_hbm, idx_v, rows_v, sem):
        wid = lax.axis_index("subcore") * SC_CORES + lax.axis_index("core")

        @pl.loop(0, per_worker // SC_SCATTER_WINDOW)
        def _(c):
            first = wid * per_worker + c * SC_SCATTER_WINDOW
            pltpu.sync_copy(r_hbm.at[pl.ds(first, SC_SCATTER_WINDOW)], rows_v)
            for k in range(TOP_K):
                pltpu.sync_copy(i_hbm.at[pl.ds(k * tt + first, SC_SCATTER_WINDOW)], idx_v)
                pltpu.async_copy(rows_v, o_hbm.at[idx_v], sem).wait()

    scatter(rows, idx, out)
    return jax.freeze(out)


def _moe_kernel(be_ref, nv_ref, rows_ref, xs_ref, wgu_ref, bgu_ref, wd_ref, bd_ref, y_ref, wgu_bf, wd_bf):
    i = pl.program_id(0)
    de = wd_ref.shape[2]
    valid = i < nv_ref[0]
    new_expert = jnp.logical_or(i == 0, be_ref[i] != be_ref[jnp.maximum(i - 1, 0)])

    @pl.when(jnp.logical_and(valid, new_expert))
    def _():
        def cast_rows(ref, out, rows):
            def body(j, carry):
                sl = pl.ds(pl.multiple_of(j * CAST_ROWS, CAST_ROWS), CAST_ROWS)
                out[sl, :] = ref[0, 0, sl, :].astype(_BF)
                return carry
            lax.fori_loop(0, rows // CAST_ROWS, body, 0)
        cast_rows(wgu_ref, wgu_bf, wgu_ref.shape[2])
        cast_rows(wd_ref, wd_bf, de)

    half = xs_ref.shape[0] // 2
    for part in range(2):
        rows = slice(part * half, (part + 1) * half)
        used = jnp.logical_and(valid, rows_ref[i] > part * half)

        @pl.when(used)
        def _():
            xb = _unpack_bf16_pairs(xs_ref[rows, :])
            gu = _dot(xb, wgu_bf[...]) + bgu_ref[0, 0]
            a = jnp.minimum(gu[:, :de], SWIGLU_LIMIT)
            u = jnp.clip(gu[:, de:], -SWIGLU_LIMIT, SWIGLU_LIMIT)
            act = a * _sigmoid(SWIGLU_ALPHA * a) * (u + 1)
            y_ref[rows, :] = _pack_bf16_pairs(_dot(act.astype(_BF), wd_bf[...]) + bd_ref[0, 0])

        @pl.when(jnp.logical_not(used))
        def _():
            y_ref[rows, :] = jnp.zeros((half, y_ref.shape[1]), y_ref.dtype)


def _moe(block_e, n_valid, block_rows, xs, w_gu, b_gu, w_down, b_down, *, layer):
    n_slots, w = xs.shape
    depth, ne, d, de2 = w_gu.shape
    de = de2 // 2
    n_blocks = n_slots // TME
    ex = lambda i, be, nv, br: (layer, be[i], 0, 0)
    grid_spec = pltpu.PrefetchScalarGridSpec(
        num_scalar_prefetch=3,
        grid=(n_blocks,),
        in_specs=[pl.BlockSpec((TME, w), lambda i, be, nv, br: (jnp.minimum(i, nv[0] - 1), 0)),
                  pl.BlockSpec((1, 1, d, de2), ex), pl.BlockSpec((1, 1, 1, de2), ex),
                  pl.BlockSpec((1, 1, de, d), ex), pl.BlockSpec((1, 1, 1, d), ex)],
        out_specs=pl.BlockSpec((TME, d // 2), lambda i, be, nv, br: (i, 0)),
        scratch_shapes=[pltpu.VMEM((d, de2), _BF), pltpu.VMEM((de, d), _BF)],
    )
    return pl.pallas_call(
        _moe_kernel,
        grid_spec=grid_spec,
        out_shape=jax.ShapeDtypeStruct((n_slots, d // 2), jnp.uint32),
        compiler_params=_params("arbitrary"),
        name="moe_experts",
    )(block_e, n_valid, block_rows, xs, w_gu, b_gu.reshape(depth, ne, 1, de2), w_down,
      b_down.reshape(depth, ne, 1, d))


def _sc_gather(table, idx):
    n_idx, (_, w) = idx.shape[0], table.shape
    workers = SC_CORES * SC_SUBCORES
    per_worker = n_idx // workers
    n_win = per_worker // SC_WINDOW
    assert n_idx % workers == 0 and per_worker % SC_WINDOW == 0 and n_win % 2 == 0
    mesh = plsc.VectorSubcoreMesh(core_axis_name="core", subcore_axis_name="subcore")
    idx_buf = pltpu.VMEM((SC_WINDOW,), jnp.int32)
    row_buf = pltpu.VMEM((SC_WINDOW, w), table.dtype)

    @pl.kernel(out_type=jax.ShapeDtypeStruct((n_idx, w), table.dtype), mesh=mesh,
               scratch_types=[idx_buf, idx_buf, row_buf, row_buf] + [pltpu.SemaphoreType.DMA] * 4)
    def gather(t_hbm, i_hbm, o_hbm, idx_a, idx_b, rows_a, rows_b, gsem_a, gsem_b, ssem_a, ssem_b):
        wid = lax.axis_index("subcore") * SC_CORES + lax.axis_index("core")
        first = wid * per_worker

        def gather_start(c, idx_v, rows_v, sem):
            pltpu.sync_copy(i_hbm.at[pl.ds(first + c * SC_WINDOW, SC_WINDOW)], idx_v)
            pltpu.make_async_copy(t_hbm.at[idx_v], rows_v, sem).start()

        def gather_wait(idx_v, rows_v, sem):
            pltpu.make_async_copy(t_hbm.at[idx_v], rows_v, sem).wait()

        def store_start(c, rows_v, sem):
            pltpu.make_async_copy(rows_v, o_hbm.at[pl.ds(first + c * SC_WINDOW, SC_WINDOW)], sem).start()

        def store_wait(rows_v, sem):
            pltpu.make_async_copy(rows_v, o_hbm.at[pl.ds(first, SC_WINDOW)], sem).wait()

        gather_start(0, idx_a, rows_a, gsem_a)

        @pl.loop(0, n_win // 2)
        def _(i):
            c = 2 * i

            @pl.when(i > 0)
            def _():
                store_wait(rows_b, ssem_b)

            gather_start(c + 1, idx_b, rows_b, gsem_b)
            gather_wait(idx_a, rows_a, gsem_a)
            store_start(c, rows_a, ssem_a)
            store_wait(rows_a, ssem_a)

            @pl.when(c + 2 < n_win)
            def _():
                gather_start(c + 2, idx_a, rows_a, gsem_a)

            gather_wait(idx_b, rows_b, gsem_b)
            store_start(c + 1, rows_b, ssem_b)

        store_wait(rows_b, ssem_b)

    return gather(table, idx)


def _combine_kernel(yg_ref, gate_ref, x_ref, mod_ref, fg_ref, o_ref, *, final):
    g = gate_ref[...]
    half = x_ref.shape[1] // 2
    lo = jnp.zeros((x_ref.shape[0], half), _F32)
    hi = jnp.zeros((x_ref.shape[0], half), _F32)
    for k in range(TOP_K):
        p = yg_ref[k]
        lo = lo + g[:, k:k + 1] * lax.bitcast_convert_type(p << 16, _F32)
        hi = hi + g[:, k:k + 1] * lax.bitcast_convert_type(p & jnp.uint32(0xFFFF0000), _F32)
    x2 = x_ref[...] + mod_ref[0][5:6] * jnp.concatenate([lo, hi], axis=1)
    if final:
        ms = jnp.mean(x2 * x2, axis=-1, keepdims=True)
        x2 = x2 * lax.rsqrt(ms + EPS) * fg_ref[...]
    o_ref[...] = x2


def _combine(yg, gate4, x1, mod, fg, *, n, n_batch, rows, final):
    d = x1.shape[1]
    row = lambda i: (i, 0)
    mod_idx = lambda i: (jnp.minimum(i * TMC // n, n_batch), 0, 0)
    return pl.pallas_call(
        functools.partial(_combine_kernel, final=final),
        grid=(rows // TMC,),
        in_specs=[pl.BlockSpec((TOP_K, TMC, d // 2), lambda i: (0, i, 0)),
                  pl.BlockSpec((TMC, LANES), row), pl.BlockSpec((TMC, d), row),
                  pl.BlockSpec((1, N_MOD, d), mod_idx), pl.BlockSpec((1, d), lambda i: (0, 0))],
        out_specs=pl.BlockSpec((TMC, d), row),
        out_shape=jax.ShapeDtypeStruct((rows, d), _F32),
        compiler_params=_params("arbitrary"),
        name="moe_combine",
    )(yg, gate4, x1, mod, fg)


def _routing_tables(counts, n_blocks):
    cnt = counts[0, :N_EXPERTS].astype(jnp.int32)
    padded = (cnt + TME - 1) // TME * TME
    pad_end = jnp.cumsum(padded)
    pad_start = pad_end - padded
    ps = jnp.zeros((1, LANES), _F32).at[0, :N_EXPERTS].set(pad_start.astype(_F32))
    blk = jnp.arange(n_blocks, dtype=jnp.int32) * TME
    block_e = jnp.sum((pad_end[None, :] <= blk[:, None]).astype(jnp.int32), axis=1)
    e_last = jnp.max(jnp.where(cnt > 0, jnp.arange(N_EXPERTS, dtype=jnp.int32), 0))
    block_e = jnp.minimum(block_e, e_last).astype(jnp.int32)
    n_valid = (pad_end[-1] // TME).astype(jnp.int32).reshape(1)
    block_rows = jnp.clip(cnt[block_e] - (blk - pad_start[block_e]), 0, TME).astype(jnp.int32)
    return ps, block_e, n_valid, pad_end.astype(jnp.int32), cnt, block_rows


def kernel(x, c, ctx, c_ctx, ada_w, ada_b, norm1_g, norm2_g, w_in, attn_sink, w_fourier_out, w_attn_out,
           w_out, router_w, router_b, expert_w_gu, expert_b_gu, expert_w_down, expert_b_down, final_norm_g):
    n_batch, n, d = x.shape
    ctx_len = ctx.shape[1]
    depth = ada_w.shape[0]
    t_lat = n_batch * n
    t_ctx = n_batch * ctx_len
    tt = t_lat + t_ctx
    fd = qd = d // 2
    assert n % TM == 0 and t_ctx % TM == 0 and n % (BLOCK * FB_NB) == 0 and n_batch < MOD_ROWS
    assert t_lat % ctx_len == 0 and TM % ctx_len == 0

    xa = jnp.concatenate([x.reshape(t_lat, d), ctx.reshape(t_ctx, d)], axis=0)
    cc = jnp.concatenate([c, c_ctx[None, :], jnp.zeros((MOD_ROWS - n_batch - 1, d), _F32)], axis=0)
    mod = _ada(cc, ada_w, ada_b).reshape(depth, MOD_ROWS, N_MOD, d)
    cos_t, sin_t = _rope_tables(n)
    consts = _dft_consts(n, ctx_len)

    n_in = fd + qd + 2 * KV_DIM
    w_a = w_in[:, :, :n_in].astype(_BF)
    w_g = w_in[:, :, n_in:].astype(_BF)
    w_fo, w_ao, w_o = w_fourier_out.astype(_BF), w_attn_out.astype(_BF), w_out.astype(_BF)
    rw = jnp.pad(router_w, ((0, 0), (0, 0), (0, LANES - N_EXPERTS))).astype(_BF)
    rb = jnp.pad(router_b, ((0, 0), (0, LANES - N_EXPERTS)), constant_values=NEG_INF)[:, None, :]

    n_blocks = -(-(tt * TOP_K + N_EXPERTS * (TME - 1)) // TME)
    dims = dict(n=n, t_lat=t_lat, n_batch=n_batch)
    for l in range(depth):
        last = l == depth - 1
        f, q, k2, v2 = _proj(xa, mod[l], norm1_g[l][None, :], w_a[l], cos_t, sin_t, **dims)
        fm_lat = _fourier_lat(f, consts, n_batch=n_batch, n=n)
        fm_ctx = _fourier_ctx(f, consts, n_batch=n_batch, ctx_len=ctx_len, t_lat=t_lat)
        at_lat = _attn_lat(attn_sink[l], q, k2, v2, n_batch=n_batch, n=n, ctx_len=ctx_len, t_lat=t_lat)
        at_ctx = _attn_ctx(attn_sink[l], q, k2, v2, n_batch=n_batch, ctx_len=ctx_len, t_lat=t_lat)
        x1, hp, idx4, gate4, counts = _merge(
            xa, mod[l], norm1_g[l][None, :], norm2_g[l][None, :], w_g[l], fm_lat, fm_ctx, at_lat, at_ctx,
            w_fo[l], w_ao[l], w_o[l], rw[l], rb[l], **dims)
        pad_start, block_e, n_valid, pad_end, cnt, block_rows = _routing_tables(counts, n_blocks)
        dest = _route(idx4, pad_start)[:, :TOP_K].T.reshape(-1)
        xs = _sc_scatter(hp, dest, _zero_tails(pad_end, cnt, n_valid, n_blocks * TME, hp.shape[1], hp.dtype))
        y = _moe(block_e, n_valid, block_rows, xs, expert_w_gu, expert_b_gu, expert_w_down, expert_b_down,
                 layer=l)
        yg = _sc_gather(y, dest).reshape(TOP_K, tt, d // 2)
        xa = _combine(yg, gate4, x1, mod[l], final_norm_g[None, :], n=n, n_batch=n_batch,
                      rows=t_lat if last else tt, final=last)
    return xa.reshape(n_batch, n, d)
```

```python
import functools

import numpy as np
import jax
import jax.numpy as jnp
from jax import lax
from jax.experimental import pallas as pl
from jax.experimental.pallas import tpu as pltpu
from jax.experimental.pallas import tpu_sc as plsc

GRID_W = 64
HEAD_DIM = 64
N_KV_HEADS = 2
Q_PER_KV = 4
KV_DIM = N_KV_HEADS * HEAD_DIM
WINDOW = 128
BLOCK = 128
ROPE_THETA = 10000.0
ROPE_FREQS = HEAD_DIM // 4
GROUP_DIM = 128
N_EXPERTS = 32
TOP_K = 4
SWIGLU_LIMIT = 7.0
SWIGLU_ALPHA = 1.702
N_MOD = 6
EPS = 1e-5
NEG_INF = -1e30
LOG2E = 1.4426950408889634
Q_SCALE = HEAD_DIM ** -0.5 * LOG2E

LANES = 128
MOD_ROWS = 8
TM = 1024
TME = 512
TMC = 512
SC_CORES = 2
SC_SUBCORES = 16
SC_WINDOW = 96
SC_SCATTER_WINDOW = 96
FA_NB = 16
FB_NB = 16
CAST_ROWS = 64
MERGE_COLS = 256
MERGE_ROW_SPLIT = 1
VMEM_LIMIT = 56 * 1024 * 1024

_BF = jnp.bfloat16
_F32 = jnp.float32


def _params(*sem):
    return pltpu.CompilerParams(dimension_semantics=sem, vmem_limit_bytes=VMEM_LIMIT)


def _dot(a, b):
    return jnp.dot(a, b, preferred_element_type=_F32)


def _sigmoid(x):
    return 0.5 * jnp.tanh(0.5 * x) + 0.5


def _norm_mod(x, g, shift, scale):
    ms = jnp.mean(x * x, axis=-1, keepdims=True)
    return (x * lax.rsqrt(ms + EPS) * g) * (1 + scale) + shift


def _ada_kernel(c_ref, w_ref, b_ref, o_ref):
    c = c_ref[...]
    s = c * jax.nn.sigmoid(c)
    o_ref[0] = jnp.dot(s, w_ref[0], precision=lax.Precision.HIGHEST,
                       preferred_element_type=_F32) + b_ref[0]


def _ada(cc, ada_w, ada_b):
    depth, d, nd = ada_w.shape
    tn = nd // 4
    return pl.pallas_call(
        _ada_kernel,
        grid=(depth, nd // tn),
        in_specs=[pl.BlockSpec((MOD_ROWS, d), lambda l, j: (0, 0)),
                  pl.BlockSpec((1, d, tn), lambda l, j: (l, 0, j)),
                  pl.BlockSpec((1, 1, tn), lambda l, j: (l, 0, j))],
        out_specs=pl.BlockSpec((1, MOD_ROWS, tn), lambda l, j: (l, 0, j)),
        out_shape=jax.ShapeDtypeStruct((depth, MOD_ROWS, nd), _F32),
        compiler_params=_params("arbitrary", "arbitrary"),
        name="ada_mod",
    )(cc, ada_w, ada_b.reshape(depth, 1, nd))


def _rope(v, cos, sin):
    lane = lax.broadcasted_iota(jnp.int32, v.shape, 1)
    partner = jnp.where((lane & ROPE_FREQS) == 0,
                        pltpu.roll(v, LANES - ROPE_FREQS, 1), pltpu.roll(v, ROPE_FREQS, 1))
    return v * cos + partner * sin


def _proj_kernel(x_ref, mod_ref, g_ref, w_ref, cos_ref, sin_ref, f_ref, q_ref, k_ref, vt_ref, *, fd, qd):
    m = mod_ref[0]
    h = _norm_mod(x_ref[...], g_ref[...], m[0:1], m[1:2]).astype(_BF)
    p = _dot(h, w_ref[...])
    cos = cos_ref[...]
    sin = sin_ref[...]
    half = BLOCK // 2
    for a in range(p.shape[0] // BLOCK):
        f_ref[a * half:(a + 1) * half, :] = _pack_rows(p[a * BLOCK:a * BLOCK + half, :fd],
                                                       p[a * BLOCK + half:(a + 1) * BLOCK, :fd])
    for j in range(qd // LANES):
        lo = fd + j * LANES
        q_ref[:, j * LANES:(j + 1) * LANES] = (
            _rope(p[:, lo:lo + LANES], cos, sin) * Q_SCALE).astype(_BF)
    k = _rope(p[:, fd + qd:fd + qd + KV_DIM], cos, sin)
    v = p[:, fd + qd + KV_DIM:fd + qd + 2 * KV_DIM]
    k_ref[:, :KV_DIM] = k.astype(_BF)
    k_ref[:, KV_DIM:] = pltpu.roll(k, HEAD_DIM, 1).astype(_BF)
    vt_ref[:KV_DIM, :] = v.T.astype(_BF)
    vt_ref[KV_DIM:, :] = pltpu.roll(v, HEAD_DIM, 1).T.astype(_BF)


def _proj(xa, mod, g, w_a, cos_t, sin_t, *, n, t_lat, n_batch):
    tt, d = xa.shape
    fd = qd = d // 2
    nt = tt // TM
    nxt = t_lat // TM
    per_seq = n // TM
    mod_idx = lambda i: (jnp.minimum(i * TM // n, n_batch), 0, 0)
    rope_idx = lambda i: (jnp.where(i < nxt, i % per_seq, per_seq), 0)
    row = lambda i: (i, 0)
    return pl.pallas_call(
        functools.partial(_proj_kernel, fd=fd, qd=qd),
        grid=(nt,),
        in_specs=[pl.BlockSpec((TM, d), row),
                  pl.BlockSpec((1, N_MOD, d), mod_idx),
                  pl.BlockSpec((1, d), lambda i: (0, 0)),
                  pl.BlockSpec(w_a.shape, lambda i: (0, 0)),
                  pl.BlockSpec((TM, LANES), rope_idx),
                  pl.BlockSpec((TM, LANES), rope_idx)],
        out_specs=[pl.BlockSpec((TM // 2, fd), row), pl.BlockSpec((TM, qd), row),
                   pl.BlockSpec((TM, 2 * KV_DIM), row), pl.BlockSpec((2 * KV_DIM, TM), lambda i: (0, i))],
        out_shape=[jax.ShapeDtypeStruct((tt // 2, fd), jnp.uint32), jax.ShapeDtypeStruct((tt, qd), _BF),
                   jax.ShapeDtypeStruct((tt, 2 * KV_DIM), _BF), jax.ShapeDtypeStruct((2 * KV_DIM, tt), _BF)],
        compiler_params=_params("arbitrary"),
        name="proj_in",
    )(xa, mod, g, w_a, cos_t, sin_t)


def _rope_tables(n):
    pos = jnp.arange(n)
    inv = ROPE_THETA ** (-jnp.arange(ROPE_FREQS, dtype=_F32) / ROPE_FREQS)
    ar = (pos // GRID_W).astype(_F32)[:, None] * inv
    ac = (pos % GRID_W).astype(_F32)[:, None] * inv
    cos = jnp.concatenate([jnp.cos(ar), jnp.cos(ar), jnp.cos(ac), jnp.cos(ac)], axis=1)
    sin = jnp.concatenate([-jnp.sin(ar), jnp.sin(ar), -jnp.sin(ac), jnp.sin(ac)], axis=1)
    cos = jnp.tile(cos, (1, LANES // HEAD_DIM))
    sin = jnp.tile(sin, (1, LANES // HEAD_DIM))
    cos = jnp.concatenate([cos, jnp.ones((TM, LANES), _F32)], axis=0)
    sin = jnp.concatenate([sin, jnp.zeros((TM, LANES), _F32)], axis=0)
    return cos, sin


def _dft_consts(n, ctx_len):
    n1 = n // BLOCK
    b = np.arange(BLOCK, dtype=np.int64)[:, None, None]
    k1 = np.arange(n1, dtype=np.int64)[None, :, None]
    a = np.arange(n1, dtype=np.int64)[None, None, :]
    ang = 2.0 * np.pi * ((a * k1 * BLOCK + b * k1) % n).astype(np.float64) / n
    m_ri = np.concatenate([np.cos(ang), -np.sin(ang)], axis=1)
    kk = np.arange(GROUP_DIM, dtype=np.int64)
    ang_c = 2.0 * np.pi * ((kk[:, None] * kk[None, :]) % GROUP_DIM) / GROUP_DIM
    c, s = np.cos(ang_c), np.sin(ang_c)
    cs = np.concatenate([np.concatenate([c, s], axis=1), np.concatenate([-s, c], axis=1)], axis=0)
    cc = np.concatenate([c, s], axis=0)
    kl = np.arange(ctx_len, dtype=np.int64)
    ang_l = 2.0 * np.pi * ((kl[:, None] * kl[None, :]) % ctx_len) / ctx_len
    wl = np.concatenate([np.cos(ang_l), -np.sin(ang_l)], axis=0)
    as_bf = lambda v: jnp.asarray(v, dtype=_F32).astype(_BF)
    return dict(m_ri=as_bf(m_ri), cs=as_bf(cs), cc=as_bf(cc), wl=as_bf(wl))


def _chan_dft(xr, xi, cc, scale):
    outs = []
    for g in range(xr.shape[1] // GROUP_DIM):
        sl = slice(g * GROUP_DIM, (g + 1) * GROUP_DIM)
        xx = jnp.concatenate([xr[:, sl], xi[:, sl]], axis=1).astype(_BF)
        outs.append(_dot(xx, cc) * scale)
    return jnp.concatenate(outs, axis=1)


def _pack_rows(lo, hi):
    a = lax.bitcast_convert_type(lo.astype(_BF).astype(_F32), jnp.uint32)
    b = lax.bitcast_convert_type(hi.astype(_BF).astype(_F32), jnp.uint32)
    return (a >> 16) | b


def _unpack_rows(p):
    lo = lax.bitcast_convert_type(p << 16, _F32).astype(_BF)
    hi = lax.bitcast_convert_type(p & jnp.uint32(0xFFFF0000), _F32).astype(_BF)
    return lo, hi


def _fa_kernel(f_ref, mlo_ref, mhi_ref, zr_ref, zi_ref):
    n1 = f_ref.shape[0]
    for t in range(FA_NB):
        lo, hi = _unpack_rows(f_ref[:, t, :])
        za = _dot(mlo_ref[t], lo)
        zb = _dot(mhi_ref[t], hi)
        zr_ref[:, t, :] = _pack_rows(za[:n1], zb[:n1])
        zi_ref[:, t, :] = _pack_rows(za[n1:], zb[n1:])


def _fb_kernel(zr_ref, zi_ref, cs_ref, cc_ref, o_ref, *, scale):
    cs, cc = cs_ref[...], cc_ref[...]
    for j in range(FB_NB):
        xx = _dot(cs, jnp.concatenate(_unpack_rows(zr_ref[j]) + _unpack_rows(zi_ref[j]), axis=0))
        o_ref[:, j, :] = _chan_dft(xx[:BLOCK], xx[BLOCK:], cc, scale).astype(_BF)


def _fourier_lat(f, consts, *, n_batch, n):
    fd = f.shape[1]
    n1 = n // BLOCK
    half = BLOCK // 2
    f3 = f.reshape(f.shape[0] // half, half, fd)
    steps = half // FA_NB
    blk_a = pl.BlockSpec((n1, FA_NB, fd), lambda b, j: (b, j, 0))
    mat_lo = pl.BlockSpec((FA_NB, 2 * n1, n1), lambda b, j: (j, 0, 0))
    mat_hi = pl.BlockSpec((FA_NB, 2 * n1, n1), lambda b, j: (j + steps, 0, 0))
    z_shape = jax.ShapeDtypeStruct((n_batch * n1, half, fd), jnp.uint32)
    zr, zi = pl.pallas_call(
        _fa_kernel,
        grid=(n_batch, steps),
        in_specs=[blk_a, mat_lo, mat_hi],
        out_specs=[blk_a, blk_a],
        out_shape=[z_shape, z_shape],
        compiler_params=_params("arbitrary", "arbitrary"),
        name="fourier_seq_a",
    )(f3, consts["m_ri"], consts["m_ri"])
    blk_z = pl.BlockSpec((FB_NB, half, fd), lambda b, j: (b * (n1 // FB_NB) + j, 0, 0))
    const = lambda shape: pl.BlockSpec(shape, lambda b, j: (0, 0))
    out = pl.pallas_call(
        functools.partial(_fb_kernel, scale=float((n * GROUP_DIM) ** -0.5)),
        grid=(n_batch, n1 // FB_NB),
        in_specs=[blk_z, blk_z, const((2 * BLOCK, 2 * BLOCK)), const((2 * GROUP_DIM, GROUP_DIM))],
        out_specs=pl.BlockSpec((BLOCK, FB_NB, fd), lambda b, j: (b, j, 0)),
        out_shape=jax.ShapeDtypeStruct((n_batch * BLOCK, n1, fd), _BF),
        compiler_params=_params("arbitrary", "arbitrary"),
        name="fourier_seq_b",
    )(zr, zi, consts["cs"], consts["cc"])
    return out.reshape(n_batch * n, fd)


def _fc_kernel(f_ref, wl_ref, cc_ref, o_ref, *, scale):
    half = BLOCK // 2
    ctx_len = 2 * f_ref.shape[0]
    lo, hi = _unpack_rows(f_ref[...])
    rows = []
    for a in range(ctx_len // BLOCK):
        rows += [lo[a * half:(a + 1) * half], hi[a * half:(a + 1) * half]]
    xx = _dot(wl_ref[...], jnp.concatenate(rows, axis=0))
    o_ref[...] = _chan_dft(xx[:ctx_len], xx[ctx_len:], cc_ref[...], scale).astype(_BF)


def _fourier_ctx(f, consts, *, n_batch, ctx_len, t_lat):
    fd = f.shape[1]
    const = lambda shape: pl.BlockSpec(shape, lambda b: (0, 0))
    return pl.pallas_call(
        functools.partial(_fc_kernel, scale=float((ctx_len * GROUP_DIM) ** -0.5)),
        grid=(n_batch,),
        in_specs=[pl.BlockSpec((ctx_len // 2, fd), lambda b: (t_lat // ctx_len + b, 0)),
                  const((2 * ctx_len, ctx_len)), const((2 * GROUP_DIM, GROUP_DIM))],
        out_specs=pl.BlockSpec((ctx_len, fd), lambda b: (b, 0)),
        out_shape=jax.ShapeDtypeStruct((n_batch * ctx_len, fd), _BF),
        compiler_params=_params("arbitrary"),
        name="fourier_ctx",
    )(f, consts["wl"], consts["cc"])


def _attn_core(sink_ref, q_ref, k, vt, o_ref, bias, s_scr, p_scr):
    tq = q_ref.shape[0]
    lane = lax.broadcasted_iota(jnp.int32, (tq, LANES), 1)
    low = lane < HEAD_DIM
    zero = jnp.zeros((tq, LANES), _BF)
    low_row = lax.broadcasted_iota(jnp.int32, (LANES, 2 * tq), 0) < HEAD_DIM
    first = lax.broadcasted_iota(jnp.int32, (1, 2 * tq), 1) < tq
    pairs = [(kvh, half) for kvh in range(N_KV_HEADS) for half in range(2)]
    sels = [slice(0, LANES) if half == kvh else slice(LANES, 2 * LANES) for kvh, half in pairs]
    for pi, (kvh, half) in enumerate(pairs):
        slabs = [q_ref[:, (2 * kvh + c) * LANES:(2 * kvh + c + 1) * LANES] for c in range(2)]
        keep = low if half == 0 else jnp.logical_not(low)
        qm = jnp.concatenate([jnp.where(keep, sl, zero) for sl in slabs], axis=0)
        s = lax.dot_general(k[:, sels[pi]], qm, (((1,), (1,)), ((), ())), preferred_element_type=_F32)
        s_scr[pi] = s if bias is None else s + bias
    dens = []
    for pi, (kvh, half) in enumerate(pairs):
        hq = Q_PER_KV * kvh + half
        sk = jnp.where(first, sink_ref[hq], sink_ref[hq + 2]) * LOG2E
        s = s_scr[pi]
        m = jnp.maximum(jnp.max(s, axis=0, keepdims=True), sk)
        p = jnp.exp2(s - m)
        dens.append(jnp.sum(p, axis=0, keepdims=True) + jnp.exp2(sk - m))
        p_scr[pi] = p.astype(_BF)
    outs = [_dot(vt[sels[pi], :], p_scr[pi]) * (1.0 / dens[pi]) for pi in range(len(pairs))]
    for kvh in range(N_KV_HEADS):
        ot = jnp.where(low_row, outs[2 * kvh], outs[2 * kvh + 1])
        for c in range(2):
            o_ref[:, (2 * kvh + c) * LANES:(2 * kvh + c + 1) * LANES] = ot[:, c * tq:(c + 1) * tq].T.astype(_BF)


def _attn_scratch(keys, tq):
    return [pltpu.VMEM((2 * N_KV_HEADS, keys, 2 * tq), _F32), pltpu.VMEM((2 * N_KV_HEADS, keys, 2 * tq), _BF)]


def _attn_lat_kernel(sink_ref, q_ref, kp_ref, kc_ref, kn_ref, vp_ref, vc_ref, vn_ref, kx_ref, vx_ref, bias_ref,
                     o_ref, s_scr, p_scr):
    k = jnp.concatenate([kp_ref[...], kc_ref[...], kn_ref[...], kx_ref[...]], axis=0)
    vt = jnp.concatenate([vp_ref[...], vc_ref[...], vn_ref[...], vx_ref[...]], axis=1)
    _attn_core(sink_ref, q_ref, k, vt, o_ref, bias_ref[0], s_scr, p_scr)


def _window_bias(ctx_len):
    s = np.arange(3 * BLOCK + ctx_len)[:, None]
    r = np.arange(2 * BLOCK)[None, :] % BLOCK
    out = []
    for v in range(4):
        lo = 0 if v & 1 else BLOCK
        hi = 3 * BLOCK if v & 2 else 2 * BLOCK
        ok = ((np.abs(s - BLOCK - r) <= WINDOW) & (s >= lo) & (s < hi)) | (s >= 3 * BLOCK)
        out.append(np.where(ok, 0.0, NEG_INF))
    return jnp.asarray(np.stack(out), dtype=_F32)


def _attn_ctx_kernel(sink_ref, q_ref, kx_ref, vx_ref, o_ref, s_scr, p_scr):
    _attn_core(sink_ref, q_ref, kx_ref[...], vx_ref[...], o_ref, None, s_scr, p_scr)


def _attn_lat(sink, q, k2, v2t, *, n_batch, n, ctx_len, t_lat):
    qd = q.shape[1]
    nb = n // BLOCK
    cur = lambda b, i: (b * nb + i, 0)
    prev = lambda b, i: (b * nb + jnp.maximum(i - 1, 0), 0)
    nxt = lambda b, i: (b * nb + jnp.minimum(i + 1, nb - 1), 0)
    cx = lambda b, i: (t_lat // ctx_len + b, 0)
    swap = lambda im: (lambda b, i: im(b, i)[::-1])
    kb = lambda im: pl.BlockSpec((BLOCK, 2 * KV_DIM), im)
    vb = lambda im: pl.BlockSpec((2 * KV_DIM, BLOCK), swap(im))
    bias = _window_bias(ctx_len)
    bias_idx = lambda b, i: ((i > 0).astype(jnp.int32) + 2 * (i < nb - 1).astype(jnp.int32), 0, 0)
    return pl.pallas_call(
        _attn_lat_kernel,
        grid=(n_batch, nb),
        in_specs=[pl.BlockSpec(memory_space=pltpu.SMEM), pl.BlockSpec((BLOCK, qd), cur),
                  kb(prev), kb(cur), kb(nxt), vb(prev), vb(cur), vb(nxt),
                  pl.BlockSpec((ctx_len, 2 * KV_DIM), cx), pl.BlockSpec((2 * KV_DIM, ctx_len), swap(cx)),
                  pl.BlockSpec((1,) + bias.shape[1:], bias_idx)],
        out_specs=pl.BlockSpec((BLOCK, qd), cur),
        out_shape=jax.ShapeDtypeStruct((t_lat, qd), _BF),
        scratch_shapes=_attn_scratch(3 * BLOCK + ctx_len, BLOCK),
        compiler_params=_params("arbitrary", "arbitrary"),
        name="attn_lat",
    )(sink, q, k2, k2, k2, v2t, v2t, v2t, k2, v2t, bias)


def _attn_ctx(sink, q, k2, v2t, *, n_batch, ctx_len, t_lat):
    qd = q.shape[1]
    cx = lambda b: (t_lat // ctx_len + b, 0)
    return pl.pallas_call(
        _attn_ctx_kernel,
        grid=(n_batch,),
        in_specs=[pl.BlockSpec(memory_space=pltpu.SMEM), pl.BlockSpec((ctx_len, qd), cx),
                  pl.BlockSpec((ctx_len, 2 * KV_DIM), cx),
                  pl.BlockSpec((2 * KV_DIM, ctx_len), lambda b: (0, t_lat // ctx_len + b))],
        out_specs=pl.BlockSpec((ctx_len, qd), lambda b: (b, 0)),
        out_shape=jax.ShapeDtypeStruct((n_batch * ctx_len, qd), _BF),
        scratch_shapes=_attn_scratch(ctx_len, ctx_len),
        compiler_params=_params("arbitrary"),
        name="attn_ctx",
    )(sink, q, k2, v2t)


def _pack_bf16_pairs(h):
    half = h.shape[1] // 2
    lo = lax.bitcast_convert_type(h[:, :half].astype(_BF).astype(_F32), jnp.uint32)
    hi = lax.bitcast_convert_type(h[:, half:].astype(_BF).astype(_F32), jnp.uint32)
    return (lo >> 16) | hi


def _unpack_bf16_pairs(p):
    lo = lax.bitcast_convert_type(p << 16, _F32)
    hi = lax.bitcast_convert_type(p & jnp.uint32(0xFFFF0000), _F32)
    return jnp.concatenate([lo, hi], axis=1).astype(_BF)


def _merge_kernel(x_ref, mod_ref, g1_ref, g2_ref, wg_ref, fml_ref, fmc_ref, atl_ref, atc_ref,
                  wfo_ref, wao_ref, wout_ref, rw_ref, rb_ref,
                  x1_ref, hp_ref, idx_ref, gate_ref, cnt_ref, h_scr, y_scr, *, nxt):
    i = pl.program_id(0)
    d = x_ref.shape[1]
    m = mod_ref[0]
    is_lat = i < nxt
    rows_per = x_ref.shape[0] // MERGE_ROW_SPLIT
    sel_sum = jnp.zeros((1, LANES), _F32)
    for grp in range(MERGE_ROW_SPLIT):
        rows = slice(grp * rows_per, (grp + 1) * rows_per)
        h_scr[rows, :] = _norm_mod(x_ref[rows, :], g1_ref[...], m[0:1], m[1:2]).astype(_BF)
        fm = jnp.where(is_lat, fml_ref[rows, :], fmc_ref[rows, :])
        at = jnp.where(is_lat, atl_ref[rows, :], atc_ref[rows, :])
        for c in range(d // MERGE_COLS):
            sl = slice(c * MERGE_COLS, (c + 1) * MERGE_COLS)
            sg = slice(d + c * MERGE_COLS, d + (c + 1) * MERGE_COLS)
            h = h_scr[rows, :]
            y = (_sigmoid(_dot(h, wg_ref[:, sl])) * _dot(fm, wfo_ref[:, sl])
                 + _sigmoid(_dot(h, wg_ref[:, sg])) * _dot(at, wao_ref[:, sl]))
            y_scr[rows, sl] = y.astype(_BF)
        x1_ref[rows, :] = x_ref[rows, :] + m[2:3] * _dot(y_scr[rows, :], wout_ref[...])
        h2 = _norm_mod(x1_ref[rows, :], g2_ref[...], m[3:4], m[4:5])
        hp_ref[rows, :] = _pack_bf16_pairs(h2)
        logits = _dot(h2.astype(_BF), rw_ref[...]) + rb_ref[...]
        lane = lax.broadcasted_iota(jnp.int32, logits.shape, 1).astype(_F32)
        vals, idxs = [], []
        for _ in range(TOP_K):
            mx = jnp.max(logits, axis=-1, keepdims=True)
            ix = jnp.min(jnp.where(logits == mx, lane, float(LANES)), axis=-1, keepdims=True)
            logits = jnp.where(lane == ix, NEG_INF, logits)
            vals.append(mx)
            idxs.append(ix)
        es = [jnp.exp(v - vals[0]) for v in vals]
        den = es[0] + es[1] + es[2] + es[3]
        idx4 = jnp.zeros(logits.shape, _F32)
        gate4 = jnp.zeros(logits.shape, _F32)
        sel = jnp.zeros(logits.shape, _F32)
        for k in range(TOP_K):
            idx4 = jnp.where(lane == float(k), idxs[k], idx4)
            gate4 = jnp.where(lane == float(k), es[k] / den, gate4)
            sel = sel + jnp.where(lane == idxs[k], 1.0, 0.0)
        idx_ref[rows, :] = idx4.astype(jnp.int32)
        gate_ref[rows, :] = gate4
        sel_sum = sel_sum + jnp.sum(sel, axis=0, keepdims=True)

    @pl.when(i == 0)
    def _():
        cnt_ref[...] = jnp.zeros(cnt_ref.shape, _F32)

    cnt_ref[0:1, :] += sel_sum


def _merge(xa, mod, g1, g2, w_g, fm_lat, fm_ctx, at_lat, at_ctx, w_fo, w_ao, w_out, rw, rb,
           *, n, t_lat, n_batch):
    tt, d = xa.shape
    fd = fm_lat.shape[1]
    nt = tt // TM
    nxt = t_lat // TM
    row = lambda i: (i, 0)
    lat = lambda i: (jnp.minimum(i, nxt - 1), 0)
    cxt = lambda i: (jnp.maximum(i - nxt, 0), 0)
    const = lambda a: pl.BlockSpec(a.shape, lambda i: (0, 0))
    mod_idx = lambda i: (jnp.minimum(i * TM // n, n_batch), 0, 0)
    return pl.pallas_call(
        functools.partial(_merge_kernel, nxt=nxt),
        grid=(nt,),
        in_specs=[pl.BlockSpec((TM, d), row), pl.BlockSpec((1, N_MOD, d), mod_idx),
                  const(g1), const(g2), const(w_g),
                  pl.BlockSpec((TM, fd), lat), pl.BlockSpec((TM, fd), cxt),
                  pl.BlockSpec((TM, fd), lat), pl.BlockSpec((TM, fd), cxt),
                  const(w_fo), const(w_ao), const(w_out), const(rw), const(rb)],
        out_specs=[pl.BlockSpec((TM, d), row), pl.BlockSpec((TM, d // 2), row),
                   pl.BlockSpec((TM, LANES), row), pl.BlockSpec((TM, LANES), row),
                   pl.BlockSpec((8, LANES), lambda i: (0, 0))],
        out_shape=[jax.ShapeDtypeStruct((tt, d), _F32), jax.ShapeDtypeStruct((tt, d // 2), jnp.uint32),
                   jax.ShapeDtypeStruct((tt, LANES), jnp.int32), jax.ShapeDtypeStruct((tt, LANES), _F32),
                   jax.ShapeDtypeStruct((8, LANES), _F32)],
        scratch_shapes=[pltpu.VMEM((TM, d), _BF), pltpu.VMEM((TM, d), _BF)],
        compiler_params=_params("arbitrary"),
        name="merge_router",
    )(xa, mod, g1, g2, w_g, fm_lat, fm_ctx, at_lat, at_ctx, w_fo, w_ao, w_out, rw, rb)


def _route_kernel(idx_ref, ps_ref, d_ref, carry):
    i = pl.program_id(0)

    @pl.when(i == 0)
    def _():
        carry[...] = ps_ref[...]

    idx4 = idx_ref[...]
    lane = lax.broadcasted_iota(jnp.int32, idx4.shape, 1)
    cols = [idx4[:, k:k + 1] for k in range(TOP_K)]
    sel = jnp.zeros(idx4.shape, _F32)
    for k in range(TOP_K):
        sel = sel + jnp.where(lane == cols[k], 1.0, 0.0)
    r = lax.broadcasted_iota(jnp.int32, (TM, TM), 0)
    c = lax.broadcasted_iota(jnp.int32, (TM, TM), 1)
    tri = jnp.where(r > c, 1.0, 0.0).astype(_BF)
    slot = carry[...] + _dot(tri, sel.astype(_BF))
    dest4 = jnp.zeros(idx4.shape, jnp.int32)
    for k in range(TOP_K):
        dk = jnp.sum(jnp.where(lane == cols[k], slot, 0.0), axis=-1, keepdims=True)
        dest4 = jnp.where(lane == k, dk.astype(jnp.int32), dest4)
    d_ref[...] = dest4
    carry[...] += jnp.sum(sel, axis=0, keepdims=True)


def _route(idx4, pad_start):
    tt = idx4.shape[0]
    return pl.pallas_call(
        _route_kernel,
        grid=(tt // TM,),
        in_specs=[pl.BlockSpec((TM, LANES), lambda i: (i, 0)), pl.BlockSpec((1, LANES), lambda i: (0, 0))],
        out_specs=pl.BlockSpec((TM, LANES), lambda i: (i, 0)),
        out_shape=jax.ShapeDtypeStruct((tt, LANES), jnp.int32),
        scratch_shapes=[pltpu.VMEM((1, LANES), _F32)],
        compiler_params=_params("arbitrary"),
        name="route_slots",
    )(idx4, pad_start)


def _zero_tails_kernel(pend_ref, cnt_ref, nv_ref, xs_ref, zbuf, sem):
    n_blocks = xs_ref.shape[0] // TME
    zbuf[...] = jnp.zeros(zbuf.shape, zbuf.dtype)

    def zero_block(start):
        return pltpu.make_async_copy(zbuf, xs_ref.at[pl.ds(pl.multiple_of(start, TME), TME)], sem)

    for e in range(N_EXPERTS):
        @pl.when(cnt_ref[e] > 0)
        def _():
            zero_block(pend_ref[e] - TME).start()

    def start_unused(j, carry):
        zero_block(j * TME).start()
        return carry

    def wait_unused(j, carry):
        zero_block(j * TME).wait()
        return carry

    lax.fori_loop(nv_ref[0], n_blocks, start_unused, 0)
    for e in range(N_EXPERTS):
        @pl.when(cnt_ref[e] > 0)
        def _():
            zero_block(pend_ref[e] - TME).wait()
    lax.fori_loop(nv_ref[0], n_blocks, wait_unused, 0)


def _zero_tails(pad_end, cnt, n_valid, n_slots, w, dtype):
    grid_spec = pltpu.PrefetchScalarGridSpec(
        num_scalar_prefetch=3,
        grid=(1,),
        in_specs=[],
        out_specs=pl.BlockSpec(memory_space=pl.ANY),
        scratch_shapes=[pltpu.VMEM((TME, w), dtype), pltpu.SemaphoreType.DMA],
    )
    return pl.pallas_call(
        _zero_tails_kernel,
        grid_spec=grid_spec,
        out_shape=jax.ShapeDtypeStruct((n_slots, w), dtype),
        compiler_params=_params("arbitrary"),
        name="moe_zero_tails",
    )(pad_end, cnt, n_valid)


def _sc_scatter(rows, idx, base):
    tt, w = rows.shape
    workers = SC_CORES * SC_SUBCORES
    per_worker = tt // workers
    assert tt % workers == 0 and per_worker % SC_SCATTER_WINDOW == 0
    mesh = plsc.VectorSubcoreMesh(core_axis_name="core", subcore_axis_name="subcore")
    out = jax.new_ref(base)

    @pl.kernel(out_type=(), mesh=mesh,
               scratch_types=[pltpu.VMEM((SC_SCATTER_WINDOW,), jnp.int32),
                              pltpu.VMEM((SC_SCATTER_WINDOW, w), rows.dtype), pltpu.SemaphoreType.DMA])
    def scatter(r_hbm, i_hbm, o_hbm, idx_v, rows_v, sem):
        wid = lax.axis_index("subcore") * SC_CORES + lax.axis_index("core")

        @pl.loop(0, per_worker // SC_SCATTER_WINDOW)
        def _(c):
            first = wid * per_worker + c * SC_SCATTER_WINDOW
            pltpu.sync_copy(r_hbm.at[pl.ds(first, SC_SCATTER_WINDOW)], rows_v)
            for k in range(TOP_K):
                pltpu.sync_copy(i_hbm.at[pl.ds(k * tt + first, SC_SCATTER_WINDOW)], idx_v)
                pltpu.async_copy(rows_v, o_hbm.at[idx_v], sem).wait()

    scatter(rows, idx, out)
    return jax.freeze(out)


def _moe_kernel(be_ref, nv_ref, xs_ref, wgu_ref, bgu_ref, wd_ref, bd_ref, y_ref, wgu_bf, wd_bf):
    i = pl.program_id(0)
    de = wd_ref.shape[2]
    valid = i < nv_ref[0]
    new_expert = jnp.logical_or(i == 0, be_ref[i] != be_ref[jnp.maximum(i - 1, 0)])

    @pl.when(jnp.logical_and(valid, new_expert))
    def _():
        def cast_rows(ref, out, rows):
            def body(j, carry):
                sl = pl.ds(pl.multiple_of(j * CAST_ROWS, CAST_ROWS), CAST_ROWS)
                out[sl, :] = ref[0, 0, sl, :].astype(_BF)
                return carry
            lax.fori_loop(0, rows // CAST_ROWS, body, 0)
        cast_rows(wgu_ref, wgu_bf, wgu_ref.shape[2])
        cast_rows(wd_ref, wd_bf, de)

    @pl.when(valid)
    def _():
        xb = _unpack_bf16_pairs(xs_ref[...])
        gu = _dot(xb, wgu_bf[...]) + bgu_ref[0, 0]
        a = jnp.minimum(gu[:, :de], SWIGLU_LIMIT)
        u = jnp.clip(gu[:, de:], -SWIGLU_LIMIT, SWIGLU_LIMIT)
        act = a * _sigmoid(SWIGLU_ALPHA * a) * (u + 1)
        y_ref[...] = _pack_bf16_pairs(_dot(act.astype(_BF), wd_bf[...]) + bd_ref[0, 0])

    @pl.when(i >= nv_ref[0])
    def _():
        y_ref[...] = jnp.zeros(y_ref.shape, y_ref.dtype)


def _moe(block_e, n_valid, xs, w_gu, b_gu, w_down, b_down, *, layer):
    n_slots, w = xs.shape
    depth, ne, d, de2 = w_gu.shape
    de = de2 // 2
    n_blocks = n_slots // TME
    ex = lambda i, be, nv: (layer, be[i], 0, 0)
    grid_spec = pltpu.PrefetchScalarGridSpec(
        num_scalar_prefetch=2,
        grid=(n_blocks,),
        in_specs=[pl.BlockSpec((TME, w), lambda i, be, nv: (jnp.minimum(i, nv[0] - 1), 0)),
                  pl.BlockSpec((1, 1, d, de2), ex), pl.BlockSpec((1, 1, 1, de2), ex),
                  pl.BlockSpec((1, 1, de, d), ex), pl.BlockSpec((1, 1, 1, d), ex)],
        out_specs=pl.BlockSpec((TME, d // 2), lambda i, be, nv: (i, 0)),
        scratch_shapes=[pltpu.VMEM((d, de2), _BF), pltpu.VMEM((de, d), _BF)],
    )
    return pl.pallas_call(
        _moe_kernel,
        grid_spec=grid_spec,
        out_shape=jax.ShapeDtypeStruct((n_slots, d // 2), jnp.uint32),
        compiler_params=_params("arbitrary"),
        name="moe_experts",
    )(block_e, n_valid, xs, w_gu, b_gu.reshape(depth, ne, 1, de2), w_down, b_down.reshape(depth, ne, 1, d))


def _sc_gather(table, idx):
    n_idx, (_, w) = idx.shape[0], table.shape
    workers = SC_CORES * SC_SUBCORES
    per_worker = n_idx // workers
    n_win = per_worker // SC_WINDOW
    assert n_idx % workers == 0 and per_worker % SC_WINDOW == 0 and n_win % 2 == 0
    mesh = plsc.VectorSubcoreMesh(core_axis_name="core", subcore_axis_name="subcore")
    idx_buf = pltpu.VMEM((SC_WINDOW,), jnp.int32)
    row_buf = pltpu.VMEM((SC_WINDOW, w), table.dtype)

    @pl.kernel(out_type=jax.ShapeDtypeStruct((n_idx, w), table.dtype), mesh=mesh,
               scratch_types=[idx_buf, idx_buf, row_buf, row_buf] + [pltpu.SemaphoreType.DMA] * 4)
    def gather(t_hbm, i_hbm, o_hbm, idx_a, idx_b, rows_a, rows_b, gsem_a, gsem_b, ssem_a, ssem_b):
        wid = lax.axis_index("subcore") * SC_CORES + lax.axis_index("core")
        first = wid * per_worker

        def gather_start(c, idx_v, rows_v, sem):
            pltpu.sync_copy(i_hbm.at[pl.ds(first + c * SC_WINDOW, SC_WINDOW)], idx_v)
            pltpu.make_async_copy(t_hbm.at[idx_v], rows_v, sem).start()

        def gather_wait(idx_v, rows_v, sem):
            pltpu.make_async_copy(t_hbm.at[idx_v], rows_v, sem).wait()

        def store_start(c, rows_v, sem):
            pltpu.make_async_copy(rows_v, o_hbm.at[pl.ds(first + c * SC_WINDOW, SC_WINDOW)], sem).start()

        def store_wait(rows_v, sem):
            pltpu.make_async_copy(rows_v, o_hbm.at[pl.ds(first, SC_WINDOW)], sem).wait()

        gather_start(0, idx_a, rows_a, gsem_a)

        @pl.loop(0, n_win // 2)
        def _(i):
            c = 2 * i

            @pl.when(i > 0)
            def _():
                store_wait(rows_b, ssem_b)

            gather_start(c + 1, idx_b, rows_b, gsem_b)
            gather_wait(idx_a, rows_a, gsem_a)
            store_start(c, rows_a, ssem_a)
            store_wait(rows_a, ssem_a)

            @pl.when(c + 2 < n_win)
            def _():
                gather_start(c + 2, idx_a, rows_a, gsem_a)

            gather_wait(idx_b, rows_b, gsem_b)
            store_start(c + 1, rows_b, ssem_b)

        store_wait(rows_b, ssem_b)

    return gather(table, idx)


def _combine_kernel(yg_ref, gate_ref, x_ref, mod_ref, fg_ref, o_ref, *, final):
    g = gate_ref[...]
    half = x_ref.shape[1] // 2
    lo = jnp.zeros((x_ref.shape[0], half), _F32)
    hi = jnp.zeros((x_ref.shape[0], half), _F32)
    for k in range(TOP_K):
        p = yg_ref[k]
        lo = lo + g[:, k:k + 1] * lax.bitcast_convert_type(p << 16, _F32)
        hi = hi + g[:, k:k + 1] * lax.bitcast_convert_type(p & jnp.uint32(0xFFFF0000), _F32)
    x2 = x_ref[...] + mod_ref[0][5:6] * jnp.concatenate([lo, hi], axis=1)
    if final:
        ms = jnp.mean(x2 * x2, axis=-1, keepdims=True)
        x2 = x2 * lax.rsqrt(ms + EPS) * fg_ref[...]
    o_ref[...] = x2


def _combine(yg, gate4, x1, mod, fg, *, n, n_batch, rows, final):
    d = x1.shape[1]
    row = lambda i: (i, 0)
    mod_idx = lambda i: (jnp.minimum(i * TMC // n, n_batch), 0, 0)
    return pl.pallas_call(
        functools.partial(_combine_kernel, final=final),
        grid=(rows // TMC,),
        in_specs=[pl.BlockSpec((TOP_K, TMC, d // 2), lambda i: (0, i, 0)),
                  pl.BlockSpec((TMC, LANES), row), pl.BlockSpec((TMC, d), row),
                  pl.BlockSpec((1, N_MOD, d), mod_idx), pl.BlockSpec((1, d), lambda i: (0, 0))],
        out_specs=pl.BlockSpec((TMC, d), row),
        out_shape=jax.ShapeDtypeStruct((rows, d), _F32),
        compiler_params=_params("arbitrary"),
        name="moe_combine",
    )(yg, gate4, x1, mod, fg)


def _routing_tables(counts, n_blocks):
    cnt = counts[0, :N_EXPERTS].astype(jnp.int32)
    padded = (cnt + TME - 1) // TME * TME
    pad_end = jnp.cumsum(padded)
    pad_start = pad_end - padded
    ps = jnp.zeros((1, LANES), _F32).at[0, :N_EXPERTS].set(pad_start.astype(_F32))
    blk = jnp.arange(n_blocks, dtype=jnp.int32) * TME
    block_e = jnp.sum((pad_end[None, :] <= blk[:, None]).astype(jnp.int32), axis=1)
    e_last = jnp.max(jnp.where(cnt > 0, jnp.arange(N_EXPERTS, dtype=jnp.int32), 0))
    block_e = jnp.minimum(block_e, e_last).astype(jnp.int32)
    n_valid = (pad_end[-1] // TME).astype(jnp.int32).reshape(1)
    return ps, block_e, n_valid, pad_end.astype(jnp.int32), cnt


def kernel(x, c, ctx, c_ctx, ada_w, ada_b, norm1_g, norm2_g, w_in, attn_sink, w_fourier_out, w_attn_out,
           w_out, router_w, router_b, expert_w_gu, expert_b_gu, expert_w_down, expert_b_down, final_norm_g):
    n_batch, n, d = x.shape
    ctx_len = ctx.shape[1]
    depth = ada_w.shape[0]
    t_lat = n_batch * n
    t_ctx = n_batch * ctx_len
    tt = t_lat + t_ctx
    fd = qd = d // 2
    assert n % TM == 0 and t_ctx % TM == 0 and n % (BLOCK * FB_NB) == 0 and n_batch < MOD_ROWS
    assert t_lat % ctx_len == 0 and TM % ctx_len == 0

    xa = jnp.concatenate([x.reshape(t_lat, d), ctx.reshape(t_ctx, d)], axis=0)
    cc = jnp.concatenate([c, c_ctx[None, :], jnp.zeros((MOD_ROWS - n_batch - 1, d), _F32)], axis=0)
    mod = _ada(cc, ada_w, ada_b).reshape(depth, MOD_ROWS, N_MOD, d)
    cos_t, sin_t = _rope_tables(n)
    consts = _dft_consts(n, ctx_len)

    n_in = fd + qd + 2 * KV_DIM
    w_a = w_in[:, :, :n_in].astype(_BF)
    w_g = w_in[:, :, n_in:].astype(_BF)
    w_fo, w_ao, w_o = w_fourier_out.astype(_BF), w_attn_out.astype(_BF), w_out.astype(_BF)
    rw = jnp.pad(router_w, ((0, 0), (0, 0), (0, LANES - N_EXPERTS))).astype(_BF)
    rb = jnp.pad(router_b, ((0, 0), (0, LANES - N_EXPERTS)), constant_values=NEG_INF)[:, None, :]

    n_blocks = -(-(tt * TOP_K + N_EXPERTS * (TME - 1)) // TME)
    dims = dict(n=n, t_lat=t_lat, n_batch=n_batch)
    for l in range(depth):
        last = l == depth - 1
        f, q, k2, v2 = _proj(xa, mod[l], norm1_g[l][None, :], w_a[l], cos_t, sin_t, **dims)
        fm_lat = _fourier_lat(f, consts, n_batch=n_batch, n=n)
        fm_ctx = _fourier_ctx(f, consts, n_batch=n_batch, ctx_len=ctx_len, t_lat=t_lat)
        at_lat = _attn_lat(attn_sink[l], q, k2, v2, n_batch=n_batch, n=n, ctx_len=ctx_len, t_lat=t_lat)
        at_ctx = _attn_ctx(attn_sink[l], q, k2, v2, n_batch=n_batch, ctx_len=ctx_len, t_lat=t_lat)
        x1, hp, idx4, gate4, counts = _merge(
            xa, mod[l], norm1_g[l][None, :], norm2_g[l][None, :], w_g[l], fm_lat, fm_ctx, at_lat, at_ctx,
            w_fo[l], w_ao[l], w_o[l], rw[l], rb[l], **dims)
        pad_start, block_e, n_valid, pad_end, cnt = _routing_tables(counts, n_blocks)
        dest = _route(idx4, pad_start)[:, :TOP_K].T.reshape(-1)
        xs = _sc_scatter(hp, dest, _zero_tails(pad_end, cnt, n_valid, n_blocks * TME, hp.shape[1], hp.dtype))
        y = _moe(block_e, n_valid, xs, expert_w_gu, expert_b_gu, expert_w_down, expert_b_down, layer=l)
        yg = _sc_gather(y, dest).reshape(TOP_K, tt, d // 2)
        xa = _combine(yg, gate4, x1, mod[l], final_norm_g[None, :], n=n, n_batch=n_batch,
                      rows=t_lat if last else tt, final=last)
    return xa.reshape(n_batch, n, d)
```

```python
import functools

import numpy as np
import jax
import jax.numpy as jnp
from jax import lax
from jax.experimental import pallas as pl
from jax.experimental.pallas import tpu as pltpu
from jax.experimental.pallas import tpu_sc as plsc

GRID_W = 64
HEAD_DIM = 64
N_KV_HEADS = 2
Q_PER_KV = 4
KV_DIM = N_KV_HEADS * HEAD_DIM
WINDOW = 128
BLOCK = 128
ROPE_THETA = 10000.0
ROPE_FREQS = HEAD_DIM // 4
GROUP_DIM = 128
N_EXPERTS = 32
TOP_K = 4
SWIGLU_LIMIT = 7.0
SWIGLU_ALPHA = 1.702
N_MOD = 6
EPS = 1e-5
NEG_INF = -1e30
LOG2E = 1.4426950408889634
Q_SCALE = HEAD_DIM ** -0.5 * LOG2E

LANES = 128
MOD_ROWS = 8
TM = 1024
TME = 512
TMC = 512
SC_CORES = 2
SC_SUBCORES = 16
SC_WINDOW = 96
SC_SCATTER_WINDOW = 96
FA_NB = 16
FB_NB = 16
CAST_ROWS = 64
MERGE_COLS = 256
MERGE_ROW_SPLIT = 1
VMEM_LIMIT = 56 * 1024 * 1024

_BF = jnp.bfloat16
_F32 = jnp.float32


def _params(*sem):
    return pltpu.CompilerParams(dimension_semantics=sem, vmem_limit_bytes=VMEM_LIMIT)


def _dot(a, b):
    return jnp.dot(a, b, preferred_element_type=_F32)


def _sigmoid(x):
    return 0.5 * jnp.tanh(0.5 * x) + 0.5


def _norm_mod(x, g, shift, scale):
    ms = jnp.mean(x * x, axis=-1, keepdims=True)
    return (x * lax.rsqrt(ms + EPS) * g) * (1 + scale) + shift


def _ada_kernel(c_ref, w_ref, b_ref, o_ref):
    c = c_ref[...]
    s = c * jax.nn.sigmoid(c)
    o_ref[0] = jnp.dot(s, w_ref[0], precision=lax.Precision.HIGHEST,
                       preferred_element_type=_F32) + b_ref[0]


def _ada(cc, ada_w, ada_b):
    depth, d, nd = ada_w.shape
    tn = nd // 4
    return pl.pallas_call(
        _ada_kernel,
        grid=(depth, nd // tn),
        in_specs=[pl.BlockSpec((MOD_ROWS, d), lambda l, j: (0, 0)),
                  pl.BlockSpec((1, d, tn), lambda l, j: (l, 0, j)),
                  pl.BlockSpec((1, 1, tn), lambda l, j: (l, 0, j))],
        out_specs=pl.BlockSpec((1, MOD_ROWS, tn), lambda l, j: (l, 0, j)),
        out_shape=jax.ShapeDtypeStruct((depth, MOD_ROWS, nd), _F32),
        compiler_params=_params("arbitrary", "arbitrary"),
        name="ada_mod",
    )(cc, ada_w, ada_b.reshape(depth, 1, nd))


def _rope(v, cos, sin):
    lane = lax.broadcasted_iota(jnp.int32, v.shape, 1)
    partner = jnp.where((lane & ROPE_FREQS) == 0,
                        pltpu.roll(v, LANES - ROPE_FREQS, 1), pltpu.roll(v, ROPE_FREQS, 1))
    return v * cos + partner * sin


def _proj_kernel(x_ref, mod_ref, g_ref, w_ref, cos_ref, sin_ref, f_ref, q_ref, k_ref, vt_ref, *, fd, qd):
    m = mod_ref[0]
    h = _norm_mod(x_ref[...], g_ref[...], m[0:1], m[1:2]).astype(_BF)
    p = _dot(h, w_ref[...])
    cos = cos_ref[...]
    sin = sin_ref[...]
    half = BLOCK // 2
    for a in range(p.shape[0] // BLOCK):
        f_ref[:, a, :] = _pack_rows(p[a * BLOCK:a * BLOCK + half, :fd], p[a * BLOCK + half:(a + 1) * BLOCK, :fd])
    for j in range(qd // LANES):
        lo = fd + j * LANES
        q_ref[:, j * LANES:(j + 1) * LANES] = (
            _rope(p[:, lo:lo + LANES], cos, sin) * Q_SCALE).astype(_BF)
    k = _rope(p[:, fd + qd:fd + qd + KV_DIM], cos, sin)
    v = p[:, fd + qd + KV_DIM:fd + qd + 2 * KV_DIM]
    k_ref[:, :KV_DIM] = k.astype(_BF)
    k_ref[:, KV_DIM:] = pltpu.roll(k, HEAD_DIM, 1).astype(_BF)
    vt_ref[:KV_DIM, :] = v.T.astype(_BF)
    vt_ref[KV_DIM:, :] = pltpu.roll(v, HEAD_DIM, 1).T.astype(_BF)


def _proj(xa, mod, g, w_a, cos_t, sin_t, *, n, t_lat, n_batch):
    tt, d = xa.shape
    fd = qd = d // 2
    nt = tt // TM
    nxt = t_lat // TM
    per_seq = n // TM
    mod_idx = lambda i: (jnp.minimum(i * TM // n, n_batch), 0, 0)
    rope_idx = lambda i: (jnp.where(i < nxt, i % per_seq, per_seq), 0)
    f_idx = lambda i: (jnp.where(i < nxt, i // per_seq, n_batch), jnp.where(i < nxt, i % per_seq, 0), 0)
    row = lambda i: (i, 0)
    return pl.pallas_call(
        functools.partial(_proj_kernel, fd=fd, qd=qd),
        grid=(nt,),
        in_specs=[pl.BlockSpec((TM, d), row),
                  pl.BlockSpec((1, N_MOD, d), mod_idx),
                  pl.BlockSpec((1, d), lambda i: (0, 0)),
                  pl.BlockSpec(w_a.shape, lambda i: (0, 0)),
                  pl.BlockSpec((TM, LANES), rope_idx),
                  pl.BlockSpec((TM, LANES), rope_idx)],
        out_specs=[pl.BlockSpec((BLOCK // 2, TM // BLOCK, fd), f_idx), pl.BlockSpec((TM, qd), row),
                   pl.BlockSpec((TM, 2 * KV_DIM), row), pl.BlockSpec((2 * KV_DIM, TM), lambda i: (0, i))],
        out_shape=[jax.ShapeDtypeStruct(((n_batch + 1) * (BLOCK // 2), n // BLOCK, fd), jnp.uint32),
                   jax.ShapeDtypeStruct((tt, qd), _BF),
                   jax.ShapeDtypeStruct((tt, 2 * KV_DIM), _BF), jax.ShapeDtypeStruct((2 * KV_DIM, tt), _BF)],
        compiler_params=_params("arbitrary"),
        name="proj_in",
    )(xa, mod, g, w_a, cos_t, sin_t)


def _rope_tables(n):
    pos = jnp.arange(n)
    inv = ROPE_THETA ** (-jnp.arange(ROPE_FREQS, dtype=_F32) / ROPE_FREQS)
    ar = (pos // GRID_W).astype(_F32)[:, None] * inv
    ac = (pos % GRID_W).astype(_F32)[:, None] * inv
    cos = jnp.concatenate([jnp.cos(ar), jnp.cos(ar), jnp.cos(ac), jnp.cos(ac)], axis=1)
    sin = jnp.concatenate([-jnp.sin(ar), jnp.sin(ar), -jnp.sin(ac), jnp.sin(ac)], axis=1)
    cos = jnp.tile(cos, (1, LANES // HEAD_DIM))
    sin = jnp.tile(sin, (1, LANES // HEAD_DIM))
    cos = jnp.concatenate([cos, jnp.ones((TM, LANES), _F32)], axis=0)
    sin = jnp.concatenate([sin, jnp.zeros((TM, LANES), _F32)], axis=0)
    return cos, sin


def _dft_consts(n, ctx_len):
    n1 = n // BLOCK
    b = np.arange(BLOCK, dtype=np.int64)[:, None, None]
    k1 = np.arange(n1, dtype=np.int64)[None, :, None]
    a = np.arange(n1, dtype=np.int64)[None, None, :]
    ang = 2.0 * np.pi * ((a * k1 * BLOCK + b * k1) % n).astype(np.float64) / n
    m_ri = np.concatenate([np.cos(ang), -np.sin(ang)], axis=1)
    kk = np.arange(GROUP_DIM, dtype=np.int64)
    ang_c = 2.0 * np.pi * ((kk[:, None] * kk[None, :]) % GROUP_DIM) / GROUP_DIM
    c, s = np.cos(ang_c), np.sin(ang_c)
    cs = np.concatenate([np.concatenate([c, s], axis=1), np.concatenate([-s, c], axis=1)], axis=0)
    cc = np.concatenate([c, s], axis=0)
    kl = np.arange(ctx_len, dtype=np.int64)
    ang_l = 2.0 * np.pi * ((kl[:, None] * kl[None, :]) % ctx_len) / ctx_len
    wl = np.concatenate([np.cos(ang_l), -np.sin(ang_l)], axis=0)
    as_bf = lambda v: jnp.asarray(v, dtype=_F32).astype(_BF)
    return dict(m_ri=as_bf(m_ri), cs=as_bf(cs), cc=as_bf(cc), wl=as_bf(wl))


def _chan_dft(xr, xi, cc, scale):
    outs = []
    for g in range(xr.shape[1] // GROUP_DIM):
        sl = slice(g * GROUP_DIM, (g + 1) * GROUP_DIM)
        xx = jnp.concatenate([xr[:, sl], xi[:, sl]], axis=1).astype(_BF)
        outs.append(_dot(xx, cc) * scale)
    return jnp.concatenate(outs, axis=1)


def _pack_rows(lo, hi):
    a = lax.bitcast_convert_type(lo.astype(_BF).astype(_F32), jnp.uint32)
    b = lax.bitcast_convert_type(hi.astype(_BF).astype(_F32), jnp.uint32)
    return (a >> 16) | b


def _unpack_rows(p):
    lo = lax.bitcast_convert_type(p << 16, _F32).astype(_BF)
    hi = lax.bitcast_convert_type(p & jnp.uint32(0xFFFF0000), _F32).astype(_BF)
    return lo, hi


def _fa_kernel(f_ref, mlo_ref, mhi_ref, zr_ref, zi_ref):
    n1 = f_ref.shape[1]
    for t in range(FA_NB):
        lo, hi = _unpack_rows(f_ref[t])
        za = _dot(mlo_ref[t], lo)
        zb = _dot(mhi_ref[t], hi)
        zr_ref[:, t, :] = _pack_rows(za[:n1], zb[:n1])
        zi_ref[:, t, :] = _pack_rows(za[n1:], zb[n1:])


def _fb_kernel(zr_ref, zi_ref, cs_ref, cc_ref, o_ref, *, scale):
    cs, cc = cs_ref[...], cc_ref[...]
    for j in range(FB_NB):
        xx = _dot(cs, jnp.concatenate(_unpack_rows(zr_ref[j]) + _unpack_rows(zi_ref[j]), axis=0))
        o_ref[:, j, :] = _chan_dft(xx[:BLOCK], xx[BLOCK:], cc, scale).astype(_BF)


def _fourier_lat(f, consts, *, n_batch, n):
    fd = f.shape[2]
    n1 = n // BLOCK
    half = BLOCK // 2
    steps = half // FA_NB
    blk_f = pl.BlockSpec((FA_NB, n1, fd), lambda b, j: (b * steps + j, 0, 0))
    blk_a = pl.BlockSpec((n1, FA_NB, fd), lambda b, j: (b, j, 0))
    mat_lo = pl.BlockSpec((FA_NB, 2 * n1, n1), lambda b, j: (j, 0, 0))
    mat_hi = pl.BlockSpec((FA_NB, 2 * n1, n1), lambda b, j: (j + steps, 0, 0))
    z_shape = jax.ShapeDtypeStruct((n_batch * n1, half, fd), jnp.uint32)
    zr, zi = pl.pallas_call(
        _fa_kernel,
        grid=(n_batch, steps),
        in_specs=[blk_f, mat_lo, mat_hi],
        out_specs=[blk_a, blk_a],
        out_shape=[z_shape, z_shape],
        compiler_params=_params("arbitrary", "arbitrary"),
        name="fourier_seq_a",
    )(f, consts["m_ri"], consts["m_ri"])
    blk_z = pl.BlockSpec((FB_NB, half, fd), lambda b, j: (b * (n1 // FB_NB) + j, 0, 0))
    const = lambda shape: pl.BlockSpec(shape, lambda b, j: (0, 0))
    out = pl.pallas_call(
        functools.partial(_fb_kernel, scale=float((n * GROUP_DIM) ** -0.5)),
        grid=(n_batch, n1 // FB_NB),
        in_specs=[blk_z, blk_z, const((2 * BLOCK, 2 * BLOCK)), const((2 * GROUP_DIM, GROUP_DIM))],
        out_specs=pl.BlockSpec((BLOCK, FB_NB, fd), lambda b, j: (b, j, 0)),
        out_shape=jax.ShapeDtypeStruct((n_batch * BLOCK, n1, fd), _BF),
        compiler_params=_params("arbitrary", "arbitrary"),
        name="fourier_seq_b",
    )(zr, zi, consts["cs"], consts["cc"])
    return out.reshape(n_batch * n, fd)


def _fc_kernel(f_ref, wl_ref, cc_ref, o_ref, *, scale, ctx_len, n_batch):
    per = ctx_len // BLOCK
    for b in range(n_batch):
        rows = []
        for a in range(per):
            rows += list(_unpack_rows(f_ref[:, b * per + a, :]))
        xx = _dot(wl_ref[...], jnp.concatenate(rows, axis=0))
        o_ref[b * ctx_len:(b + 1) * ctx_len, :] = _chan_dft(xx[:ctx_len], xx[ctx_len:], cc_ref[...],
                                                           scale).astype(_BF)


def _fourier_ctx(f, consts, *, n_batch, ctx_len):
    fd = f.shape[2]
    blocks = n_batch * ctx_len // BLOCK
    const = lambda shape: pl.BlockSpec(shape, lambda i: (0, 0))
    return pl.pallas_call(
        functools.partial(_fc_kernel, scale=float((ctx_len * GROUP_DIM) ** -0.5), ctx_len=ctx_len, n_batch=n_batch),
        grid=(1,),
        in_specs=[pl.BlockSpec((BLOCK // 2, blocks, fd), lambda i: (n_batch, 0, 0)),
                  const((2 * ctx_len, ctx_len)), const((2 * GROUP_DIM, GROUP_DIM))],
        out_specs=pl.BlockSpec((n_batch * ctx_len, fd), lambda i: (0, 0)),
        out_shape=jax.ShapeDtypeStruct((n_batch * ctx_len, fd), _BF),
        compiler_params=_params("arbitrary"),
        name="fourier_ctx",
    )(f, consts["wl"], consts["cc"])


def _attn_core(sink_ref, q_ref, k, vt, o_ref, bias, s_scr, p_scr):
    tq = q_ref.shape[0]
    lane = lax.broadcasted_iota(jnp.int32, (tq, LANES), 1)
    low = lane < HEAD_DIM
    zero = jnp.zeros((tq, LANES), _BF)
    low_row = lax.broadcasted_iota(jnp.int32, (LANES, 2 * tq), 0) < HEAD_DIM
    first = lax.broadcasted_iota(jnp.int32, (1, 2 * tq), 1) < tq
    pairs = [(kvh, half) for kvh in range(N_KV_HEADS) for half in range(2)]
    sels = [slice(0, LANES) if half == kvh else slice(LANES, 2 * LANES) for kvh, half in pairs]
    for pi, (kvh, half) in enumerate(pairs):
        slabs = [q_ref[:, (2 * kvh + c) * LANES:(2 * kvh + c + 1) * LANES] for c in range(2)]
        keep = low if half == 0 else jnp.logical_not(low)
        qm = jnp.concatenate([jnp.where(keep, sl, zero) for sl in slabs], axis=0)
        s = lax.dot_general(k[:, sels[pi]], qm, (((1,), (1,)), ((), ())), preferred_element_type=_F32)
        s_scr[pi] = s if bias is None else s + bias
    dens = []
    for pi, (kvh, half) in enumerate(pairs):
        hq = Q_PER_KV * kvh + half
        sk = jnp.where(first, sink_ref[hq], sink_ref[hq + 2]) * LOG2E
        s = s_scr[pi]
        m = jnp.maximum(jnp.max(s, axis=0, keepdims=True), sk)
        p = jnp.exp2(s - m)
        dens.append(jnp.sum(p, axis=0, keepdims=True) + jnp.exp2(sk - m))
        p_scr[pi] = p.astype(_BF)
    outs = [_dot(vt[sels[pi], :], p_scr[pi]) * (1.0 / dens[pi]) for pi in range(len(pairs))]
    for kvh in range(N_KV_HEADS):
        ot = jnp.where(low_row, outs[2 * kvh], outs[2 * kvh + 1])
        for c in range(2):
            o_ref[:, (2 * kvh + c) * LANES:(2 * kvh + c + 1) * LANES] = ot[:, c * tq:(c + 1) * tq].T.astype(_BF)


def _attn_scratch(keys, tq):
    return [pltpu.VMEM((2 * N_KV_HEADS, keys, 2 * tq), _F32), pltpu.VMEM((2 * N_KV_HEADS, keys, 2 * tq), _BF)]


def _attn_lat_kernel(sink_ref, q_ref, kp_ref, kc_ref, kn_ref, vp_ref, vc_ref, vn_ref, kx_ref, vx_ref, bias_ref,
                     o_ref, s_scr, p_scr):
    k = jnp.concatenate([kp_ref[...], kc_ref[...], kn_ref[...], kx_ref[...]], axis=0)
    vt = jnp.concatenate([vp_ref[...], vc_ref[...], vn_ref[...], vx_ref[...]], axis=1)
    _attn_core(sink_ref, q_ref, k, vt, o_ref, bias_ref[0], s_scr, p_scr)


def _window_bias(ctx_len):
    s = np.arange(3 * BLOCK + ctx_len)[:, None]
    r = np.arange(2 * BLOCK)[None, :] % BLOCK
    out = []
    for v in range(4):
        lo = 0 if v & 1 else BLOCK
        hi = 3 * BLOCK if v & 2 else 2 * BLOCK
        ok = ((np.abs(s - BLOCK - r) <= WINDOW) & (s >= lo) & (s < hi)) | (s >= 3 * BLOCK)
        out.append(np.where(ok, 0.0, NEG_INF))
    return jnp.asarray(np.stack(out), dtype=_F32)


def _attn_ctx_kernel(sink_ref, q_ref, kx_ref, vx_ref, o_ref, s_scr, p_scr):
    _attn_core(sink_ref, q_ref, kx_ref[...], vx_ref[...], o_ref, None, s_scr, p_scr)


def _attn_lat(sink, q, k2, v2t, *, n_batch, n, ctx_len, t_lat):
    qd = q.shape[1]
    nb = n // BLOCK
    cur = lambda b, i: (b * nb + i, 0)
    prev = lambda b, i: (b * nb + jnp.maximum(i - 1, 0), 0)
    nxt = lambda b, i: (b * nb + jnp.minimum(i + 1, nb - 1), 0)
    cx = lambda b, i: (t_lat // ctx_len + b, 0)
    swap = lambda im: (lambda b, i: im(b, i)[::-1])
    kb = lambda im: pl.BlockSpec((BLOCK, 2 * KV_DIM), im)
    vb = lambda im: pl.BlockSpec((2 * KV_DIM, BLOCK), swap(im))
    bias = _window_bias(ctx_len)
    bias_idx = lambda b, i: ((i > 0).astype(jnp.int32) + 2 * (i < nb - 1).astype(jnp.int32), 0, 0)
    return pl.pallas_call(
        _attn_lat_kernel,
        grid=(n_batch, nb),
        in_specs=[pl.BlockSpec(memory_space=pltpu.SMEM), pl.BlockSpec((BLOCK, qd), cur),
                  kb(prev), kb(cur), kb(nxt), vb(prev), vb(cur), vb(nxt),
                  pl.BlockSpec((ctx_len, 2 * KV_DIM), cx), pl.BlockSpec((2 * KV_DIM, ctx_len), swap(cx)),
                  pl.BlockSpec((1,) + bias.shape[1:], bias_idx)],
        out_specs=pl.BlockSpec((BLOCK, qd), cur),
        out_shape=jax.ShapeDtypeStruct((t_lat, qd), _BF),
        scratch_shapes=_attn_scratch(3 * BLOCK + ctx_len, BLOCK),
        compiler_params=_params("arbitrary", "arbitrary"),
        name="attn_lat",
    )(sink, q, k2, k2, k2, v2t, v2t, v2t, k2, v2t, bias)


def _attn_ctx(sink, q, k2, v2t, *, n_batch, ctx_len, t_lat):
    qd = q.shape[1]
    cx = lambda b: (t_lat // ctx_len + b, 0)
    return pl.pallas_call(
        _attn_ctx_kernel,
        grid=(n_batch,),
        in_specs=[pl.BlockSpec(memory_space=pltpu.SMEM), pl.BlockSpec((ctx_len, qd), cx),
                  pl.BlockSpec((ctx_len, 2 * KV_DIM), cx),
                  pl.BlockSpec((2 * KV_DIM, ctx_len), lambda b: (0, t_lat // ctx_len + b))],
        out_specs=pl.BlockSpec((ctx_len, qd), lambda b: (b, 0)),
        out_shape=jax.ShapeDtypeStruct((n_batch * ctx_len, qd), _BF),
        scratch_shapes=_attn_scratch(ctx_len, ctx_len),
        compiler_params=_params("arbitrary"),
        name="attn_ctx",
    )(sink, q, k2, v2t)


def _pack_bf16_pairs(h):
    half = h.shape[1] // 2
    lo = lax.bitcast_convert_type(h[:, :half].astype(_BF).astype(_F32), jnp.uint32)
    hi = lax.bitcast_convert_type(h[:, half:].astype(_BF).astype(_F32), jnp.uint32)
    return (lo >> 16) | hi


def _unpack_bf16_pairs(p):
    lo = lax.bitcast_convert_type(p << 16, _F32)
    hi = lax.bitcast_convert_type(p & jnp.uint32(0xFFFF0000), _F32)
    return jnp.concatenate([lo, hi], axis=1).astype(_BF)


def _merge_kernel(x_ref, mod_ref, g1_ref, g2_ref, wg_ref, fml_ref, fmc_ref, atl_ref, atc_ref,
                  wfo_ref, wao_ref, wout_ref, rw_ref, rb_ref,
                  x1_ref, hp_ref, idx_ref, gate_ref, cnt_ref, h_scr, y_scr, *, nxt):
    i = pl.program_id(0)
    d = x_ref.shape[1]
    m = mod_ref[0]
    is_lat = i < nxt
    rows_per = x_ref.shape[0] // MERGE_ROW_SPLIT
    sel_sum = jnp.zeros((1, LANES), _F32)
    for grp in range(MERGE_ROW_SPLIT):
        rows = slice(grp * rows_per, (grp + 1) * rows_per)
        h_scr[rows, :] = _norm_mod(x_ref[rows, :], g1_ref[...], m[0:1], m[1:2]).astype(_BF)
        fm = jnp.where(is_lat, fml_ref[rows, :], fmc_ref[rows, :])
        at = jnp.where(is_lat, atl_ref[rows, :], atc_ref[rows, :])
        for c in range(d // MERGE_COLS):
            sl = slice(c * MERGE_COLS, (c + 1) * MERGE_COLS)
            sg = slice(d + c * MERGE_COLS, d + (c + 1) * MERGE_COLS)
            h = h_scr[rows, :]
            y = (_sigmoid(_dot(h, wg_ref[:, sl])) * _dot(fm, wfo_ref[:, sl])
                 + _sigmoid(_dot(h, wg_ref[:, sg])) * _dot(at, wao_ref[:, sl]))
            y_scr[rows, sl] = y.astype(_BF)
        x1_ref[rows, :] = x_ref[rows, :] + m[2:3] * _dot(y_scr[rows, :], wout_ref[...])
        h2 = _norm_mod(x1_ref[rows, :], g2_ref[...], m[3:4], m[4:5])
        hp_ref[rows, :] = _pack_bf16_pairs(h2)
        logits = _dot(h2.astype(_BF), rw_ref[...]) + rb_ref[...]
        lane = lax.broadcasted_iota(jnp.int32, logits.shape, 1).astype(_F32)
        vals, idxs = [], []
        for _ in range(TOP_K):
            mx = jnp.max(logits, axis=-1, keepdims=True)
            ix = jnp.min(jnp.where(logits == mx, lane, float(LANES)), axis=-1, keepdims=True)
            logits = jnp.where(lane == ix, NEG_INF, logits)
            vals.append(mx)
            idxs.append(ix)
        es = [jnp.exp(v - vals[0]) for v in vals]
        den = es[0] + es[1] + es[2] + es[3]
        idx4 = jnp.zeros(logits.shape, _F32)
        gate4 = jnp.zeros(logits.shape, _F32)
        sel = jnp.zeros(logits.shape, _F32)
        for k in range(TOP_K):
            idx4 = jnp.where(lane == float(k), idxs[k], idx4)
            gate4 = jnp.where(lane == float(k), es[k] / den, gate4)
            sel = sel + jnp.where(lane == idxs[k], 1.0, 0.0)
        idx_ref[rows, :] = idx4.astype(jnp.int32)
        gate_ref[rows, :] = gate4
        sel_sum = sel_sum + jnp.sum(sel, axis=0, keepdims=True)

    @pl.when(i == 0)
    def _():
        cnt_ref[...] = jnp.zeros(cnt_ref.shape, _F32)

    cnt_ref[0:1, :] += sel_sum


def _merge(xa, mod, g1, g2, w_g, fm_lat, fm_ctx, at_lat, at_ctx, w_fo, w_ao, w_out, rw, rb,
           *, n, t_lat, n_batch):
    tt, d = xa.shape
    fd = fm_lat.shape[1]
    nt = tt // TM
    nxt = t_lat // TM
    row = lambda i: (i, 0)
    lat = lambda i: (jnp.minimum(i, nxt - 1), 0)
    cxt = lambda i: (jnp.maximum(i - nxt, 0), 0)
    const = lambda a: pl.BlockSpec(a.shape, lambda i: (0, 0))
    mod_idx = lambda i: (jnp.minimum(i * TM // n, n_batch), 0, 0)
    return pl.pallas_call(
        functools.partial(_merge_kernel, nxt=nxt),
        grid=(nt,),
        in_specs=[pl.BlockSpec((TM, d), row), pl.BlockSpec((1, N_MOD, d), mod_idx),
                  const(g1), const(g2), const(w_g),
                  pl.BlockSpec((TM, fd), lat), pl.BlockSpec((TM, fd), cxt),
                  pl.BlockSpec((TM, fd), lat), pl.BlockSpec((TM, fd), cxt),
                  const(w_fo), const(w_ao), const(w_out), const(rw), const(rb)],
        out_specs=[pl.BlockSpec((TM, d), row), pl.BlockSpec((TM, d // 2), row),
                   pl.BlockSpec((TM, LANES), row), pl.BlockSpec((TM, LANES), row),
                   pl.BlockSpec((8, LANES), lambda i: (0, 0))],
        out_shape=[jax.ShapeDtypeStruct((tt, d), _F32), jax.ShapeDtypeStruct((tt, d // 2), jnp.uint32),
                   jax.ShapeDtypeStruct((tt, LANES), jnp.int32), jax.ShapeDtypeStruct((tt, LANES), _F32),
                   jax.ShapeDtypeStruct((8, LANES), _F32)],
        scratch_shapes=[pltpu.VMEM((TM, d), _BF), pltpu.VMEM((TM, d), _BF)],
        compiler_params=_params("arbitrary"),
        name="merge_router",
    )(xa, mod, g1, g2, w_g, fm_lat, fm_ctx, at_lat, at_ctx, w_fo, w_ao, w_out, rw, rb)


def _route_kernel(idx_ref, ps_ref, d_ref, carry):
    i = pl.program_id(0)

    @pl.when(i == 0)
    def _():
        carry[...] = ps_ref[...]

    idx4 = idx_ref[...]
    lane = lax.broadcasted_iota(jnp.int32, idx4.shape, 1)
    cols = [idx4[:, k:k + 1] for k in range(TOP_K)]
    sel = jnp.zeros(idx4.shape, _F32)
    for k in range(TOP_K):
        sel = sel + jnp.where(lane == cols[k], 1.0, 0.0)
    r = lax.broadcasted_iota(jnp.int32, (TM, TM), 0)
    c = lax.broadcasted_iota(jnp.int32, (TM, TM), 1)
    tri = jnp.where(r > c, 1.0, 0.0).astype(_BF)
    slot = carry[...] + _dot(tri, sel.astype(_BF))
    dest4 = jnp.zeros(idx4.shape, jnp.int32)
    for k in range(TOP_K):
        dk = jnp.sum(jnp.where(lane == cols[k], slot, 0.0), axis=-1, keepdims=True)
        dest4 = jnp.where(lane == k, dk.astype(jnp.int32), dest4)
    d_ref[...] = dest4
    carry[...] += jnp.sum(sel, axis=0, keepdims=True)


def _route(idx4, pad_start):
    tt = idx4.shape[0]
    return pl.pallas_call(
        _route_kernel,
        grid=(tt // TM,),
        in_specs=[pl.BlockSpec((TM, LANES), lambda i: (i, 0)), pl.BlockSpec((1, LANES), lambda i: (0, 0))],
        out_specs=pl.BlockSpec((TM, LANES), lambda i: (i, 0)),
        out_shape=jax.ShapeDtypeStruct((tt, LANES), jnp.int32),
        scratch_shapes=[pltpu.VMEM((1, LANES), _F32)],
        compiler_params=_params("arbitrary"),
        name="route_slots",
    )(idx4, pad_start)


def _zero_tails_kernel(pend_ref, cnt_ref, nv_ref, xs_ref, zbuf, sem):
    n_blocks = xs_ref.shape[0] // TME
    zbuf[...] = jnp.zeros(zbuf.shape, zbuf.dtype)

    def zero_block(start):
        return pltpu.make_async_copy(zbuf, xs_ref.at[pl.ds(pl.multiple_of(start, TME), TME)], sem)

    for e in range(N_EXPERTS):
        @pl.when(cnt_ref[e] > 0)
        def _():
            zero_block(pend_ref[e] - TME).start()

    def start_unused(j, carry):
        zero_block(j * TME).start()
        return carry

    def wait_unused(j, carry):
        zero_block(j * TME).wait()
        return carry

    lax.fori_loop(nv_ref[0], n_blocks, start_unused, 0)
    for e in range(N_EXPERTS):
        @pl.when(cnt_ref[e] > 0)
        def _():
            zero_block(pend_ref[e] - TME).wait()
    lax.fori_loop(nv_ref[0], n_blocks, wait_unused, 0)


def _zero_tails(pad_end, cnt, n_valid, n_slots, w, dtype):
    grid_spec = pltpu.PrefetchScalarGridSpec(
        num_scalar_prefetch=3,
        grid=(1,),
        in_specs=[],
        out_specs=pl.BlockSpec(memory_space=pl.ANY),
        scratch_shapes=[pltpu.VMEM((TME, w), dtype), pltpu.SemaphoreType.DMA],
    )
    return pl.pallas_call(
        _zero_tails_kernel,
        grid_spec=grid_spec,
        out_shape=jax.ShapeDtypeStruct((n_slots, w), dtype),
        compiler_params=_params("arbitrary"),
        name="moe_zero_tails",
    )(pad_end, cnt, n_valid)


def _sc_scatter(rows, idx, base):
    tt, w = rows.shape
    workers = SC_CORES * SC_SUBCORES
    per_worker = tt // workers
    assert tt % workers == 0 and per_worker % SC_SCATTER_WINDOW == 0
    mesh = plsc.VectorSubcoreMesh(core_axis_name="core", subcore_axis_name="subcore")
    out = jax.new_ref(base)

    @pl.kernel(out_type=(), mesh=mesh,
               scratch_types=[pltpu.VMEM((SC_SCATTER_WINDOW,), jnp.int32),
                              pltpu.VMEM((SC_SCATTER_WINDOW, w), rows.dtype), pltpu.SemaphoreType.DMA])
    def scatter(r_hbm, i_hbm, o_hbm, idx_v, rows_v, sem):
        wid = lax.axis_index("subcore") * SC_CORES + lax.axis_index("core")

        @pl.loop(0, per_worker // SC_SCATTER_WINDOW)
        def _(c):
            first = wid * per_worker + c * SC_SCATTER_WINDOW
            pltpu.sync_copy(r_hbm.at[pl.ds(first, SC_SCATTER_WINDOW)], rows_v)
            for k in range(TOP_K):
                pltpu.sync_copy(i_hbm.at[pl.ds(k * tt + first, SC_SCATTER_WINDOW)], idx_v)
                pltpu.async_copy(rows_v, o_hbm.at[idx_v], sem).wait()

    scatter(rows, idx, out)
    return jax.freeze(out)


def _moe_kernel(be_ref, nv_ref, xs_ref, wgu_ref, bgu_ref, wd_ref, bd_ref, y_ref, wgu_bf, wd_bf):
    i = pl.program_id(0)
    de = wd_ref.shape[2]
    valid = i < nv_ref[0]
    new_expert = jnp.logical_or(i == 0, be_ref[i] != be_ref[jnp.maximum(i - 1, 0)])

    @pl.when(jnp.logical_and(valid, new_expert))
    def _():
        def cast_rows(ref, out, rows):
            def body(j, carry):
                sl = pl.ds(pl.multiple_of(j * CAST_ROWS, CAST_ROWS), CAST_ROWS)
                out[sl, :] = ref[0, 0, sl, :].astype(_BF)
                return carry
            lax.fori_loop(0, rows // CAST_ROWS, body, 0)
        cast_rows(wgu_ref, wgu_bf, wgu_ref.shape[2])
        cast_rows(wd_ref, wd_bf, de)

    @pl.when(valid)
    def _():
        xb = _unpack_bf16_pairs(xs_ref[...])
        gu = _dot(xb, wgu_bf[...]) + bgu_ref[0, 0]
        a = jnp.minimum(gu[:, :de], SWIGLU_LIMIT)
        u = jnp.clip(gu[:, de:], -SWIGLU_LIMIT, SWIGLU_LIMIT)
        act = a * _sigmoid(SWIGLU_ALPHA * a) * (u + 1)
        y_ref[...] = _pack_bf16_pairs(_dot(act.astype(_BF), wd_bf[...]) + bd_ref[0, 0])

    @pl.when(i >= nv_ref[0])
    def _():
        y_ref[...] = jnp.zeros(y_ref.shape, y_ref.dtype)


def _moe(block_e, n_valid, xs, w_gu, b_gu, w_down, b_down, *, layer):
    n_slots, w = xs.shape
    depth, ne, d, de2 = w_gu.shape
    de = de2 // 2
    n_blocks = n_slots // TME
    ex = lambda i, be, nv: (layer, be[i], 0, 0)
    grid_spec = pltpu.PrefetchScalarGridSpec(
        num_scalar_prefetch=2,
        grid=(n_blocks,),
        in_specs=[pl.BlockSpec((TME, w), lambda i, be, nv: (jnp.minimum(i, nv[0] - 1), 0)),
                  pl.BlockSpec((1, 1, d, de2), ex), pl.BlockSpec((1, 1, 1, de2), ex),
                  pl.BlockSpec((1, 1, de, d), ex), pl.BlockSpec((1, 1, 1, d), ex)],
        out_specs=pl.BlockSpec((TME, d // 2), lambda i, be, nv: (i, 0)),
        scratch_shapes=[pltpu.VMEM((d, de2), _BF), pltpu.VMEM((de, d), _BF)],
    )
    return pl.pallas_call(
        _moe_kernel,
        grid_spec=grid_spec,
        out_shape=jax.ShapeDtypeStruct((n_slots, d // 2), jnp.uint32),
        compiler_params=_params("arbitrary"),
        name="moe_experts",
    )(block_e, n_valid, xs, w_gu, b_gu.reshape(depth, ne, 1, de2), w_down, b_down.reshape(depth, ne, 1, d))


def _sc_gather(table, idx):
    n_idx, (_, w) = idx.shape[0], table.shape
    workers = SC_CORES * SC_SUBCORES
    per_worker = n_idx // workers
    n_win = per_worker // SC_WINDOW
    assert n_idx % workers == 0 and per_worker % SC_WINDOW == 0 and n_win % 2 == 0
    mesh = plsc.VectorSubcoreMesh(core_axis_name="core", subcore_axis_name="subcore")
    idx_buf = pltpu.VMEM((SC_WINDOW,), jnp.int32)
    row_buf = pltpu.VMEM((SC_WINDOW, w), table.dtype)

    @pl.kernel(out_type=jax.ShapeDtypeStruct((n_idx, w), table.dtype), mesh=mesh,
               scratch_types=[idx_buf, idx_buf, row_buf, row_buf] + [pltpu.SemaphoreType.DMA] * 4)
    def gather(t_hbm, i_hbm, o_hbm, idx_a, idx_b, rows_a, rows_b, gsem_a, gsem_b, ssem_a, ssem_b):
        wid = lax.axis_index("subcore") * SC_CORES + lax.axis_index("core")
        first = wid * per_worker

        def gather_start(c, idx_v, rows_v, sem):
            pltpu.sync_copy(i_hbm.at[pl.ds(first + c * SC_WINDOW, SC_WINDOW)], idx_v)
            pltpu.make_async_copy(t_hbm.at[idx_v], rows_v, sem).start()

        def gather_wait(idx_v, rows_v, sem):
            pltpu.make_async_copy(t_hbm.at[idx_v], rows_v, sem).wait()

        def store_start(c, rows_v, sem):
            pltpu.make_async_copy(rows_v, o_hbm.at[pl.ds(first + c * SC_WINDOW, SC_WINDOW)], sem).start()

        def store_wait(rows_v, sem):
            pltpu.make_async_copy(rows_v, o_hbm.at[pl.ds(first, SC_WINDOW)], sem).wait()

        gather_start(0, idx_a, rows_a, gsem_a)

        @pl.loop(0, n_win // 2)
        def _(i):
            c = 2 * i

            @pl.when(i > 0)
            def _():
                store_wait(rows_b, ssem_b)

            gather_start(c + 1, idx_b, rows_b, gsem_b)
            gather_wait(idx_a, rows_a, gsem_a)
            store_start(c, rows_a, ssem_a)
            store_wait(rows_a, ssem_a)

            @pl.when(c + 2 < n_win)
            def _():
                gather_start(c + 2, idx_a, rows_a, gsem_a)

            gather_wait(idx_b, rows_b, gsem_b)
            store_start(c + 1, rows_b, ssem_b)

        store_wait(rows_b, ssem_b)

    return gather(table, idx)


def _combine_kernel(yg_ref, gate_ref, x_ref, mod_ref, fg_ref, o_ref, *, final):
    g = gate_ref[...]
    half = x_ref.shape[1] // 2
    lo = jnp.zeros((x_ref.shape[0], half), _F32)
    hi = jnp.zeros((x_ref.shape[0], half), _F32)
    for k in range(TOP_K):
        p = yg_ref[k]
        lo = lo + g[:, k:k + 1] * lax.bitcast_convert_type(p << 16, _F32)
        hi = hi + g[:, k:k + 1] * lax.bitcast_convert_type(p & jnp.uint32(0xFFFF0000), _F32)
    x2 = x_ref[...] + mod_ref[0][5:6] * jnp.concatenate([lo, hi], axis=1)
    if final:
        ms = jnp.mean(x2 * x2, axis=-1, keepdims=True)
        x2 = x2 * lax.rsqrt(ms + EPS) * fg_ref[...]
    o_ref[...] = x2


def _combine(yg, gate4, x1, mod, fg, *, n, n_batch, rows, final):
    d = x1.shape[1]
    row = lambda i: (i, 0)
    mod_idx = lambda i: (jnp.minimum(i * TMC // n, n_batch), 0, 0)
    return pl.pallas_call(
        functools.partial(_combine_kernel, final=final),
        grid=(rows // TMC,),
        in_specs=[pl.BlockSpec((TOP_K, TMC, d // 2), lambda i: (0, i, 0)),
                  pl.BlockSpec((TMC, LANES), row), pl.BlockSpec((TMC, d), row),
                  pl.BlockSpec((1, N_MOD, d), mod_idx), pl.BlockSpec((1, d), lambda i: (0, 0))],
        out_specs=pl.BlockSpec((TMC, d), row),
        out_shape=jax.ShapeDtypeStruct((rows, d), _F32),
        compiler_params=_params("arbitrary"),
        name="moe_combine",
    )(yg, gate4, x1, mod, fg)


def _routing_tables(counts, n_blocks):
    cnt = counts[0, :N_EXPERTS].astype(jnp.int32)
    padded = (cnt + TME - 1) // TME * TME
    pad_end = jnp.cumsum(padded)
    pad_start = pad_end - padded
    ps = jnp.zeros((1, LANES), _F32).at[0, :N_EXPERTS].set(pad_start.astype(_F32))
    blk = jnp.arange(n_blocks, dtype=jnp.int32) * TME
    block_e = jnp.sum((pad_end[None, :] <= blk[:, None]).astype(jnp.int32), axis=1)
    e_last = jnp.max(jnp.where(cnt > 0, jnp.arange(N_EXPERTS, dtype=jnp.int32), 0))
    block_e = jnp.minimum(block_e, e_last).astype(jnp.int32)
    n_valid = (pad_end[-1] // TME).astype(jnp.int32).reshape(1)
    return ps, block_e, n_valid, pad_end.astype(jnp.int32), cnt


def kernel(x, c, ctx, c_ctx, ada_w, ada_b, norm1_g, norm2_g, w_in, attn_sink, w_fourier_out, w_attn_out,
           w_out, router_w, router_b, expert_w_gu, expert_b_gu, expert_w_down, expert_b_down, final_norm_g):
    n_batch, n, d = x.shape
    ctx_len = ctx.shape[1]
    depth = ada_w.shape[0]
    t_lat = n_batch * n
    t_ctx = n_batch * ctx_len
    tt = t_lat + t_ctx
    fd = qd = d // 2
    assert n % TM == 0 and t_ctx == TM and n % (BLOCK * FB_NB) == 0 and n_batch < MOD_ROWS
    assert t_lat % ctx_len == 0 and TM % ctx_len == 0

    xa = jnp.concatenate([x.reshape(t_lat, d), ctx.reshape(t_ctx, d)], axis=0)
    cc = jnp.concatenate([c, c_ctx[None, :], jnp.zeros((MOD_ROWS - n_batch - 1, d), _F32)], axis=0)
    mod = _ada(cc, ada_w, ada_b).reshape(depth, MOD_ROWS, N_MOD, d)
    cos_t, sin_t = _rope_tables(n)
    consts = _dft_consts(n, ctx_len)

    n_in = fd + qd + 2 * KV_DIM
    w_a = w_in[:, :, :n_in].astype(_BF)
    w_g = w_in[:, :, n_in:].astype(_BF)
    w_fo, w_ao, w_o = w_fourier_out.astype(_BF), w_attn_out.astype(_BF), w_out.astype(_BF)
    rw = jnp.pad(router_w, ((0, 0), (0, 0), (0, LANES - N_EXPERTS))).astype(_BF)
    rb = jnp.pad(router_b, ((0, 0), (0, LANES - N_EXPERTS)), constant_values=NEG_INF)[:, None, :]

    n_blocks = -(-(tt * TOP_K + N_EXPERTS * (TME - 1)) // TME)
    dims = dict(n=n, t_lat=t_lat, n_batch=n_batch)
    for l in range(depth):
        last = l == depth - 1
        f, q, k2, v2 = _proj(xa, mod[l], norm1_g[l][None, :], w_a[l], cos_t, sin_t, **dims)
        fm_lat = _fourier_lat(f, consts, n_batch=n_batch, n=n)
        fm_ctx = _fourier_ctx(f, consts, n_batch=n_batch, ctx_len=ctx_len)
        at_lat = _attn_lat(attn_sink[l], q, k2, v2, n_batch=n_batch, n=n, ctx_len=ctx_len, t_lat=t_lat)
        at_ctx = _attn_ctx(attn_sink[l], q, k2, v2, n_batch=n_batch, ctx_len=ctx_len, t_lat=t_lat)
        x1, hp, idx4, gate4, counts = _merge(
            xa, mod[l], norm1_g[l][None, :], norm2_g[l][None, :], w_g[l], fm_lat, fm_ctx, at_lat, at_ctx,
            w_fo[l], w_ao[l], w_o[l], rw[l], rb[l], **dims)
        pad_start, block_e, n_valid, pad_end, cnt = _routing_tables(counts, n_blocks)
        dest = _route(idx4, pad_start)[:, :TOP_K].T.reshape(-1)
        xs = _sc_scatter(hp, dest, _zero_tails(pad_end, cnt, n_valid, n_blocks * TME, hp.shape[1], hp.dtype))
        y = _moe(block_e, n_valid, xs, expert_w_gu, expert_b_gu, expert_w_down, expert_b_down, layer=l)
        yg = _sc_gather(y, dest).reshape(TOP_K, tt, d // 2)
        xa = _combine(yg, gate4, x1, mod[l], final_norm_g[None, :], n=n, n_batch=n_batch,
                      rows=t_lat if last else tt, final=last)
    return xa.reshape(n_batch, n, d)
```

```python
import functools

import numpy as np
import jax
import jax.numpy as jnp
from jax import lax
from jax.experimental import pallas as pl
from jax.experimental.pallas import tpu as pltpu
from jax.experimental.pallas import tpu_sc as plsc

GRID_W = 64
HEAD_DIM = 64
N_KV_HEADS = 2
Q_PER_KV = 4
KV_DIM = N_KV_HEADS * HEAD_DIM
WINDOW = 128
BLOCK = 128
ROPE_THETA = 10000.0
ROPE_FREQS = HEAD_DIM // 4
GROUP_DIM = 128
N_EXPERTS = 32
TOP_K = 4
SWIGLU_LIMIT = 7.0
SWIGLU_ALPHA = 1.702
N_MOD = 6
EPS = 1e-5
NEG_INF = -1e30
LOG2E = 1.4426950408889634
Q_SCALE = HEAD_DIM ** -0.5 * LOG2E

LANES = 128
MOD_ROWS = 8
TM = 1024
TME = 512
TMC = 512
SC_CORES = 2
SC_SUBCORES = 16
SC_WINDOW = 96
SC_SCATTER_WINDOW = 88
FA_NB = 16
FB_NB = 16
CAST_ROWS = 64
MERGE_COLS = 256
VMEM_LIMIT = 56 * 1024 * 1024

_BF = jnp.bfloat16
_F32 = jnp.float32


def _params(*sem):
    return pltpu.CompilerParams(dimension_semantics=sem, vmem_limit_bytes=VMEM_LIMIT)


def _dot(a, b):
    return jnp.dot(a, b, preferred_element_type=_F32)


def _sigmoid(x):
    return 0.5 * jnp.tanh(0.5 * x) + 0.5


def _norm_mod(x, g, shift, scale):
    ms = jnp.mean(x * x, axis=-1, keepdims=True)
    return x * lax.rsqrt(ms + EPS) * (g * (1 + scale)) + shift


def _ada_kernel(c_ref, w_ref, b_ref, o_ref):
    c = c_ref[...]
    s = c * jax.nn.sigmoid(c)
    o_ref[0] = jnp.dot(s, w_ref[0], precision=lax.Precision.HIGHEST,
                       preferred_element_type=_F32) + b_ref[0]


def _ada(cc, ada_w, ada_b):
    depth, d, nd = ada_w.shape
    tn = nd // 4
    return pl.pallas_call(
        _ada_kernel,
        grid=(depth, nd // tn),
        in_specs=[pl.BlockSpec((MOD_ROWS, d), lambda l, j: (0, 0)),
                  pl.BlockSpec((1, d, tn), lambda l, j: (l, 0, j)),
                  pl.BlockSpec((1, 1, tn), lambda l, j: (l, 0, j))],
        out_specs=pl.BlockSpec((1, MOD_ROWS, tn), lambda l, j: (l, 0, j)),
        out_shape=jax.ShapeDtypeStruct((depth, MOD_ROWS, nd), _F32),
        compiler_params=_params("arbitrary", "arbitrary"),
        name="ada_mod",
    )(cc, ada_w, ada_b.reshape(depth, 1, nd))


def _rope(v, cos, sin):
    lane = lax.broadcasted_iota(jnp.int32, v.shape, 1)
    partner = jnp.where((lane & ROPE_FREQS) == 0,
                        pltpu.roll(v, LANES - ROPE_FREQS, 1), pltpu.roll(v, ROPE_FREQS, 1))
    return v * cos + partner * sin


def _proj_kernel(x_ref, mod_ref, g_ref, w_ref, cos_ref, sin_ref, f_ref, q_ref, k_ref, vt_ref, *, fd, qd):
    m = mod_ref[0]
    h = _norm_mod(x_ref[...], g_ref[...], m[0:1], m[1:2]).astype(_BF)
    p = _dot(h, w_ref[...])
    cos = cos_ref[...]
    sin = sin_ref[...]
    half = BLOCK // 2
    for a in range(p.shape[0] // BLOCK):
        f_ref[:, a, :] = _pack_rows(p[a * BLOCK:a * BLOCK + half, :fd], p[a * BLOCK + half:(a + 1) * BLOCK, :fd])
    for j in range(qd // LANES):
        lo = fd + j * LANES
        q_ref[:, j * LANES:(j + 1) * LANES] = (
            _rope(p[:, lo:lo + LANES], cos, sin) * Q_SCALE).astype(_BF)
    k = _rope(p[:, fd + qd:fd + qd + KV_DIM], cos, sin)
    v = p[:, fd + qd + KV_DIM:fd + qd + 2 * KV_DIM]
    k_ref[:, :KV_DIM] = k.astype(_BF)
    k_ref[:, KV_DIM:] = pltpu.roll(k, HEAD_DIM, 1).astype(_BF)
    vt_ref[:KV_DIM, :] = v.T.astype(_BF)
    vt_ref[KV_DIM:, :] = pltpu.roll(v, HEAD_DIM, 1).T.astype(_BF)


def _proj(xa, mod, g, w_a, cos_t, sin_t, *, n, t_lat, n_batch):
    tt, d = xa.shape
    fd = qd = d // 2
    nt = tt // TM
    nxt = t_lat // TM
    per_seq = n // TM
    mod_idx = lambda i: (jnp.minimum(i * TM // n, n_batch), 0, 0)
    rope_idx = lambda i: (jnp.where(i < nxt, i % per_seq, per_seq), 0)
    f_idx = lambda i: (jnp.where(i < nxt, i // per_seq, n_batch), jnp.where(i < nxt, i % per_seq, 0), 0)
    row = lambda i: (i, 0)
    return pl.pallas_call(
        functools.partial(_proj_kernel, fd=fd, qd=qd),
        grid=(nt,),
        in_specs=[pl.BlockSpec((TM, d), row),
                  pl.BlockSpec((1, N_MOD, d), mod_idx),
                  pl.BlockSpec((1, d), lambda i: (0, 0)),
                  pl.BlockSpec(w_a.shape, lambda i: (0, 0)),
                  pl.BlockSpec((TM, LANES), rope_idx),
                  pl.BlockSpec((TM, LANES), rope_idx)],
        out_specs=[pl.BlockSpec((BLOCK // 2, TM // BLOCK, fd), f_idx), pl.BlockSpec((TM, qd), row),
                   pl.BlockSpec((TM, 2 * KV_DIM), row), pl.BlockSpec((2 * KV_DIM, TM), lambda i: (0, i))],
        out_shape=[jax.ShapeDtypeStruct(((n_batch + 1) * (BLOCK // 2), n // BLOCK, fd), jnp.uint32),
                   jax.ShapeDtypeStruct((tt, qd), _BF),
                   jax.ShapeDtypeStruct((tt, 2 * KV_DIM), _BF), jax.ShapeDtypeStruct((2 * KV_DIM, tt), _BF)],
        compiler_params=_params("arbitrary"),
        name="proj_in",
    )(xa, mod, g, w_a, cos_t, sin_t)


def _rope_tables(n):
    pos = jnp.arange(n)
    inv = ROPE_THETA ** (-jnp.arange(ROPE_FREQS, dtype=_F32) / ROPE_FREQS)
    ar = (pos // GRID_W).astype(_F32)[:, None] * inv
    ac = (pos % GRID_W).astype(_F32)[:, None] * inv
    cos = jnp.concatenate([jnp.cos(ar), jnp.cos(ar), jnp.cos(ac), jnp.cos(ac)], axis=1)
    sin = jnp.concatenate([-jnp.sin(ar), jnp.sin(ar), -jnp.sin(ac), jnp.sin(ac)], axis=1)
    cos = jnp.tile(cos, (1, LANES // HEAD_DIM))
    sin = jnp.tile(sin, (1, LANES // HEAD_DIM))
    cos = jnp.concatenate([cos, jnp.ones((TM, LANES), _F32)], axis=0)
    sin = jnp.concatenate([sin, jnp.zeros((TM, LANES), _F32)], axis=0)
    return cos, sin


def _dft_consts(n, ctx_len):
    n1 = n // BLOCK
    b = np.arange(BLOCK, dtype=np.int64)[:, None, None]
    k1 = np.arange(n1, dtype=np.int64)[None, :, None]
    a = np.arange(n1, dtype=np.int64)[None, None, :]
    ang = 2.0 * np.pi * ((a * k1 * BLOCK + b * k1) % n).astype(np.float64) / n
    m_ri = np.concatenate([np.cos(ang), -np.sin(ang)], axis=1)
    kk = np.arange(GROUP_DIM, dtype=np.int64)
    ang_c = 2.0 * np.pi * ((kk[:, None] * kk[None, :]) % GROUP_DIM) / GROUP_DIM
    c, s = np.cos(ang_c), np.sin(ang_c)
    cs = np.concatenate([np.concatenate([c, s], axis=1), np.concatenate([-s, c], axis=1)], axis=0)
    cc = np.concatenate([c, s], axis=0)
    kl = np.arange(ctx_len, dtype=np.int64)
    ang_l = 2.0 * np.pi * ((kl[:, None] * kl[None, :]) % ctx_len) / ctx_len
    wl = np.concatenate([np.cos(ang_l), -np.sin(ang_l)], axis=0)
    as_bf = lambda v: jnp.asarray(v, dtype=_F32).astype(_BF)
    return dict(m_ri=as_bf(m_ri), cs=as_bf(cs), cc=as_bf(cc), wl=as_bf(wl))


def _chan_dft(xr, xi, cc, scale):
    outs = []
    for g in range(xr.shape[1] // GROUP_DIM):
        sl = slice(g * GROUP_DIM, (g + 1) * GROUP_DIM)
        xx = jnp.concatenate([xr[:, sl], xi[:, sl]], axis=1).astype(_BF)
        outs.append(_dot(xx, cc) * scale)
    return jnp.concatenate(outs, axis=1)


def _pack_rows(lo, hi):
    a = lax.bitcast_convert_type(lo.astype(_BF).astype(_F32), jnp.uint32)
    b = lax.bitcast_convert_type(hi.astype(_BF).astype(_F32), jnp.uint32)
    return (a >> 16) | b


def _unpack_rows(p):
    lo = lax.bitcast_convert_type(p << 16, _F32).astype(_BF)
    hi = lax.bitcast_convert_type(p & jnp.uint32(0xFFFF0000), _F32).astype(_BF)
    return lo, hi


def _fa_kernel(f_ref, mlo_ref, mhi_ref, zr_ref, zi_ref):
    n1 = f_ref.shape[1]
    for t in range(FA_NB):
        lo, hi = _unpack_rows(f_ref[t])
        za = _dot(mlo_ref[t], lo)
        zb = _dot(mhi_ref[t], hi)
        zr_ref[:, t, :] = _pack_rows(za[:n1], zb[:n1])
        zi_ref[:, t, :] = _pack_rows(za[n1:], zb[n1:])


def _fb_kernel(zr_ref, zi_ref, cs_ref, cc_ref, o_ref, *, scale):
    cs, cc = cs_ref[...], cc_ref[...]
    for j in range(FB_NB):
        xx = _dot(cs, jnp.concatenate(_unpack_rows(zr_ref[j]) + _unpack_rows(zi_ref[j]), axis=0))
        o_ref[:, j, :] = _chan_dft(xx[:BLOCK], xx[BLOCK:], cc, scale).astype(_BF)


def _fourier_lat(f, consts, *, n_batch, n):
    fd = f.shape[2]
    n1 = n // BLOCK
    half = BLOCK // 2
    steps = half // FA_NB
    blk_f = pl.BlockSpec((FA_NB, n1, fd), lambda b, j: (b * steps + j, 0, 0))
    blk_a = pl.BlockSpec((n1, FA_NB, fd), lambda b, j: (b, j, 0))
    mat_lo = pl.BlockSpec((FA_NB, 2 * n1, n1), lambda b, j: (j, 0, 0))
    mat_hi = pl.BlockSpec((FA_NB, 2 * n1, n1), lambda b, j: (j + steps, 0, 0))
    z_shape = jax.ShapeDtypeStruct((n_batch * n1, half, fd), jnp.uint32)
    zr, zi = pl.pallas_call(
        _fa_kernel,
        grid=(n_batch, steps),
        in_specs=[blk_f, mat_lo, mat_hi],
        out_specs=[blk_a, blk_a],
        out_shape=[z_shape, z_shape],
        compiler_params=_params("arbitrary", "arbitrary"),
        name="fourier_seq_a",
    )(f, consts["m_ri"], consts["m_ri"])
    blk_z = pl.BlockSpec((FB_NB, half, fd), lambda b, j: (b * (n1 // FB_NB) + j, 0, 0))
    const = lambda shape: pl.BlockSpec(shape, lambda b, j: (0, 0))
    out = pl.pallas_call(
        functools.partial(_fb_kernel, scale=float((n * GROUP_DIM) ** -0.5)),
        grid=(n_batch, n1 // FB_NB),
        in_specs=[blk_z, blk_z, const((2 * BLOCK, 2 * BLOCK)), const((2 * GROUP_DIM, GROUP_DIM))],
        out_specs=pl.BlockSpec((BLOCK, FB_NB, fd), lambda b, j: (b, j, 0)),
        out_shape=jax.ShapeDtypeStruct((n_batch * BLOCK, n1, fd), _BF),
        compiler_params=_params("arbitrary", "arbitrary"),
        name="fourier_seq_b",
    )(zr, zi, consts["cs"], consts["cc"])
    return out.reshape(n_batch * n, fd)


def _fc_kernel(f_ref, wl_ref, cc_ref, o_ref, *, scale, ctx_len, n_batch):
    per = ctx_len // BLOCK
    for b in range(n_batch):
        rows = []
        for a in range(per):
            rows += list(_unpack_rows(f_ref[:, b * per + a, :]))
        xx = _dot(wl_ref[...], jnp.concatenate(rows, axis=0))
        o_ref[b * ctx_len:(b + 1) * ctx_len, :] = _chan_dft(xx[:ctx_len], xx[ctx_len:], cc_ref[...],
                                                           scale).astype(_BF)


def _fourier_ctx(f, consts, *, n_batch, ctx_len):
    fd = f.shape[2]
    blocks = n_batch * ctx_len // BLOCK
    const = lambda shape: pl.BlockSpec(shape, lambda i: (0, 0))
    return pl.pallas_call(
        functools.partial(_fc_kernel, scale=float((ctx_len * GROUP_DIM) ** -0.5), ctx_len=ctx_len, n_batch=n_batch),
        grid=(1,),
        in_specs=[pl.BlockSpec((BLOCK // 2, blocks, fd), lambda i: (n_batch, 0, 0)),
                  const((2 * ctx_len, ctx_len)), const((2 * GROUP_DIM, GROUP_DIM))],
        out_specs=pl.BlockSpec((n_batch * ctx_len, fd), lambda i: (0, 0)),
        out_shape=jax.ShapeDtypeStruct((n_batch * ctx_len, fd), _BF),
        compiler_params=_params("arbitrary"),
        name="fourier_ctx",
    )(f, consts["wl"], consts["cc"])


def _attn_core(sink_ref, q_ref, k, vt, o_ref, bias, s_scr, p_scr):
    tq = q_ref.shape[0]
    lane = lax.broadcasted_iota(jnp.int32, (tq, LANES), 1)
    low = lane < HEAD_DIM
    zero = jnp.zeros((tq, LANES), _BF)
    low_row = lax.broadcasted_iota(jnp.int32, (LANES, 2 * tq), 0) < HEAD_DIM
    first = lax.broadcasted_iota(jnp.int32, (1, 2 * tq), 1) < tq
    pairs = [(kvh, half) for kvh in range(N_KV_HEADS) for half in range(2)]
    sels = [slice(0, LANES) if half == kvh else slice(LANES, 2 * LANES) for kvh, half in pairs]
    for pi, (kvh, half) in enumerate(pairs):
        slabs = [q_ref[:, (2 * kvh + c) * LANES:(2 * kvh + c + 1) * LANES] for c in range(2)]
        keep = low if half == 0 else jnp.logical_not(low)
        qm = jnp.concatenate([jnp.where(keep, sl, zero) for sl in slabs], axis=0)
        s = lax.dot_general(k[:, sels[pi]], qm, (((1,), (1,)), ((), ())), preferred_element_type=_F32)
        s_scr[pi] = s if bias is None else s + bias
    dens = []
    for pi, (kvh, half) in enumerate(pairs):
        hq = Q_PER_KV * kvh + half
        sk = jnp.where(first, sink_ref[hq], sink_ref[hq + 2]) * LOG2E
        s = s_scr[pi]
        m = jnp.maximum(jnp.max(s, axis=0, keepdims=True), sk)
        p = jnp.exp2(s - m)
        dens.append(jnp.sum(p, axis=0, keepdims=True) + jnp.exp2(sk - m))
        p_scr[pi] = p.astype(_BF)
    outs = [_dot(vt[sels[pi], :], p_scr[pi]) * (1.0 / dens[pi]) for pi in range(len(pairs))]
    for kvh in range(N_KV_HEADS):
        ot = jnp.where(low_row, outs[2 * kvh], outs[2 * kvh + 1])
        for c in range(2):
            o_ref[:, (2 * kvh + c) * LANES:(2 * kvh + c + 1) * LANES] = ot[:, c * tq:(c + 1) * tq].T.astype(_BF)


def _attn_scratch(keys, tq):
    return [pltpu.VMEM((2 * N_KV_HEADS, keys, 2 * tq), _F32), pltpu.VMEM((2 * N_KV_HEADS, keys, 2 * tq), _BF)]


def _attn_lat_kernel(sink_ref, q_ref, kp_ref, kc_ref, kn_ref, vp_ref, vc_ref, vn_ref, kx_ref, vx_ref, bias_ref,
                     o_ref, s_scr, p_scr):
    k = jnp.concatenate([kp_ref[...], kc_ref[...], kn_ref[...], kx_ref[...]], axis=0)
    vt = jnp.concatenate([vp_ref[...], vc_ref[...], vn_ref[...], vx_ref[...]], axis=1)
    _attn_core(sink_ref, q_ref, k, vt, o_ref, bias_ref[0], s_scr, p_scr)


def _window_bias(ctx_len):
    s = np.arange(3 * BLOCK + ctx_len)[:, None]
    r = np.arange(2 * BLOCK)[None, :] % BLOCK
    out = []
    for v in range(4):
        lo = 0 if v & 1 else BLOCK
        hi = 3 * BLOCK if v & 2 else 2 * BLOCK
        ok = ((np.abs(s - BLOCK - r) <= WINDOW) & (s >= lo) & (s < hi)) | (s >= 3 * BLOCK)
        out.append(np.where(ok, 0.0, NEG_INF))
    return jnp.asarray(np.stack(out), dtype=_F32)


def _attn_ctx_kernel(sink_ref, q_ref, kx_ref, vx_ref, o_ref, s_scr, p_scr):
    _attn_core(sink_ref, q_ref, kx_ref[...], vx_ref[...], o_ref, None, s_scr, p_scr)


def _attn_lat(sink, q, k2, v2t, *, n_batch, n, ctx_len, t_lat):
    qd = q.shape[1]
    nb = n // BLOCK
    cur = lambda b, i: (b * nb + i, 0)
    prev = lambda b, i: (b * nb + jnp.maximum(i - 1, 0), 0)
    nxt = lambda b, i: (b * nb + jnp.minimum(i + 1, nb - 1), 0)
    cx = lambda b, i: (t_lat // ctx_len + b, 0)
    swap = lambda im: (lambda b, i: im(b, i)[::-1])
    kb = lambda im: pl.BlockSpec((BLOCK, 2 * KV_DIM), im)
    vb = lambda im: pl.BlockSpec((2 * KV_DIM, BLOCK), swap(im))
    bias = _window_bias(ctx_len)
    bias_idx = lambda b, i: ((i > 0).astype(jnp.int32) + 2 * (i < nb - 1).astype(jnp.int32), 0, 0)
    return pl.pallas_call(
        _attn_lat_kernel,
        grid=(n_batch, nb),
        in_specs=[pl.BlockSpec(memory_space=pltpu.SMEM), pl.BlockSpec((BLOCK, qd), cur),
                  kb(prev), kb(cur), kb(nxt), vb(prev), vb(cur), vb(nxt),
                  pl.BlockSpec((ctx_len, 2 * KV_DIM), cx), pl.BlockSpec((2 * KV_DIM, ctx_len), swap(cx)),
                  pl.BlockSpec((1,) + bias.shape[1:], bias_idx)],
        out_specs=pl.BlockSpec((BLOCK, qd), cur),
        out_shape=jax.ShapeDtypeStruct((t_lat, qd), _BF),
        scratch_shapes=_attn_scratch(3 * BLOCK + ctx_len, BLOCK),
        compiler_params=_params("arbitrary", "arbitrary"),
        name="attn_lat",
    )(sink, q, k2, k2, k2, v2t, v2t, v2t, k2, v2t, bias)


def _attn_ctx(sink, q, k2, v2t, *, n_batch, ctx_len, t_lat):
    qd = q.shape[1]
    cx = lambda b: (t_lat // ctx_len + b, 0)
    return pl.pallas_call(
        _attn_ctx_kernel,
        grid=(n_batch,),
        in_specs=[pl.BlockSpec(memory_space=pltpu.SMEM), pl.BlockSpec((ctx_len, qd), cx),
                  pl.BlockSpec((ctx_len, 2 * KV_DIM), cx),
                  pl.BlockSpec((2 * KV_DIM, ctx_len), lambda b: (0, t_lat // ctx_len + b))],
        out_specs=pl.BlockSpec((ctx_len, qd), lambda b: (b, 0)),
        out_shape=jax.ShapeDtypeStruct((n_batch * ctx_len, qd), _BF),
        scratch_shapes=_attn_scratch(ctx_len, ctx_len),
        compiler_params=_params("arbitrary"),
        name="attn_ctx",
    )(sink, q, k2, v2t)


def _pack_bf16_pairs(h):
    half = h.shape[1] // 2
    lo = lax.bitcast_convert_type(h[:, :half].astype(_BF).astype(_F32), jnp.uint32)
    hi = lax.bitcast_convert_type(h[:, half:].astype(_BF).astype(_F32), jnp.uint32)
    return (lo >> 16) | hi


def _unpack_bf16_pairs(p):
    lo = lax.bitcast_convert_type(p << 16, _F32)
    hi = lax.bitcast_convert_type(p & jnp.uint32(0xFFFF0000), _F32)
    return jnp.concatenate([lo, hi], axis=1).astype(_BF)


def _merge_kernel(x_ref, mod_ref, g1_ref, g2_ref, wg_ref, fml_ref, fmc_ref, atl_ref, atc_ref,
                  wfo_ref, wao_ref, wout_ref, rw_ref, rb_ref,
                  x1_ref, hp_ref, idx_ref, gate_ref, cnt_ref, h_scr, y_scr, *, nxt):
    i = pl.program_id(0)
    d = x_ref.shape[1]
    m = mod_ref[0]
    is_lat = i < nxt
    h_scr[...] = _norm_mod(x_ref[...], g1_ref[...], m[0:1], m[1:2]).astype(_BF)
    fm = jnp.where(is_lat, fml_ref[...], fmc_ref[...])
    at = jnp.where(is_lat, atl_ref[...], atc_ref[...])
    for c in range(d // MERGE_COLS):
        sl = slice(c * MERGE_COLS, (c + 1) * MERGE_COLS)
        sg = slice(d + c * MERGE_COLS, d + (c + 1) * MERGE_COLS)
        h = h_scr[...]
        y = (_sigmoid(_dot(h, wg_ref[:, sl])) * _dot(fm, wfo_ref[:, sl])
             + _sigmoid(_dot(h, wg_ref[:, sg])) * _dot(at, wao_ref[:, sl]))
        y_scr[:, sl] = y.astype(_BF)
    x1_ref[...] = x_ref[...] + m[2:3] * _dot(y_scr[...], wout_ref[...])
    h2 = _norm_mod(x1_ref[...], g2_ref[...], m[3:4], m[4:5])
    hp_ref[...] = _pack_bf16_pairs(h2)
    logits = _dot(h2.astype(_BF), rw_ref[...]) + rb_ref[...]
    lane = lax.broadcasted_iota(jnp.int32, logits.shape, 1).astype(_F32)
    vals, idxs = [], []
    for _ in range(TOP_K):
        mx = jnp.max(logits, axis=-1, keepdims=True)
        ix = jnp.min(jnp.where(logits == mx, lane, float(LANES)), axis=-1, keepdims=True)
        logits = jnp.where(lane == ix, NEG_INF, logits)
        vals.append(mx)
        idxs.append(ix)
    es = [jnp.exp(v - vals[0]) for v in vals]
    den = es[0] + es[1] + es[2] + es[3]
    idx4 = jnp.zeros(logits.shape, _F32)
    gate4 = jnp.zeros(logits.shape, _F32)
    sel = jnp.zeros(logits.shape, _F32)
    for k in range(TOP_K):
        idx4 = jnp.where(lane == float(k), idxs[k], idx4)
        gate4 = jnp.where(lane == float(k), es[k] / den, gate4)
        sel = sel + jnp.where(lane == idxs[k], 1.0, 0.0)
    idx_ref[...] = idx4.astype(jnp.int32)
    gate_ref[...] = gate4

    @pl.when(i == 0)
    def _():
        cnt_ref[...] = jnp.zeros(cnt_ref.shape, _F32)

    cnt_ref[0:1, :] += jnp.sum(sel, axis=0, keepdims=True)


def _merge(xa, mod, g1, g2, w_g, fm_lat, fm_ctx, at_lat, at_ctx, w_fo, w_ao, w_out, rw, rb,
           *, n, t_lat, n_batch):
    tt, d = xa.shape
    fd = fm_lat.shape[1]
    nt = tt // TM
    nxt = t_lat // TM
    row = lambda i: (i, 0)
    lat = lambda i: (jnp.minimum(i, nxt - 1), 0)
    cxt = lambda i: (jnp.maximum(i - nxt, 0), 0)
    const = lambda a: pl.BlockSpec(a.shape, lambda i: (0, 0))
    mod_idx = lambda i: (jnp.minimum(i * TM // n, n_batch), 0, 0)
    return pl.pallas_call(
        functools.partial(_merge_kernel, nxt=nxt),
        grid=(nt,),
        in_specs=[pl.BlockSpec((TM, d), row), pl.BlockSpec((1, N_MOD, d), mod_idx),
                  const(g1), const(g2), const(w_g),
                  pl.BlockSpec((TM, fd), lat), pl.BlockSpec((TM, fd), cxt),
                  pl.BlockSpec((TM, fd), lat), pl.BlockSpec((TM, fd), cxt),
                  const(w_fo), const(w_ao), const(w_out), const(rw), const(rb)],
        out_specs=[pl.BlockSpec((TM, d), row), pl.BlockSpec((TM, d // 2), row),
                   pl.BlockSpec((TM, LANES), row), pl.BlockSpec((TM, LANES), row),
                   pl.BlockSpec((8, LANES), lambda i: (0, 0))],
        out_shape=[jax.ShapeDtypeStruct((tt, d), _F32), jax.ShapeDtypeStruct((tt, d // 2), jnp.uint32),
                   jax.ShapeDtypeStruct((tt, LANES), jnp.int32), jax.ShapeDtypeStruct((tt, LANES), _F32),
                   jax.ShapeDtypeStruct((8, LANES), _F32)],
        scratch_shapes=[pltpu.VMEM((TM, d), _BF), pltpu.VMEM((TM, d), _BF)],
        compiler_params=_params("arbitrary"),
        name="merge_router",
    )(xa, mod, g1, g2, w_g, fm_lat, fm_ctx, at_lat, at_ctx, w_fo, w_ao, w_out, rw, rb)


def _route_kernel(idx_ref, ps_ref, d_ref, carry):
    i = pl.program_id(0)

    @pl.when(i == 0)
    def _():
        carry[...] = ps_ref[...]

    idx4 = idx_ref[...]
    lane = lax.broadcasted_iota(jnp.int32, idx4.shape, 1)
    cols = [idx4[:, k:k + 1] for k in range(TOP_K)]
    sel = jnp.zeros(idx4.shape, _F32)
    for k in range(TOP_K):
        sel = sel + jnp.where(lane == cols[k], 1.0, 0.0)
    r = lax.broadcasted_iota(jnp.int32, (TM, TM), 0)
    c = lax.broadcasted_iota(jnp.int32, (TM, TM), 1)
    tri = jnp.where(r > c, 1.0, 0.0).astype(_BF)
    slot = carry[...] + _dot(tri, sel.astype(_BF))
    dest4 = jnp.zeros(idx4.shape, jnp.int32)
    for k in range(TOP_K):
        dk = jnp.sum(jnp.where(lane == cols[k], slot, 0.0), axis=-1, keepdims=True)
        dest4 = jnp.where(lane == k, dk.astype(jnp.int32), dest4)
    d_ref[...] = dest4
    carry[...] += jnp.sum(sel, axis=0, keepdims=True)


def _route(idx4, pad_start):
    tt = idx4.shape[0]
    return pl.pallas_call(
        _route_kernel,
        grid=(tt // TM,),
        in_specs=[pl.BlockSpec((TM, LANES), lambda i: (i, 0)), pl.BlockSpec((1, LANES), lambda i: (0, 0))],
        out_specs=pl.BlockSpec((TM, LANES), lambda i: (i, 0)),
        out_shape=jax.ShapeDtypeStruct((tt, LANES), jnp.int32),
        scratch_shapes=[pltpu.VMEM((1, LANES), _F32)],
        compiler_params=_params("arbitrary"),
        name="route_slots",
    )(idx4, pad_start)


def _zero_tails_kernel(pend_ref, cnt_ref, nv_ref, xs_ref, zbuf, sem):
    n_blocks = xs_ref.shape[0] // TME
    zbuf[...] = jnp.zeros(zbuf.shape, zbuf.dtype)

    def zero_block(start):
        return pltpu.make_async_copy(zbuf, xs_ref.at[pl.ds(pl.multiple_of(start, TME), TME)], sem)

    for e in range(N_EXPERTS):
        @pl.when(cnt_ref[e] > 0)
        def _():
            zero_block(pend_ref[e] - TME).start()

    def start_unused(j, carry):
        zero_block(j * TME).start()
        return carry

    def wait_unused(j, carry):
        zero_block(j * TME).wait()
        return carry

    lax.fori_loop(nv_ref[0], n_blocks, start_unused, 0)
    for e in range(N_EXPERTS):
        @pl.when(cnt_ref[e] > 0)
        def _():
            zero_block(pend_ref[e] - TME).wait()
    lax.fori_loop(nv_ref[0], n_blocks, wait_unused, 0)


def _zero_tails(pad_end, cnt, n_valid, n_slots, w, dtype):
    grid_spec = pltpu.PrefetchScalarGridSpec(
        num_scalar_prefetch=3,
        grid=(1,),
        in_specs=[],
        out_specs=pl.BlockSpec(memory_space=pl.ANY),
        scratch_shapes=[pltpu.VMEM((TME, w), dtype), pltpu.SemaphoreType.DMA],
    )
    return pl.pallas_call(
        _zero_tails_kernel,
        grid_spec=grid_spec,
        out_shape=jax.ShapeDtypeStruct((n_slots, w), dtype),
        compiler_params=_params("arbitrary"),
        name="moe_zero_tails",
    )(pad_end, cnt, n_valid)


def _sc_scatter(rows, idx, base):
    tt, w = rows.shape
    win = SC_SCATTER_WINDOW
    workers = SC_CORES * SC_SUBCORES
    per_worker = tt // workers
    n_win = per_worker // win
    assert tt % workers == 0 and per_worker % win == 0 and n_win % 2 == 0
    mesh = plsc.VectorSubcoreMesh(core_axis_name="core", subcore_axis_name="subcore")
    out = jax.new_ref(base)
    idx_buf = pltpu.VMEM((win,), jnp.int32)
    row_buf = pltpu.VMEM((win, w), rows.dtype)

    @pl.kernel(out_type=(), mesh=mesh,
               scratch_types=[idx_buf] * (2 * TOP_K) + [row_buf, row_buf] + [pltpu.SemaphoreType.DMA] * 4)
    def scatter(r_hbm, i_hbm, o_hbm, *scratch):
        idx_a, idx_b = scratch[:TOP_K], scratch[TOP_K:2 * TOP_K]
        rows_a, rows_b, lsem_a, lsem_b, ssem_a, ssem_b = scratch[2 * TOP_K:]
        wid = lax.axis_index("subcore") * SC_CORES + lax.axis_index("core")
        first = wid * per_worker

        def load_start(c, rows_v, sem):
            pltpu.make_async_copy(r_hbm.at[pl.ds(first + c * win, win)], rows_v, sem).start()

        def load_wait(rows_v, sem):
            pltpu.make_async_copy(r_hbm.at[pl.ds(first, win)], rows_v, sem).wait()

        def scatter_start(c, rows_v, idxs, sem):
            for k in range(TOP_K):
                pltpu.sync_copy(i_hbm.at[pl.ds(k * tt + first + c * win, win)], idxs[k])
                pltpu.make_async_copy(rows_v, o_hbm.at[idxs[k]], sem).start()

        def scatter_wait(rows_v, idxs, sem):
            for k in range(TOP_K):
                pltpu.make_async_copy(rows_v, o_hbm.at[idxs[k]], sem).wait()

        load_start(0, rows_a, lsem_a)

        @pl.loop(0, n_win // 2)
        def _(i):
            c = 2 * i

            @pl.when(i > 0)
            def _():
                scatter_wait(rows_b, idx_b, ssem_b)

            load_start(c + 1, rows_b, lsem_b)
            load_wait(rows_a, lsem_a)
            scatter_start(c, rows_a, idx_a, ssem_a)
            scatter_wait(rows_a, idx_a, ssem_a)

            @pl.when(c + 2 < n_win)
            def _():
                load_start(c + 2, rows_a, lsem_a)

            load_wait(rows_b, lsem_b)
            scatter_start(c + 1, rows_b, idx_b, ssem_b)

        scatter_wait(rows_b, idx_b, ssem_b)

    scatter(rows, idx, out)
    return jax.freeze(out)


def _moe_kernel(be_ref, nv_ref, xs_ref, wgu_ref, bgu_ref, wd_ref, bd_ref, y_ref, wgu_bf, wd_bf):
    i = pl.program_id(0)
    de = wd_ref.shape[2]
    valid = i < nv_ref[0]
    new_expert = jnp.logical_or(i == 0, be_ref[i] != be_ref[jnp.maximum(i - 1, 0)])

    @pl.when(jnp.logical_and(valid, new_expert))
    def _():
        def cast_rows(ref, out, rows):
            def body(j, carry):
                sl = pl.ds(pl.multiple_of(j * CAST_ROWS, CAST_ROWS), CAST_ROWS)
                out[sl, :] = ref[0, 0, sl, :].astype(_BF)
                return carry
            lax.fori_loop(0, rows // CAST_ROWS, body, 0)
        cast_rows(wgu_ref, wgu_bf, wgu_ref.shape[2])
        cast_rows(wd_ref, wd_bf, de)

    @pl.when(valid)
    def _():
        xb = _unpack_bf16_pairs(xs_ref[...])
        gu = _dot(xb, wgu_bf[...]) + bgu_ref[0, 0]
        a = jnp.minimum(gu[:, :de], SWIGLU_LIMIT)
        u = jnp.clip(gu[:, de:], -SWIGLU_LIMIT, SWIGLU_LIMIT)
        act = a * _sigmoid(SWIGLU_ALPHA * a) * (u + 1)
        y_ref[...] = _pack_bf16_pairs(_dot(act.astype(_BF), wd_bf[...]) + bd_ref[0, 0])

    @pl.when(i >= nv_ref[0])
    def _():
        y_ref[...] = jnp.zeros(y_ref.shape, y_ref.dtype)


def _moe(block_e, n_valid, xs, w_gu, b_gu, w_down, b_down, *, layer):
    n_slots, w = xs.shape
    depth, ne, d, de2 = w_gu.shape
    de = de2 // 2
    n_blocks = n_slots // TME
    ex = lambda i, be, nv: (layer, be[i], 0, 0)
    grid_spec = pltpu.PrefetchScalarGridSpec(
        num_scalar_prefetch=2,
        grid=(n_blocks,),
        in_specs=[pl.BlockSpec((TME, w), lambda i, be, nv: (jnp.minimum(i, nv[0] - 1), 0)),
                  pl.BlockSpec((1, 1, d, de2), ex), pl.BlockSpec((1, 1, 1, de2), ex),
                  pl.BlockSpec((1, 1, de, d), ex), pl.BlockSpec((1, 1, 1, d), ex)],
        out_specs=pl.BlockSpec((TME, d // 2), lambda i, be, nv: (i, 0)),
        scratch_shapes=[pltpu.VMEM((d, de2), _BF), pltpu.VMEM((de, d), _BF)],
    )
    return pl.pallas_call(
        _moe_kernel,
        grid_spec=grid_spec,
        out_shape=jax.ShapeDtypeStruct((n_slots, d // 2), jnp.uint32),
        compiler_params=_params("arbitrary"),
        name="moe_experts",
    )(block_e, n_valid, xs, w_gu, b_gu.reshape(depth, ne, 1, de2), w_down, b_down.reshape(depth, ne, 1, d))


def _sc_gather(table, idx):
    n_idx, (_, w) = idx.shape[0], table.shape
    workers = SC_CORES * SC_SUBCORES
    per_worker = n_idx // workers
    n_win = per_worker // SC_WINDOW
    assert n_idx % workers == 0 and per_worker % SC_WINDOW == 0 and n_win % 2 == 0
    mesh = plsc.VectorSubcoreMesh(core_axis_name="core", subcore_axis_name="subcore")
    idx_buf = pltpu.VMEM((SC_WINDOW,), jnp.int32)
    row_buf = pltpu.VMEM((SC_WINDOW, w), table.dtype)

    @pl.kernel(out_type=jax.ShapeDtypeStruct((n_idx, w), table.dtype), mesh=mesh,
               scratch_types=[idx_buf, idx_buf, row_buf, row_buf] + [pltpu.SemaphoreType.DMA] * 4)
    def gather(t_hbm, i_hbm, o_hbm, idx_a, idx_b, rows_a, rows_b, gsem_a, gsem_b, ssem_a, ssem_b):
        wid = lax.axis_index("subcore") * SC_CORES + lax.axis_index("core")
        first = wid * per_worker

        def gather_start(c, idx_v, rows_v, sem):
            pltpu.sync_copy(i_hbm.at[pl.ds(first + c * SC_WINDOW, SC_WINDOW)], idx_v)
            pltpu.make_async_copy(t_hbm.at[idx_v], rows_v, sem).start()

        def gather_wait(idx_v, rows_v, sem):
            pltpu.make_async_copy(t_hbm.at[idx_v], rows_v, sem).wait()

        def store_start(c, rows_v, sem):
            pltpu.make_async_copy(rows_v, o_hbm.at[pl.ds(first + c * SC_WINDOW, SC_WINDOW)], sem).start()

        def store_wait(rows_v, sem):
            pltpu.make_async_copy(rows_v, o_hbm.at[pl.ds(first, SC_WINDOW)], sem).wait()

        gather_start(0, idx_a, rows_a, gsem_a)

        @pl.loop(0, n_win // 2)
        def _(i):
            c = 2 * i

            @pl.when(i > 0)
            def _():
                store_wait(rows_b, ssem_b)

            gather_start(c + 1, idx_b, rows_b, gsem_b)
            gather_wait(idx_a, rows_a, gsem_a)
            store_start(c, rows_a, ssem_a)
            store_wait(rows_a, ssem_a)

            @pl.when(c + 2 < n_win)
            def _():
                gather_start(c + 2, idx_a, rows_a, gsem_a)

            gather_wait(idx_b, rows_b, gsem_b)
            store_start(c + 1, rows_b, ssem_b)

        store_wait(rows_b, ssem_b)

    return gather(table, idx)


def _combine_kernel(yg_ref, gate_ref, x_ref, mod_ref, fg_ref, o_ref, *, final):
    g = gate_ref[...]
    half = x_ref.shape[1] // 2
    lo = jnp.zeros((x_ref.shape[0], half), _F32)
    hi = jnp.zeros((x_ref.shape[0], half), _F32)
    for k in range(TOP_K):
        p = yg_ref[k]
        lo = lo + g[:, k:k + 1] * lax.bitcast_convert_type(p << 16, _F32)
        hi = hi + g[:, k:k + 1] * lax.bitcast_convert_type(p & jnp.uint32(0xFFFF0000), _F32)
    x2 = x_ref[...] + mod_ref[0][5:6] * jnp.concatenate([lo, hi], axis=1)
    if final:
        ms = jnp.mean(x2 * x2, axis=-1, keepdims=True)
        x2 = x2 * lax.rsqrt(ms + EPS) * fg_ref[...]
    o_ref[...] = x2


def _combine(yg, gate4, x1, mod, fg, *, n, n_batch, rows, final):
    d = x1.shape[1]
    row = lambda i: (i, 0)
    mod_idx = lambda i: (jnp.minimum(i * TMC // n, n_batch), 0, 0)
    return pl.pallas_call(
        functools.partial(_combine_kernel, final=final),
        grid=(rows // TMC,),
        in_specs=[pl.BlockSpec((TOP_K, TMC, d // 2), lambda i: (0, i, 0)),
                  pl.BlockSpec((TMC, LANES), row), pl.BlockSpec((TMC, d), row),
                  pl.BlockSpec((1, N_MOD, d), mod_idx), pl.BlockSpec((1, d), lambda i: (0, 0))],
        out_specs=pl.BlockSpec((TMC, d), row),
        out_shape=jax.ShapeDtypeStruct((rows, d), _F32),
        compiler_params=_params("arbitrary"),
        name="moe_combine",
    )(yg, gate4, x1, mod, fg)


def _routing_tables(counts, n_blocks):
    cnt = counts[0, :N_EXPERTS].astype(jnp.int32)
    padded = (cnt + TME - 1) // TME * TME
    pad_end = jnp.cumsum(padded)
    pad_start = pad_end - padded
    ps = jnp.zeros((1, LANES), _F32).at[0, :N_EXPERTS].set(pad_start.astype(_F32))
    blk = jnp.arange(n_blocks, dtype=jnp.int32) * TME
    block_e = jnp.sum((pad_end[None, :] <= blk[:, None]).astype(jnp.int32), axis=1)
    e_last = jnp.max(jnp.where(cnt > 0, jnp.arange(N_EXPERTS, dtype=jnp.int32), 0))
    block_e = jnp.minimum(block_e, e_last).astype(jnp.int32)
    n_valid = (pad_end[-1] // TME).astype(jnp.int32).reshape(1)
    return ps, block_e, n_valid, pad_end.astype(jnp.int32), cnt


def kernel(x, c, ctx, c_ctx, ada_w, ada_b, norm1_g, norm2_g, w_in, attn_sink, w_fourier_out, w_attn_out,
           w_out, router_w, router_b, expert_w_gu, expert_b_gu, expert_w_down, expert_b_down, final_norm_g):
    n_batch, n, d = x.shape
    ctx_len = ctx.shape[1]
    depth = ada_w.shape[0]
    t_lat = n_batch * n
    t_ctx = n_batch * ctx_len
    tt = t_lat + t_ctx
    fd = qd = d // 2
    assert n % TM == 0 and t_ctx == TM and n % (BLOCK * FB_NB) == 0 and n_batch < MOD_ROWS
    assert t_lat % ctx_len == 0 and TM % ctx_len == 0

    xa = jnp.concatenate([x.reshape(t_lat, d), ctx.reshape(t_ctx, d)], axis=0)
    cc = jnp.concatenate([c, c_ctx[None, :], jnp.zeros((MOD_ROWS - n_batch - 1, d), _F32)], axis=0)
    mod = _ada(cc, ada_w, ada_b).reshape(depth, MOD_ROWS, N_MOD, d)
    cos_t, sin_t = _rope_tables(n)
    consts = _dft_consts(n, ctx_len)

    n_in = fd + qd + 2 * KV_DIM
    w_a = w_in[:, :, :n_in].astype(_BF)
    w_g = w_in[:, :, n_in:].astype(_BF)
    w_fo, w_ao, w_o = w_fourier_out.astype(_BF), w_attn_out.astype(_BF), w_out.astype(_BF)
    rw = jnp.pad(router_w, ((0, 0), (0, 0), (0, LANES - N_EXPERTS))).astype(_BF)
    rb = jnp.pad(router_b, ((0, 0), (0, LANES - N_EXPERTS)), constant_values=NEG_INF)[:, None, :]

    n_blocks = -(-(tt * TOP_K + N_EXPERTS * (TME - 1)) // TME)
    dims = dict(n=n, t_lat=t_lat, n_batch=n_batch)
    for l in range(depth):
        last = l == depth - 1
        f, q, k2, v2 = _proj(xa, mod[l], norm1_g[l][None, :], w_a[l], cos_t, sin_t, **dims)
        fm_lat = _fourier_lat(f, consts, n_batch=n_batch, n=n)
        fm_ctx = _fourier_ctx(f, consts, n_batch=n_batch, ctx_len=ctx_len)
        at_lat = _attn_lat(attn_sink[l], q, k2, v2, n_batch=n_batch, n=n, ctx_len=ctx_len, t_lat=t_lat)
        at_ctx = _attn_ctx(attn_sink[l], q, k2, v2, n_batch=n_batch, ctx_len=ctx_len, t_lat=t_lat)
        x1, hp, idx4, gate4, counts = _merge(
            xa, mod[l], norm1_g[l][None, :], norm2_g[l][None, :], w_g[l], fm_lat, fm_ctx, at_lat, at_ctx,
            w_fo[l], w_ao[l], w_o[l], rw[l], rb[l], **dims)
        pad_start, block_e, n_valid, pad_end, cnt = _routing_tables(counts, n_blocks)
        dest = _route(idx4, pad_start)[:, :TOP_K].T.reshape(-1)
        xs = _sc_scatter(hp, dest, _zero_tails(pad_end, cnt, n_valid, n_blocks * TME, hp.shape[1], hp.dtype))
        y = _moe(block_e, n_valid, xs, expert_w_gu, expert_b_gu, expert_w_down, expert_b_down, layer=l)
        yg = _sc_gather(y, dest).reshape(TOP_K, tt, d // 2)
        xa = _combine(yg, gate4, x1, mod[l], final_norm_g[None, :], n=n, n_batch=n_batch,
                      rows=t_lat if last else tt, final=last)
    return xa.reshape(n_batch, n, d)
```

```python
import functools

import numpy as np
import jax
import jax.numpy as jnp
from jax import lax
from jax.experimental import pallas as pl
from jax.experimental.pallas import tpu as pltpu
from jax.experimental.pallas import tpu_sc as plsc

GRID_W = 64
HEAD_DIM = 64
N_KV_HEADS = 2
Q_PER_KV = 4
KV_DIM = N_KV_HEADS * HEAD_DIM
WINDOW = 128
BLOCK = 128
ROPE_THETA = 10000.0
ROPE_FREQS = HEAD_DIM // 4
GROUP_DIM = 128
N_EXPERTS = 32
TOP_K = 4
SWIGLU_LIMIT = 7.0
SWIGLU_ALPHA = 1.702
N_MOD = 6
EPS = 1e-5
NEG_INF = -1e30
LOG2E = 1.4426950408889634
Q_SCALE = HEAD_DIM ** -0.5 * LOG2E

LANES = 128
MOD_ROWS = 8
TM = 1024
TME = 512
TMC = 512
SC_CORES = 2
SC_SUBCORES = 16
SC_WINDOW = 96
SC_SPLIT = 2
SC_SCATTER_WINDOW = 88
FA_NB = 16
FB_NB = 16
CAST_ROWS = 64
MERGE_COLS = 256
VMEM_LIMIT = 56 * 1024 * 1024

_BF = jnp.bfloat16
_F32 = jnp.float32


def _params(*sem):
    return pltpu.CompilerParams(dimension_semantics=sem, vmem_limit_bytes=VMEM_LIMIT)


def _dot(a, b):
    return jnp.dot(a, b, preferred_element_type=_F32)


def _sigmoid(x):
    return 0.5 * jnp.tanh(0.5 * x) + 0.5


def _norm_mod(x, g, shift, scale):
    ms = jnp.mean(x * x, axis=-1, keepdims=True)
    return x * lax.rsqrt(ms + EPS) * (g * (1 + scale)) + shift


def _ada_kernel(c_ref, w_ref, b_ref, o_ref):
    c = c_ref[...]
    s = c * jax.nn.sigmoid(c)
    o_ref[0] = jnp.dot(s, w_ref[0], precision=lax.Precision.HIGHEST,
                       preferred_element_type=_F32) + b_ref[0]


def _ada(cc, ada_w, ada_b):
    depth, d, nd = ada_w.shape
    tn = nd // 4
    return pl.pallas_call(
        _ada_kernel,
        grid=(depth, nd // tn),
        in_specs=[pl.BlockSpec((MOD_ROWS, d), lambda l, j: (0, 0)),
                  pl.BlockSpec((1, d, tn), lambda l, j: (l, 0, j)),
                  pl.BlockSpec((1, 1, tn), lambda l, j: (l, 0, j))],
        out_specs=pl.BlockSpec((1, MOD_ROWS, tn), lambda l, j: (l, 0, j)),
        out_shape=jax.ShapeDtypeStruct((depth, MOD_ROWS, nd), _F32),
        compiler_params=_params("arbitrary", "arbitrary"),
        name="ada_mod",
    )(cc, ada_w, ada_b.reshape(depth, 1, nd))


def _rope(v, cos, sin):
    lane = lax.broadcasted_iota(jnp.int32, v.shape, 1)
    partner = jnp.where((lane & ROPE_FREQS) == 0,
                        pltpu.roll(v, LANES - ROPE_FREQS, 1), pltpu.roll(v, ROPE_FREQS, 1))
    return v * cos + partner * sin


def _proj_kernel(x_ref, mod_ref, g_ref, w_ref, cos_ref, sin_ref, f_ref, q_ref, k_ref, vt_ref, *, fd, qd):
    m = mod_ref[0]
    h = _norm_mod(x_ref[...], g_ref[...], m[0:1], m[1:2]).astype(_BF)
    p = _dot(h, w_ref[...])
    cos = cos_ref[...]
    sin = sin_ref[...]
    half = BLOCK // 2
    for a in range(p.shape[0] // BLOCK):
        f_ref[:, a, :] = _pack_rows(p[a * BLOCK:a * BLOCK + half, :fd], p[a * BLOCK + half:(a + 1) * BLOCK, :fd])
    for j in range(qd // LANES):
        lo = fd + j * LANES
        q_ref[:, j * LANES:(j + 1) * LANES] = (
            _rope(p[:, lo:lo + LANES], cos, sin) * Q_SCALE).astype(_BF)
    k = _rope(p[:, fd + qd:fd + qd + KV_DIM], cos, sin)
    v = p[:, fd + qd + KV_DIM:fd + qd + 2 * KV_DIM]
    k_ref[:, :KV_DIM] = k.astype(_BF)
    k_ref[:, KV_DIM:] = pltpu.roll(k, HEAD_DIM, 1).astype(_BF)
    vt_ref[:KV_DIM, :] = v.T.astype(_BF)
    vt_ref[KV_DIM:, :] = pltpu.roll(v, HEAD_DIM, 1).T.astype(_BF)


def _proj(xa, mod, g, w_a, cos_t, sin_t, *, n, t_lat, n_batch):
    tt, d = xa.shape
    fd = qd = d // 2
    nt = tt // TM
    nxt = t_lat // TM
    per_seq = n // TM
    mod_idx = lambda i: (jnp.minimum(i * TM // n, n_batch), 0, 0)
    rope_idx = lambda i: (jnp.where(i < nxt, i % per_seq, per_seq), 0)
    f_idx = lambda i: (jnp.where(i < nxt, i // per_seq, n_batch), jnp.where(i < nxt, i % per_seq, 0), 0)
    row = lambda i: (i, 0)
    return pl.pallas_call(
        functools.partial(_proj_kernel, fd=fd, qd=qd),
        grid=(nt,),
        in_specs=[pl.BlockSpec((TM, d), row),
                  pl.BlockSpec((1, N_MOD, d), mod_idx),
                  pl.BlockSpec((1, d), lambda i: (0, 0)),
                  pl.BlockSpec(w_a.shape, lambda i: (0, 0)),
                  pl.BlockSpec((TM, LANES), rope_idx),
                  pl.BlockSpec((TM, LANES), rope_idx)],
        out_specs=[pl.BlockSpec((BLOCK // 2, TM // BLOCK, fd), f_idx), pl.BlockSpec((TM, qd), row),
                   pl.BlockSpec((TM, 2 * KV_DIM), row), pl.BlockSpec((2 * KV_DIM, TM), lambda i: (0, i))],
        out_shape=[jax.ShapeDtypeStruct(((n_batch + 1) * (BLOCK // 2), n // BLOCK, fd), jnp.uint32),
                   jax.ShapeDtypeStruct((tt, qd), _BF),
                   jax.ShapeDtypeStruct((tt, 2 * KV_DIM), _BF), jax.ShapeDtypeStruct((2 * KV_DIM, tt), _BF)],
        compiler_params=_params("arbitrary"),
        name="proj_in",
    )(xa, mod, g, w_a, cos_t, sin_t)


def _rope_tables(n):
    pos = jnp.arange(n)
    inv = ROPE_THETA ** (-jnp.arange(ROPE_FREQS, dtype=_F32) / ROPE_FREQS)
    ar = (pos // GRID_W).astype(_F32)[:, None] * inv
    ac = (pos % GRID_W).astype(_F32)[:, None] * inv
    cos = jnp.concatenate([jnp.cos(ar), jnp.cos(ar), jnp.cos(ac), jnp.cos(ac)], axis=1)
    sin = jnp.concatenate([-jnp.sin(ar), jnp.sin(ar), -jnp.sin(ac), jnp.sin(ac)], axis=1)
    cos = jnp.tile(cos, (1, LANES // HEAD_DIM))
    sin = jnp.tile(sin, (1, LANES // HEAD_DIM))
    cos = jnp.concatenate([cos, jnp.ones((TM, LANES), _F32)], axis=0)
    sin = jnp.concatenate([sin, jnp.zeros((TM, LANES), _F32)], axis=0)
    return cos, sin


def _dft_consts(n, ctx_len):
    n1 = n // BLOCK
    b = np.arange(BLOCK, dtype=np.int64)[:, None, None]
    k1 = np.arange(n1, dtype=np.int64)[None, :, None]
    a = np.arange(n1, dtype=np.int64)[None, None, :]
    ang = 2.0 * np.pi * ((a * k1 * BLOCK + b * k1) % n).astype(np.float64) / n
    m_ri = np.concatenate([np.cos(ang), -np.sin(ang)], axis=1)
    kk = np.arange(GROUP_DIM, dtype=np.int64)
    ang_c = 2.0 * np.pi * ((kk[:, None] * kk[None, :]) % GROUP_DIM) / GROUP_DIM
    c, s = np.cos(ang_c), np.sin(ang_c)
    cs = np.concatenate([np.concatenate([c, s], axis=1), np.concatenate([-s, c], axis=1)], axis=0)
    cc = np.concatenate([c, s], axis=0)
    kl = np.arange(ctx_len, dtype=np.int64)
    ang_l = 2.0 * np.pi * ((kl[:, None] * kl[None, :]) % ctx_len) / ctx_len
    wl = np.concatenate([np.cos(ang_l), -np.sin(ang_l)], axis=0)
    as_bf = lambda v: jnp.asarray(v, dtype=_F32).astype(_BF)
    return dict(m_ri=as_bf(m_ri), cs=as_bf(cs), cc=as_bf(cc), wl=as_bf(wl))


def _chan_dft(xr, xi, cc, scale):
    outs = []
    for g in range(xr.shape[1] // GROUP_DIM):
        sl = slice(g * GROUP_DIM, (g + 1) * GROUP_DIM)
        xx = jnp.concatenate([xr[:, sl], xi[:, sl]], axis=1).astype(_BF)
        outs.append(_dot(xx, cc) * scale)
    return jnp.concatenate(outs, axis=1)


def _pack_rows(lo, hi):
    a = lax.bitcast_convert_type(lo.astype(_BF).astype(_F32), jnp.uint32)
    b = lax.bitcast_convert_type(hi.astype(_BF).astype(_F32), jnp.uint32)
    return (a >> 16) | b


def _unpack_rows(p):
    lo = lax.bitcast_convert_type(p << 16, _F32).astype(_BF)
    hi = lax.bitcast_convert_type(p & jnp.uint32(0xFFFF0000), _F32).astype(_BF)
    return lo, hi


def _fa_kernel(f_ref, mlo_ref, mhi_ref, zr_ref, zi_ref):
    n1 = f_ref.shape[1]
    for t in range(FA_NB):
        lo, hi = _unpack_rows(f_ref[t])
        za = _dot(mlo_ref[t], lo)
        zb = _dot(mhi_ref[t], hi)
        zr_ref[:, t, :] = _pack_rows(za[:n1], zb[:n1])
        zi_ref[:, t, :] = _pack_rows(za[n1:], zb[n1:])


def _fb_kernel(zr_ref, zi_ref, cs_ref, cc_ref, o_ref, *, scale):
    cs, cc = cs_ref[...], cc_ref[...]
    for j in range(FB_NB):
        xx = _dot(cs, jnp.concatenate(_unpack_rows(zr_ref[j]) + _unpack_rows(zi_ref[j]), axis=0))
        o_ref[:, j, :] = _chan_dft(xx[:BLOCK], xx[BLOCK:], cc, scale).astype(_BF)


def _fourier_lat(f, consts, *, n_batch, n):
    fd = f.shape[2]
    n1 = n // BLOCK
    half = BLOCK // 2
    steps = half // FA_NB
    blk_f = pl.BlockSpec((FA_NB, n1, fd), lambda b, j: (b * steps + j, 0, 0))
    blk_a = pl.BlockSpec((n1, FA_NB, fd), lambda b, j: (b, j, 0))
    mat_lo = pl.BlockSpec((FA_NB, 2 * n1, n1), lambda b, j: (j, 0, 0))
    mat_hi = pl.BlockSpec((FA_NB, 2 * n1, n1), lambda b, j: (j + steps, 0, 0))
    z_shape = jax.ShapeDtypeStruct((n_batch * n1, half, fd), jnp.uint32)
    zr, zi = pl.pallas_call(
        _fa_kernel,
        grid=(n_batch, steps),
        in_specs=[blk_f, mat_lo, mat_hi],
        out_specs=[blk_a, blk_a],
        out_shape=[z_shape, z_shape],
        compiler_params=_params("arbitrary", "arbitrary"),
        name="fourier_seq_a",
    )(f, consts["m_ri"], consts["m_ri"])
    blk_z = pl.BlockSpec((FB_NB, half, fd), lambda b, j: (b * (n1 // FB_NB) + j, 0, 0))
    const = lambda shape: pl.BlockSpec(shape, lambda b, j: (0, 0))
    out = pl.pallas_call(
        functools.partial(_fb_kernel, scale=float((n * GROUP_DIM) ** -0.5)),
        grid=(n_batch, n1 // FB_NB),
        in_specs=[blk_z, blk_z, const((2 * BLOCK, 2 * BLOCK)), const((2 * GROUP_DIM, GROUP_DIM))],
        out_specs=pl.BlockSpec((BLOCK, FB_NB, fd), lambda b, j: (b, j, 0)),
        out_shape=jax.ShapeDtypeStruct((n_batch * BLOCK, n1, fd), _BF),
        compiler_params=_params("arbitrary", "arbitrary"),
        name="fourier_seq_b",
    )(zr, zi, consts["cs"], consts["cc"])
    return out.reshape(n_batch * n, fd)


def _fc_kernel(f_ref, wl_ref, cc_ref, o_ref, *, scale, ctx_len, n_batch):
    per = ctx_len // BLOCK
    for b in range(n_batch):
        rows = []
        for a in range(per):
            rows += list(_unpack_rows(f_ref[:, b * per + a, :]))
        xx = _dot(wl_ref[...], jnp.concatenate(rows, axis=0))
        o_ref[b * ctx_len:(b + 1) * ctx_len, :] = _chan_dft(xx[:ctx_len], xx[ctx_len:], cc_ref[...],
                                                           scale).astype(_BF)


def _fourier_ctx(f, consts, *, n_batch, ctx_len):
    fd = f.shape[2]
    blocks = n_batch * ctx_len // BLOCK
    const = lambda shape: pl.BlockSpec(shape, lambda i: (0, 0))
    return pl.pallas_call(
        functools.partial(_fc_kernel, scale=float((ctx_len * GROUP_DIM) ** -0.5), ctx_len=ctx_len, n_batch=n_batch),
        grid=(1,),
        in_specs=[pl.BlockSpec((BLOCK // 2, blocks, fd), lambda i: (n_batch, 0, 0)),
                  const((2 * ctx_len, ctx_len)), const((2 * GROUP_DIM, GROUP_DIM))],
        out_specs=pl.BlockSpec((n_batch * ctx_len, fd), lambda i: (0, 0)),
        out_shape=jax.ShapeDtypeStruct((n_batch * ctx_len, fd), _BF),
        compiler_params=_params("arbitrary"),
        name="fourier_ctx",
    )(f, consts["wl"], consts["cc"])


def _attn_core(sink_ref, q_ref, k, vt, o_ref, bias, s_scr, p_scr):
    tq = q_ref.shape[0]
    lane = lax.broadcasted_iota(jnp.int32, (tq, LANES), 1)
    low = lane < HEAD_DIM
    zero = jnp.zeros((tq, LANES), _BF)
    low_row = lax.broadcasted_iota(jnp.int32, (LANES, 2 * tq), 0) < HEAD_DIM
    first = lax.broadcasted_iota(jnp.int32, (1, 2 * tq), 1) < tq
    pairs = [(kvh, half) for kvh in range(N_KV_HEADS) for half in range(2)]
    sels = [slice(0, LANES) if half == kvh else slice(LANES, 2 * LANES) for kvh, half in pairs]
    for pi, (kvh, half) in enumerate(pairs):
        slabs = [q_ref[:, (2 * kvh + c) * LANES:(2 * kvh + c + 1) * LANES] for c in range(2)]
        keep = low if half == 0 else jnp.logical_not(low)
        qm = jnp.concatenate([jnp.where(keep, sl, zero) for sl in slabs], axis=0)
        s = lax.dot_general(k[:, sels[pi]], qm, (((1,), (1,)), ((), ())), preferred_element_type=_F32)
        s_scr[pi] = s if bias is None else s + bias
    dens = []
    for pi, (kvh, half) in enumerate(pairs):
        hq = Q_PER_KV * kvh + half
        sk = jnp.where(first, sink_ref[hq], sink_ref[hq + 2]) * LOG2E
        s = s_scr[pi]
        m = jnp.maximum(jnp.max(s, axis=0, keepdims=True), sk)
        p = jnp.exp2(s - m)
        dens.append(jnp.sum(p, axis=0, keepdims=True) + jnp.exp2(sk - m))
        p_scr[pi] = p.astype(_BF)
    outs = [_dot(vt[sels[pi], :], p_scr[pi]) * (1.0 / dens[pi]) for pi in range(len(pairs))]
    for kvh in range(N_KV_HEADS):
        ot = jnp.where(low_row, outs[2 * kvh], outs[2 * kvh + 1])
        for c in range(2):
            o_ref[:, (2 * kvh + c) * LANES:(2 * kvh + c + 1) * LANES] = ot[:, c * tq:(c + 1) * tq].T.astype(_BF)


def _attn_scratch(keys, tq):
    return [pltpu.VMEM((2 * N_KV_HEADS, keys, 2 * tq), _F32), pltpu.VMEM((2 * N_KV_HEADS, keys, 2 * tq), _BF)]


def _attn_lat_kernel(sink_ref, q_ref, kp_ref, kc_ref, kn_ref, vp_ref, vc_ref, vn_ref, kx_ref, vx_ref, bias_ref,
                     o_ref, s_scr, p_scr):
    k = jnp.concatenate([kp_ref[...], kc_ref[...], kn_ref[...], kx_ref[...]], axis=0)
    vt = jnp.concatenate([vp_ref[...], vc_ref[...], vn_ref[...], vx_ref[...]], axis=1)
    _attn_core(sink_ref, q_ref, k, vt, o_ref, bias_ref[0], s_scr, p_scr)


def _window_bias(ctx_len):
    s = np.arange(3 * BLOCK + ctx_len)[:, None]
    r = np.arange(2 * BLOCK)[None, :] % BLOCK
    out = []
    for v in range(4):
        lo = 0 if v & 1 else BLOCK
        hi = 3 * BLOCK if v & 2 else 2 * BLOCK
        ok = ((np.abs(s - BLOCK - r) <= WINDOW) & (s >= lo) & (s < hi)) | (s >= 3 * BLOCK)
        out.append(np.where(ok, 0.0, NEG_INF))
    return jnp.asarray(np.stack(out), dtype=_F32)


def _attn_ctx_kernel(sink_ref, q_ref, kx_ref, vx_ref, o_ref, s_scr, p_scr):
    _attn_core(sink_ref, q_ref, kx_ref[...], vx_ref[...], o_ref, None, s_scr, p_scr)


def _attn_lat(sink, q, k2, v2t, *, n_batch, n, ctx_len, t_lat):
    qd = q.shape[1]
    nb = n // BLOCK
    cur = lambda b, i: (b * nb + i, 0)
    prev = lambda b, i: (b * nb + jnp.maximum(i - 1, 0), 0)
    nxt = lambda b, i: (b * nb + jnp.minimum(i + 1, nb - 1), 0)
    cx = lambda b, i: (t_lat // ctx_len + b, 0)
    swap = lambda im: (lambda b, i: im(b, i)[::-1])
    kb = lambda im: pl.BlockSpec((BLOCK, 2 * KV_DIM), im)
    vb = lambda im: pl.BlockSpec((2 * KV_DIM, BLOCK), swap(im))
    bias = _window_bias(ctx_len)
    bias_idx = lambda b, i: ((i > 0).astype(jnp.int32) + 2 * (i < nb - 1).astype(jnp.int32), 0, 0)
    return pl.pallas_call(
        _attn_lat_kernel,
        grid=(n_batch, nb),
        in_specs=[pl.BlockSpec(memory_space=pltpu.SMEM), pl.BlockSpec((BLOCK, qd), cur),
                  kb(prev), kb(cur), kb(nxt), vb(prev), vb(cur), vb(nxt),
                  pl.BlockSpec((ctx_len, 2 * KV_DIM), cx), pl.BlockSpec((2 * KV_DIM, ctx_len), swap(cx)),
                  pl.BlockSpec((1,) + bias.shape[1:], bias_idx)],
        out_specs=pl.BlockSpec((BLOCK, qd), cur),
        out_shape=jax.ShapeDtypeStruct((t_lat, qd), _BF),
        scratch_shapes=_attn_scratch(3 * BLOCK + ctx_len, BLOCK),
        compiler_params=_params("arbitrary", "arbitrary"),
        name="attn_lat",
    )(sink, q, k2, k2, k2, v2t, v2t, v2t, k2, v2t, bias)


def _attn_ctx(sink, q, k2, v2t, *, n_batch, ctx_len, t_lat):
    qd = q.shape[1]
    cx = lambda b: (t_lat // ctx_len + b, 0)
    return pl.pallas_call(
        _attn_ctx_kernel,
        grid=(n_batch,),
        in_specs=[pl.BlockSpec(memory_space=pltpu.SMEM), pl.BlockSpec((ctx_len, qd), cx),
                  pl.BlockSpec((ctx_len, 2 * KV_DIM), cx),
                  pl.BlockSpec((2 * KV_DIM, ctx_len), lambda b: (0, t_lat // ctx_len + b))],
        out_specs=pl.BlockSpec((ctx_len, qd), lambda b: (b, 0)),
        out_shape=jax.ShapeDtypeStruct((n_batch * ctx_len, qd), _BF),
        scratch_shapes=_attn_scratch(ctx_len, ctx_len),
        compiler_params=_params("arbitrary"),
        name="attn_ctx",
    )(sink, q, k2, v2t)


def _pack_bf16_pairs(h):
    half = h.shape[1] // 2
    lo = lax.bitcast_convert_type(h[:, :half].astype(_BF).astype(_F32), jnp.uint32)
    hi = lax.bitcast_convert_type(h[:, half:].astype(_BF).astype(_F32), jnp.uint32)
    return (lo >> 16) | hi


def _unpack_bf16_pairs(p):
    lo = lax.bitcast_convert_type(p << 16, _F32)
    hi = lax.bitcast_convert_type(p & jnp.uint32(0xFFFF0000), _F32)
    return jnp.concatenate([lo, hi], axis=1).astype(_BF)


def _merge_kernel(x_ref, mod_ref, g1_ref, g2_ref, wg_ref, fml_ref, fmc_ref, atl_ref, atc_ref,
                  wfo_ref, wao_ref, wout_ref, rw_ref, rb_ref,
                  x1_ref, hp_ref, idx_ref, gate_ref, cnt_ref, h_scr, y_scr, *, nxt):
    i = pl.program_id(0)
    d = x_ref.shape[1]
    m = mod_ref[0]
    is_lat = i < nxt
    h_scr[...] = _norm_mod(x_ref[...], g1_ref[...], m[0:1], m[1:2]).astype(_BF)
    fm = jnp.where(is_lat, fml_ref[...], fmc_ref[...])
    at = jnp.where(is_lat, atl_ref[...], atc_ref[...])
    for c in range(d // MERGE_COLS):
        sl = slice(c * MERGE_COLS, (c + 1) * MERGE_COLS)
        sg = slice(d + c * MERGE_COLS, d + (c + 1) * MERGE_COLS)
        h = h_scr[...]
        y = (_sigmoid(_dot(h, wg_ref[:, sl])) * _dot(fm, wfo_ref[:, sl])
             + _sigmoid(_dot(h, wg_ref[:, sg])) * _dot(at, wao_ref[:, sl]))
        y_scr[:, sl] = y.astype(_BF)
    x1_ref[...] = x_ref[...] + m[2:3] * _dot(y_scr[...], wout_ref[...])
    h2 = _norm_mod(x1_ref[...], g2_ref[...], m[3:4], m[4:5])
    hp_ref[...] = _pack_bf16_pairs(h2)
    logits = _dot(h2.astype(_BF), rw_ref[...]) + rb_ref[...]
    lane = lax.broadcasted_iota(jnp.int32, logits.shape, 1).astype(_F32)
    vals, idxs = [], []
    for _ in range(TOP_K):
        mx = jnp.max(logits, axis=-1, keepdims=True)
        ix = jnp.min(jnp.where(logits == mx, lane, float(LANES)), axis=-1, keepdims=True)
        logits = jnp.where(lane == ix, NEG_INF, logits)
        vals.append(mx)
        idxs.append(ix)
    es = [jnp.exp(v - vals[0]) for v in vals]
    den = es[0] + es[1] + es[2] + es[3]
    idx4 = jnp.zeros(logits.shape, _F32)
    gate4 = jnp.zeros(logits.shape, _F32)
    sel = jnp.zeros(logits.shape, _F32)
    for k in range(TOP_K):
        idx4 = jnp.where(lane == float(k), idxs[k], idx4)
        gate4 = jnp.where(lane == float(k), es[k] / den, gate4)
        sel = sel + jnp.where(lane == idxs[k], 1.0, 0.0)
    idx_ref[...] = idx4.astype(jnp.int32)
    gate_ref[...] = gate4

    @pl.when(i == 0)
    def _():
        cnt_ref[...] = jnp.zeros(cnt_ref.shape, _F32)

    cnt_ref[0:1, :] += jnp.sum(sel, axis=0, keepdims=True)


def _merge(xa, mod, g1, g2, w_g, fm_lat, fm_ctx, at_lat, at_ctx, w_fo, w_ao, w_out, rw, rb,
           *, n, t_lat, n_batch):
    tt, d = xa.shape
    fd = fm_lat.shape[1]
    nt = tt // TM
    nxt = t_lat // TM
    row = lambda i: (i, 0)
    lat = lambda i: (jnp.minimum(i, nxt - 1), 0)
    cxt = lambda i: (jnp.maximum(i - nxt, 0), 0)
    const = lambda a: pl.BlockSpec(a.shape, lambda i: (0, 0))
    mod_idx = lambda i: (jnp.minimum(i * TM // n, n_batch), 0, 0)
    return pl.pallas_call(
        functools.partial(_merge_kernel, nxt=nxt),
        grid=(nt,),
        in_specs=[pl.BlockSpec((TM, d), row), pl.BlockSpec((1, N_MOD, d), mod_idx),
                  const(g1), const(g2), const(w_g),
                  pl.BlockSpec((TM, fd), lat), pl.BlockSpec((TM, fd), cxt),
                  pl.BlockSpec((TM, fd), lat), pl.BlockSpec((TM, fd), cxt),
                  const(w_fo), const(w_ao), const(w_out), const(rw), const(rb)],
        out_specs=[pl.BlockSpec((TM, d), row), pl.BlockSpec((TM, d // 2), row),
                   pl.BlockSpec((TM, LANES), row), pl.BlockSpec((TM, LANES), row),
                   pl.BlockSpec((8, LANES), lambda i: (0, 0))],
        out_shape=[jax.ShapeDtypeStruct((tt, d), _F32), jax.ShapeDtypeStruct((tt, d // 2), jnp.uint32),
                   jax.ShapeDtypeStruct((tt, LANES), jnp.int32), jax.ShapeDtypeStruct((tt, LANES), _F32),
                   jax.ShapeDtypeStruct((8, LANES), _F32)],
        scratch_shapes=[pltpu.VMEM((TM, d), _BF), pltpu.VMEM((TM, d), _BF)],
        compiler_params=_params("arbitrary"),
        name="merge_router",
    )(xa, mod, g1, g2, w_g, fm_lat, fm_ctx, at_lat, at_ctx, w_fo, w_ao, w_out, rw, rb)


def _route_kernel(idx_ref, ps_ref, d_ref, carry):
    i = pl.program_id(0)

    @pl.when(i == 0)
    def _():
        carry[...] = ps_ref[...]

    idx4 = idx_ref[...]
    lane = lax.broadcasted_iota(jnp.int32, idx4.shape, 1)
    cols = [idx4[:, k:k + 1] for k in range(TOP_K)]
    sel = jnp.zeros(idx4.shape, _F32)
    for k in range(TOP_K):
        sel = sel + jnp.where(lane == cols[k], 1.0, 0.0)
    r = lax.broadcasted_iota(jnp.int32, (TM, TM), 0)
    c = lax.broadcasted_iota(jnp.int32, (TM, TM), 1)
    tri = jnp.where(r > c, 1.0, 0.0).astype(_BF)
    slot = carry[...] + _dot(tri, sel.astype(_BF))
    dest4 = jnp.zeros(idx4.shape, jnp.int32)
    for k in range(TOP_K):
        dk = jnp.sum(jnp.where(lane == cols[k], slot, 0.0), axis=-1, keepdims=True)
        dest4 = jnp.where(lane == k, dk.astype(jnp.int32), dest4)
    d_ref[...] = dest4
    carry[...] += jnp.sum(sel, axis=0, keepdims=True)


def _route(idx4, pad_start):
    tt = idx4.shape[0]
    return pl.pallas_call(
        _route_kernel,
        grid=(tt // TM,),
        in_specs=[pl.BlockSpec((TM, LANES), lambda i: (i, 0)), pl.BlockSpec((1, LANES), lambda i: (0, 0))],
        out_specs=pl.BlockSpec((TM, LANES), lambda i: (i, 0)),
        out_shape=jax.ShapeDtypeStruct((tt, LANES), jnp.int32),
        scratch_shapes=[pltpu.VMEM((1, LANES), _F32)],
        compiler_params=_params("arbitrary"),
        name="route_slots",
    )(idx4, pad_start)


def _zero_tails_kernel(pend_ref, cnt_ref, nv_ref, xs_ref, zbuf, sem):
    n_blocks = xs_ref.shape[0] // TME
    zbuf[...] = jnp.zeros(zbuf.shape, zbuf.dtype)

    def zero_block(start):
        return pltpu.make_async_copy(zbuf, xs_ref.at[pl.ds(pl.multiple_of(start, TME), TME)], sem)

    for e in range(N_EXPERTS):
        @pl.when(cnt_ref[e] > 0)
        def _():
            zero_block(pend_ref[e] - TME).start()

    def start_unused(j, carry):
        zero_block(j * TME).start()
        return carry

    def wait_unused(j, carry):
        zero_block(j * TME).wait()
        return carry

    lax.fori_loop(nv_ref[0], n_blocks, start_unused, 0)
    for e in range(N_EXPERTS):
        @pl.when(cnt_ref[e] > 0)
        def _():
            zero_block(pend_ref[e] - TME).wait()
    lax.fori_loop(nv_ref[0], n_blocks, wait_unused, 0)


def _zero_tails(pad_end, cnt, n_valid, n_slots, w, dtype):
    grid_spec = pltpu.PrefetchScalarGridSpec(
        num_scalar_prefetch=3,
        grid=(1,),
        in_specs=[],
        out_specs=pl.BlockSpec(memory_space=pl.ANY),
        scratch_shapes=[pltpu.VMEM((TME, w), dtype), pltpu.SemaphoreType.DMA],
    )
    return pl.pallas_call(
        _zero_tails_kernel,
        grid_spec=grid_spec,
        out_shape=jax.ShapeDtypeStruct((n_slots, w), dtype),
        compiler_params=_params("arbitrary"),
        name="moe_zero_tails",
    )(pad_end, cnt, n_valid)


def _sc_scatter(rows, idx, base):
    tt, w = rows.shape
    win = SC_SCATTER_WINDOW
    workers = SC_CORES * SC_SUBCORES
    per_worker = tt // workers
    n_win = per_worker // win
    assert tt % workers == 0 and per_worker % win == 0 and n_win % 2 == 0
    mesh = plsc.VectorSubcoreMesh(core_axis_name="core", subcore_axis_name="subcore")
    out = jax.new_ref(base)
    idx_buf = pltpu.VMEM((win,), jnp.int32)
    row_buf = pltpu.VMEM((win, w), rows.dtype)

    @pl.kernel(out_type=(), mesh=mesh,
               scratch_types=[idx_buf] * (2 * TOP_K) + [row_buf, row_buf] + [pltpu.SemaphoreType.DMA] * 4)
    def scatter(r_hbm, i_hbm, o_hbm, *scratch):
        idx_a, idx_b = scratch[:TOP_K], scratch[TOP_K:2 * TOP_K]
        rows_a, rows_b, lsem_a, lsem_b, ssem_a, ssem_b = scratch[2 * TOP_K:]
        wid = lax.axis_index("subcore") * SC_CORES + lax.axis_index("core")
        first = wid * per_worker

        def load_start(c, rows_v, sem):
            pltpu.make_async_copy(r_hbm.at[pl.ds(first + c * win, win)], rows_v, sem).start()

        def load_wait(rows_v, sem):
            pltpu.make_async_copy(r_hbm.at[pl.ds(first, win)], rows_v, sem).wait()

        def scatter_start(c, rows_v, idxs, sem):
            for k in range(TOP_K):
                pltpu.sync_copy(i_hbm.at[pl.ds(k * tt + first + c * win, win)], idxs[k])
                pltpu.make_async_copy(rows_v, o_hbm.at[idxs[k]], sem).start()

        def scatter_wait(rows_v, idxs, sem):
            for k in range(TOP_K):
                pltpu.make_async_copy(rows_v, o_hbm.at[idxs[k]], sem).wait()

        load_start(0, rows_a, lsem_a)

        @pl.loop(0, n_win // 2)
        def _(i):
            c = 2 * i

            @pl.when(i > 0)
            def _():
                scatter_wait(rows_b, idx_b, ssem_b)

            load_start(c + 1, rows_b, lsem_b)
            load_wait(rows_a, lsem_a)
            scatter_start(c, rows_a, idx_a, ssem_a)
            scatter_wait(rows_a, idx_a, ssem_a)

            @pl.when(c + 2 < n_win)
            def _():
                load_start(c + 2, rows_a, lsem_a)

            load_wait(rows_b, lsem_b)
            scatter_start(c + 1, rows_b, idx_b, ssem_b)

        scatter_wait(rows_b, idx_b, ssem_b)

    scatter(rows, idx, out)
    return jax.freeze(out)


def _moe_kernel(be_ref, nv_ref, xs_ref, wgu_ref, bgu_ref, wd_ref, bd_ref, y_ref, wgu_bf, wd_bf):
    i = pl.program_id(0)
    de = wd_ref.shape[2]
    valid = i < nv_ref[0]
    new_expert = jnp.logical_or(i == 0, be_ref[i] != be_ref[jnp.maximum(i - 1, 0)])

    @pl.when(jnp.logical_and(valid, new_expert))
    def _():
        def cast_rows(ref, out, rows):
            def body(j, carry):
                sl = pl.ds(pl.multiple_of(j * CAST_ROWS, CAST_ROWS), CAST_ROWS)
                out[sl, :] = ref[0, 0, sl, :].astype(_BF)
                return carry
            lax.fori_loop(0, rows // CAST_ROWS, body, 0)
        cast_rows(wgu_ref, wgu_bf, wgu_ref.shape[2])
        cast_rows(wd_ref, wd_bf, de)

    @pl.when(valid)
    def _():
        xb = _unpack_bf16_pairs(xs_ref[...])
        gu = _dot(xb, wgu_bf[...]) + bgu_ref[0, 0]
        a = jnp.minimum(gu[:, :de], SWIGLU_LIMIT)
        u = jnp.clip(gu[:, de:], -SWIGLU_LIMIT, SWIGLU_LIMIT)
        act = a * _sigmoid(SWIGLU_ALPHA * a) * (u + 1)
        y_ref[...] = _pack_bf16_pairs(_dot(act.astype(_BF), wd_bf[...]) + bd_ref[0, 0])

    @pl.when(i >= nv_ref[0])
    def _():
        y_ref[...] = jnp.zeros(y_ref.shape, y_ref.dtype)


def _moe(block_e, n_valid, xs, w_gu, b_gu, w_down, b_down, *, layer):
    n_slots, w = xs.shape
    depth, ne, d, de2 = w_gu.shape
    de = de2 // 2
    n_blocks = n_slots // TME
    ex = lambda i, be, nv: (layer, be[i], 0, 0)
    grid_spec = pltpu.PrefetchScalarGridSpec(
        num_scalar_prefetch=2,
        grid=(n_blocks,),
        in_specs=[pl.BlockSpec((TME, w), lambda i, be, nv: (jnp.minimum(i, nv[0] - 1), 0)),
                  pl.BlockSpec((1, 1, d, de2), ex), pl.BlockSpec((1, 1, 1, de2), ex),
                  pl.BlockSpec((1, 1, de, d), ex), pl.BlockSpec((1, 1, 1, d), ex)],
        out_specs=pl.BlockSpec((TME, d // 2), lambda i, be, nv: (i, 0)),
        scratch_shapes=[pltpu.VMEM((d, de2), _BF), pltpu.VMEM((de, d), _BF)],
    )
    return pl.pallas_call(
        _moe_kernel,
        grid_spec=grid_spec,
        out_shape=jax.ShapeDtypeStruct((n_slots, d // 2), jnp.uint32),
        compiler_params=_params("arbitrary"),
        name="moe_experts",
    )(block_e, n_valid, xs, w_gu, b_gu.reshape(depth, ne, 1, de2), w_down, b_down.reshape(depth, ne, 1, d))


def _sc_gather(table, idx):
    n_idx, (_, w) = idx.shape[0], table.shape
    workers = SC_CORES * SC_SUBCORES
    per_worker = n_idx // workers
    win = SC_WINDOW // SC_SPLIT
    n_win = per_worker // SC_WINDOW
    assert n_idx % workers == 0 and per_worker % SC_WINDOW == 0 and n_win % 2 == 0 and SC_WINDOW % SC_SPLIT == 0
    mesh = plsc.VectorSubcoreMesh(core_axis_name="core", subcore_axis_name="subcore")
    idx_buf = pltpu.VMEM((win,), jnp.int32)
    row_buf = pltpu.VMEM((win, w), table.dtype)

    @pl.kernel(out_type=jax.ShapeDtypeStruct((n_idx, w), table.dtype), mesh=mesh,
               scratch_types=[idx_buf] * (2 * SC_SPLIT) + [row_buf] * (2 * SC_SPLIT) + [pltpu.SemaphoreType.DMA] * 4)
    def gather(t_hbm, i_hbm, o_hbm, *scratch):
        idx_a, idx_b = scratch[:SC_SPLIT], scratch[SC_SPLIT:2 * SC_SPLIT]
        rows_a, rows_b = scratch[2 * SC_SPLIT:3 * SC_SPLIT], scratch[3 * SC_SPLIT:4 * SC_SPLIT]
        gsem_a, gsem_b, ssem_a, ssem_b = scratch[4 * SC_SPLIT:]
        wid = lax.axis_index("subcore") * SC_CORES + lax.axis_index("core")
        first = wid * per_worker

        def gather_start(c, idx_v, rows_v, sem):
            for j in range(SC_SPLIT):
                pltpu.sync_copy(i_hbm.at[pl.ds(first + c * SC_WINDOW + j * win, win)], idx_v[j])
                pltpu.make_async_copy(t_hbm.at[idx_v[j]], rows_v[j], sem).start()

        def gather_wait(idx_v, rows_v, sem):
            for j in range(SC_SPLIT):
                pltpu.make_async_copy(t_hbm.at[idx_v[j]], rows_v[j], sem).wait()

        def store_start(c, rows_v, sem):
            for j in range(SC_SPLIT):
                pltpu.make_async_copy(rows_v[j], o_hbm.at[pl.ds(first + c * SC_WINDOW + j * win, win)], sem).start()

        def store_wait(rows_v, sem):
            for j in range(SC_SPLIT):
                pltpu.make_async_copy(rows_v[j], o_hbm.at[pl.ds(first, win)], sem).wait()

        gather_start(0, idx_a, rows_a, gsem_a)

        @pl.loop(0, n_win // 2)
        def _(i):
            c = 2 * i

            @pl.when(i > 0)
            def _():
                store_wait(rows_b, ssem_b)

            gather_start(c + 1, idx_b, rows_b, gsem_b)
            gather_wait(idx_a, rows_a, gsem_a)
            store_start(c, rows_a, ssem_a)
            store_wait(rows_a, ssem_a)

            @pl.when(c + 2 < n_win)
            def _():
                gather_start(c + 2, idx_a, rows_a, gsem_a)

            gather_wait(idx_b, rows_b, gsem_b)
            store_start(c + 1, rows_b, ssem_b)

        store_wait(rows_b, ssem_b)

    return gather(table, idx)


def _combine_kernel(yg_ref, gate_ref, x_ref, mod_ref, fg_ref, o_ref, *, final):
    g = gate_ref[...]
    half = x_ref.shape[1] // 2
    lo = jnp.zeros((x_ref.shape[0], half), _F32)
    hi = jnp.zeros((x_ref.shape[0], half), _F32)
    for k in range(TOP_K):
        p = yg_ref[k]
        lo = lo + g[:, k:k + 1] * lax.bitcast_convert_type(p << 16, _F32)
        hi = hi + g[:, k:k + 1] * lax.bitcast_convert_type(p & jnp.uint32(0xFFFF0000), _F32)
    x2 = x_ref[...] + mod_ref[0][5:6] * jnp.concatenate([lo, hi], axis=1)
    if final:
        ms = jnp.mean(x2 * x2, axis=-1, keepdims=True)
        x2 = x2 * lax.rsqrt(ms + EPS) * fg_ref[...]
    o_ref[...] = x2


def _combine(yg, gate4, x1, mod, fg, *, n, n_batch, rows, final):
    d = x1.shape[1]
    row = lambda i: (i, 0)
    mod_idx = lambda i: (jnp.minimum(i * TMC // n, n_batch), 0, 0)
    return pl.pallas_call(
        functools.partial(_combine_kernel, final=final),
        grid=(rows // TMC,),
        in_specs=[pl.BlockSpec((TOP_K, TMC, d // 2), lambda i: (0, i, 0)),
                  pl.BlockSpec((TMC, LANES), row), pl.BlockSpec((TMC, d), row),
                  pl.BlockSpec((1, N_MOD, d), mod_idx), pl.BlockSpec((1, d), lambda i: (0, 0))],
        out_specs=pl.BlockSpec((TMC, d), row),
        out_shape=jax.ShapeDtypeStruct((rows, d), _F32),
        compiler_params=_params("arbitrary"),
        name="moe_combine",
    )(yg, gate4, x1, mod, fg)


def _routing_tables(counts, n_blocks):
    cnt = counts[0, :N_EXPERTS].astype(jnp.int32)
    padded = (cnt + TME - 1) // TME * TME
    pad_end = jnp.cumsum(padded)
    pad_start = pad_end - padded
    ps = jnp.zeros((1, LANES), _F32).at[0, :N_EXPERTS].set(pad_start.astype(_F32))
    blk = jnp.arange(n_blocks, dtype=jnp.int32) * TME
    block_e = jnp.sum((pad_end[None, :] <= blk[:, None]).astype(jnp.int32), axis=1)
    e_last = jnp.max(jnp.where(cnt > 0, jnp.arange(N_EXPERTS, dtype=jnp.int32), 0))
    block_e = jnp.minimum(block_e, e_last).astype(jnp.int32)
    n_valid = (pad_end[-1] // TME).astype(jnp.int32).reshape(1)
    return ps, block_e, n_valid, pad_end.astype(jnp.int32), cnt


def kernel(x, c, ctx, c_ctx, ada_w, ada_b, norm1_g, norm2_g, w_in, attn_sink, w_fourier_out, w_attn_out,
           w_out, router_w, router_b, expert_w_gu, expert_b_gu, expert_w_down, expert_b_down, final_norm_g):
    n_batch, n, d = x.shape
    ctx_len = ctx.shape[1]
    depth = ada_w.shape[0]
    t_lat = n_batch * n
    t_ctx = n_batch * ctx_len
    tt = t_lat + t_ctx
    fd = qd = d // 2
    assert n % TM == 0 and t_ctx == TM and n % (BLOCK * FB_NB) == 0 and n_batch < MOD_ROWS
    assert t_lat % ctx_len == 0 and TM % ctx_len == 0

    xa = jnp.concatenate([x.reshape(t_lat, d), ctx.reshape(t_ctx, d)], axis=0)
    cc = jnp.concatenate([c, c_ctx[None, :], jnp.zeros((MOD_ROWS - n_batch - 1, d), _F32)], axis=0)
    mod = _ada(cc, ada_w, ada_b).reshape(depth, MOD_ROWS, N_MOD, d)
    cos_t, sin_t = _rope_tables(n)
    consts = _dft_consts(n, ctx_len)

    n_in = fd + qd + 2 * KV_DIM
    w_a = w_in[:, :, :n_in].astype(_BF)
    w_g = w_in[:, :, n_in:].astype(_BF)
    w_fo, w_ao, w_o = w_fourier_out.astype(_BF), w_attn_out.astype(_BF), w_out.astype(_BF)
    rw = jnp.pad(router_w, ((0, 0), (0, 0), (0, LANES - N_EXPERTS))).astype(_BF)
    rb = jnp.pad(router_b, ((0, 0), (0, LANES - N_EXPERTS)), constant_values=NEG_INF)[:, None, :]

    n_blocks = -(-(tt * TOP_K + N_EXPERTS * (TME - 1)) // TME)
    dims = dict(n=n, t_lat=t_lat, n_batch=n_batch)
    for l in range(depth):
        last = l == depth - 1
        f, q, k2, v2 = _proj(xa, mod[l], norm1_g[l][None, :], w_a[l], cos_t, sin_t, **dims)
        fm_lat = _fourier_lat(f, consts, n_batch=n_batch, n=n)
        fm_ctx = _fourier_ctx(f, consts, n_batch=n_batch, ctx_len=ctx_len)
        at_lat = _attn_lat(attn_sink[l], q, k2, v2, n_batch=n_batch, n=n, ctx_len=ctx_len, t_lat=t_lat)
        at_ctx = _attn_ctx(attn_sink[l], q, k2, v2, n_batch=n_batch, ctx_len=ctx_len, t_lat=t_lat)
        x1, hp, idx4, gate4, counts = _merge(
            xa, mod[l], norm1_g[l][None, :], norm2_g[l][None, :], w_g[l], fm_lat, fm_ctx, at_lat, at_ctx,
            w_fo[l], w_ao[l], w_o[l], rw[l], rb[l], **dims)
        pad_start, block_e, n_valid, pad_end, cnt = _routing_tables(counts, n_blocks)
        dest = _route(idx4, pad_start)[:, :TOP_K].T.reshape(-1)
        xs = _sc_scatter(hp, dest, _zero_tails(pad_end, cnt, n_valid, n_blocks * TME, hp.shape[1], hp.dtype))
        y = _moe(block_e, n_valid, xs, expert_w_gu, expert_b_gu, expert_w_down, expert_b_down, layer=l)
        yg = _sc_gather(y, dest).reshape(TOP_K, tt, d // 2)
        xa = _combine(yg, gate4, x1, mod[l], final_norm_g[None, :], n=n, n_batch=n_batch,
                      rows=t_lat if last else tt, final=last)
    return xa.reshape(n_batch, n, d)
```

```python
import functools

import numpy as np
import jax
import jax.numpy as jnp
from jax import lax
from jax.experimental import pallas as pl
from jax.experimental.pallas import tpu as pltpu
from jax.experimental.pallas import tpu_sc as plsc

GRID_W = 64
HEAD_DIM = 64
N_KV_HEADS = 2
Q_PER_KV = 4
KV_DIM = N_KV_HEADS * HEAD_DIM
WINDOW = 128
BLOCK = 128
ROPE_THETA = 10000.0
ROPE_FREQS = HEAD_DIM // 4
GROUP_DIM = 128
N_EXPERTS = 32
TOP_K = 4
SWIGLU_LIMIT = 7.0
SWIGLU_ALPHA = 1.702
N_MOD = 6
EPS = 1e-5
NEG_INF = -1e30
LOG2E = 1.4426950408889634
Q_SCALE = HEAD_DIM ** -0.5 * LOG2E

LANES = 128
MOD_ROWS = 8
TM = 1024
TME = 512
TMC = 512
SC_CORES = 2
SC_SUBCORES = 16
SC_WINDOW = 96
SC_SCATTER_WINDOW = 88
FA_NB = 16
FB_NB = 16
CAST_ROWS = 64
MERGE_COLS = 256
VMEM_LIMIT = 56 * 1024 * 1024

_BF = jnp.bfloat16
_F32 = jnp.float32


def _params(*sem):
    return pltpu.CompilerParams(dimension_semantics=sem, vmem_limit_bytes=VMEM_LIMIT)


def _dot(a, b):
    return jnp.dot(a, b, preferred_element_type=_F32)


def _sigmoid(x):
    return 0.5 * jnp.tanh(0.5 * x) + 0.5


def _norm_mod(x, g, shift, scale):
    ms = jnp.mean(x * x, axis=-1, keepdims=True)
    return x * lax.rsqrt(ms + EPS) * (g * (1 + scale)) + shift


def _ada_kernel(c_ref, w_ref, b_ref, o_ref):
    c = c_ref[...]
    s = c * jax.nn.sigmoid(c)
    o_ref[0] = jnp.dot(s, w_ref[0], precision=lax.Precision.HIGHEST,
                       preferred_element_type=_F32) + b_ref[0]


def _ada(cc, ada_w, ada_b):
    depth, d, nd = ada_w.shape
    tn = nd // 4
    return pl.pallas_call(
        _ada_kernel,
        grid=(depth, nd // tn),
        in_specs=[pl.BlockSpec((MOD_ROWS, d), lambda l, j: (0, 0)),
                  pl.BlockSpec((1, d, tn), lambda l, j: (l, 0, j)),
                  pl.BlockSpec((1, 1, tn), lambda l, j: (l, 0, j))],
        out_specs=pl.BlockSpec((1, MOD_ROWS, tn), lambda l, j: (l, 0, j)),
        out_shape=jax.ShapeDtypeStruct((depth, MOD_ROWS, nd), _F32),
        compiler_params=_params("arbitrary", "arbitrary"),
        name="ada_mod",
    )(cc, ada_w, ada_b.reshape(depth, 1, nd))


def _rope(v, cos, sin):
    lane = lax.broadcasted_iota(jnp.int32, v.shape, 1)
    partner = jnp.where((lane & ROPE_FREQS) == 0,
                        pltpu.roll(v, LANES - ROPE_FREQS, 1), pltpu.roll(v, ROPE_FREQS, 1))
    return v * cos + partner * sin


def _proj_kernel(x_ref, mod_ref, g_ref, w_ref, cos_ref, sin_ref, f_ref, q_ref, k_ref, vt_ref, *, fd, qd):
    m = mod_ref[0]
    h = _norm_mod(x_ref[...], g_ref[...], m[0:1], m[1:2]).astype(_BF)
    p = _dot(h, w_ref[...])
    cos = cos_ref[...]
    sin = sin_ref[...]
    half = BLOCK // 2
    for a in range(p.shape[0] // BLOCK):
        f_ref[:, a, :] = _pack_rows(p[a * BLOCK:a * BLOCK + half, :fd], p[a * BLOCK + half:(a + 1) * BLOCK, :fd])
    for j in range(qd // LANES):
        lo = fd + j * LANES
        q_ref[:, j * LANES:(j + 1) * LANES] = (
            _rope(p[:, lo:lo + LANES], cos, sin) * Q_SCALE).astype(_BF)
    k = _rope(p[:, fd + qd:fd + qd + KV_DIM], cos, sin)
    v = p[:, fd + qd + KV_DIM:fd + qd + 2 * KV_DIM]
    k_ref[:, :KV_DIM] = k.astype(_BF)
    k_ref[:, KV_DIM:] = pltpu.roll(k, HEAD_DIM, 1).astype(_BF)
    vt_ref[:KV_DIM, :] = v.T.astype(_BF)
    vt_ref[KV_DIM:, :] = pltpu.roll(v, HEAD_DIM, 1).T.astype(_BF)


def _proj(xa, mod, g, w_a, cos_t, sin_t, *, n, t_lat, n_batch):
    tt, d = xa.shape
    fd = qd = d // 2
    nt = tt // TM
    nxt = t_lat // TM
    per_seq = n // TM
    mod_idx = lambda i: (jnp.minimum(i * TM // n, n_batch), 0, 0)
    rope_idx = lambda i: (jnp.where(i < nxt, i % per_seq, per_seq), 0)
    f_idx = lambda i: (jnp.where(i < nxt, i // per_seq, n_batch), jnp.where(i < nxt, i % per_seq, 0), 0)
    row = lambda i: (i, 0)
    return pl.pallas_call(
        functools.partial(_proj_kernel, fd=fd, qd=qd),
        grid=(nt,),
        in_specs=[pl.BlockSpec((TM, d), row),
                  pl.BlockSpec((1, N_MOD, d), mod_idx),
                  pl.BlockSpec((1, d), lambda i: (0, 0)),
                  pl.BlockSpec(w_a.shape, lambda i: (0, 0)),
                  pl.BlockSpec((TM, LANES), rope_idx),
                  pl.BlockSpec((TM, LANES), rope_idx)],
        out_specs=[pl.BlockSpec((BLOCK // 2, TM // BLOCK, fd), f_idx), pl.BlockSpec((TM, qd), row),
                   pl.BlockSpec((TM, 2 * KV_DIM), row), pl.BlockSpec((2 * KV_DIM, TM), lambda i: (0, i))],
        out_shape=[jax.ShapeDtypeStruct(((n_batch + 1) * (BLOCK // 2), n // BLOCK, fd), jnp.uint32),
                   jax.ShapeDtypeStruct((tt, qd), _BF),
                   jax.ShapeDtypeStruct((tt, 2 * KV_DIM), _BF), jax.ShapeDtypeStruct((2 * KV_DIM, tt), _BF)],
        compiler_params=_params("arbitrary"),
        name="proj_in",
    )(xa, mod, g, w_a, cos_t, sin_t)


def _rope_tables(n):
    pos = jnp.arange(n)
    inv = ROPE_THETA ** (-jnp.arange(ROPE_FREQS, dtype=_F32) / ROPE_FREQS)
    ar = (pos // GRID_W).astype(_F32)[:, None] * inv
    ac = (pos % GRID_W).astype(_F32)[:, None] * inv
    cos = jnp.concatenate([jnp.cos(ar), jnp.cos(ar), jnp.cos(ac), jnp.cos(ac)], axis=1)
    sin = jnp.concatenate([-jnp.sin(ar), jnp.sin(ar), -jnp.sin(ac), jnp.sin(ac)], axis=1)
    cos = jnp.tile(cos, (1, LANES // HEAD_DIM))
    sin = jnp.tile(sin, (1, LANES // HEAD_DIM))
    cos = jnp.concatenate([cos, jnp.ones((TM, LANES), _F32)], axis=0)
    sin = jnp.concatenate([sin, jnp.zeros((TM, LANES), _F32)], axis=0)
    return cos, sin


def _dft_consts(n, ctx_len):
    n1 = n // BLOCK
    b = np.arange(BLOCK, dtype=np.int64)[:, None, None]
    k1 = np.arange(n1, dtype=np.int64)[None, :, None]
    a = np.arange(n1, dtype=np.int64)[None, None, :]
    ang = 2.0 * np.pi * ((a * k1 * BLOCK + b * k1) % n).astype(np.float64) / n
    m_ri = np.concatenate([np.cos(ang), -np.sin(ang)], axis=1)
    kk = np.arange(GROUP_DIM, dtype=np.int64)
    ang_c = 2.0 * np.pi * ((kk[:, None] * kk[None, :]) % GROUP_DIM) / GROUP_DIM
    c, s = np.cos(ang_c), np.sin(ang_c)
    cs = np.concatenate([np.concatenate([c, s], axis=1), np.concatenate([-s, c], axis=1)], axis=0)
    cc = np.concatenate([c, s], axis=0)
    kl = np.arange(ctx_len, dtype=np.int64)
    ang_l = 2.0 * np.pi * ((kl[:, None] * kl[None, :]) % ctx_len) / ctx_len
    wl = np.concatenate([np.cos(ang_l), -np.sin(ang_l)], axis=0)
    as_bf = lambda v: jnp.asarray(v, dtype=_F32).astype(_BF)
    return dict(m_ri=as_bf(m_ri), cs=as_bf(cs), cc=as_bf(cc), wl=as_bf(wl))


def _chan_dft(xr, xi, cc, scale):
    outs = []
    for g in range(xr.shape[1] // GROUP_DIM):
        sl = slice(g * GROUP_DIM, (g + 1) * GROUP_DIM)
        xx = jnp.concatenate([xr[:, sl], xi[:, sl]], axis=1).astype(_BF)
        outs.append(_dot(xx, cc) * scale)
    return jnp.concatenate(outs, axis=1)


def _pack_rows(lo, hi):
    a = lax.bitcast_convert_type(lo.astype(_BF).astype(_F32), jnp.uint32)
    b = lax.bitcast_convert_type(hi.astype(_BF).astype(_F32), jnp.uint32)
    return (a >> 16) | b


def _unpack_rows(p):
    lo = lax.bitcast_convert_type(p << 16, _F32).astype(_BF)
    hi = lax.bitcast_convert_type(p & jnp.uint32(0xFFFF0000), _F32).astype(_BF)
    return lo, hi


def _fa_kernel(f_ref, mlo_ref, mhi_ref, zr_ref, zi_ref):
    n1 = f_ref.shape[1]
    for t in range(FA_NB):
        lo, hi = _unpack_rows(f_ref[t])
        za = _dot(mlo_ref[t], lo)
        zb = _dot(mhi_ref[t], hi)
        zr_ref[:, t, :] = _pack_rows(za[:n1], zb[:n1])
        zi_ref[:, t, :] = _pack_rows(za[n1:], zb[n1:])


def _fb_kernel(zr_ref, zi_ref, cs_ref, cc_ref, o_ref, *, scale):
    cs, cc = cs_ref[...], cc_ref[...]
    for j in range(FB_NB):
        xx = _dot(cs, jnp.concatenate(_unpack_rows(zr_ref[j]) + _unpack_rows(zi_ref[j]), axis=0))
        o_ref[:, j, :] = _chan_dft(xx[:BLOCK], xx[BLOCK:], cc, scale).astype(_BF)


def _fourier_lat(f, consts, *, n_batch, n):
    fd = f.shape[2]
    n1 = n // BLOCK
    half = BLOCK // 2
    steps = half // FA_NB
    blk_f = pl.BlockSpec((FA_NB, n1, fd), lambda b, j: (b * steps + j, 0, 0))
    blk_a = pl.BlockSpec((n1, FA_NB, fd), lambda b, j: (b, j, 0))
    mat_lo = pl.BlockSpec((FA_NB, 2 * n1, n1), lambda b, j: (j, 0, 0))
    mat_hi = pl.BlockSpec((FA_NB, 2 * n1, n1), lambda b, j: (j + steps, 0, 0))
    z_shape = jax.ShapeDtypeStruct((n_batch * n1, half, fd), jnp.uint32)
    zr, zi = pl.pallas_call(
        _fa_kernel,
        grid=(n_batch, steps),
        in_specs=[blk_f, mat_lo, mat_hi],
        out_specs=[blk_a, blk_a],
        out_shape=[z_shape, z_shape],
        compiler_params=_params("arbitrary", "arbitrary"),
        name="fourier_seq_a",
    )(f, consts["m_ri"], consts["m_ri"])
    blk_z = pl.BlockSpec((FB_NB, half, fd), lambda b, j: (b * (n1 // FB_NB) + j, 0, 0))
    const = lambda shape: pl.BlockSpec(shape, lambda b, j: (0, 0))
    out = pl.pallas_call(
        functools.partial(_fb_kernel, scale=float((n * GROUP_DIM) ** -0.5)),
        grid=(n_batch, n1 // FB_NB),
        in_specs=[blk_z, blk_z, const((2 * BLOCK, 2 * BLOCK)), const((2 * GROUP_DIM, GROUP_DIM))],
        out_specs=pl.BlockSpec((BLOCK, FB_NB, fd), lambda b, j: (b, j, 0)),
        out_shape=jax.ShapeDtypeStruct((n_batch * BLOCK, n1, fd), _BF),
        compiler_params=_params("arbitrary", "arbitrary"),
        name="fourier_seq_b",
    )(zr, zi, consts["cs"], consts["cc"])
    return out.reshape(n_batch * n, fd)


def _fc_kernel(f_ref, wl_ref, cc_ref, o_ref, *, scale, ctx_len, n_batch):
    per = ctx_len // BLOCK
    for b in range(n_batch):
        rows = []
        for a in range(per):
            rows += list(_unpack_rows(f_ref[:, b * per + a, :]))
        xx = _dot(wl_ref[...], jnp.concatenate(rows, axis=0))
        o_ref[b * ctx_len:(b + 1) * ctx_len, :] = _chan_dft(xx[:ctx_len], xx[ctx_len:], cc_ref[...],
                                                           scale).astype(_BF)


def _fourier_ctx(f, consts, *, n_batch, ctx_len):
    fd = f.shape[2]
    blocks = n_batch * ctx_len // BLOCK
    const = lambda shape: pl.BlockSpec(shape, lambda i: (0, 0))
    return pl.pallas_call(
        functools.partial(_fc_kernel, scale=float((ctx_len * GROUP_DIM) ** -0.5), ctx_len=ctx_len, n_batch=n_batch),
        grid=(1,),
        in_specs=[pl.BlockSpec((BLOCK // 2, blocks, fd), lambda i: (n_batch, 0, 0)),
                  const((2 * ctx_len, ctx_len)), const((2 * GROUP_DIM, GROUP_DIM))],
        out_specs=pl.BlockSpec((n_batch * ctx_len, fd), lambda i: (0, 0)),
        out_shape=jax.ShapeDtypeStruct((n_batch * ctx_len, fd), _BF),
        compiler_params=_params("arbitrary"),
        name="fourier_ctx",
    )(f, consts["wl"], consts["cc"])


def _attn_core(sink_ref, q_ref, k, vt, o_ref, bias, s_scr, p_scr):
    tq = q_ref.shape[0]
    lane = lax.broadcasted_iota(jnp.int32, (tq, LANES), 1)
    low = lane < HEAD_DIM
    zero = jnp.zeros((tq, LANES), _BF)
    low_row = lax.broadcasted_iota(jnp.int32, (LANES, 2 * tq), 0) < HEAD_DIM
    first = lax.broadcasted_iota(jnp.int32, (1, 2 * tq), 1) < tq
    pairs = [(kvh, half) for kvh in range(N_KV_HEADS) for half in range(2)]
    sels = [slice(0, LANES) if half == kvh else slice(LANES, 2 * LANES) for kvh, half in pairs]
    for pi, (kvh, half) in enumerate(pairs):
        slabs = [q_ref[:, (2 * kvh + c) * LANES:(2 * kvh + c + 1) * LANES] for c in range(2)]
        keep = low if half == 0 else jnp.logical_not(low)
        qm = jnp.concatenate([jnp.where(keep, sl, zero) for sl in slabs], axis=0)
        s = lax.dot_general(k[:, sels[pi]], qm, (((1,), (1,)), ((), ())), preferred_element_type=_F32)
        s_scr[pi] = s if bias is None else s + bias
    dens = []
    for pi, (kvh, half) in enumerate(pairs):
        hq = Q_PER_KV * kvh + half
        sk = jnp.where(first, sink_ref[hq], sink_ref[hq + 2]) * LOG2E
        s = s_scr[pi]
        m = jnp.maximum(jnp.max(s, axis=0, keepdims=True), sk)
        p = jnp.exp2(s - m)
        dens.append(jnp.sum(p, axis=0, keepdims=True) + jnp.exp2(sk - m))
        p_scr[pi] = p.astype(_BF)
    outs = [_dot(vt[sels[pi], :], p_scr[pi]) * (1.0 / dens[pi]) for pi in range(len(pairs))]
    for kvh in range(N_KV_HEADS):
        ot = jnp.where(low_row, outs[2 * kvh], outs[2 * kvh + 1])
        for c in range(2):
            o_ref[:, (2 * kvh + c) * LANES:(2 * kvh + c + 1) * LANES] = ot[:, c * tq:(c + 1) * tq].T.astype(_BF)


def _attn_scratch(keys, tq, blocks=1):
    pairs = 2 * N_KV_HEADS * blocks
    return [pltpu.VMEM((pairs, keys, 2 * tq), _F32), pltpu.VMEM((pairs, keys, 2 * tq), _BF)]


def _attn_lat_kernel(sink_ref, q_ref, kp_ref, kc_ref, kn_ref, vp_ref, vc_ref, vn_ref, kx_ref, vx_ref,
                     bias_a_ref, bias_b_ref, o_ref, s_scr, p_scr):
    k_all = jnp.concatenate([kp_ref[...], kc_ref[...], kn_ref[...]], axis=0)
    vt_all = jnp.concatenate([vp_ref[...], vc_ref[...], vn_ref[...]], axis=1)
    pairs = 2 * N_KV_HEADS
    for blk, bias_ref in enumerate((bias_a_ref, bias_b_ref)):
        lo = blk * BLOCK
        k = jnp.concatenate([k_all[lo:lo + 3 * BLOCK], kx_ref[...]], axis=0)
        vt = jnp.concatenate([vt_all[:, lo:lo + 3 * BLOCK], vx_ref[...]], axis=1)
        rows = pl.ds(lo, BLOCK)
        _attn_core(sink_ref, q_ref.at[rows], k, vt, o_ref.at[rows], bias_ref[0],
                   s_scr.at[pl.ds(blk * pairs, pairs)], p_scr.at[pl.ds(blk * pairs, pairs)])


def _window_bias(ctx_len):
    s = np.arange(3 * BLOCK + ctx_len)[:, None]
    r = np.arange(2 * BLOCK)[None, :] % BLOCK
    out = []
    for v in range(4):
        lo = 0 if v & 1 else BLOCK
        hi = 3 * BLOCK if v & 2 else 2 * BLOCK
        ok = ((np.abs(s - BLOCK - r) <= WINDOW) & (s >= lo) & (s < hi)) | (s >= 3 * BLOCK)
        out.append(np.where(ok, 0.0, NEG_INF))
    return jnp.asarray(np.stack(out), dtype=_F32)


def _attn_ctx_kernel(sink_ref, q_ref, kx_ref, vx_ref, o_ref, s_scr, p_scr):
    _attn_core(sink_ref, q_ref, kx_ref[...], vx_ref[...], o_ref, None, s_scr, p_scr)


def _attn_lat(sink, q, k2, v2t, *, n_batch, n, ctx_len, t_lat):
    qd = q.shape[1]
    nb = n // BLOCK
    steps = nb // 2
    assert nb % 2 == 0
    cur = lambda b, i: (b * steps + i, 0)
    prev = lambda b, i: (b * nb + jnp.maximum(2 * i - 1, 0), 0)
    nxt = lambda b, i: (b * nb + jnp.minimum(2 * i + 2, nb - 1), 0)
    cx = lambda b, i: (t_lat // ctx_len + b, 0)
    swap = lambda im: (lambda b, i: im(b, i)[::-1])
    kb = lambda im: pl.BlockSpec((BLOCK, 2 * KV_DIM), im)
    vb = lambda im: pl.BlockSpec((2 * KV_DIM, BLOCK), swap(im))
    bias = _window_bias(ctx_len)
    bias_a = lambda b, i: ((i > 0).astype(jnp.int32) + 2, 0, 0)
    bias_b = lambda b, i: (1 + 2 * (i < steps - 1).astype(jnp.int32), 0, 0)
    bias_spec = lambda im: pl.BlockSpec((1,) + bias.shape[1:], im)
    return pl.pallas_call(
        _attn_lat_kernel,
        grid=(n_batch, steps),
        in_specs=[pl.BlockSpec(memory_space=pltpu.SMEM), pl.BlockSpec((2 * BLOCK, qd), cur),
                  kb(prev), pl.BlockSpec((2 * BLOCK, 2 * KV_DIM), cur), kb(nxt),
                  vb(prev), pl.BlockSpec((2 * KV_DIM, 2 * BLOCK), swap(cur)), vb(nxt),
                  pl.BlockSpec((ctx_len, 2 * KV_DIM), cx), pl.BlockSpec((2 * KV_DIM, ctx_len), swap(cx)),
                  bias_spec(bias_a), bias_spec(bias_b)],
        out_specs=pl.BlockSpec((2 * BLOCK, qd), cur),
        out_shape=jax.ShapeDtypeStruct((t_lat, qd), _BF),
        scratch_shapes=_attn_scratch(3 * BLOCK + ctx_len, BLOCK, blocks=2),
        compiler_params=_params("arbitrary", "arbitrary"),
        name="attn_lat",
    )(sink, q, k2, k2, k2, v2t, v2t, v2t, k2, v2t, bias, bias)


def _attn_ctx(sink, q, k2, v2t, *, n_batch, ctx_len, t_lat):
    qd = q.shape[1]
    cx = lambda b: (t_lat // ctx_len + b, 0)
    return pl.pallas_call(
        _attn_ctx_kernel,
        grid=(n_batch,),
        in_specs=[pl.BlockSpec(memory_space=pltpu.SMEM), pl.BlockSpec((ctx_len, qd), cx),
                  pl.BlockSpec((ctx_len, 2 * KV_DIM), cx),
                  pl.BlockSpec((2 * KV_DIM, ctx_len), lambda b: (0, t_lat // ctx_len + b))],
        out_specs=pl.BlockSpec((ctx_len, qd), lambda b: (b, 0)),
        out_shape=jax.ShapeDtypeStruct((n_batch * ctx_len, qd), _BF),
        scratch_shapes=_attn_scratch(ctx_len, ctx_len),
        compiler_params=_params("arbitrary"),
        name="attn_ctx",
    )(sink, q, k2, v2t)


def _pack_bf16_pairs(h):
    half = h.shape[1] // 2
    lo = lax.bitcast_convert_type(h[:, :half].astype(_BF).astype(_F32), jnp.uint32)
    hi = lax.bitcast_convert_type(h[:, half:].astype(_BF).astype(_F32), jnp.uint32)
    return (lo >> 16) | hi


def _unpack_bf16_pairs(p):
    lo = lax.bitcast_convert_type(p << 16, _F32)
    hi = lax.bitcast_convert_type(p & jnp.uint32(0xFFFF0000), _F32)
    return jnp.concatenate([lo, hi], axis=1).astype(_BF)


def _merge_kernel(x_ref, mod_ref, g1_ref, g2_ref, wg_ref, fml_ref, fmc_ref, atl_ref, atc_ref,
                  wfo_ref, wao_ref, wout_ref, rw_ref, rb_ref,
                  x1_ref, hp_ref, idx_ref, gate_ref, cnt_ref, h_scr, y_scr, *, nxt):
    i = pl.program_id(0)
    d = x_ref.shape[1]
    m = mod_ref[0]
    is_lat = i < nxt
    h_scr[...] = _norm_mod(x_ref[...], g1_ref[...], m[0:1], m[1:2]).astype(_BF)
    fm = jnp.where(is_lat, fml_ref[...], fmc_ref[...])
    at = jnp.where(is_lat, atl_ref[...], atc_ref[...])
    for c in range(d // MERGE_COLS):
        sl = slice(c * MERGE_COLS, (c + 1) * MERGE_COLS)
        sg = slice(d + c * MERGE_COLS, d + (c + 1) * MERGE_COLS)
        h = h_scr[...]
        y = (_sigmoid(_dot(h, wg_ref[:, sl])) * _dot(fm, wfo_ref[:, sl])
             + _sigmoid(_dot(h, wg_ref[:, sg])) * _dot(at, wao_ref[:, sl]))
        y_scr[:, sl] = y.astype(_BF)
    x1_ref[...] = x_ref[...] + m[2:3] * _dot(y_scr[...], wout_ref[...])
    h2 = _norm_mod(x1_ref[...], g2_ref[...], m[3:4], m[4:5])
    hp_ref[...] = _pack_bf16_pairs(h2)
    logits = _dot(h2.astype(_BF), rw_ref[...]) + rb_ref[...]
    lane = lax.broadcasted_iota(jnp.int32, logits.shape, 1).astype(_F32)
    vals, idxs = [], []
    for _ in range(TOP_K):
        mx = jnp.max(logits, axis=-1, keepdims=True)
        ix = jnp.min(jnp.where(logits == mx, lane, float(LANES)), axis=-1, keepdims=True)
        logits = jnp.where(lane == ix, NEG_INF, logits)
        vals.append(mx)
        idxs.append(ix)
    es = [jnp.exp(v - vals[0]) for v in vals]
    den = es[0] + es[1] + es[2] + es[3]
    idx4 = jnp.zeros(logits.shape, _F32)
    gate4 = jnp.zeros(logits.shape, _F32)
    sel = jnp.zeros(logits.shape, _F32)
    for k in range(TOP_K):
        idx4 = jnp.where(lane == float(k), idxs[k], idx4)
        gate4 = jnp.where(lane == float(k), es[k] / den, gate4)
        sel = sel + jnp.where(lane == idxs[k], 1.0, 0.0)
    idx_ref[...] = idx4.astype(jnp.int32)
    gate_ref[...] = gate4

    @pl.when(i == 0)
    def _():
        cnt_ref[...] = jnp.zeros(cnt_ref.shape, _F32)

    cnt_ref[0:1, :] += jnp.sum(sel, axis=0, keepdims=True)


def _merge(xa, mod, g1, g2, w_g, fm_lat, fm_ctx, at_lat, at_ctx, w_fo, w_ao, w_out, rw, rb,
           *, n, t_lat, n_batch):
    tt, d = xa.shape
    fd = fm_lat.shape[1]
    nt = tt // TM
    nxt = t_lat // TM
    row = lambda i: (i, 0)
    lat = lambda i: (jnp.minimum(i, nxt - 1), 0)
    cxt = lambda i: (jnp.maximum(i - nxt, 0), 0)
    const = lambda a: pl.BlockSpec(a.shape, lambda i: (0, 0))
    mod_idx = lambda i: (jnp.minimum(i * TM // n, n_batch), 0, 0)
    return pl.pallas_call(
        functools.partial(_merge_kernel, nxt=nxt),
        grid=(nt,),
        in_specs=[pl.BlockSpec((TM, d), row), pl.BlockSpec((1, N_MOD, d), mod_idx),
                  const(g1), const(g2), const(w_g),
                  pl.BlockSpec((TM, fd), lat), pl.BlockSpec((TM, fd), cxt),
                  pl.BlockSpec((TM, fd), lat), pl.BlockSpec((TM, fd), cxt),
                  const(w_fo), const(w_ao), const(w_out), const(rw), const(rb)],
        out_specs=[pl.BlockSpec((TM, d), row), pl.BlockSpec((TM, d // 2), row),
                   pl.BlockSpec((TM, LANES), row), pl.BlockSpec((TM, LANES), row),
                   pl.BlockSpec((8, LANES), lambda i: (0, 0))],
        out_shape=[jax.ShapeDtypeStruct((tt, d), _F32), jax.ShapeDtypeStruct((tt, d // 2), jnp.uint32),
                   jax.ShapeDtypeStruct((tt, LANES), jnp.int32), jax.ShapeDtypeStruct((tt, LANES), _F32),
                   jax.ShapeDtypeStruct((8, LANES), _F32)],
        scratch_shapes=[pltpu.VMEM((TM, d), _BF), pltpu.VMEM((TM, d), _BF)],
        compiler_params=_params("arbitrary"),
        name="merge_router",
    )(xa, mod, g1, g2, w_g, fm_lat, fm_ctx, at_lat, at_ctx, w_fo, w_ao, w_out, rw, rb)


def _route_kernel(idx_ref, ps_ref, d_ref, carry):
    i = pl.program_id(0)

    @pl.when(i == 0)
    def _():
        carry[...] = ps_ref[...]

    idx4 = idx_ref[...]
    lane = lax.broadcasted_iota(jnp.int32, idx4.shape, 1)
    cols = [idx4[:, k:k + 1] for k in range(TOP_K)]
    sel = jnp.zeros(idx4.shape, _F32)
    for k in range(TOP_K):
        sel = sel + jnp.where(lane == cols[k], 1.0, 0.0)
    r = lax.broadcasted_iota(jnp.int32, (TM, TM), 0)
    c = lax.broadcasted_iota(jnp.int32, (TM, TM), 1)
    tri = jnp.where(r > c, 1.0, 0.0).astype(_BF)
    slot = carry[...] + _dot(tri, sel.astype(_BF))
    dest4 = jnp.zeros(idx4.shape, jnp.int32)
    for k in range(TOP_K):
        dk = jnp.sum(jnp.where(lane == cols[k], slot, 0.0), axis=-1, keepdims=True)
        dest4 = jnp.where(lane == k, dk.astype(jnp.int32), dest4)
    d_ref[...] = dest4
    carry[...] += jnp.sum(sel, axis=0, keepdims=True)


def _route(idx4, pad_start):
    tt = idx4.shape[0]
    return pl.pallas_call(
        _route_kernel,
        grid=(tt // TM,),
        in_specs=[pl.BlockSpec((TM, LANES), lambda i: (i, 0)), pl.BlockSpec((1, LANES), lambda i: (0, 0))],
        out_specs=pl.BlockSpec((TM, LANES), lambda i: (i, 0)),
        out_shape=jax.ShapeDtypeStruct((tt, LANES), jnp.int32),
        scratch_shapes=[pltpu.VMEM((1, LANES), _F32)],
        compiler_params=_params("arbitrary"),
        name="route_slots",
    )(idx4, pad_start)


def _zero_tails_kernel(pend_ref, cnt_ref, nv_ref, xs_ref, zbuf, sem):
    n_blocks = xs_ref.shape[0] // TME
    zbuf[...] = jnp.zeros(zbuf.shape, zbuf.dtype)

    def zero_block(start):
        return pltpu.make_async_copy(zbuf, xs_ref.at[pl.ds(pl.multiple_of(start, TME), TME)], sem)

    for e in range(N_EXPERTS):
        @pl.when(cnt_ref[e] > 0)
        def _():
            zero_block(pend_ref[e] - TME).start()

    def start_unused(j, carry):
        zero_block(j * TME).start()
        return carry

    def wait_unused(j, carry):
        zero_block(j * TME).wait()
        return carry

    lax.fori_loop(nv_ref[0], n_blocks, start_unused, 0)
    for e in range(N_EXPERTS):
        @pl.when(cnt_ref[e] > 0)
        def _():
            zero_block(pend_ref[e] - TME).wait()
    lax.fori_loop(nv_ref[0], n_blocks, wait_unused, 0)


def _zero_tails(pad_end, cnt, n_valid, n_slots, w, dtype):
    grid_spec = pltpu.PrefetchScalarGridSpec(
        num_scalar_prefetch=3,
        grid=(1,),
        in_specs=[],
        out_specs=pl.BlockSpec(memory_space=pl.ANY),
        scratch_shapes=[pltpu.VMEM((TME, w), dtype), pltpu.SemaphoreType.DMA],
    )
    return pl.pallas_call(
        _zero_tails_kernel,
        grid_spec=grid_spec,
        out_shape=jax.ShapeDtypeStruct((n_slots, w), dtype),
        compiler_params=_params("arbitrary"),
        name="moe_zero_tails",
    )(pad_end, cnt, n_valid)


def _sc_scatter(rows, idx, base):
    tt, w = rows.shape
    win = SC_SCATTER_WINDOW
    workers = SC_CORES * SC_SUBCORES
    per_worker = tt // workers
    n_win = per_worker // win
    assert tt % workers == 0 and per_worker % win == 0 and n_win % 2 == 0
    mesh = plsc.VectorSubcoreMesh(core_axis_name="core", subcore_axis_name="subcore")
    out = jax.new_ref(base)
    idx_buf = pltpu.VMEM((win,), jnp.int32)
    row_buf = pltpu.VMEM((win, w), rows.dtype)

    @pl.kernel(out_type=(), mesh=mesh,
               scratch_types=[idx_buf] * (2 * TOP_K) + [row_buf, row_buf] + [pltpu.SemaphoreType.DMA] * 4)
    def scatter(r_hbm, i_hbm, o_hbm, *scratch):
        idx_a, idx_b = scratch[:TOP_K], scratch[TOP_K:2 * TOP_K]
        rows_a, rows_b, lsem_a, lsem_b, ssem_a, ssem_b = scratch[2 * TOP_K:]
        wid = lax.axis_index("subcore") * SC_CORES + lax.axis_index("core")
        first = wid * per_worker

        def load_start(c, rows_v, sem):
            pltpu.make_async_copy(r_hbm.at[pl.ds(first + c * win, win)], rows_v, sem).start()

        def load_wait(rows_v, sem):
            pltpu.make_async_copy(r_hbm.at[pl.ds(first, win)], rows_v, sem).wait()

        def scatter_start(c, rows_v, idxs, sem):
            for k in range(TOP_K):
                pltpu.sync_copy(i_hbm.at[pl.ds(k * tt + first + c * win, win)], idxs[k])
                pltpu.make_async_copy(rows_v, o_hbm.at[idxs[k]], sem).start()

        def scatter_wait(rows_v, idxs, sem):
            for k in range(TOP_K):
                pltpu.make_async_copy(rows_v, o_hbm.at[idxs[k]], sem).wait()

        load_start(0, rows_a, lsem_a)

        @pl.loop(0, n_win // 2)
        def _(i):
            c = 2 * i

            @pl.when(i > 0)
            def _():
                scatter_wait(rows_b, idx_b, ssem_b)

            load_start(c + 1, rows_b, lsem_b)
            load_wait(rows_a, lsem_a)
            scatter_start(c, rows_a, idx_a, ssem_a)
            scatter_wait(rows_a, idx_a, ssem_a)

            @pl.when(c + 2 < n_win)
            def _():
                load_start(c + 2, rows_a, lsem_a)

            load_wait(rows_b, lsem_b)
            scatter_start(c + 1, rows_b, idx_b, ssem_b)

        scatter_wait(rows_b, idx_b, ssem_b)

    scatter(rows, idx, out)
    return jax.freeze(out)


def _moe_kernel(be_ref, nv_ref, xs_ref, wgu_ref, bgu_ref, wd_ref, bd_ref, y_ref, wgu_bf, wd_bf):
    i = pl.program_id(0)
    de = wd_ref.shape[2]
    valid = i < nv_ref[0]
    new_expert = jnp.logical_or(i == 0, be_ref[i] != be_ref[jnp.maximum(i - 1, 0)])

    @pl.when(jnp.logical_and(valid, new_expert))
    def _():
        def cast_rows(ref, out, rows):
            def body(j, carry):
                sl = pl.ds(pl.multiple_of(j * CAST_ROWS, CAST_ROWS), CAST_ROWS)
                out[sl, :] = ref[0, 0, sl, :].astype(_BF)
                return carry
            lax.fori_loop(0, rows // CAST_ROWS, body, 0)
        cast_rows(wgu_ref, wgu_bf, wgu_ref.shape[2])
        cast_rows(wd_ref, wd_bf, de)

    @pl.when(valid)
    def _():
        xb = _unpack_bf16_pairs(xs_ref[...])
        gu = _dot(xb, wgu_bf[...]) + bgu_ref[0, 0]
        a = jnp.minimum(gu[:, :de], SWIGLU_LIMIT)
        u = jnp.clip(gu[:, de:], -SWIGLU_LIMIT, SWIGLU_LIMIT)
        act = a * _sigmoid(SWIGLU_ALPHA * a) * (u + 1)
        y_ref[...] = _pack_bf16_pairs(_dot(act.astype(_BF), wd_bf[...]) + bd_ref[0, 0])

    @pl.when(i >= nv_ref[0])
    def _():
        y_ref[...] = jnp.zeros(y_ref.shape, y_ref.dtype)


def _moe(block_e, n_valid, xs, w_gu, b_gu, w_down, b_down, *, layer):
    n_slots, w = xs.shape
    depth, ne, d, de2 = w_gu.shape
    de = de2 // 2
    n_blocks = n_slots // TME
    ex = lambda i, be, nv: (layer, be[i], 0, 0)
    grid_spec = pltpu.PrefetchScalarGridSpec(
        num_scalar_prefetch=2,
        grid=(n_blocks,),
        in_specs=[pl.BlockSpec((TME, w), lambda i, be, nv: (jnp.minimum(i, nv[0] - 1), 0)),
                  pl.BlockSpec((1, 1, d, de2), ex), pl.BlockSpec((1, 1, 1, de2), ex),
                  pl.BlockSpec((1, 1, de, d), ex), pl.BlockSpec((1, 1, 1, d), ex)],
        out_specs=pl.BlockSpec((TME, d // 2), lambda i, be, nv: (i, 0)),
        scratch_shapes=[pltpu.VMEM((d, de2), _BF), pltpu.VMEM((de, d), _BF)],
    )
    return pl.pallas_call(
        _moe_kernel,
        grid_spec=grid_spec,
        out_shape=jax.ShapeDtypeStruct((n_slots, d // 2), jnp.uint32),
        compiler_params=_params("arbitrary"),
        name="moe_experts",
    )(block_e, n_valid, xs, w_gu, b_gu.reshape(depth, ne, 1, de2), w_down, b_down.reshape(depth, ne, 1, d))


def _sc_gather(table, idx):
    n_idx, (_, w) = idx.shape[0], table.shape
    workers = SC_CORES * SC_SUBCORES
    per_worker = n_idx // workers
    n_win = per_worker // SC_WINDOW
    assert n_idx % workers == 0 and per_worker % SC_WINDOW == 0 and n_win % 2 == 0
    mesh = plsc.VectorSubcoreMesh(core_axis_name="core", subcore_axis_name="subcore")
    idx_buf = pltpu.VMEM((SC_WINDOW,), jnp.int32)
    row_buf = pltpu.VMEM((SC_WINDOW, w), table.dtype)

    @pl.kernel(out_type=jax.ShapeDtypeStruct((n_idx, w), table.dtype), mesh=mesh,
               scratch_types=[idx_buf, idx_buf, row_buf, row_buf] + [pltpu.SemaphoreType.DMA] * 4)
    def gather(t_hbm, i_hbm, o_hbm, idx_a, idx_b, rows_a, rows_b, gsem_a, gsem_b, ssem_a, ssem_b):
        wid = lax.axis_index("subcore") * SC_CORES + lax.axis_index("core")
        first = wid * per_worker

        def gather_start(c, idx_v, rows_v, sem):
            pltpu.sync_copy(i_hbm.at[pl.ds(first + c * SC_WINDOW, SC_WINDOW)], idx_v)
            pltpu.make_async_copy(t_hbm.at[idx_v], rows_v, sem).start()

        def gather_wait(idx_v, rows_v, sem):
            pltpu.make_async_copy(t_hbm.at[idx_v], rows_v, sem).wait()

        def store_start(c, rows_v, sem):
            pltpu.make_async_copy(rows_v, o_hbm.at[pl.ds(first + c * SC_WINDOW, SC_WINDOW)], sem).start()

        def store_wait(rows_v, sem):
            pltpu.make_async_copy(rows_v, o_hbm.at[pl.ds(first, SC_WINDOW)], sem).wait()

        gather_start(0, idx_a, rows_a, gsem_a)

        @pl.loop(0, n_win // 2)
        def _(i):
            c = 2 * i

            @pl.when(i > 0)
            def _():
                store_wait(rows_b, ssem_b)

            gather_start(c + 1, idx_b, rows_b, gsem_b)
            gather_wait(idx_a, rows_a, gsem_a)
            store_start(c, rows_a, ssem_a)
            store_wait(rows_a, ssem_a)

            @pl.when(c + 2 < n_win)
            def _():
                gather_start(c + 2, idx_a, rows_a, gsem_a)

            gather_wait(idx_b, rows_b, gsem_b)
            store_start(c + 1, rows_b, ssem_b)

        store_wait(rows_b, ssem_b)

    return gather(table, idx)


def _combine_kernel(yg_ref, gate_ref, x_ref, mod_ref, fg_ref, o_ref, *, final):
    g = gate_ref[...]
    half = x_ref.shape[1] // 2
    lo = jnp.zeros((x_ref.shape[0], half), _F32)
    hi = jnp.zeros((x_ref.shape[0], half), _F32)
    for k in range(TOP_K):
        p = yg_ref[k]
        lo = lo + g[:, k:k + 1] * lax.bitcast_convert_type(p << 16, _F32)
        hi = hi + g[:, k:k + 1] * lax.bitcast_convert_type(p & jnp.uint32(0xFFFF0000), _F32)
    x2 = x_ref[...] + mod_ref[0][5:6] * jnp.concatenate([lo, hi], axis=1)
    if final:
        ms = jnp.mean(x2 * x2, axis=-1, keepdims=True)
        x2 = x2 * lax.rsqrt(ms + EPS) * fg_ref[...]
    o_ref[...] = x2


def _combine(yg, gate4, x1, mod, fg, *, n, n_batch, rows, final):
    d = x1.shape[1]
    row = lambda i: (i, 0)
    mod_idx = lambda i: (jnp.minimum(i * TMC // n, n_batch), 0, 0)
    return pl.pallas_call(
        functools.partial(_combine_kernel, final=final),
        grid=(rows // TMC,),
        in_specs=[pl.BlockSpec((TOP_K, TMC, d // 2), lambda i: (0, i, 0)),
                  pl.BlockSpec((TMC, LANES), row), pl.BlockSpec((TMC, d), row),
                  pl.BlockSpec((1, N_MOD, d), mod_idx), pl.BlockSpec((1, d), lambda i: (0, 0))],
        out_specs=pl.BlockSpec((TMC, d), row),
        out_shape=jax.ShapeDtypeStruct((rows, d), _F32),
        compiler_params=_params("arbitrary"),
        name="moe_combine",
    )(yg, gate4, x1, mod, fg)


def _routing_tables(counts, n_blocks):
    cnt = counts[0, :N_EXPERTS].astype(jnp.int32)
    padded = (cnt + TME - 1) // TME * TME
    pad_end = jnp.cumsum(padded)
    pad_start = pad_end - padded
    ps = jnp.zeros((1, LANES), _F32).at[0, :N_EXPERTS].set(pad_start.astype(_F32))
    blk = jnp.arange(n_blocks, dtype=jnp.int32) * TME
    block_e = jnp.sum((pad_end[None, :] <= blk[:, None]).astype(jnp.int32), axis=1)
    e_last = jnp.max(jnp.where(cnt > 0, jnp.arange(N_EXPERTS, dtype=jnp.int32), 0))
    block_e = jnp.minimum(block_e, e_last).astype(jnp.int32)
    n_valid = (pad_end[-1] // TME).astype(jnp.int32).reshape(1)
    return ps, block_e, n_valid, pad_end.astype(jnp.int32), cnt


def kernel(x, c, ctx, c_ctx, ada_w, ada_b, norm1_g, norm2_g, w_in, attn_sink, w_fourier_out, w_attn_out,
           w_out, router_w, router_b, expert_w_gu, expert_b_gu, expert_w_down, expert_b_down, final_norm_g):
    n_batch, n, d = x.shape
    ctx_len = ctx.shape[1]
    depth = ada_w.shape[0]
    t_lat = n_batch * n
    t_ctx = n_batch * ctx_len
    tt = t_lat + t_ctx
    fd = qd = d // 2
    assert n % TM == 0 and t_ctx == TM and n % (BLOCK * FB_NB) == 0 and n_batch < MOD_ROWS
    assert t_lat % ctx_len == 0 and TM % ctx_len == 0

    xa = jnp.concatenate([x.reshape(t_lat, d), ctx.reshape(t_ctx, d)], axis=0)
    cc = jnp.concatenate([c, c_ctx[None, :], jnp.zeros((MOD_ROWS - n_batch - 1, d), _F32)], axis=0)
    mod = _ada(cc, ada_w, ada_b).reshape(depth, MOD_ROWS, N_MOD, d)
    cos_t, sin_t = _rope_tables(n)
    consts = _dft_consts(n, ctx_len)

    n_in = fd + qd + 2 * KV_DIM
    w_a = w_in[:, :, :n_in].astype(_BF)
    w_g = w_in[:, :, n_in:].astype(_BF)
    w_fo, w_ao, w_o = w_fourier_out.astype(_BF), w_attn_out.astype(_BF), w_out.astype(_BF)
    rw = jnp.pad(router_w, ((0, 0), (0, 0), (0, LANES - N_EXPERTS))).astype(_BF)
    rb = jnp.pad(router_b, ((0, 0), (0, LANES - N_EXPERTS)), constant_values=NEG_INF)[:, None, :]

    n_blocks = -(-(tt * TOP_K + N_EXPERTS * (TME - 1)) // TME)
    dims = dict(n=n, t_lat=t_lat, n_batch=n_batch)
    for l in range(depth):
        last = l == depth - 1
        f, q, k2, v2 = _proj(xa, mod[l], norm1_g[l][None, :], w_a[l], cos_t, sin_t, **dims)
        fm_lat = _fourier_lat(f, consts, n_batch=n_batch, n=n)
        fm_ctx = _fourier_ctx(f, consts, n_batch=n_batch, ctx_len=ctx_len)
        at_lat = _attn_lat(attn_sink[l], q, k2, v2, n_batch=n_batch, n=n, ctx_len=ctx_len, t_lat=t_lat)
        at_ctx = _attn_ctx(attn_sink[l], q, k2, v2, n_batch=n_batch, ctx_len=ctx_len, t_lat=t_lat)
        x1, hp, idx4, gate4, counts = _merge(
            xa, mod[l], norm1_g[l][None, :], norm2_g[l][None, :], w_g[l], fm_lat, fm_ctx, at_lat, at_ctx,
            w_fo[l], w_ao[l], w_o[l], rw[l], rb[l], **dims)
        pad_start, block_e, n_valid, pad_end, cnt = _routing_tables(counts, n_blocks)
        dest = _route(idx4, pad_start)[:, :TOP_K].T.reshape(-1)
        xs = _sc_scatter(hp, dest, _zero_tails(pad_end, cnt, n_valid, n_blocks * TME, hp.shape[1], hp.dtype))
        y = _moe(block_e, n_valid, xs, expert_w_gu, expert_b_gu, expert_w_down, expert_b_down, layer=l)
        yg = _sc_gather(y, dest).reshape(TOP_K, tt, d // 2)
        xa = _combine(yg, gate4, x1, mod[l], final_norm_g[None, :], n=n, n_batch=n_batch,
                      rows=t_lat if last else tt, final=last)
    return xa.reshape(n_batch, n, d)
```

```python
import functools

import numpy as np
import jax
import jax.numpy as jnp
from jax import lax
from jax.experimental import pallas as pl
from jax.experimental.pallas import tpu as pltpu
from jax.experimental.pallas import tpu_sc as plsc

GRID_W = 64
HEAD_DIM = 64
N_KV_HEADS = 2
Q_PER_KV = 4
KV_DIM = N_KV_HEADS * HEAD_DIM
WINDOW = 128
BLOCK = 128
ROPE_THETA = 10000.0
ROPE_FREQS = HEAD_DIM // 4
GROUP_DIM = 128
N_EXPERTS = 32
TOP_K = 4
SWIGLU_LIMIT = 7.0
SWIGLU_ALPHA = 1.702
N_MOD = 6
EPS = 1e-5
NEG_INF = -1e30
LOG2E = 1.4426950408889634
Q_SCALE = HEAD_DIM ** -0.5 * LOG2E

LANES = 128
MOD_ROWS = 8
TM = 1024
TME = 512
TMC = 512
SC_CORES = 2
SC_SUBCORES = 16
SC_WINDOW = 96
SC_SCATTER_WINDOW = 88
FA_NB = 16
FB_NB = 16
CAST_ROWS = 64
MERGE_COLS = 256
ATTN_BLOCKS = 4
VMEM_LIMIT = 56 * 1024 * 1024

_BF = jnp.bfloat16
_F32 = jnp.float32


def _params(*sem):
    return pltpu.CompilerParams(dimension_semantics=sem, vmem_limit_bytes=VMEM_LIMIT)


def _dot(a, b):
    return jnp.dot(a, b, preferred_element_type=_F32)


def _sigmoid(x):
    return 0.5 * jnp.tanh(0.5 * x) + 0.5


def _norm_mod(x, g, shift, scale):
    ms = jnp.mean(x * x, axis=-1, keepdims=True)
    return x * lax.rsqrt(ms + EPS) * (g * (1 + scale)) + shift


def _ada_kernel(c_ref, w_ref, b_ref, o_ref):
    c = c_ref[...]
    s = c * jax.nn.sigmoid(c)
    o_ref[0] = jnp.dot(s, w_ref[0], precision=lax.Precision.HIGHEST,
                       preferred_element_type=_F32) + b_ref[0]


def _ada(cc, ada_w, ada_b):
    depth, d, nd = ada_w.shape
    tn = nd // 4
    return pl.pallas_call(
        _ada_kernel,
        grid=(depth, nd // tn),
        in_specs=[pl.BlockSpec((MOD_ROWS, d), lambda l, j: (0, 0)),
                  pl.BlockSpec((1, d, tn), lambda l, j: (l, 0, j)),
                  pl.BlockSpec((1, 1, tn), lambda l, j: (l, 0, j))],
        out_specs=pl.BlockSpec((1, MOD_ROWS, tn), lambda l, j: (l, 0, j)),
        out_shape=jax.ShapeDtypeStruct((depth, MOD_ROWS, nd), _F32),
        compiler_params=_params("arbitrary", "arbitrary"),
        name="ada_mod",
    )(cc, ada_w, ada_b.reshape(depth, 1, nd))


def _rope(v, cos, sin):
    lane = lax.broadcasted_iota(jnp.int32, v.shape, 1)
    partner = jnp.where((lane & ROPE_FREQS) == 0,
                        pltpu.roll(v, LANES - ROPE_FREQS, 1), pltpu.roll(v, ROPE_FREQS, 1))
    return v * cos + partner * sin


def _proj_kernel(x_ref, mod_ref, g_ref, w_ref, cos_ref, sin_ref, f_ref, q_ref, k_ref, vt_ref, *, fd, qd):
    m = mod_ref[0]
    h = _norm_mod(x_ref[...], g_ref[...], m[0:1], m[1:2]).astype(_BF)
    p = _dot(h, w_ref[...])
    cos = cos_ref[...]
    sin = sin_ref[...]
    half = BLOCK // 2
    for a in range(p.shape[0] // BLOCK):
        f_ref[:, a, :] = _pack_rows(p[a * BLOCK:a * BLOCK + half, :fd], p[a * BLOCK + half:(a + 1) * BLOCK, :fd])
    for j in range(qd // LANES):
        lo = fd + j * LANES
        q_ref[:, j * LANES:(j + 1) * LANES] = (
            _rope(p[:, lo:lo + LANES], cos, sin) * Q_SCALE).astype(_BF)
    k = _rope(p[:, fd + qd:fd + qd + KV_DIM], cos, sin)
    v = p[:, fd + qd + KV_DIM:fd + qd + 2 * KV_DIM]
    k_ref[:, :KV_DIM] = k.astype(_BF)
    k_ref[:, KV_DIM:] = pltpu.roll(k, HEAD_DIM, 1).astype(_BF)
    vt_ref[:KV_DIM, :] = v.T.astype(_BF)
    vt_ref[KV_DIM:, :] = pltpu.roll(v, HEAD_DIM, 1).T.astype(_BF)


def _proj(xa, mod, g, w_a, cos_t, sin_t, *, n, t_lat, n_batch):
    tt, d = xa.shape
    fd = qd = d // 2
    nt = tt // TM
    nxt = t_lat // TM
    per_seq = n // TM
    mod_idx = lambda i: (jnp.minimum(i * TM // n, n_batch), 0, 0)
    rope_idx = lambda i: (jnp.where(i < nxt, i % per_seq, per_seq), 0)
    f_idx = lambda i: (jnp.where(i < nxt, i // per_seq, n_batch), jnp.where(i < nxt, i % per_seq, 0), 0)
    row = lambda i: (i, 0)
    return pl.pallas_call(
        functools.partial(_proj_kernel, fd=fd, qd=qd),
        grid=(nt,),
        in_specs=[pl.BlockSpec((TM, d), row),
                  pl.BlockSpec((1, N_MOD, d), mod_idx),
                  pl.BlockSpec((1, d), lambda i: (0, 0)),
                  pl.BlockSpec(w_a.shape, lambda i: (0, 0)),
                  pl.BlockSpec((TM, LANES), rope_idx),
                  pl.BlockSpec((TM, LANES), rope_idx)],
        out_specs=[pl.BlockSpec((BLOCK // 2, TM // BLOCK, fd), f_idx), pl.BlockSpec((TM, qd), row),
                   pl.BlockSpec((TM, 2 * KV_DIM), row), pl.BlockSpec((2 * KV_DIM, TM), lambda i: (0, i))],
        out_shape=[jax.ShapeDtypeStruct(((n_batch + 1) * (BLOCK // 2), n // BLOCK, fd), jnp.uint32),
                   jax.ShapeDtypeStruct((tt, qd), _BF),
                   jax.ShapeDtypeStruct((tt, 2 * KV_DIM), _BF), jax.ShapeDtypeStruct((2 * KV_DIM, tt), _BF)],
        compiler_params=_params("arbitrary"),
        name="proj_in",
    )(xa, mod, g, w_a, cos_t, sin_t)


def _rope_tables(n):
    pos = jnp.arange(n)
    inv = ROPE_THETA ** (-jnp.arange(ROPE_FREQS, dtype=_F32) / ROPE_FREQS)
    ar = (pos // GRID_W).astype(_F32)[:, None] * inv
    ac = (pos % GRID_W).astype(_F32)[:, None] * inv
    cos = jnp.concatenate([jnp.cos(ar), jnp.cos(ar), jnp.cos(ac), jnp.cos(ac)], axis=1)
    sin = jnp.concatenate([-jnp.sin(ar), jnp.sin(ar), -jnp.sin(ac), jnp.sin(ac)], axis=1)
    cos = jnp.tile(cos, (1, LANES // HEAD_DIM))
    sin = jnp.tile(sin, (1, LANES // HEAD_DIM))
    cos = jnp.concatenate([cos, jnp.ones((TM, LANES), _F32)], axis=0)
    sin = jnp.concatenate([sin, jnp.zeros((TM, LANES), _F32)], axis=0)
    return cos, sin


def _dft_consts(n, ctx_len):
    n1 = n // BLOCK
    b = np.arange(BLOCK, dtype=np.int64)[:, None, None]
    k1 = np.arange(n1, dtype=np.int64)[None, :, None]
    a = np.arange(n1, dtype=np.int64)[None, None, :]
    ang = 2.0 * np.pi * ((a * k1 * BLOCK + b * k1) % n).astype(np.float64) / n
    m_ri = np.concatenate([np.cos(ang), -np.sin(ang)], axis=1)
    kk = np.arange(GROUP_DIM, dtype=np.int64)
    ang_c = 2.0 * np.pi * ((kk[:, None] * kk[None, :]) % GROUP_DIM) / GROUP_DIM
    c, s = np.cos(ang_c), np.sin(ang_c)
    cs = np.concatenate([np.concatenate([c, s], axis=1), np.concatenate([-s, c], axis=1)], axis=0)
    cc = np.concatenate([c, s], axis=0)
    kl = np.arange(ctx_len, dtype=np.int64)
    ang_l = 2.0 * np.pi * ((kl[:, None] * kl[None, :]) % ctx_len) / ctx_len
    wl = np.concatenate([np.cos(ang_l), -np.sin(ang_l)], axis=0)
    as_bf = lambda v: jnp.asarray(v, dtype=_F32).astype(_BF)
    return dict(m_ri=as_bf(m_ri), cs=as_bf(cs), cc=as_bf(cc), wl=as_bf(wl))


def _chan_dft(xr, xi, cc, scale):
    outs = []
    for g in range(xr.shape[1] // GROUP_DIM):
        sl = slice(g * GROUP_DIM, (g + 1) * GROUP_DIM)
        xx = jnp.concatenate([xr[:, sl], xi[:, sl]], axis=1).astype(_BF)
        outs.append(_dot(xx, cc) * scale)
    return jnp.concatenate(outs, axis=1)


def _pack_rows(lo, hi):
    a = lax.bitcast_convert_type(lo.astype(_BF).astype(_F32), jnp.uint32)
    b = lax.bitcast_convert_type(hi.astype(_BF).astype(_F32), jnp.uint32)
    return (a >> 16) | b


def _unpack_rows(p):
    lo = lax.bitcast_convert_type(p << 16, _F32).astype(_BF)
    hi = lax.bitcast_convert_type(p & jnp.uint32(0xFFFF0000), _F32).astype(_BF)
    return lo, hi


def _fa_kernel(f_ref, mlo_ref, mhi_ref, zr_ref, zi_ref):
    n1 = f_ref.shape[1]
    for t in range(FA_NB):
        lo, hi = _unpack_rows(f_ref[t])
        za = _dot(mlo_ref[t], lo)
        zb = _dot(mhi_ref[t], hi)
        zr_ref[:, t, :] = _pack_rows(za[:n1], zb[:n1])
        zi_ref[:, t, :] = _pack_rows(za[n1:], zb[n1:])


def _fb_kernel(zr_ref, zi_ref, cs_ref, cc_ref, o_ref, *, scale):
    cs, cc = cs_ref[...], cc_ref[...]
    for j in range(FB_NB):
        xx = _dot(cs, jnp.concatenate(_unpack_rows(zr_ref[j]) + _unpack_rows(zi_ref[j]), axis=0))
        o_ref[:, j, :] = _chan_dft(xx[:BLOCK], xx[BLOCK:], cc, scale).astype(_BF)


def _fourier_lat(f, consts, *, n_batch, n):
    fd = f.shape[2]
    n1 = n // BLOCK
    half = BLOCK // 2
    steps = half // FA_NB
    blk_f = pl.BlockSpec((FA_NB, n1, fd), lambda b, j: (b * steps + j, 0, 0))
    blk_a = pl.BlockSpec((n1, FA_NB, fd), lambda b, j: (b, j, 0))
    mat_lo = pl.BlockSpec((FA_NB, 2 * n1, n1), lambda b, j: (j, 0, 0))
    mat_hi = pl.BlockSpec((FA_NB, 2 * n1, n1), lambda b, j: (j + steps, 0, 0))
    z_shape = jax.ShapeDtypeStruct((n_batch * n1, half, fd), jnp.uint32)
    zr, zi = pl.pallas_call(
        _fa_kernel,
        grid=(n_batch, steps),
        in_specs=[blk_f, mat_lo, mat_hi],
        out_specs=[blk_a, blk_a],
        out_shape=[z_shape, z_shape],
        compiler_params=_params("arbitrary", "arbitrary"),
        name="fourier_seq_a",
    )(f, consts["m_ri"], consts["m_ri"])
    blk_z = pl.BlockSpec((FB_NB, half, fd), lambda b, j: (b * (n1 // FB_NB) + j, 0, 0))
    const = lambda shape: pl.BlockSpec(shape, lambda b, j: (0, 0))
    out = pl.pallas_call(
        functools.partial(_fb_kernel, scale=float((n * GROUP_DIM) ** -0.5)),
        grid=(n_batch, n1 // FB_NB),
        in_specs=[blk_z, blk_z, const((2 * BLOCK, 2 * BLOCK)), const((2 * GROUP_DIM, GROUP_DIM))],
        out_specs=pl.BlockSpec((BLOCK, FB_NB, fd), lambda b, j: (b, j, 0)),
        out_shape=jax.ShapeDtypeStruct((n_batch * BLOCK, n1, fd), _BF),
        compiler_params=_params("arbitrary", "arbitrary"),
        name="fourier_seq_b",
    )(zr, zi, consts["cs"], consts["cc"])
    return out.reshape(n_batch * n, fd)


def _fc_kernel(f_ref, wl_ref, cc_ref, o_ref, *, scale, ctx_len, n_batch):
    per = ctx_len // BLOCK
    for b in range(n_batch):
        rows = []
        for a in range(per):
            rows += list(_unpack_rows(f_ref[:, b * per + a, :]))
        xx = _dot(wl_ref[...], jnp.concatenate(rows, axis=0))
        o_ref[b * ctx_len:(b + 1) * ctx_len, :] = _chan_dft(xx[:ctx_len], xx[ctx_len:], cc_ref[...],
                                                           scale).astype(_BF)


def _fourier_ctx(f, consts, *, n_batch, ctx_len):
    fd = f.shape[2]
    blocks = n_batch * ctx_len // BLOCK
    const = lambda shape: pl.BlockSpec(shape, lambda i: (0, 0))
    return pl.pallas_call(
        functools.partial(_fc_kernel, scale=float((ctx_len * GROUP_DIM) ** -0.5), ctx_len=ctx_len, n_batch=n_batch),
        grid=(1,),
        in_specs=[pl.BlockSpec((BLOCK // 2, blocks, fd), lambda i: (n_batch, 0, 0)),
                  const((2 * ctx_len, ctx_len)), const((2 * GROUP_DIM, GROUP_DIM))],
        out_specs=pl.BlockSpec((n_batch * ctx_len, fd), lambda i: (0, 0)),
        out_shape=jax.ShapeDtypeStruct((n_batch * ctx_len, fd), _BF),
        compiler_params=_params("arbitrary"),
        name="fourier_ctx",
    )(f, consts["wl"], consts["cc"])


def _attn_core(sink_ref, q_ref, k, vt, o_ref, bias, s_scr, p_scr):
    tq = q_ref.shape[0]
    lane = lax.broadcasted_iota(jnp.int32, (tq, LANES), 1)
    low = lane < HEAD_DIM
    zero = jnp.zeros((tq, LANES), _BF)
    low_row = lax.broadcasted_iota(jnp.int32, (LANES, 2 * tq), 0) < HEAD_DIM
    first = lax.broadcasted_iota(jnp.int32, (1, 2 * tq), 1) < tq
    pairs = [(kvh, half) for kvh in range(N_KV_HEADS) for half in range(2)]
    sels = [slice(0, LANES) if half == kvh else slice(LANES, 2 * LANES) for kvh, half in pairs]
    for pi, (kvh, half) in enumerate(pairs):
        slabs = [q_ref[:, (2 * kvh + c) * LANES:(2 * kvh + c + 1) * LANES] for c in range(2)]
        keep = low if half == 0 else jnp.logical_not(low)
        qm = jnp.concatenate([jnp.where(keep, sl, zero) for sl in slabs], axis=0)
        s = lax.dot_general(k[:, sels[pi]], qm, (((1,), (1,)), ((), ())), preferred_element_type=_F32)
        s_scr[pi] = s if bias is None else s + bias
    dens = []
    for pi, (kvh, half) in enumerate(pairs):
        hq = Q_PER_KV * kvh + half
        sk = jnp.where(first, sink_ref[hq], sink_ref[hq + 2]) * LOG2E
        s = s_scr[pi]
        m = jnp.maximum(jnp.max(s, axis=0, keepdims=True), sk)
        p = jnp.exp2(s - m)
        dens.append(jnp.sum(p, axis=0, keepdims=True) + jnp.exp2(sk - m))
        p_scr[pi] = p.astype(_BF)
    outs = [_dot(vt[sels[pi], :], p_scr[pi]) * (1.0 / dens[pi]) for pi in range(len(pairs))]
    for kvh in range(N_KV_HEADS):
        ot = jnp.where(low_row, outs[2 * kvh], outs[2 * kvh + 1])
        for c in range(2):
            o_ref[:, (2 * kvh + c) * LANES:(2 * kvh + c + 1) * LANES] = ot[:, c * tq:(c + 1) * tq].T.astype(_BF)


def _attn_scratch(keys, tq, blocks=1):
    pairs = 2 * N_KV_HEADS * blocks
    return [pltpu.VMEM((pairs, keys, 2 * tq), _F32), pltpu.VMEM((pairs, keys, 2 * tq), _BF)]


def _attn_lat_kernel(sink_ref, q_ref, kp_ref, kc_ref, kn_ref, vp_ref, vc_ref, vn_ref, kx_ref, vx_ref,
                     bias_a_ref, bias_m_ref, bias_b_ref, o_ref, s_scr, p_scr):
    k_all = jnp.concatenate([kp_ref[...], kc_ref[...], kn_ref[...]], axis=0)
    vt_all = jnp.concatenate([vp_ref[...], vc_ref[...], vn_ref[...]], axis=1)
    pairs = 2 * N_KV_HEADS
    for blk in range(ATTN_BLOCKS):
        bias_ref = bias_a_ref if blk == 0 else (bias_b_ref if blk == ATTN_BLOCKS - 1 else bias_m_ref)
        lo = blk * BLOCK
        k = jnp.concatenate([k_all[lo:lo + 3 * BLOCK], kx_ref[...]], axis=0)
        vt = jnp.concatenate([vt_all[:, lo:lo + 3 * BLOCK], vx_ref[...]], axis=1)
        rows = pl.ds(lo, BLOCK)
        _attn_core(sink_ref, q_ref.at[rows], k, vt, o_ref.at[rows], bias_ref[0],
                   s_scr.at[pl.ds(blk * pairs, pairs)], p_scr.at[pl.ds(blk * pairs, pairs)])


def _window_bias(ctx_len):
    s = np.arange(3 * BLOCK + ctx_len)[:, None]
    r = np.arange(2 * BLOCK)[None, :] % BLOCK
    out = []
    for v in range(4):
        lo = 0 if v & 1 else BLOCK
        hi = 3 * BLOCK if v & 2 else 2 * BLOCK
        ok = ((np.abs(s - BLOCK - r) <= WINDOW) & (s >= lo) & (s < hi)) | (s >= 3 * BLOCK)
        out.append(np.where(ok, 0.0, NEG_INF))
    return jnp.asarray(np.stack(out), dtype=_F32)


def _attn_ctx_kernel(sink_ref, q_ref, kx_ref, vx_ref, o_ref, s_scr, p_scr):
    _attn_core(sink_ref, q_ref, kx_ref[...], vx_ref[...], o_ref, None, s_scr, p_scr)


def _attn_lat(sink, q, k2, v2t, *, n_batch, n, ctx_len, t_lat):
    qd = q.shape[1]
    nb = n // BLOCK
    ab = ATTN_BLOCKS
    steps = nb // ab
    assert nb % ab == 0
    cur = lambda b, i: (b * steps + i, 0)
    prev = lambda b, i: (b * nb + jnp.maximum(ab * i - 1, 0), 0)
    nxt = lambda b, i: (b * nb + jnp.minimum(ab * i + ab, nb - 1), 0)
    cx = lambda b, i: (t_lat // ctx_len + b, 0)
    swap = lambda im: (lambda b, i: im(b, i)[::-1])
    kb = lambda im: pl.BlockSpec((BLOCK, 2 * KV_DIM), im)
    vb = lambda im: pl.BlockSpec((2 * KV_DIM, BLOCK), swap(im))
    bias = _window_bias(ctx_len)
    bias_a = lambda b, i: ((i > 0).astype(jnp.int32) + 2, 0, 0)
    bias_m = lambda b, i: (3, 0, 0)
    bias_b = lambda b, i: (1 + 2 * (i < steps - 1).astype(jnp.int32), 0, 0)
    bias_spec = lambda im: pl.BlockSpec((1,) + bias.shape[1:], im)
    return pl.pallas_call(
        _attn_lat_kernel,
        grid=(n_batch, steps),
        in_specs=[pl.BlockSpec(memory_space=pltpu.SMEM), pl.BlockSpec((ab * BLOCK, qd), cur),
                  kb(prev), pl.BlockSpec((ab * BLOCK, 2 * KV_DIM), cur), kb(nxt),
                  vb(prev), pl.BlockSpec((2 * KV_DIM, ab * BLOCK), swap(cur)), vb(nxt),
                  pl.BlockSpec((ctx_len, 2 * KV_DIM), cx), pl.BlockSpec((2 * KV_DIM, ctx_len), swap(cx)),
                  bias_spec(bias_a), bias_spec(bias_m), bias_spec(bias_b)],
        out_specs=pl.BlockSpec((ab * BLOCK, qd), cur),
        out_shape=jax.ShapeDtypeStruct((t_lat, qd), _BF),
        scratch_shapes=_attn_scratch(3 * BLOCK + ctx_len, BLOCK, blocks=ab),
        compiler_params=_params("arbitrary", "arbitrary"),
        name="attn_lat",
    )(sink, q, k2, k2, k2, v2t, v2t, v2t, k2, v2t, bias, bias, bias)


def _attn_ctx(sink, q, k2, v2t, *, n_batch, ctx_len, t_lat):
    qd = q.shape[1]
    cx = lambda b: (t_lat // ctx_len + b, 0)
    return pl.pallas_call(
        _attn_ctx_kernel,
        grid=(n_batch,),
        in_specs=[pl.BlockSpec(memory_space=pltpu.SMEM), pl.BlockSpec((ctx_len, qd), cx),
                  pl.BlockSpec((ctx_len, 2 * KV_DIM), cx),
                  pl.BlockSpec((2 * KV_DIM, ctx_len), lambda b: (0, t_lat // ctx_len + b))],
        out_specs=pl.BlockSpec((ctx_len, qd), lambda b: (b, 0)),
        out_shape=jax.ShapeDtypeStruct((n_batch * ctx_len, qd), _BF),
        scratch_shapes=_attn_scratch(ctx_len, ctx_len),
        compiler_params=_params("arbitrary"),
        name="attn_ctx",
    )(sink, q, k2, v2t)


def _pack_bf16_pairs(h):
    half = h.shape[1] // 2
    lo = lax.bitcast_convert_type(h[:, :half].astype(_BF).astype(_F32), jnp.uint32)
    hi = lax.bitcast_convert_type(h[:, half:].astype(_BF).astype(_F32), jnp.uint32)
    return (lo >> 16) | hi


def _unpack_bf16_pairs(p):
    lo = lax.bitcast_convert_type(p << 16, _F32)
    hi = lax.bitcast_convert_type(p & jnp.uint32(0xFFFF0000), _F32)
    return jnp.concatenate([lo, hi], axis=1).astype(_BF)


def _merge_kernel(x_ref, mod_ref, g1_ref, g2_ref, wg_ref, fml_ref, fmc_ref, atl_ref, atc_ref,
                  wfo_ref, wao_ref, wout_ref, rw_ref, rb_ref,
                  x1_ref, hp_ref, idx_ref, gate_ref, cnt_ref, h_scr, y_scr, *, nxt):
    i = pl.program_id(0)
    d = x_ref.shape[1]
    m = mod_ref[0]
    is_lat = i < nxt
    h_scr[...] = _norm_mod(x_ref[...], g1_ref[...], m[0:1], m[1:2]).astype(_BF)
    fm = jnp.where(is_lat, fml_ref[...], fmc_ref[...])
    at = jnp.where(is_lat, atl_ref[...], atc_ref[...])
    for c in range(d // MERGE_COLS):
        sl = slice(c * MERGE_COLS, (c + 1) * MERGE_COLS)
        sg = slice(d + c * MERGE_COLS, d + (c + 1) * MERGE_COLS)
        h = h_scr[...]
        y = (_sigmoid(_dot(h, wg_ref[:, sl])) * _dot(fm, wfo_ref[:, sl])
             + _sigmoid(_dot(h, wg_ref[:, sg])) * _dot(at, wao_ref[:, sl]))
        y_scr[:, sl] = y.astype(_BF)
    x1_ref[...] = x_ref[...] + m[2:3] * _dot(y_scr[...], wout_ref[...])
    h2 = _norm_mod(x1_ref[...], g2_ref[...], m[3:4], m[4:5])
    hp_ref[...] = _pack_bf16_pairs(h2)
    logits = _dot(h2.astype(_BF), rw_ref[...]) + rb_ref[...]
    lane = lax.broadcasted_iota(jnp.int32, logits.shape, 1).astype(_F32)
    vals, idxs = [], []
    for _ in range(TOP_K):
        mx = jnp.max(logits, axis=-1, keepdims=True)
        ix = jnp.min(jnp.where(logits == mx, lane, float(LANES)), axis=-1, keepdims=True)
        logits = jnp.where(lane == ix, NEG_INF, logits)
        vals.append(mx)
        idxs.append(ix)
    es = [jnp.exp(v - vals[0]) for v in vals]
    den = es[0] + es[1] + es[2] + es[3]
    idx4 = jnp.zeros(logits.shape, _F32)
    gate4 = jnp.zeros(logits.shape, _F32)
    sel = jnp.zeros(logits.shape, _F32)
    for k in range(TOP_K):
        idx4 = jnp.where(lane == float(k), idxs[k], idx4)
        gate4 = jnp.where(lane == float(k), es[k] / den, gate4)
        sel = sel + jnp.where(lane == idxs[k], 1.0, 0.0)
    idx_ref[...] = idx4.astype(jnp.int32)
    gate_ref[...] = gate4

    @pl.when(i == 0)
    def _():
        cnt_ref[...] = jnp.zeros(cnt_ref.shape, _F32)

    cnt_ref[0:1, :] += jnp.sum(sel, axis=0, keepdims=True)


def _merge(xa, mod, g1, g2, w_g, fm_lat, fm_ctx, at_lat, at_ctx, w_fo, w_ao, w_out, rw, rb,
           *, n, t_lat, n_batch):
    tt, d = xa.shape
    fd = fm_lat.shape[1]
    nt = tt // TM
    nxt = t_lat // TM
    row = lambda i: (i, 0)
    lat = lambda i: (jnp.minimum(i, nxt - 1), 0)
    cxt = lambda i: (jnp.maximum(i - nxt, 0), 0)
    const = lambda a: pl.BlockSpec(a.shape, lambda i: (0, 0))
    mod_idx = lambda i: (jnp.minimum(i * TM // n, n_batch), 0, 0)
    return pl.pallas_call(
        functools.partial(_merge_kernel, nxt=nxt),
        grid=(nt,),
        in_specs=[pl.BlockSpec((TM, d), row), pl.BlockSpec((1, N_MOD, d), mod_idx),
                  const(g1), const(g2), const(w_g),
                  pl.BlockSpec((TM, fd), lat), pl.BlockSpec((TM, fd), cxt),
                  pl.BlockSpec((TM, fd), lat), pl.BlockSpec((TM, fd), cxt),
                  const(w_fo), const(w_ao), const(w_out), const(rw), const(rb)],
        out_specs=[pl.BlockSpec((TM, d), row), pl.BlockSpec((TM, d // 2), row),
                   pl.BlockSpec((TM, LANES), row), pl.BlockSpec((TM, LANES), row),
                   pl.BlockSpec((8, LANES), lambda i: (0, 0))],
        out_shape=[jax.ShapeDtypeStruct((tt, d), _F32), jax.ShapeDtypeStruct((tt, d // 2), jnp.uint32),
                   jax.ShapeDtypeStruct((tt, LANES), jnp.int32), jax.ShapeDtypeStruct((tt, LANES), _F32),
                   jax.ShapeDtypeStruct((8, LANES), _F32)],
        scratch_shapes=[pltpu.VMEM((TM, d), _BF), pltpu.VMEM((TM, d), _BF)],
        compiler_params=_params("arbitrary"),
        name="merge_router",
    )(xa, mod, g1, g2, w_g, fm_lat, fm_ctx, at_lat, at_ctx, w_fo, w_ao, w_out, rw, rb)


def _route_kernel(idx_ref, ps_ref, d_ref, carry):
    i = pl.program_id(0)

    @pl.when(i == 0)
    def _():
        carry[...] = ps_ref[...]

    idx4 = idx_ref[...]
    lane = lax.broadcasted_iota(jnp.int32, idx4.shape, 1)
    cols = [idx4[:, k:k + 1] for k in range(TOP_K)]
    sel = jnp.zeros(idx4.shape, _F32)
    for k in range(TOP_K):
        sel = sel + jnp.where(lane == cols[k], 1.0, 0.0)
    r = lax.broadcasted_iota(jnp.int32, (TM, TM), 0)
    c = lax.broadcasted_iota(jnp.int32, (TM, TM), 1)
    tri = jnp.where(r > c, 1.0, 0.0).astype(_BF)
    slot = carry[...] + _dot(tri, sel.astype(_BF))
    dest4 = jnp.zeros(idx4.shape, jnp.int32)
    for k in range(TOP_K):
        dk = jnp.sum(jnp.where(lane == cols[k], slot, 0.0), axis=-1, keepdims=True)
        dest4 = jnp.where(lane == k, dk.astype(jnp.int32), dest4)
    d_ref[...] = dest4
    carry[...] += jnp.sum(sel, axis=0, keepdims=True)


def _route(idx4, pad_start):
    tt = idx4.shape[0]
    return pl.pallas_call(
        _route_kernel,
        grid=(tt // TM,),
        in_specs=[pl.BlockSpec((TM, LANES), lambda i: (i, 0)), pl.BlockSpec((1, LANES), lambda i: (0, 0))],
        out_specs=pl.BlockSpec((TM, LANES), lambda i: (i, 0)),
        out_shape=jax.ShapeDtypeStruct((tt, LANES), jnp.int32),
        scratch_shapes=[pltpu.VMEM((1, LANES), _F32)],
        compiler_params=_params("arbitrary"),
        name="route_slots",
    )(idx4, pad_start)


def _zero_tails_kernel(pend_ref, cnt_ref, nv_ref, xs_ref, zbuf, sem):
    n_blocks = xs_ref.shape[0] // TME
    zbuf[...] = jnp.zeros(zbuf.shape, zbuf.dtype)

    def zero_block(start):
        return pltpu.make_async_copy(zbuf, xs_ref.at[pl.ds(pl.multiple_of(start, TME), TME)], sem)

    for e in range(N_EXPERTS):
        @pl.when(cnt_ref[e] > 0)
        def _():
            zero_block(pend_ref[e] - TME).start()

    def start_unused(j, carry):
        zero_block(j * TME).start()
        return carry

    def wait_unused(j, carry):
        zero_block(j * TME).wait()
        return carry

    lax.fori_loop(nv_ref[0], n_blocks, start_unused, 0)
    for e in range(N_EXPERTS):
        @pl.when(cnt_ref[e] > 0)
        def _():
            zero_block(pend_ref[e] - TME).wait()
    lax.fori_loop(nv_ref[0], n_blocks, wait_unused, 0)


def _zero_tails(pad_end, cnt, n_valid, n_slots, w, dtype):
    grid_spec = pltpu.PrefetchScalarGridSpec(
        num_scalar_prefetch=3,
        grid=(1,),
        in_specs=[],
        out_specs=pl.BlockSpec(memory_space=pl.ANY),
        scratch_shapes=[pltpu.VMEM((TME, w), dtype), pltpu.SemaphoreType.DMA],
    )
    return pl.pallas_call(
        _zero_tails_kernel,
        grid_spec=grid_spec,
        out_shape=jax.ShapeDtypeStruct((n_slots, w), dtype),
        compiler_params=_params("arbitrary"),
        name="moe_zero_tails",
    )(pad_end, cnt, n_valid)


def _sc_scatter(rows, idx, base):
    tt, w = rows.shape
    win = SC_SCATTER_WINDOW
    workers = SC_CORES * SC_SUBCORES
    per_worker = tt // workers
    n_win = per_worker // win
    assert tt % workers == 0 and per_worker % win == 0 and n_win % 2 == 0
    mesh = plsc.VectorSubcoreMesh(core_axis_name="core", subcore_axis_name="subcore")
    out = jax.new_ref(base)
    idx_buf = pltpu.VMEM((win,), jnp.int32)
    row_buf = pltpu.VMEM((win, w), rows.dtype)

    @pl.kernel(out_type=(), mesh=mesh,
               scratch_types=[idx_buf] * (2 * TOP_K) + [row_buf, row_buf] + [pltpu.SemaphoreType.DMA] * 4)
    def scatter(r_hbm, i_hbm, o_hbm, *scratch):
        idx_a, idx_b = scratch[:TOP_K], scratch[TOP_K:2 * TOP_K]
        rows_a, rows_b, lsem_a, lsem_b, ssem_a, ssem_b = scratch[2 * TOP_K:]
        wid = lax.axis_index("subcore") * SC_CORES + lax.axis_index("core")
        first = wid * per_worker

        def load_start(c, rows_v, sem):
            pltpu.make_async_copy(r_hbm.at[pl.ds(first + c * win, win)], rows_v, sem).start()

        def load_wait(rows_v, sem):
            pltpu.make_async_copy(r_hbm.at[pl.ds(first, win)], rows_v, sem).wait()

        def scatter_start(c, rows_v, idxs, sem):
            for k in range(TOP_K):
                pltpu.sync_copy(i_hbm.at[pl.ds(k * tt + first + c * win, win)], idxs[k])
                pltpu.make_async_copy(rows_v, o_hbm.at[idxs[k]], sem).start()

        def scatter_wait(rows_v, idxs, sem):
            for k in range(TOP_K):
                pltpu.make_async_copy(rows_v, o_hbm.at[idxs[k]], sem).wait()

        load_start(0, rows_a, lsem_a)

        @pl.loop(0, n_win // 2)
        def _(i):
            c = 2 * i

            @pl.when(i > 0)
            def _():
                scatter_wait(rows_b, idx_b, ssem_b)

            load_start(c + 1, rows_b, lsem_b)
            load_wait(rows_a, lsem_a)
            scatter_start(c, rows_a, idx_a, ssem_a)
            scatter_wait(rows_a, idx_a, ssem_a)

            @pl.when(c + 2 < n_win)
            def _():
                load_start(c + 2, rows_a, lsem_a)

            load_wait(rows_b, lsem_b)
            scatter_start(c + 1, rows_b, idx_b, ssem_b)

        scatter_wait(rows_b, idx_b, ssem_b)

    scatter(rows, idx, out)
    return jax.freeze(out)


def _moe_kernel(be_ref, nv_ref, xs_ref, wgu_ref, bgu_ref, wd_ref, bd_ref, y_ref, wgu_bf, wd_bf):
    i = pl.program_id(0)
    de = wd_ref.shape[2]
    valid = i < nv_ref[0]
    new_expert = jnp.logical_or(i == 0, be_ref[i] != be_ref[jnp.maximum(i - 1, 0)])

    @pl.when(jnp.logical_and(valid, new_expert))
    def _():
        def cast_rows(ref, out, rows):
            def body(j, carry):
                sl = pl.ds(pl.multiple_of(j * CAST_ROWS, CAST_ROWS), CAST_ROWS)
                out[sl, :] = ref[0, 0, sl, :].astype(_BF)
                return carry
            lax.fori_loop(0, rows // CAST_ROWS, body, 0)
        cast_rows(wgu_ref, wgu_bf, wgu_ref.shape[2])
        cast_rows(wd_ref, wd_bf, de)

    @pl.when(valid)
    def _():
        xb = _unpack_bf16_pairs(xs_ref[...])
        gu = _dot(xb, wgu_bf[...]) + bgu_ref[0, 0]
        a = jnp.minimum(gu[:, :de], SWIGLU_LIMIT)
        u = jnp.clip(gu[:, de:], -SWIGLU_LIMIT, SWIGLU_LIMIT)
        act = a * _sigmoid(SWIGLU_ALPHA * a) * (u + 1)
        y_ref[...] = _pack_bf16_pairs(_dot(act.astype(_BF), wd_bf[...]) + bd_ref[0, 0])

    @pl.when(i >= nv_ref[0])
    def _():
        y_ref[...] = jnp.zeros(y_ref.shape, y_ref.dtype)


def _moe(block_e, n_valid, xs, w_gu, b_gu, w_down, b_down, *, layer):
    n_slots, w = xs.shape
    depth, ne, d, de2 = w_gu.shape
    de = de2 // 2
    n_blocks = n_slots // TME
    ex = lambda i, be, nv: (layer, be[i], 0, 0)
    grid_spec = pltpu.PrefetchScalarGridSpec(
        num_scalar_prefetch=2,
        grid=(n_blocks,),
        in_specs=[pl.BlockSpec((TME, w), lambda i, be, nv: (jnp.minimum(i, nv[0] - 1), 0)),
                  pl.BlockSpec((1, 1, d, de2), ex), pl.BlockSpec((1, 1, 1, de2), ex),
                  pl.BlockSpec((1, 1, de, d), ex), pl.BlockSpec((1, 1, 1, d), ex)],
        out_specs=pl.BlockSpec((TME, d // 2), lambda i, be, nv: (i, 0)),
        scratch_shapes=[pltpu.VMEM((d, de2), _BF), pltpu.VMEM((de, d), _BF)],
    )
    return pl.pallas_call(
        _moe_kernel,
        grid_spec=grid_spec,
        out_shape=jax.ShapeDtypeStruct((n_slots, d // 2), jnp.uint32),
        compiler_params=_params("arbitrary"),
        name="moe_experts",
    )(block_e, n_valid, xs, w_gu, b_gu.reshape(depth, ne, 1, de2), w_down, b_down.reshape(depth, ne, 1, d))


def _sc_gather(table, idx):
    n_idx, (_, w) = idx.shape[0], table.shape
    workers = SC_CORES * SC_SUBCORES
    per_worker = n_idx // workers
    n_win = per_worker // SC_WINDOW
    assert n_idx % workers == 0 and per_worker % SC_WINDOW == 0 and n_win % 2 == 0
    mesh = plsc.VectorSubcoreMesh(core_axis_name="core", subcore_axis_name="subcore")
    idx_buf = pltpu.VMEM((SC_WINDOW,), jnp.int32)
    row_buf = pltpu.VMEM((SC_WINDOW, w), table.dtype)

    @pl.kernel(out_type=jax.ShapeDtypeStruct((n_idx, w), table.dtype), mesh=mesh,
               scratch_types=[idx_buf, idx_buf, row_buf, row_buf] + [pltpu.SemaphoreType.DMA] * 4)
    def gather(t_hbm, i_hbm, o_hbm, idx_a, idx_b, rows_a, rows_b, gsem_a, gsem_b, ssem_a, ssem_b):
        wid = lax.axis_index("subcore") * SC_CORES + lax.axis_index("core")
        first = wid * per_worker

        def gather_start(c, idx_v, rows_v, sem):
            pltpu.sync_copy(i_hbm.at[pl.ds(first + c * SC_WINDOW, SC_WINDOW)], idx_v)
            pltpu.make_async_copy(t_hbm.at[idx_v], rows_v, sem).start()

        def gather_wait(idx_v, rows_v, sem):
            pltpu.make_async_copy(t_hbm.at[idx_v], rows_v, sem).wait()

        def store_start(c, rows_v, sem):
            pltpu.make_async_copy(rows_v, o_hbm.at[pl.ds(first + c * SC_WINDOW, SC_WINDOW)], sem).start()

        def store_wait(rows_v, sem):
            pltpu.make_async_copy(rows_v, o_hbm.at[pl.ds(first, SC_WINDOW)], sem).wait()

        gather_start(0, idx_a, rows_a, gsem_a)

        @pl.loop(0, n_win // 2)
        def _(i):
            c = 2 * i

            @pl.when(i > 0)
            def _():
                store_wait(rows_b, ssem_b)

            gather_start(c + 1, idx_b, rows_b, gsem_b)
            gather_wait(idx_a, rows_a, gsem_a)
            store_start(c, rows_a, ssem_a)
            store_wait(rows_a, ssem_a)

            @pl.when(c + 2 < n_win)
            def _():
                gather_start(c + 2, idx_a, rows_a, gsem_a)

            gather_wait(idx_b, rows_b, gsem_b)
            store_start(c + 1, rows_b, ssem_b)

        store_wait(rows_b, ssem_b)

    return gather(table, idx)


def _combine_kernel(yg_ref, gate_ref, x_ref, mod_ref, fg_ref, o_ref, *, final):
    g = gate_ref[...]
    half = x_ref.shape[1] // 2
    lo = jnp.zeros((x_ref.shape[0], half), _F32)
    hi = jnp.zeros((x_ref.shape[0], half), _F32)
    for k in range(TOP_K):
        p = yg_ref[k]
        lo = lo + g[:, k:k + 1] * lax.bitcast_convert_type(p << 16, _F32)
        hi = hi + g[:, k:k + 1] * lax.bitcast_convert_type(p & jnp.uint32(0xFFFF0000), _F32)
    x2 = x_ref[...] + mod_ref[0][5:6] * jnp.concatenate([lo, hi], axis=1)
    if final:
        ms = jnp.mean(x2 * x2, axis=-1, keepdims=True)
        x2 = x2 * lax.rsqrt(ms + EPS) * fg_ref[...]
    o_ref[...] = x2


def _combine(yg, gate4, x1, mod, fg, *, n, n_batch, rows, final):
    d = x1.shape[1]
    row = lambda i: (i, 0)
    mod_idx = lambda i: (jnp.minimum(i * TMC // n, n_batch), 0, 0)
    return pl.pallas_call(
        functools.partial(_combine_kernel, final=final),
        grid=(rows // TMC,),
        in_specs=[pl.BlockSpec((TOP_K, TMC, d // 2), lambda i: (0, i, 0)),
                  pl.BlockSpec((TMC, LANES), row), pl.BlockSpec((TMC, d), row),
                  pl.BlockSpec((1, N_MOD, d), mod_idx), pl.BlockSpec((1, d), lambda i: (0, 0))],
        out_specs=pl.BlockSpec((TMC, d), row),
        out_shape=jax.ShapeDtypeStruct((rows, d), _F32),
        compiler_params=_params("arbitrary"),
        name="moe_combine",
    )(yg, gate4, x1, mod, fg)


def _routing_tables(counts, n_blocks):
    cnt = counts[0, :N_EXPERTS].astype(jnp.int32)
    padded = (cnt + TME - 1) // TME * TME
    pad_end = jnp.cumsum(padded)
    pad_start = pad_end - padded
    ps = jnp.zeros((1, LANES), _F32).at[0, :N_EXPERTS].set(pad_start.astype(_F32))
    blk = jnp.arange(n_blocks, dtype=jnp.int32) * TME
    block_e = jnp.sum((pad_end[None, :] <= blk[:, None]).astype(jnp.int32), axis=1)
    e_last = jnp.max(jnp.where(cnt > 0, jnp.arange(N_EXPERTS, dtype=jnp.int32), 0))
    block_e = jnp.minimum(block_e, e_last).astype(jnp.int32)
    n_valid = (pad_end[-1] // TME).astype(jnp.int32).reshape(1)
    return ps, block_e, n_valid, pad_end.astype(jnp.int32), cnt


def kernel(x, c, ctx, c_ctx, ada_w, ada_b, norm1_g, norm2_g, w_in, attn_sink, w_fourier_out, w_attn_out,
           w_out, router_w, router_b, expert_w_gu, expert_b_gu, expert_w_down, expert_b_down, final_norm_g):
    n_batch, n, d = x.shape
    ctx_len = ctx.shape[1]
    depth = ada_w.shape[0]
    t_lat = n_batch * n
    t_ctx = n_batch * ctx_len
    tt = t_lat + t_ctx
    fd = qd = d // 2
    assert n % TM == 0 and t_ctx == TM and n % (BLOCK * FB_NB) == 0 and n_batch < MOD_ROWS
    assert t_lat % ctx_len == 0 and TM % ctx_len == 0

    xa = jnp.concatenate([x.reshape(t_lat, d), ctx.reshape(t_ctx, d)], axis=0)
    cc = jnp.concatenate([c, c_ctx[None, :], jnp.zeros((MOD_ROWS - n_batch - 1, d), _F32)], axis=0)
    mod = _ada(cc, ada_w, ada_b).reshape(depth, MOD_ROWS, N_MOD, d)
    cos_t, sin_t = _rope_tables(n)
    consts = _dft_consts(n, ctx_len)

    n_in = fd + qd + 2 * KV_DIM
    w_a = w_in[:, :, :n_in].astype(_BF)
    w_g = w_in[:, :, n_in:].astype(_BF)
    w_fo, w_ao, w_o = w_fourier_out.astype(_BF), w_attn_out.astype(_BF), w_out.astype(_BF)
    rw = jnp.pad(router_w, ((0, 0), (0, 0), (0, LANES - N_EXPERTS))).astype(_BF)
    rb = jnp.pad(router_b, ((0, 0), (0, LANES - N_EXPERTS)), constant_values=NEG_INF)[:, None, :]

    n_blocks = -(-(tt * TOP_K + N_EXPERTS * (TME - 1)) // TME)
    dims = dict(n=n, t_lat=t_lat, n_batch=n_batch)
    for l in range(depth):
        last = l == depth - 1
        f, q, k2, v2 = _proj(xa, mod[l], norm1_g[l][None, :], w_a[l], cos_t, sin_t, **dims)
        fm_lat = _fourier_lat(f, consts, n_batch=n_batch, n=n)
        fm_ctx = _fourier_ctx(f, consts, n_batch=n_batch, ctx_len=ctx_len)
        at_lat = _attn_lat(attn_sink[l], q, k2, v2, n_batch=n_batch, n=n, ctx_len=ctx_len, t_lat=t_lat)
        at_ctx = _attn_ctx(attn_sink[l], q, k2, v2, n_batch=n_batch, ctx_len=ctx_len, t_lat=t_lat)
        x1, hp, idx4, gate4, counts = _merge(
            xa, mod[l], norm1_g[l][None, :], norm2_g[l][None, :], w_g[l], fm_lat, fm_ctx, at_lat, at_ctx,
            w_fo[l], w_ao[l], w_o[l], rw[l], rb[l], **dims)
        pad_start, block_e, n_valid, pad_end, cnt = _routing_tables(counts, n_blocks)
        dest = _route(idx4, pad_start)[:, :TOP_K].T.reshape(-1)
        xs = _sc_scatter(hp, dest, _zero_tails(pad_end, cnt, n_valid, n_blocks * TME, hp.shape[1], hp.dtype))
        y = _moe(block_e, n_valid, xs, expert_w_gu, expert_b_gu, expert_w_down, expert_b_down, layer=l)
        yg = _sc_gather(y, dest).reshape(TOP_K, tt, d // 2)
        xa = _combine(yg, gate4, x1, mod[l], final_norm_g[None, :], n=n, n_batch=n_batch,
                      rows=t_lat if last else tt, final=last)
    return xa.reshape(n_batch, n, d)
```
